```python
import math
import jax, jax.numpy as jnp
from jax import lax
import numpy as np

D_MODEL = 1024
BATCH = 4
SEQ = 8192
DEPTH = 1

D_MIX = D_MODEL
CONV_CH = D_MIX // 2
CONV_W = 3
N_HEADS = 8
HEAD_DIM = (D_MIX - CONV_CH) // N_HEADS
N_KV = 2
GQA = N_HEADS // N_KV
CMP_BLOCK = 32
CMP_STRIDE = 16
CMP_HIDDEN = 2 * HEAD_DIM
SEL_BLOCK = 64
N_SEL = 16
WINDOW = 512
Q_BLOCK = 128
ATTN_SCALE = HEAD_DIM ** -0.5
FORCE_SCORE = 1e4
NEG_INF = -1e30
N_GROUPS = 4
EXP_PER_GROUP = 8
N_EXPERTS = N_GROUPS * EXP_PER_GROUP
TOP_K = 2
D_EXPERT = D_MODEL // 2
MOE_CHUNK = 256
D_PLE = 256
ALPHA = (2 * DEPTH) ** 0.25
BETA = (8 * DEPTH) ** -0.25
LN_EPS = 1e-5

SPLIT_SIZES = (CONV_CH, CONV_CH, CONV_CH,
               N_HEADS * HEAD_DIM,
               N_KV * HEAD_DIM, N_KV * HEAD_DIM,
               N_KV * HEAD_DIM, N_KV * HEAD_DIM,
               N_KV * HEAD_DIM, N_KV * HEAD_DIM,
               3 * N_HEADS)
D_IN_PROJ = sum(SPLIT_SIZES)
SPLIT_POINTS = tuple(int(v) for v in np.cumsum(SPLIT_SIZES)[:-1])

kernel_name = "hybrid_conv_nsa_hmoe_deepnorm"


def layer_norm(x, g, b):
    xf = x.astype(jnp.float32)
    mu = jnp.mean(xf, axis=-1, keepdims=True)
    var = jnp.mean(jnp.square(xf - mu), axis=-1, keepdims=True)
    y = (xf - mu) * lax.rsqrt(var + LN_EPS) * g.astype(jnp.float32) + b.astype(jnp.float32)
    return y.astype(x.dtype)


def masked_softmax(s, mask):
    s = jnp.where(mask, s.astype(jnp.float32), NEG_INF)
    pr = jax.nn.softmax(s, axis=-1)
    return jnp.where(mask, pr, 0.0)


def causal_dwconv(u, w):
    return lax.conv_general_dilated(
        u, w[:, None, :].astype(u.dtype), window_strides=(1,),
        padding=[(CONV_W - 1, 0)], dimension_numbers=("NWC", "WIO", "NWC"),
        feature_group_count=u.shape[-1])


def compress_blocks(kv, pe, w1, w2):
    bsz, s = kv.shape[:2]
    n_cmp = (s - CMP_BLOCK) // CMP_STRIDE + 1
    idx = jnp.arange(n_cmp)[:, None] * CMP_STRIDE + jnp.arange(CMP_BLOCK)[None, :]
    blk = kv[:, idx] + pe[None, None, :, None, :]
    flat = blk.transpose(0, 1, 3, 2, 4).reshape(bsz, n_cmp, N_KV, CMP_BLOCK * HEAD_DIM)
    return jax.nn.gelu(flat @ w1) @ w2


def nsa_attention(q, kc, vc, ks, vs, kw, vw, gates,
                  pe_ck, w1_ck, w2_ck, pe_cv, w1_cv, w2_cv):
    bsz, s = q.shape[:2]
    q = q.reshape(bsz, s, N_KV, GQA, HEAD_DIM)
    kc, vc, ks, vs, kw, vw = [a.reshape(bsz, s, N_KV, HEAD_DIM) for a in (kc, vc, ks, vs, kw, vw)]
    k_cmp = compress_blocks(kc, pe_ck, w1_ck, w2_ck)
    v_cmp = compress_blocks(vc, pe_cv, w1_cv, w2_cv)
    n_cmp = k_cmp.shape[1]
    n_sel = s // SEL_BLOCK
    n_top = min(N_SEL, n_sel)
    cmp_start = jnp.arange(n_cmp) * CMP_STRIDE
    cmp_end = cmp_start + CMP_BLOCK - 1
    sel_start = jnp.arange(n_sel) * SEL_BLOCK
    cover = ((cmp_start[:, None] < sel_start[None, :] + SEL_BLOCK) &
             (cmp_start[:, None] + CMP_BLOCK > sel_start[None, :])).astype(jnp.float32)
    ks_b = ks.reshape(bsz, n_sel, SEL_BLOCK, N_KV, HEAD_DIM).transpose(0, 3, 1, 2, 4)
    vs_b = vs.reshape(bsz, n_sel, SEL_BLOCK, N_KV, HEAD_DIM).transpose(0, 3, 1, 2, 4)
    kw_p = jnp.pad(kw, ((0, 0), (WINDOW, 0), (0, 0), (0, 0)))
    vw_p = jnp.pad(vw, ((0, 0), (WINDOW, 0), (0, 0), (0, 0)))
    g = jax.nn.sigmoid(gates.astype(jnp.float32)).reshape(bsz, s, N_KV, GQA, 3)
    b_ix = jnp.arange(bsz)[:, None, None, None]
    h_ix = jnp.arange(N_KV)[None, None, :, None]
    blk_ids = jnp.arange(n_sel)
    tok_in_blk = jnp.arange(SEL_BLOCK)

    def one_block(c):
        q0 = c * Q_BLOCK
        qc = lax.dynamic_slice_in_dim(q, q0, Q_BLOCK, 1)
        gc = lax.dynamic_slice_in_dim(g, q0, Q_BLOCK, 1)
        t = q0 + jnp.arange(Q_BLOCK)
        sc = jnp.einsum('bqkgd,bnkd->bqkgn', qc, k_cmp) * ATTN_SCALE
        mask_c = (cmp_end[None, :] <= t[:, None])[None, :, None, None, :]
        p_cmp = masked_softmax(sc, mask_c)
        o_cmp = jnp.einsum('bqkgn,bnkd->bqkgd', p_cmp.astype(v_cmp.dtype), v_cmp)
        imp = jnp.einsum('bqkgn,nj->bqkj', p_cmp, cover)
        jt = (t // SEL_BLOCK)[None, :, None, None]
        forced = (blk_ids == 0) | (blk_ids == jt) | (blk_ids == jt - 1)
        score = jnp.where(blk_ids > jt, -1.0, jnp.where(forced, FORCE_SCORE, imp))
        top_v, top_i = lax.top_k(score, n_top)
        k_g = ks_b[b_ix, h_ix, top_i].reshape(bsz, Q_BLOCK, N_KV, n_top * SEL_BLOCK, HEAD_DIM)
        v_g = vs_b[b_ix, h_ix, top_i].reshape(bsz, Q_BLOCK, N_KV, n_top * SEL_BLOCK, HEAD_DIM)
        pos = top_i[..., None] * SEL_BLOCK + tok_in_blk
        mask_s = (top_v >= 0)[..., None] & (pos <= t[None, :, None, None, None])
        mask_s = mask_s.reshape(bsz, Q_BLOCK, N_KV, n_top * SEL_BLOCK)[:, :, :, None, :]
        ss = jnp.einsum('bqkgd,bqkmd->bqkgm', qc, k_g) * ATTN_SCALE
        p_sel = masked_softmax(ss, mask_s)
        o_sel = jnp.einsum('bqkgm,bqkmd->bqkgd', p_sel.astype(v_g.dtype), v_g)
        kwc = lax.dynamic_slice_in_dim(kw_p, q0, WINDOW + Q_BLOCK, 1)
        vwc = lax.dynamic_slice_in_dim(vw_p, q0, WINDOW + Q_BLOCK, 1)
        spos = q0 - WINDOW + jnp.arange(WINDOW + Q_BLOCK)
        dist = t[:, None] - spos[None, :]
        mask_w = ((dist >= 0) & (dist < WINDOW) & (spos[None, :] >= 0))[None, :, None, None, :]
        sw = jnp.einsum('bqkgd,bmkd->bqkgm', qc, kwc) * ATTN_SCALE
        p_win = masked_softmax(sw, mask_w)
        o_win = jnp.einsum('bqkgm,bmkd->bqkgd', p_win.astype(vwc.dtype), vwc)
        o = gc[..., 0:1] * o_cmp + gc[..., 1:2] * o_sel + gc[..., 2:3] * o_win
        return o.reshape(bsz, Q_BLOCK, N_HEADS * HEAD_DIM).astype(q.dtype)

    out = lax.map(one_block, jnp.arange(s // Q_BLOCK))
    return out.transpose(1, 0, 2, 3).reshape(bsz, s, N_HEADS * HEAD_DIM)


def hybrid_mixer(h, w_in, w_conv, pe_ck, w1_ck, w2_ck, pe_cv, w1_cv, w2_cv, w_out):
    z = h @ w_in
    cb, cc, ch, q, kc, vc, ks, vs, kw, vw, gates = jnp.split(z, SPLIT_POINTS, axis=-1)
    y_conv = cb * causal_dwconv(cc * ch, w_conv)
    y_nsa = nsa_attention(q, kc, vc, ks, vs, kw, vw, gates,
                          pe_ck, w1_ck, w2_ck, pe_cv, w1_cv, w2_cv)
    return jnp.concatenate([y_conv, y_nsa.astype(y_conv.dtype)], axis=-1) @ w_out


def hier_moe(x, w_rg, b_rg, w_re, b_re, w_e_in, w_e_out):
    bsz, s, d = x.shape
    n_tok = bsz * s
    xt = x.reshape(n_tok, d)
    gprob = jax.nn.softmax((xt @ w_rg + b_rg).astype(jnp.float32), axis=-1)
    gsel = jnp.argmax(gprob, axis=-1)
    gp = jnp.max(gprob, axis=-1)
    elog = (xt @ w_re + b_re).astype(jnp.float32).reshape(n_tok, N_GROUPS, EXP_PER_GROUP)
    elog = jnp.einsum('tge,tg->te', elog, jax.nn.one_hot(gsel, N_GROUPS, dtype=jnp.float32))
    top_p, top_i = lax.top_k(jax.nn.softmax(elog, axis=-1), TOP_K)
    top_p = top_p / jnp.sum(top_p, axis=-1, keepdims=True)
    wts = (gp[:, None] * top_p).reshape(-1)
    eid = (gsel[:, None] * EXP_PER_GROUP + top_i).reshape(-1)
    tok = jnp.repeat(jnp.arange(n_tok), TOP_K)
    n_asg = n_tok * TOP_K
    order = jnp.argsort(eid)
    eid_s, tok_s, w_s = eid[order], tok[order], wts[order]
    counts = jnp.bincount(eid, length=N_EXPERTS)
    starts = jnp.cumsum(counts) - counts
    pcounts = (counts + MOE_CHUNK - 1) // MOE_CHUNK * MOE_CHUNK
    pends = jnp.cumsum(pcounts)
    pstarts = pends - pcounts
    dest = pstarts[eid_s] + (jnp.arange(n_asg) - starts[eid_s])
    n_chunk = -(-n_asg // MOE_CHUNK) + N_EXPERTS
    n_rows = n_chunk * MOE_CHUNK
    tok_buf = jnp.zeros((n_rows,), jnp.int32).at[dest].set(tok_s.astype(jnp.int32))
    w_buf = jnp.zeros((n_rows,), jnp.float32).at[dest].set(w_s)
    chunk_exp = jnp.clip(jnp.searchsorted(pends, jnp.arange(n_chunk) * MOE_CHUNK, side='right'),
                         0, N_EXPERTS - 1)

    def run_chunk(args):
        tok_c, e_c = args
        gate, up = jnp.split(xt[tok_c] @ w_e_in[e_c], 2, axis=-1)
        return (jax.nn.silu(gate) * up) @ w_e_out[e_c]

    out = lax.map(run_chunk, (tok_buf.reshape(n_chunk, MOE_CHUNK), chunk_exp))
    y = jax.ops.segment_sum(out.reshape(n_rows, d) * w_buf[:, None], tok_buf, num_segments=n_tok)
    return y.reshape(bsz, s, d).astype(x.dtype)


def setup_inputs(seed: int = 0) -> dict:
    key = jax.random.key(seed)
    ks = jax.random.split(key, 32)
    nrm = lambda k, shape, scale: jax.random.normal(k, shape, jnp.float32) * scale
    L = DEPTH
    return {
        "x": nrm(ks[0], (BATCH, SEQ, D_MODEL), 1.0),
        "p": nrm(ks[1], (L, BATCH, SEQ, D_PLE), 1.0),
        "w_in": nrm(ks[2], (L, D_MODEL, D_IN_PROJ), D_MODEL ** -0.5),
        "w_conv": nrm(ks[3], (L, CONV_W, CONV_CH), CONV_W ** -0.5),
        "pe_ck": nrm(ks[4], (L, CMP_BLOCK, HEAD_DIM), 0.02),
        "w1_ck": nrm(ks[5], (L, CMP_BLOCK * HEAD_DIM, CMP_HIDDEN), (CMP_BLOCK * HEAD_DIM) ** -0.5),
        "w2_ck": nrm(ks[6], (L, CMP_HIDDEN, HEAD_DIM), CMP_HIDDEN ** -0.5),
        "pe_cv": nrm(ks[7], (L, CMP_BLOCK, HEAD_DIM), 0.02),
        "w1_cv": nrm(ks[8], (L, CMP_BLOCK * HEAD_DIM, CMP_HIDDEN), (CMP_BLOCK * HEAD_DIM) ** -0.5),
        "w2_cv": nrm(ks[9], (L, CMP_HIDDEN, HEAD_DIM), CMP_HIDDEN ** -0.5),
        "w_out": nrm(ks[10], (L, D_MIX, D_MODEL), D_MIX ** -0.5 * BETA),
        "ln1_g": 1.0 + nrm(ks[11], (L, D_MODEL), 0.02),
        "ln1_b": nrm(ks[12], (L, D_MODEL), 0.02),
        "w_rg": nrm(ks[13], (L, D_MODEL, N_GROUPS), D_MODEL ** -0.5),
        "b_rg": nrm(ks[14], (L, N_GROUPS), 0.01),
        "w_re": nrm(ks[15], (L, D_MODEL, N_EXPERTS), D_MODEL ** -0.5),
        "b_re": nrm(ks[16], (L, N_EXPERTS), 0.01),
        "w_e_in": nrm(ks[17], (L, N_EXPERTS, D_MODEL, 2 * D_EXPERT), D_MODEL ** -0.5),
        "w_e_out": nrm(ks[18], (L, N_EXPERTS, D_EXPERT, D_MODEL), D_EXPERT ** -0.5 * BETA),
        "ln2_g": 1.0 + nrm(ks[19], (L, D_MODEL), 0.02),
        "ln2_b": nrm(ks[20], (L, D_MODEL), 0.02),
        "w_ple": nrm(ks[21], (L, D_PLE, D_MODEL), D_PLE ** -0.5 * BETA),
        "w_ple_gate": nrm(ks[22], (L, D_MODEL, D_MODEL), D_MODEL ** -0.5),
        "b_ple_gate": nrm(ks[23], (L, D_MODEL), 0.02),
        "ln3_g": 1.0 + nrm(ks[24], (L, D_MODEL), 0.02),
        "ln3_b": nrm(ks[25], (L, D_MODEL), 0.02),
    }


def reference(x, p, w_in, w_conv, pe_ck, w1_ck, w2_ck, pe_cv, w1_cv, w2_cv, w_out,
              ln1_g, ln1_b, w_rg, b_rg, w_re, b_re, w_e_in, w_e_out, ln2_g, ln2_b,
              w_ple, w_ple_gate, b_ple_gate, ln3_g, ln3_b):
    h = x
    for i in range(DEPTH):
        mix = hybrid_mixer(h, w_in[i], w_conv[i], pe_ck[i], w1_ck[i], w2_ck[i],
                           pe_cv[i], w1_cv[i], w2_cv[i], w_out[i])
        h = layer_norm(ALPHA * h + mix, ln1_g[i], ln1_b[i])
        ffn = hier_moe(h, w_rg[i], b_rg[i], w_re[i], b_re[i], w_e_in[i], w_e_out[i])
        h = layer_norm(ALPHA * h + ffn, ln2_g[i], ln2_b[i])
        ple = (p[i] @ w_ple[i]) * jax.nn.sigmoid(h @ w_ple_gate[i] + b_ple_gate[i])
        h = layer_norm(ALPHA * h + ple.astype(h.dtype), ln3_g[i], ln3_b[i])
    return h
```

```python
import functools

import jax
import jax.numpy as jnp
import numpy as np
from jax import lax
from jax.experimental import pallas as pl
from jax.experimental.pallas import tpu as pltpu

F32 = jnp.float32
BF16 = jnp.bfloat16
HIGHEST = lax.Precision.HIGHEST

CONV_CH = 512
CONV_W = 3
N_HEADS = 8
HEAD_DIM = 64
N_KV = 2
GQA = N_HEADS // N_KV
CMP_BLOCK = 32
CMP_STRIDE = 16
CMP_HIDDEN = 2 * HEAD_DIM
SEL_BLOCK = 64
N_SEL = 16
WINDOW = 512
ATTN_SCALE = HEAD_DIM ** -0.5
FORCE_SCORE = 1e4
NEG_INF = -1e30
N_GROUPS = 4
EXP_PER_GROUP = 8
N_EXPERTS = N_GROUPS * EXP_PER_GROUP
D_EXPERT = 512
MOE_CHUNK = 256
DEPTH = 1
ALPHA = (2 * DEPTH) ** 0.25
LN_EPS = 1e-5

LANES = 128
SUBLANES = 8
VMEM_LIMIT = 56 * 1024 * 1024

PROJ_TM = 512
Q_TILE = 128
SEL_TK = 512
WIN_KEYS = WINDOW + Q_TILE
POST_TM = 512
ROW_TM = 256
ROUTE_LANE0 = N_GROUPS
GATE_ROWS = 16
V_ROWS = HEAD_DIM + 16


def _cparams(sem, vmem=VMEM_LIMIT):
    return pltpu.CompilerParams(dimension_semantics=sem, vmem_limit_bytes=vmem)


def _layer_norm(v, g, b):
    mu = jnp.mean(v, axis=-1, keepdims=True)
    d = v - mu
    var = jnp.mean(d * d, axis=-1, keepdims=True)
    return d * lax.rsqrt(var + LN_EPS) * g + b


def _proj_kernel(x_ref, wn_ref, wt_ref, wc_ref,
                 yconv_ref, cmpkv_ref, kvn_ref, qt_ref, vst_ref, vwt_ref, gt_ref,
                 carry_ref):
    si = pl.program_id(1)
    tm = x_ref.shape[1]
    xb = x_ref[0].astype(BF16)
    zn = jnp.dot(xb, wn_ref[...], preferred_element_type=F32)
    zt = lax.dot_general(wt_ref[...], xb, (((1,), (1,)), ((), ())),
                         preferred_element_type=F32)

    cb = zn[:, 0:CONV_CH]
    u = zn[:, CONV_CH:2 * CONV_CH] * zn[:, 2 * CONV_CH:3 * CONV_CH]

    @pl.when(si == 0)
    def _():
        carry_ref[...] = jnp.zeros_like(carry_ref)

    prev = carry_ref[...]
    rows = lax.broadcasted_iota(jnp.int32, u.shape, 0)
    u1 = jnp.where(rows == 0, prev[7:8, :], pltpu.roll(u, 1, 0))
    u2 = jnp.where(rows == 0, prev[6:7, :],
                   jnp.where(rows == 1, prev[7:8, :], pltpu.roll(u, 2, 0)))
    w = wc_ref[...]
    yconv_ref[0] = (cb * (w[0:1, :] * u2 + w[1:2, :] * u1 + w[2:3, :] * u)).astype(BF16)
    carry_ref[...] = u[tm - SUBLANES:tm, :]

    cmpkv_ref[0] = zn[:, 1536:1792]
    kvn_ref[0] = zn[:, 1792:2048].astype(BF16)

    qt_ref[0] = zt[0:512, :].astype(BF16)
    ones = jnp.ones((V_ROWS - HEAD_DIM, LANES), BF16)
    for h in range(N_KV):
        for i in range(tm // LANES):
            cols = slice(i * LANES, (i + 1) * LANES)
            vst_ref[0, h, i, 0:HEAD_DIM, :] = zt[512 + h * 64:512 + (h + 1) * 64, cols].astype(BF16)
            vst_ref[0, h, i, HEAD_DIM:V_ROWS, :] = ones
            vwt_ref[0, h, i, 0:HEAD_DIM, :] = zt[640 + h * 64:640 + (h + 1) * 64, cols].astype(BF16)
            vwt_ref[0, h, i, HEAD_DIM:V_ROWS, :] = ones
    gt_ref[0] = jax.nn.sigmoid(zt[768:800, :])


def _proj_call(x, wn, wt, wc):
    bsz, s, d = x.shape
    tm = PROJ_TM
    nblk = s // LANES
    grid = (bsz, s // tm)
    out_shape = (
        jax.ShapeDtypeStruct((bsz, s, CONV_CH), BF16),
        jax.ShapeDtypeStruct((bsz, s, 256), F32),
        jax.ShapeDtypeStruct((bsz, s, 256), BF16),
        jax.ShapeDtypeStruct((bsz, 512, s), BF16),
        jax.ShapeDtypeStruct((bsz, N_KV, nblk, V_ROWS, LANES), BF16),
        jax.ShapeDtypeStruct((bsz, N_KV, nblk, V_ROWS, LANES), BF16),
        jax.ShapeDtypeStruct((bsz, N_KV * GATE_ROWS, s), F32),
    )
    vspec = pl.BlockSpec((1, N_KV, tm // LANES, V_ROWS, LANES), lambda b, i: (b, 0, i, 0, 0))
    return pl.pallas_call(
        _proj_kernel,
        grid=grid,
        in_specs=[
            pl.BlockSpec((1, tm, d), lambda b, i: (b, i, 0)),
            pl.BlockSpec(wn.shape, lambda b, i: (0, 0)),
            pl.BlockSpec(wt.shape, lambda b, i: (0, 0)),
            pl.BlockSpec(wc.shape, lambda b, i: (0, 0)),
        ],
        out_specs=(
            pl.BlockSpec((1, tm, CONV_CH), lambda b, i: (b, i, 0)),
            pl.BlockSpec((1, tm, 256), lambda b, i: (b, i, 0)),
            pl.BlockSpec((1, tm, 256), lambda b, i: (b, i, 0)),
            pl.BlockSpec((1, 512, tm), lambda b, i: (b, 0, i)),
            vspec, vspec,
            pl.BlockSpec((1, N_KV * GATE_ROWS, tm), lambda b, i: (b, 0, i)),
        ),
        out_shape=out_shape,
        scratch_shapes=[pltpu.VMEM((SUBLANES, CONV_CH), F32)],
        compiler_params=_cparams(("arbitrary", "arbitrary")),
        name="in_proj_conv",
    )(x, wn, wt, wc)


def _compress_kernel(g_ref, pe_ref, w1_ref, w2_ref, o_ref):
    g = g_ref[0, 0, 0]
    pe = pe_ref[0]
    half = g.shape[1]
    a_lo = jnp.dot((g + pe[0:1, :]).astype(BF16), w1_ref[0, 0:half, :], preferred_element_type=F32)
    a_hi = jnp.dot((g + pe[1:2, :]).astype(BF16), w1_ref[0, half:2 * half, :], preferred_element_type=F32)
    n = g.shape[0]
    hid = a_lo + pltpu.roll(a_hi, n - 1, 0)
    act = jax.nn.gelu(hid)
    o_ref[0, 0, 0] = jnp.dot(act.astype(BF16), w2_ref[0], preferred_element_type=F32)


def _compress_call(ckv, pes, w1s, w2s):
    bsz, _, _, n, width = ckv.shape
    return pl.pallas_call(
        _compress_kernel,
        grid=(bsz, 2, N_KV),
        in_specs=[
            pl.BlockSpec((1, 1, 1, n, width), lambda b, k, h: (b, k, h, 0, 0)),
            pl.BlockSpec((1, 2, width), lambda b, k, h: (k, 0, 0)),
            pl.BlockSpec((1, 2 * width, CMP_HIDDEN), lambda b, k, h: (k, 0, 0)),
            pl.BlockSpec((1, CMP_HIDDEN, HEAD_DIM), lambda b, k, h: (k, 0, 0)),
        ],
        out_specs=pl.BlockSpec((1, 1, 1, n, HEAD_DIM), lambda b, k, h: (b, k, h, 0, 0)),
        out_shape=jax.ShapeDtypeStruct((bsz, 2, N_KV, n, HEAD_DIM), F32),
        compiler_params=_cparams(("arbitrary", "arbitrary", "arbitrary")),
        name="compress_mlp",
    )(ckv, pes, w1s, w2s)


def _attn_kernel(n_sel, n_top,
                 q_ref, g_ref, kc_ref, vct_ref, cov_ref, ksa_ref, vst_ref, kw_ref, vwt_ref,
                 o_ref, qa_ref, sc_ref):
    qb = pl.program_id(2)
    q0 = qb * Q_TILE
    nq = GQA * Q_TILE
    blk = q_ref[0]
    q4 = jnp.concatenate([blk[g * HEAD_DIM:(g + 1) * HEAD_DIM, :] for g in range(GQA)], axis=1)
    lane = lax.broadcasted_iota(jnp.int32, (1, nq), 1)
    t4 = q0 + (lane & (Q_TILE - 1))

    n_cmp = kc_ref.shape[2]
    sc = jnp.dot(kc_ref[0, 0], q4, preferred_element_type=F32)
    ci = lax.broadcasted_iota(jnp.int32, (n_cmp, nq), 0)
    mask_c = ci * CMP_STRIDE + (CMP_BLOCK - 1) <= t4
    scm = jnp.where(mask_c, sc, NEG_INF)
    m_c = jnp.max(scm, axis=0, keepdims=True)
    e_c = jnp.where(mask_c, jnp.exp(scm - m_c), 0.0)
    l_c = jnp.sum(e_c, axis=0, keepdims=True)
    p_c = e_c * jnp.where(l_c > 0.0, 1.0 / l_c, 0.0)
    o_cmp = jnp.dot(vct_ref[0, 0], p_c.astype(BF16), preferred_element_type=F32)

    psum = (p_c[:, 0:Q_TILE] + p_c[:, Q_TILE:2 * Q_TILE]
            + p_c[:, 2 * Q_TILE:3 * Q_TILE] + p_c[:, 3 * Q_TILE:4 * Q_TILE])
    imp = jnp.dot(cov_ref[...], psum, precision=HIGHEST, preferred_element_type=F32)
    jidx = lax.broadcasted_iota(jnp.int32, (n_sel, Q_TILE), 0)
    tq = q0 + lax.broadcasted_iota(jnp.int32, (n_sel, Q_TILE), 1)
    jt = jnp.right_shift(tq, SEL_BLOCK.bit_length() - 1)
    forced = (jidx == 0) | (jidx == jt) | (jidx == jt - 1)
    score = jnp.where(jidx > jt, -1.0, jnp.where(forced, FORCE_SCORE, imp))
    sc_ref[...] = score

    def rank_body(jp, cnt):
        row = sc_ref[pl.ds(jp, 1), :]
        ahead = (row > score) | ((row == score) & (jp < jidx))
        return cnt + jnp.where(ahead, 1.0, 0.0)

    n_live = jnp.minimum((q0 + Q_TILE) // SEL_BLOCK, n_sel)
    rank = lax.fori_loop(0, n_live, rank_body, jnp.zeros((n_sel, Q_TILE), F32))
    chosen = (rank < float(n_top)) & (score >= 0.0)
    bias = jnp.where(chosen, 0.0, NEG_INF).astype(BF16)
    qa_ref[0:HEAD_DIM, :] = q4
    qa_ref[HEAD_DIM:HEAD_DIM + n_sel, :] = jnp.concatenate([bias] * GQA, axis=1)
    if qa_ref.shape[0] > HEAD_DIM + n_sel:
        qa_ref[HEAD_DIM + n_sel:, :] = jnp.zeros((qa_ref.shape[0] - HEAD_DIM - n_sel, nq), BF16)

    nsub = SEL_TK // LANES

    def sel_tile(c, carry, causal):
        m, acc = carry
        k0 = pl.multiple_of(c * SEL_TK, SEL_TK)
        s = jnp.dot(ksa_ref[0, 0, pl.ds(k0, SEL_TK), :], qa_ref[...], preferred_element_type=F32)
        if causal:
            kpos = k0 + lax.broadcasted_iota(jnp.int32, (SEL_TK, nq), 0)
            s = jnp.where(kpos <= t4, s, NEG_INF)
        m_new = jnp.maximum(m, jnp.max(s, axis=0, keepdims=True))
        p = jnp.exp(s - m_new)
        vt = jnp.concatenate([vst_ref[0, 0, c * nsub + i] for i in range(nsub)], axis=1)
        acc = jnp.exp(m - m_new) * acc + jnp.dot(vt, p.astype(BF16), preferred_element_type=F32)
        return m_new, acc

    n_full = qb // (SEL_TK // Q_TILE)
    init = (jnp.full((1, nq), NEG_INF, F32), jnp.zeros((V_ROWS, nq), F32))
    carry = lax.fori_loop(0, n_full, lambda c, cr: sel_tile(c, cr, False), init)
    _, acc_s = sel_tile(n_full, carry, True)
    o_sel = acc_s[0:HEAD_DIM, :] * (1.0 / acc_s[HEAD_DIM:HEAD_DIM + 1, :])

    kstart = pl.multiple_of(jnp.maximum(q0 - WINDOW, 0), Q_TILE)
    sw = jnp.dot(kw_ref[0, 0, pl.ds(kstart, WIN_KEYS), :], q4, preferred_element_type=F32)
    dist = t4 - (kstart + lax.broadcasted_iota(jnp.int32, (WIN_KEYS, nq), 0))
    mask_w = (dist >= 0) & (dist < WINDOW)
    swm = jnp.where(mask_w, sw, NEG_INF)
    m_w = jnp.max(swm, axis=0, keepdims=True)
    p_w = jnp.exp(swm - m_w)
    wblk = kstart // LANES
    vwt = jnp.concatenate([vwt_ref[0, 0, wblk + i] for i in range(WIN_KEYS // LANES)], axis=1)
    acc_w = jnp.dot(vwt, p_w.astype(BF16), preferred_element_type=F32)
    o_win = acc_w[0:HEAD_DIM, :] * (1.0 / acc_w[HEAD_DIM:HEAD_DIM + 1, :])

    gt = g_ref[0]

    def gate(br):
        return jnp.concatenate([gt[br * GQA + g:br * GQA + g + 1, :] for g in range(GQA)], axis=1)

    ot = gate(0) * o_cmp + gate(1) * o_sel + gate(2) * o_win
    stacked = jnp.concatenate([ot[:, g * Q_TILE:(g + 1) * Q_TILE] for g in range(GQA)], axis=0)
    o_ref[0] = stacked.T.astype(BF16)


def _attn_call(qt, gt, kcmp, vct, cover_t, ksa, vst, kw, vwt):
    bsz, _, s = qt.shape
    n_sel = s // SEL_BLOCK
    n_top = min(N_SEL, n_sel)
    n_cmp = kcmp.shape[2]
    ka = ksa.shape[3]
    nblk = s // LANES
    grid = (bsz, N_KV, s // Q_TILE)
    kern = functools.partial(_attn_kernel, n_sel, n_top)
    return pl.pallas_call(
        kern,
        grid=grid,
        in_specs=[
            pl.BlockSpec((1, GQA * HEAD_DIM, Q_TILE), lambda b, h, i: (b, h, i)),
            pl.BlockSpec((1, GATE_ROWS, Q_TILE), lambda b, h, i: (b, h, i)),
            pl.BlockSpec((1, 1, n_cmp, HEAD_DIM), lambda b, h, i: (b, h, 0, 0)),
            pl.BlockSpec((1, 1, HEAD_DIM, n_cmp), lambda b, h, i: (b, h, 0, 0)),
            pl.BlockSpec((n_sel, n_cmp), lambda b, h, i: (0, 0)),
            pl.BlockSpec((1, 1, s, ka), lambda b, h, i: (b, h, 0, 0)),
            pl.BlockSpec((1, 1, nblk, V_ROWS, LANES), lambda b, h, i: (b, h, 0, 0, 0)),
            pl.BlockSpec((1, 1, s, HEAD_DIM), lambda b, h, i: (b, h, 0, 0)),
            pl.BlockSpec((1, 1, nblk, V_ROWS, LANES), lambda b, h, i: (b, h, 0, 0, 0)),
        ],
        out_specs=pl.BlockSpec((1, Q_TILE, GQA * HEAD_DIM), lambda b, h, i: (b, i, h)),
        out_shape=jax.ShapeDtypeStruct((bsz, s, N_HEADS * HEAD_DIM), BF16),
        scratch_shapes=[pltpu.VMEM((ka, GQA * Q_TILE), BF16),
                        pltpu.VMEM((n_sel, Q_TILE), F32)],
        compiler_params=_cparams(("arbitrary", "arbitrary", "arbitrary")),
        name="nsa_attention",
    )(qt, gt, kcmp, vct, cover_t, ksa, vst, kw, vwt)


def _post_kernel(x_ref, yc_ref, yn_ref, wo_ref, g1_ref, b1_ref, wr_ref, br_ref, tril_ref,
                 h1_ref, ri_ref, rw_ref, cnt_ref):
    step = pl.program_id(0)
    half = yc_ref.shape[1]
    mix = (jnp.dot(yc_ref[...], wo_ref[0:half, :], preferred_element_type=F32)
           + jnp.dot(yn_ref[...], wo_ref[half:2 * half, :], preferred_element_type=F32))
    h1 = _layer_norm(ALPHA * x_ref[...] + mix, g1_ref[...], b1_ref[...])
    h1_ref[...] = h1

    logits = jnp.dot(h1, wr_ref[...], precision=HIGHEST, preferred_element_type=F32) + br_ref[...]
    tm = logits.shape[0]
    lane = lax.broadcasted_iota(jnp.int32, (tm, LANES), 1)
    big = jnp.int32(LANES)

    def masked_softmax(valid):
        lg = jnp.where(valid, logits, NEG_INF)
        mx = jnp.max(lg, axis=1, keepdims=True)
        ex = jnp.where(valid, jnp.exp(lg - mx), 0.0)
        return ex / jnp.sum(ex, axis=1, keepdims=True)

    def first_max(vals, valid):
        top = jnp.max(jnp.where(valid, vals, -1.0), axis=1, keepdims=True)
        idx = jnp.min(jnp.where(valid & (vals == top), lane, big), axis=1, keepdims=True)
        return top, idx

    is_group = lane < N_GROUPS
    gp, gsel = first_max(masked_softmax(is_group), is_group)
    lo = ROUTE_LANE0 + EXP_PER_GROUP * gsel
    in_group = (lane >= lo) & (lane < lo + EXP_PER_GROUP)
    eprob = masked_softmax(in_group)
    p1, i1 = first_max(eprob, in_group)
    p2, i2 = first_max(eprob, in_group & (lane != i1))
    den = p1 + p2
    w1 = gp * (p1 / den)
    w2 = gp * (p2 / den)

    @pl.when(step == 0)
    def _():
        cnt_ref[...] = jnp.zeros_like(cnt_ref)

    onehot = (lane == i1) | (lane == i2)
    before = jnp.dot(tril_ref[...], onehot.astype(BF16), preferred_element_type=F32)
    rk = before + cnt_ref[0:1, :]
    r1 = jnp.sum(jnp.where(lane == i1, rk, 0.0), axis=1, keepdims=True)
    r2 = jnp.sum(jnp.where(lane == i2, rk, 0.0), axis=1, keepdims=True)
    cnt_ref[...] = cnt_ref[...] + jnp.sum(onehot.astype(F32), axis=0, keepdims=True)

    col = lax.broadcasted_iota(jnp.int32, (tm, ri_ref.shape[1]), 1)
    ri_ref[...] = jnp.where(col == 0, i1 - ROUTE_LANE0,
                            jnp.where(col == 1, i2 - ROUTE_LANE0,
                                      jnp.where(col == 2, r1.astype(jnp.int32),
                                                jnp.where(col == 3, r2.astype(jnp.int32), 0))))
    rw_ref[...] = jnp.where(col == 0, w1, jnp.where(col == 1, w2, 0.0))


def _post_call(x2, yconv, ynsa, wo, g1, b1, wr, br, tril):
    n_tok, d = x2.shape
    tm = POST_TM
    const = lambda a: pl.BlockSpec(a.shape, lambda i: (0, 0))
    return pl.pallas_call(
        _post_kernel,
        grid=(n_tok // tm,),
        in_specs=[
            pl.BlockSpec((tm, d), lambda i: (i, 0)),
            pl.BlockSpec((tm, yconv.shape[1]), lambda i: (i, 0)),
            pl.BlockSpec((tm, ynsa.shape[1]), lambda i: (i, 0)),
            const(wo), const(g1), const(b1), const(wr), const(br), const(tril),
        ],
        out_specs=(
            pl.BlockSpec((tm, d), lambda i: (i, 0)),
            pl.BlockSpec((tm, SUBLANES), lambda i: (i, 0)),
            pl.BlockSpec((tm, SUBLANES), lambda i: (i, 0)),
            pl.BlockSpec((SUBLANES, LANES), lambda i: (0, 0)),
        ),
        out_shape=(
            jax.ShapeDtypeStruct((n_tok, d), F32),
            jax.ShapeDtypeStruct((n_tok, SUBLANES), jnp.int32),
            jax.ShapeDtypeStruct((n_tok, SUBLANES), F32),
            jax.ShapeDtypeStruct((SUBLANES, LANES), F32),
        ),
        compiler_params=_cparams(("arbitrary",)),
        name="out_proj_ln1_router",
    )(x2, yconv, ynsa, wo, g1, b1, wr, br, tril)


def _row_copy(src_ref, src_row, dst_ref, dst_row, sem):
    return pltpu.make_async_copy(src_ref.at[pl.ds(src_row, 1), :], dst_ref.at[pl.ds(dst_row, 1), :], sem)


def _dispatch_kernel(dest_ref, h_ref, init_ref, xs_ref, sem):
    del init_ref
    tm = h_ref.shape[0]

    def issue(r, _):
        for slot in range(2):
            _row_copy(h_ref, r, xs_ref, dest_ref[2 * r + slot], sem).start()
        return 0

    lax.fori_loop(0, tm, issue, 0)

    def drain(r, _):
        for slot in range(2):
            _row_copy(h_ref, r, xs_ref, dest_ref[2 * r + slot], sem).wait()
        return 0

    lax.fori_loop(0, tm, drain, 0)


def _dispatch_call(dest, h1, xs_init):
    n_tok, d = h1.shape
    tm = ROW_TM
    return pl.pallas_call(
        _dispatch_kernel,
        grid=(n_tok // tm,),
        in_specs=[
            pl.BlockSpec((2 * tm,), lambda i: (i,), memory_space=pltpu.SMEM),
            pl.BlockSpec((tm, d), lambda i: (i, 0)),
            pl.BlockSpec(memory_space=pl.ANY),
        ],
        out_specs=pl.BlockSpec(memory_space=pl.ANY),
        out_shape=jax.ShapeDtypeStruct(xs_init.shape, xs_init.dtype),
        scratch_shapes=[pltpu.SemaphoreType.DMA],
        input_output_aliases={2: 0},
        compiler_params=_cparams(("arbitrary",)),
        name="moe_dispatch",
    )(dest, h1, xs_init)


def _expert_kernel(ce_ref, nu_ref, xs_ref, wi_ref, wo_ref, o_ref):
    del ce_ref
    live = pl.program_id(0) < nu_ref[0]

    @pl.when(live)
    def _():
        xb = xs_ref[...].astype(BF16)
        gu = jnp.dot(xb, wi_ref[0], preferred_element_type=F32)
        gate = gu[:, 0:D_EXPERT]
        act = gate * jax.nn.sigmoid(gate) * gu[:, D_EXPERT:2 * D_EXPERT]
        o_ref[...] = jnp.dot(act.astype(BF16), wo_ref[0], preferred_element_type=F32)

    @pl.when(jnp.logical_not(live))
    def _():
        o_ref[...] = jnp.zeros_like(o_ref)


def _expert_call(chunk_exp, n_used, xs, wi, wo):
    n_rows, d = xs.shape
    n_chunk = n_rows // MOE_CHUNK

    def live(c, nu):
        return jnp.minimum(c, nu[0] - 1)

    grid_spec = pltpu.PrefetchScalarGridSpec(
        num_scalar_prefetch=2,
        grid=(n_chunk,),
        in_specs=[
            pl.BlockSpec((MOE_CHUNK, d), lambda c, ce, nu: (live(c, nu), 0)),
            pl.BlockSpec((1,) + wi.shape[1:], lambda c, ce, nu: (ce[live(c, nu)], 0, 0)),
            pl.BlockSpec((1,) + wo.shape[1:], lambda c, ce, nu: (ce[live(c, nu)], 0, 0)),
        ],
        out_specs=pl.BlockSpec((MOE_CHUNK, d), lambda c, ce, nu: (c, 0)),
    )
    return pl.pallas_call(
        _expert_kernel,
        grid_spec=grid_spec,
        out_shape=jax.ShapeDtypeStruct((n_rows, d), F32),
        compiler_params=_cparams(("arbitrary",)),
        name="moe_experts",
    )(chunk_exp, n_used, xs, wi, wo)


def _combine_kernel(dest_ref, h1_ref, rw_ref, p_ref, ys_ref,
                    g2_ref, b2_ref, wp_ref, wg_ref, bg_ref, g3_ref, b3_ref,
                    o_ref, rows_ref, sem):
    tm = h1_ref.shape[0]

    def issue(r, _):
        for slot in range(2):
            _row_copy(ys_ref, dest_ref[2 * r + slot], rows_ref.at[slot], r, sem).start()
        return 0

    lax.fori_loop(0, tm, issue, 0)

    def drain(r, _):
        for slot in range(2):
            _row_copy(ys_ref, dest_ref[2 * r + slot], rows_ref.at[slot], r, sem).wait()
        return 0

    lax.fori_loop(0, tm, drain, 0)

    rw = rw_ref[...]
    ffn = rw[:, 0:1] * rows_ref[0] + rw[:, 1:2] * rows_ref[1]
    h2 = _layer_norm(ALPHA * h1_ref[...] + ffn, g2_ref[...], b2_ref[...])
    emb = jnp.dot(p_ref[...].astype(BF16), wp_ref[...], preferred_element_type=F32)
    gate = jax.nn.sigmoid(jnp.dot(h2.astype(BF16), wg_ref[...], preferred_element_type=F32) + bg_ref[...])
    o_ref[...] = _layer_norm(ALPHA * h2 + emb * gate, g3_ref[...], b3_ref[...])


def _combine_call(dest, h1, rw, p2, ys, g2, b2, wp, wg, bg, g3, b3):
    n_tok, d = h1.shape
    tm = ROW_TM
    const = lambda a: pl.BlockSpec(a.shape, lambda i: (0, 0))
    return pl.pallas_call(
        _combine_kernel,
        grid=(n_tok // tm,),
        in_specs=[
            pl.BlockSpec((2 * tm,), lambda i: (i,), memory_space=pltpu.SMEM),
            pl.BlockSpec((tm, d), lambda i: (i, 0)),
            pl.BlockSpec((tm, rw.shape[1]), lambda i: (i, 0)),
            pl.BlockSpec((tm, p2.shape[1]), lambda i: (i, 0)),
            pl.BlockSpec(memory_space=pl.ANY),
            const(g2), const(b2), const(wp), const(wg), const(bg), const(g3), const(b3),
        ],
        out_specs=pl.BlockSpec((tm, d), lambda i: (i, 0)),
        out_shape=jax.ShapeDtypeStruct((n_tok, d), F32),
        scratch_shapes=[pltpu.VMEM((2, tm, d), F32), pltpu.SemaphoreType.DMA],
        compiler_params=_cparams(("arbitrary",)),
        name="moe_combine_ln_ple",
    )(dest, h1, rw, p2, ys, g2, b2, wp, wg, bg, g3, b3)


def _gate_columns():
    cols = np.zeros((N_KV, GATE_ROWS), np.int32)
    live = np.zeros((N_KV, GATE_ROWS), np.float32)
    for h in range(N_KV):
        for br in range(3):
            for g in range(GQA):
                cols[h, br * GQA + g] = (h * GQA + g) * 3 + br
                live[h, br * GQA + g] = 1.0
    return cols.reshape(-1), live.reshape(-1)


def _layer(x, p, w_in, w_conv, pe_ck, w1_ck, w2_ck, pe_cv, w1_cv, w2_cv, w_out, ln1_g, ln1_b,
           w_rg, b_rg, w_re, b_re, w_e_in, w_e_out, ln2_g, ln2_b, w_ple, w_ple_gate, b_ple_gate,
           ln3_g, ln3_b):
    bsz, s, d = x.shape
    n_tok = bsz * s
    row = lambda v: v.reshape(1, -1)

    c_q, c_kc, c_vc, c_ks, c_vs, c_kw, c_vw, c_g = 1536, 2048, 2176, 2304, 2432, 2560, 2688, 2816
    wn = jnp.concatenate([w_in[:, 0:c_q], w_in[:, c_kc:c_ks], w_in[:, c_ks:c_vs], w_in[:, c_kw:c_vw]],
                         axis=1).astype(BF16)
    gcols, glive = _gate_columns()
    w_gate = w_in[:, c_g:c_g + 3 * N_HEADS][:, gcols] * glive[None, :]
    wt = jnp.concatenate([w_in[:, c_q:c_kc] * ATTN_SCALE, w_in[:, c_vs:c_kw], w_in[:, c_vw:c_g], w_gate],
                         axis=1).T.astype(BF16)

    yconv, cmpkv, kvn, qt, vst, vwt, gt = _proj_call(x, wn, wt, w_conv)

    def per_head(a):
        return a.reshape(bsz, s, N_KV, HEAD_DIM).transpose(0, 2, 1, 3)

    grp = CMP_STRIDE * HEAD_DIM
    ckv = jnp.stack([per_head(cmpkv[..., 0:128]), per_head(cmpkv[..., 128:256])], axis=1)
    ckv = ckv.reshape(bsz, 2, N_KV, s // CMP_STRIDE, grp)
    pes = jnp.stack([pe_ck.reshape(2, grp), pe_cv.reshape(2, grp)])
    w1s = jnp.stack([w1_ck, w1_cv]).astype(BF16)
    w2s = jnp.stack([w2_ck, w2_cv]).astype(BF16)
    cmp_out = _compress_call(ckv, pes, w1s, w2s)
    kcmp = cmp_out[:, 0].astype(BF16)
    vct = cmp_out[:, 1].transpose(0, 1, 3, 2).astype(BF16)

    n_sel = s // SEL_BLOCK
    n_cmp = s // CMP_STRIDE
    ka = -(-(HEAD_DIM + n_sel) // LANES) * LANES
    blk_onehot = np.zeros((s, ka - HEAD_DIM), np.float32)
    blk_onehot[np.arange(s), np.arange(s) // SEL_BLOCK] = 1.0
    ksa = jnp.concatenate([per_head(kvn[..., 0:128]),
                           jnp.broadcast_to(jnp.asarray(blk_onehot, BF16), (bsz, N_KV, s, ka - HEAD_DIM))],
                          axis=-1)
    kw = per_head(kvn[..., 128:256])
    cs = np.arange(n_cmp)[None, :] * CMP_STRIDE
    ss = np.arange(n_sel)[:, None] * SEL_BLOCK
    cover_t = jnp.asarray(((cs < ss + SEL_BLOCK) & (cs + CMP_BLOCK > ss)).astype(np.float32))
    ynsa = _attn_call(qt, gt, kcmp, vct, cover_t, ksa, vst, kw, vwt)

    wr = jnp.zeros((d, LANES), F32).at[:, 0:N_GROUPS].set(w_rg)
    wr = wr.at[:, ROUTE_LANE0:ROUTE_LANE0 + N_EXPERTS].set(w_re)
    br = jnp.zeros((1, LANES), F32).at[0, 0:N_GROUPS].set(b_rg)
    br = br.at[0, ROUTE_LANE0:ROUTE_LANE0 + N_EXPERTS].set(b_re)
    tril = jnp.asarray(np.tril(np.ones((POST_TM, POST_TM), np.float32), -1), BF16)
    h1, ri, rw, cnt = _post_call(x.reshape(n_tok, d), yconv.reshape(n_tok, -1), ynsa.reshape(n_tok, -1),
                                 w_out.astype(BF16), row(ln1_g), row(ln1_b), wr, br, tril)

    counts = cnt[0, ROUTE_LANE0:ROUTE_LANE0 + N_EXPERTS].astype(jnp.int32)
    pcounts = (counts + MOE_CHUNK - 1) // MOE_CHUNK * MOE_CHUNK
    pends = jnp.cumsum(pcounts)
    pstarts = pends - pcounts
    dest = (pstarts[ri[:, 0:2]] + ri[:, 2:4]).reshape(-1).astype(jnp.int32)
    n_asg = n_tok * 2
    n_chunk = -(-n_asg // MOE_CHUNK) + N_EXPERTS
    chunk_exp = jnp.clip(jnp.searchsorted(pends, jnp.arange(n_chunk) * MOE_CHUNK, side='right'),
                         0, N_EXPERTS - 1).astype(jnp.int32)
    n_used = (pends[-1:] // MOE_CHUNK).astype(jnp.int32)

    xs = _dispatch_call(dest, h1, jnp.zeros((n_chunk * MOE_CHUNK, d), F32))
    ys = _expert_call(chunk_exp, n_used, xs, w_e_in.astype(BF16), w_e_out.astype(BF16))
    out = _combine_call(dest, h1, rw, p.reshape(n_tok, -1), ys, row(ln2_g), row(ln2_b),
                        w_ple.astype(BF16), w_ple_gate.astype(BF16), row(b_ple_gate), row(ln3_g), row(ln3_b))
    return out.reshape(bsz, s, d)


def kernel(x, p, w_in, w_conv, pe_ck, w1_ck, w2_ck, pe_cv, w1_cv, w2_cv, w_out, ln1_g, ln1_b, w_rg, b_rg, w_re, b_re, w_e_in, w_e_out, ln2_g, ln2_b, w_ple, w_ple_gate, b_ple_gate, ln3_g, ln3_b):
    assert w_in.shape[0] == DEPTH, "residual scaling ALPHA is derived from DEPTH"
    h = x
    for i in range(DEPTH):
        h = _layer(h, p[i], w_in[i], w_conv[i], pe_ck[i], w1_ck[i], w2_ck[i], pe_cv[i], w1_cv[i], w2_cv[i],
                   w_out[i], ln1_g[i], ln1_b[i], w_rg[i], b_rg[i], w_re[i], b_re[i], w_e_in[i], w_e_out[i],
                   ln2_g[i], ln2_b[i], w_ple[i], w_ple_gate[i], b_ple_gate[i], ln3_g[i], ln3_b[i])
    return h
```

```python
import functools

import jax
import jax.numpy as jnp
import numpy as np
from jax import lax
from jax.experimental import pallas as pl
from jax.experimental.pallas import tpu as pltpu

F32 = jnp.float32
BF16 = jnp.bfloat16
HIGHEST = lax.Precision.HIGHEST

CONV_CH = 512
CONV_W = 3
N_HEADS = 8
HEAD_DIM = 64
N_KV = 2
GQA = N_HEADS // N_KV
CMP_BLOCK = 32
CMP_STRIDE = 16
CMP_HIDDEN = 2 * HEAD_DIM
SEL_BLOCK = 64
N_SEL = 16
WINDOW = 512
ATTN_SCALE = HEAD_DIM ** -0.5
Q_SCALE = ATTN_SCALE * float(np.log2(np.e))
FORCE_SCORE = 1e4
NEG_INF = -1e30
N_GROUPS = 4
EXP_PER_GROUP = 8
N_EXPERTS = N_GROUPS * EXP_PER_GROUP
D_EXPERT = 512
MOE_CHUNK = 256
DEPTH = 1
ALPHA = (2 * DEPTH) ** 0.25
LN_EPS = 1e-5

LANES = 128
SUBLANES = 8
VMEM_LIMIT = 56 * 1024 * 1024

PROJ_TM = 512
Q_TILE = 128
SEL_TK = 512
WIN_KEYS = WINDOW + Q_TILE
POST_TM = 512
ROW_TM = 256
ROUTE_LANE0 = N_GROUPS
GATE_ROWS = 16
V_ROWS = HEAD_DIM + 16


def _cparams(sem, vmem=VMEM_LIMIT):
    return pltpu.CompilerParams(dimension_semantics=sem, vmem_limit_bytes=vmem)


def _layer_norm(v, g, b):
    mu = jnp.mean(v, axis=-1, keepdims=True)
    d = v - mu
    var = jnp.mean(d * d, axis=-1, keepdims=True)
    return d * lax.rsqrt(var + LN_EPS) * g + b


def _proj_kernel(x_ref, wn_ref, wt_ref, wc_ref,
                 yconv_ref, cmpkv_ref, kvn_ref, qt_ref, vst_ref, vwt_ref, gt_ref,
                 carry_ref):
    si = pl.program_id(1)
    tm = x_ref.shape[1]
    xb = x_ref[0].astype(BF16)
    zn = jnp.dot(xb, wn_ref[...], preferred_element_type=F32)
    zt = lax.dot_general(wt_ref[...], xb, (((1,), (1,)), ((), ())),
                         preferred_element_type=F32)

    cb = zn[:, 0:CONV_CH]
    u = zn[:, CONV_CH:2 * CONV_CH] * zn[:, 2 * CONV_CH:3 * CONV_CH]

    @pl.when(si == 0)
    def _():
        carry_ref[...] = jnp.zeros_like(carry_ref)

    prev = carry_ref[...]
    rows = lax.broadcasted_iota(jnp.int32, u.shape, 0)
    u1 = jnp.where(rows == 0, prev[7:8, :], pltpu.roll(u, 1, 0))
    u2 = jnp.where(rows == 0, prev[6:7, :],
                   jnp.where(rows == 1, prev[7:8, :], pltpu.roll(u, 2, 0)))
    w = wc_ref[...]
    yconv_ref[0] = (cb * (w[0:1, :] * u2 + w[1:2, :] * u1 + w[2:3, :] * u)).astype(BF16)
    carry_ref[...] = u[tm - SUBLANES:tm, :]

    cmpkv_ref[0] = zn[:, 1536:1792]
    kvn_ref[0] = zn[:, 1792:2048].astype(BF16)

    qt_ref[0] = zt[0:512, :].astype(BF16)
    ones = jnp.ones((V_ROWS - HEAD_DIM, LANES), BF16)
    for h in range(N_KV):
        for i in range(tm // LANES):
            cols = slice(i * LANES, (i + 1) * LANES)
            vst_ref[0, h, i, 0:HEAD_DIM, :] = zt[512 + h * 64:512 + (h + 1) * 64, cols].astype(BF16)
            vst_ref[0, h, i, HEAD_DIM:V_ROWS, :] = ones
            vwt_ref[0, h, i, 0:HEAD_DIM, :] = zt[640 + h * 64:640 + (h + 1) * 64, cols].astype(BF16)
            vwt_ref[0, h, i, HEAD_DIM:V_ROWS, :] = ones
    gt_ref[0] = jax.nn.sigmoid(zt[768:800, :])


def _proj_call(x, wn, wt, wc):
    bsz, s, d = x.shape
    tm = PROJ_TM
    nblk = s // LANES
    grid = (bsz, s // tm)
    out_shape = (
        jax.ShapeDtypeStruct((bsz, s, CONV_CH), BF16),
        jax.ShapeDtypeStruct((bsz, s, 256), F32),
        jax.ShapeDtypeStruct((bsz, s, 256), BF16),
        jax.ShapeDtypeStruct((bsz, 512, s), BF16),
        jax.ShapeDtypeStruct((bsz, N_KV, nblk, V_ROWS, LANES), BF16),
        jax.ShapeDtypeStruct((bsz, N_KV, nblk, V_ROWS, LANES), BF16),
        jax.ShapeDtypeStruct((bsz, N_KV * GATE_ROWS, s), F32),
    )
    vspec = pl.BlockSpec((1, N_KV, tm // LANES, V_ROWS, LANES), lambda b, i: (b, 0, i, 0, 0))
    return pl.pallas_call(
        _proj_kernel,
        grid=grid,
        in_specs=[
            pl.BlockSpec((1, tm, d), lambda b, i: (b, i, 0)),
            pl.BlockSpec(wn.shape, lambda b, i: (0, 0)),
            pl.BlockSpec(wt.shape, lambda b, i: (0, 0)),
            pl.BlockSpec(wc.shape, lambda b, i: (0, 0)),
        ],
        out_specs=(
            pl.BlockSpec((1, tm, CONV_CH), lambda b, i: (b, i, 0)),
            pl.BlockSpec((1, tm, 256), lambda b, i: (b, i, 0)),
            pl.BlockSpec((1, tm, 256), lambda b, i: (b, i, 0)),
            pl.BlockSpec((1, 512, tm), lambda b, i: (b, 0, i)),
            vspec, vspec,
            pl.BlockSpec((1, N_KV * GATE_ROWS, tm), lambda b, i: (b, 0, i)),
        ),
        out_shape=out_shape,
        scratch_shapes=[pltpu.VMEM((SUBLANES, CONV_CH), F32)],
        compiler_params=_cparams(("arbitrary", "arbitrary")),
        name="in_proj_conv",
    )(x, wn, wt, wc)


def _compress_kernel(g_ref, pe_ref, w1_ref, w2_ref, o_ref):
    g = g_ref[0, 0, 0]
    pe = pe_ref[0]
    half = g.shape[1]
    a_lo = jnp.dot((g + pe[0:1, :]).astype(BF16), w1_ref[0, 0:half, :], preferred_element_type=F32)
    a_hi = jnp.dot((g + pe[1:2, :]).astype(BF16), w1_ref[0, half:2 * half, :], preferred_element_type=F32)
    n = g.shape[0]
    hid = a_lo + pltpu.roll(a_hi, n - 1, 0)
    act = jax.nn.gelu(hid)
    o_ref[0, 0, 0] = jnp.dot(act.astype(BF16), w2_ref[0], preferred_element_type=F32)


def _compress_call(ckv, pes, w1s, w2s):
    bsz, _, _, n, width = ckv.shape
    return pl.pallas_call(
        _compress_kernel,
        grid=(bsz, 2, N_KV),
        in_specs=[
            pl.BlockSpec((1, 1, 1, n, width), lambda b, k, h: (b, k, h, 0, 0)),
            pl.BlockSpec((1, 2, width), lambda b, k, h: (k, 0, 0)),
            pl.BlockSpec((1, 2 * width, CMP_HIDDEN), lambda b, k, h: (k, 0, 0)),
            pl.BlockSpec((1, CMP_HIDDEN, HEAD_DIM), lambda b, k, h: (k, 0, 0)),
        ],
        out_specs=pl.BlockSpec((1, 1, 1, n, HEAD_DIM), lambda b, k, h: (b, k, h, 0, 0)),
        out_shape=jax.ShapeDtypeStruct((bsz, 2, N_KV, n, HEAD_DIM), F32),
        compiler_params=_cparams(("arbitrary", "arbitrary", "arbitrary")),
        name="compress_mlp",
    )(ckv, pes, w1s, w2s)


def _attn_kernel(n_sel, n_top,
                 q_ref, g_ref, kc_ref, vct_ref, ksa_ref, vst_ref, kw_ref, vwt_ref,
                 o_ref, qa_ref, ps_ref):
    qb = pl.program_id(2)
    q0 = qb * Q_TILE
    nq = GQA * Q_TILE
    blk = q_ref[0]
    q4 = jnp.concatenate([blk[g * HEAD_DIM:(g + 1) * HEAD_DIM, :] for g in range(GQA)], axis=1)
    lane = lax.broadcasted_iota(jnp.int32, (1, nq), 1)
    t4 = q0 + (lane & (Q_TILE - 1))

    n_cmp = kc_ref.shape[2]
    sc = jnp.dot(kc_ref[0, 0], q4, preferred_element_type=F32)
    ci = lax.broadcasted_iota(jnp.int32, (n_cmp, nq), 0)
    mask_c = ci * CMP_STRIDE + (CMP_BLOCK - 1) <= t4
    scm = jnp.where(mask_c, sc, NEG_INF)
    m_c = jnp.max(scm, axis=0, keepdims=True)
    e_c = jnp.where(mask_c, jnp.exp2(scm - m_c), 0.0)
    l_c = jnp.sum(e_c, axis=0, keepdims=True)
    p_c = e_c * jnp.where(l_c > 0.0, 1.0 / l_c, 0.0)
    o_cmp = jnp.dot(vct_ref[0, 0], p_c.astype(BF16), preferred_element_type=F32)

    kstart = pl.multiple_of(jnp.maximum(q0 - WINDOW, 0), Q_TILE)
    sw = jnp.dot(kw_ref[0, 0, pl.ds(kstart, WIN_KEYS), :], q4, preferred_element_type=F32)
    dist = t4 - (kstart + lax.broadcasted_iota(jnp.int32, (WIN_KEYS, nq), 0))
    mask_w = (dist >= 0) & (dist < WINDOW)
    swm = jnp.where(mask_w, sw, NEG_INF)
    m_w = jnp.max(swm, axis=0, keepdims=True)
    p_w = jnp.exp2(swm - m_w)
    wblk = kstart // LANES
    vwt = jnp.concatenate([vwt_ref[0, 0, wblk + i] for i in range(WIN_KEYS // LANES)], axis=1)
    acc_w = jnp.dot(vwt, p_w.astype(BF16), preferred_element_type=F32)
    o_win = acc_w[0:HEAD_DIM, :] * (1.0 / acc_w[HEAD_DIM:HEAD_DIM + 1, :])

    per_sel = SEL_BLOCK // CMP_STRIDE
    ps_ref[0:SUBLANES, :] = jnp.zeros((SUBLANES, Q_TILE), F32)
    ps_ref[SUBLANES:, :] = (p_c[:, 0:Q_TILE] + p_c[:, Q_TILE:2 * Q_TILE]
                            + p_c[:, 2 * Q_TILE:3 * Q_TILE] + p_c[:, 3 * Q_TILE:4 * Q_TILE])
    imp = ps_ref[pl.ds(SUBLANES - 1, n_sel, stride=per_sel), :]
    for off in range(CMP_BLOCK // CMP_STRIDE + per_sel - 2):
        imp = imp + ps_ref[pl.ds(SUBLANES + off, n_sel, stride=per_sel), :]

    jidx = lax.broadcasted_iota(jnp.int32, (n_sel, Q_TILE), 0)
    tq = q0 + lax.broadcasted_iota(jnp.int32, (n_sel, Q_TILE), 1)
    jt = jnp.right_shift(tq, SEL_BLOCK.bit_length() - 1)
    forced = (jidx == 0) | (jidx == jt) | (jidx == jt - 1)
    score = jnp.where(forced, FORCE_SCORE, imp)
    key = jnp.where(jidx > jt, -1, lax.bitcast_convert_type(score, jnp.int32))
    theta = jnp.zeros((1, Q_TILE), jnp.int32)
    for bit in range(30, -1, -1):
        cand = theta | (1 << bit)
        reach = jnp.sum((key >= cand).astype(jnp.int32), axis=0, keepdims=True)
        theta = jnp.where(reach >= n_top, cand, theta)
    above = key > theta
    tied = key == theta
    n_above = jnp.sum(above.astype(jnp.int32), axis=0, keepdims=True)
    lower = (lax.broadcasted_iota(jnp.int32, (n_sel, n_sel), 1)
             < lax.broadcasted_iota(jnp.int32, (n_sel, n_sel), 0)).astype(BF16)
    tied_before = jnp.dot(lower, tied.astype(BF16), preferred_element_type=F32)
    chosen = above | (tied & (tied_before < (n_top - n_above).astype(F32)))
    bias = jnp.where(chosen, 0.0, NEG_INF).astype(BF16)
    qa_ref[0:HEAD_DIM, :] = q4
    qa_ref[HEAD_DIM:HEAD_DIM + n_sel, :] = jnp.concatenate([bias] * GQA, axis=1)
    if qa_ref.shape[0] > HEAD_DIM + n_sel:
        qa_ref[HEAD_DIM + n_sel:, :] = jnp.zeros((qa_ref.shape[0] - HEAD_DIM - n_sel, nq), BF16)

    nsub = SEL_TK // LANES

    def scores(c):
        k0 = pl.multiple_of(c * SEL_TK, SEL_TK)
        return jnp.dot(ksa_ref[0, 0, pl.ds(k0, SEL_TK), :], qa_ref[...], preferred_element_type=F32)

    def accumulate(c, s, m, acc):
        m_new = jnp.maximum(m, jnp.max(s, axis=0, keepdims=True))
        p = jnp.exp2(s - m_new)
        vt = jnp.concatenate([vst_ref[0, 0, c * nsub + i] for i in range(nsub)], axis=1)
        acc = jnp.exp2(m - m_new) * acc + jnp.dot(vt, p.astype(BF16), preferred_element_type=F32)
        return m_new, acc

    def sel_body(c, carry):
        s, m, acc = carry
        s_next = scores(c + 1)
        m, acc = accumulate(c, s, m, acc)
        return s_next, m, acc

    n_full = qb // (SEL_TK // Q_TILE)
    init = (scores(0), jnp.full((1, nq), NEG_INF, F32), jnp.zeros((V_ROWS, nq), F32))
    s_last, m_s, acc_s = lax.fori_loop(0, n_full, sel_body, init)
    kpos = n_full * SEL_TK + lax.broadcasted_iota(jnp.int32, (SEL_TK, nq), 0)
    _, acc_s = accumulate(n_full, jnp.where(kpos <= t4, s_last, NEG_INF), m_s, acc_s)
    o_sel = acc_s[0:HEAD_DIM, :] * (1.0 / acc_s[HEAD_DIM:HEAD_DIM + 1, :])

    gt = g_ref[0]

    def gate(br):
        return jnp.concatenate([gt[br * GQA + g:br * GQA + g + 1, :] for g in range(GQA)], axis=1)

    ot = gate(0) * o_cmp + gate(1) * o_sel + gate(2) * o_win
    stacked = jnp.concatenate([ot[:, g * Q_TILE:(g + 1) * Q_TILE] for g in range(GQA)], axis=0)
    o_ref[0] = stacked.T.astype(BF16)


def _attn_call(qt, gt, kcmp, vct, ksa, vst, kw, vwt):
    bsz, _, s = qt.shape
    n_sel = s // SEL_BLOCK
    n_top = min(N_SEL, n_sel)
    n_cmp = kcmp.shape[2]
    ka = ksa.shape[3]
    nblk = s // LANES
    grid = (bsz, N_KV, s // Q_TILE)
    kern = functools.partial(_attn_kernel, n_sel, n_top)
    return pl.pallas_call(
        kern,
        grid=grid,
        in_specs=[
            pl.BlockSpec((1, GQA * HEAD_DIM, Q_TILE), lambda b, h, i: (b, h, i)),
            pl.BlockSpec((1, GATE_ROWS, Q_TILE), lambda b, h, i: (b, h, i)),
            pl.BlockSpec((1, 1, n_cmp, HEAD_DIM), lambda b, h, i: (b, h, 0, 0)),
            pl.BlockSpec((1, 1, HEAD_DIM, n_cmp), lambda b, h, i: (b, h, 0, 0)),
            pl.BlockSpec((1, 1, s, ka), lambda b, h, i: (b, h, 0, 0)),
            pl.BlockSpec((1, 1, nblk, V_ROWS, LANES), lambda b, h, i: (b, h, 0, 0, 0)),
            pl.BlockSpec((1, 1, s, HEAD_DIM), lambda b, h, i: (b, h, 0, 0)),
            pl.BlockSpec((1, 1, nblk, V_ROWS, LANES), lambda b, h, i: (b, h, 0, 0, 0)),
        ],
        out_specs=pl.BlockSpec((1, Q_TILE, GQA * HEAD_DIM), lambda b, h, i: (b, i, h)),
        out_shape=jax.ShapeDtypeStruct((bsz, s, N_HEADS * HEAD_DIM), BF16),
        scratch_shapes=[pltpu.VMEM((ka, GQA * Q_TILE), BF16),
                        pltpu.VMEM((n_cmp + SUBLANES, Q_TILE), F32)],
        compiler_params=_cparams(("arbitrary", "arbitrary", "arbitrary")),
        name="nsa_attention",
    )(qt, gt, kcmp, vct, ksa, vst, kw, vwt)


def _post_kernel(x_ref, yc_ref, yn_ref, wo_ref, g1_ref, b1_ref, wr_ref, br_ref, tril_ref,
                 h1_ref, ri_ref, rw_ref, cnt_ref):
    step = pl.program_id(0)
    half = yc_ref.shape[1]
    mix = (jnp.dot(yc_ref[...], wo_ref[0:half, :], preferred_element_type=F32)
           + jnp.dot(yn_ref[...], wo_ref[half:2 * half, :], preferred_element_type=F32))
    h1 = _layer_norm(ALPHA * x_ref[...] + mix, g1_ref[...], b1_ref[...])
    h1_ref[...] = h1

    logits = jnp.dot(h1, wr_ref[...], precision=HIGHEST, preferred_element_type=F32) + br_ref[...]
    tm = logits.shape[0]
    lane = lax.broadcasted_iota(jnp.int32, (tm, LANES), 1)
    big = jnp.int32(LANES)

    def masked_softmax(valid):
        lg = jnp.where(valid, logits, NEG_INF)
        mx = jnp.max(lg, axis=1, keepdims=True)
        ex = jnp.where(valid, jnp.exp(lg - mx), 0.0)
        return ex / jnp.sum(ex, axis=1, keepdims=True)

    def first_max(vals, valid):
        top = jnp.max(jnp.where(valid, vals, -1.0), axis=1, keepdims=True)
        idx = jnp.min(jnp.where(valid & (vals == top), lane, big), axis=1, keepdims=True)
        return top, idx

    is_group = lane < N_GROUPS
    gp, gsel = first_max(masked_softmax(is_group), is_group)
    lo = ROUTE_LANE0 + EXP_PER_GROUP * gsel
    in_group = (lane >= lo) & (lane < lo + EXP_PER_GROUP)
    eprob = masked_softmax(in_group)
    p1, i1 = first_max(eprob, in_group)
    p2, i2 = first_max(eprob, in_group & (lane != i1))
    den = p1 + p2
    w1 = gp * (p1 / den)
    w2 = gp * (p2 / den)

    @pl.when(step == 0)
    def _():
        cnt_ref[...] = jnp.zeros_like(cnt_ref)

    onehot = (lane == i1) | (lane == i2)
    before = jnp.dot(tril_ref[...], onehot.astype(BF16), preferred_element_type=F32)
    rk = before + cnt_ref[0:1, :]
    r1 = jnp.sum(jnp.where(lane == i1, rk, 0.0), axis=1, keepdims=True)
    r2 = jnp.sum(jnp.where(lane == i2, rk, 0.0), axis=1, keepdims=True)
    cnt_ref[...] = cnt_ref[...] + jnp.sum(onehot.astype(F32), axis=0, keepdims=True)

    col = lax.broadcasted_iota(jnp.int32, (tm, ri_ref.shape[1]), 1)
    ri_ref[...] = jnp.where(col == 0, i1 - ROUTE_LANE0,
                            jnp.where(col == 1, i2 - ROUTE_LANE0,
                                      jnp.where(col == 2, r1.astype(jnp.int32),
                                                jnp.where(col == 3, r2.astype(jnp.int32), 0))))
    rw_ref[...] = jnp.where(col == 0, w1, jnp.where(col == 1, w2, 0.0))


def _post_call(x2, yconv, ynsa, wo, g1, b1, wr, br, tril):
    n_tok, d = x2.shape
    tm = POST_TM
    const = lambda a: pl.BlockSpec(a.shape, lambda i: (0, 0))
    return pl.pallas_call(
        _post_kernel,
        grid=(n_tok // tm,),
        in_specs=[
            pl.BlockSpec((tm, d), lambda i: (i, 0)),
            pl.BlockSpec((tm, yconv.shape[1]), lambda i: (i, 0)),
            pl.BlockSpec((tm, ynsa.shape[1]), lambda i: (i, 0)),
            const(wo), const(g1), const(b1), const(wr), const(br), const(tril),
        ],
        out_specs=(
            pl.BlockSpec((tm, d), lambda i: (i, 0)),
            pl.BlockSpec((tm, SUBLANES), lambda i: (i, 0)),
            pl.BlockSpec((tm, SUBLANES), lambda i: (i, 0)),
            pl.BlockSpec((SUBLANES, LANES), lambda i: (0, 0)),
        ),
        out_shape=(
            jax.ShapeDtypeStruct((n_tok, d), F32),
            jax.ShapeDtypeStruct((n_tok, SUBLANES), jnp.int32),
            jax.ShapeDtypeStruct((n_tok, SUBLANES), F32),
            jax.ShapeDtypeStruct((SUBLANES, LANES), F32),
        ),
        compiler_params=_cparams(("arbitrary",)),
        name="out_proj_ln1_router",
    )(x2, yconv, ynsa, wo, g1, b1, wr, br, tril)


def _row_copy(src_ref, src_row, dst_ref, dst_row, sem):
    return pltpu.make_async_copy(src_ref.at[pl.ds(src_row, 1), :], dst_ref.at[pl.ds(dst_row, 1), :], sem)


def _dispatch_kernel(dest_ref, h_ref, init_ref, xs_ref, sem):
    del init_ref
    tm = h_ref.shape[0]

    def issue(r, _):
        for slot in range(2):
            _row_copy(h_ref, r, xs_ref, dest_ref[2 * r + slot], sem).start()
        return 0

    lax.fori_loop(0, tm, issue, 0)

    def drain(r, _):
        for slot in range(2):
            _row_copy(h_ref, r, xs_ref, dest_ref[2 * r + slot], sem).wait()
        return 0

    lax.fori_loop(0, tm, drain, 0)


def _dispatch_call(dest, h1, xs_init):
    n_tok, d = h1.shape
    tm = ROW_TM
    return pl.pallas_call(
        _dispatch_kernel,
        grid=(n_tok // tm,),
        in_specs=[
            pl.BlockSpec((2 * tm,), lambda i: (i,), memory_space=pltpu.SMEM),
            pl.BlockSpec((tm, d), lambda i: (i, 0)),
            pl.BlockSpec(memory_space=pl.ANY),
        ],
        out_specs=pl.BlockSpec(memory_space=pl.ANY),
        out_shape=jax.ShapeDtypeStruct(xs_init.shape, xs_init.dtype),
        scratch_shapes=[pltpu.SemaphoreType.DMA],
        input_output_aliases={2: 0},
        compiler_params=_cparams(("arbitrary",)),
        name="moe_dispatch",
    )(dest, h1, xs_init)


def _expert_kernel(ce_ref, nu_ref, xs_ref, wi_ref, wo_ref, o_ref):
    del ce_ref
    live = pl.program_id(0) < nu_ref[0]

    @pl.when(live)
    def _():
        xb = xs_ref[...].astype(BF16)
        gu = jnp.dot(xb, wi_ref[0], preferred_element_type=F32)
        gate = gu[:, 0:D_EXPERT]
        act = gate * jax.nn.sigmoid(gate) * gu[:, D_EXPERT:2 * D_EXPERT]
        o_ref[...] = jnp.dot(act.astype(BF16), wo_ref[0], preferred_element_type=F32)

    @pl.when(jnp.logical_not(live))
    def _():
        o_ref[...] = jnp.zeros_like(o_ref)


def _expert_call(chunk_exp, n_used, xs, wi, wo):
    n_rows, d = xs.shape
    n_chunk = n_rows // MOE_CHUNK

    def live(c, nu):
        return jnp.minimum(c, nu[0] - 1)

    grid_spec = pltpu.PrefetchScalarGridSpec(
        num_scalar_prefetch=2,
        grid=(n_chunk,),
        in_specs=[
            pl.BlockSpec((MOE_CHUNK, d), lambda c, ce, nu: (live(c, nu), 0)),
            pl.BlockSpec((1,) + wi.shape[1:], lambda c, ce, nu: (ce[live(c, nu)], 0, 0)),
            pl.BlockSpec((1,) + wo.shape[1:], lambda c, ce, nu: (ce[live(c, nu)], 0, 0)),
        ],
        out_specs=pl.BlockSpec((MOE_CHUNK, d), lambda c, ce, nu: (c, 0)),
    )
    return pl.pallas_call(
        _expert_kernel,
        grid_spec=grid_spec,
        out_shape=jax.ShapeDtypeStruct((n_rows, d), F32),
        compiler_params=_cparams(("arbitrary",)),
        name="moe_experts",
    )(chunk_exp, n_used, xs, wi, wo)


def _combine_kernel(dest_ref, h1_ref, rw_ref, p_ref, ys_ref,
                    g2_ref, b2_ref, wp_ref, wg_ref, bg_ref, g3_ref, b3_ref,
                    o_ref, rows_ref, sem):
    tm = h1_ref.shape[0]

    def issue(r, _):
        for slot in range(2):
            _row_copy(ys_ref, dest_ref[2 * r + slot], rows_ref.at[slot], r, sem).start()
        return 0

    lax.fori_loop(0, tm, issue, 0)

    def drain(r, _):
        for slot in range(2):
            _row_copy(ys_ref, dest_ref[2 * r + slot], rows_ref.at[slot], r, sem).wait()
        return 0

    lax.fori_loop(0, tm, drain, 0)

    rw = rw_ref[...]
    ffn = rw[:, 0:1] * rows_ref[0] + rw[:, 1:2] * rows_ref[1]
    h2 = _layer_norm(ALPHA * h1_ref[...] + ffn, g2_ref[...], b2_ref[...])
    emb = jnp.dot(p_ref[...].astype(BF16), wp_ref[...], preferred_element_type=F32)
    gate = jax.nn.sigmoid(jnp.dot(h2.astype(BF16), wg_ref[...], preferred_element_type=F32) + bg_ref[...])
    o_ref[...] = _layer_norm(ALPHA * h2 + emb * gate, g3_ref[...], b3_ref[...])


def _combine_call(dest, h1, rw, p2, ys, g2, b2, wp, wg, bg, g3, b3):
    n_tok, d = h1.shape
    tm = ROW_TM
    const = lambda a: pl.BlockSpec(a.shape, lambda i: (0, 0))
    return pl.pallas_call(
        _combine_kernel,
        grid=(n_tok // tm,),
        in_specs=[
            pl.BlockSpec((2 * tm,), lambda i: (i,), memory_space=pltpu.SMEM),
            pl.BlockSpec((tm, d), lambda i: (i, 0)),
            pl.BlockSpec((tm, rw.shape[1]), lambda i: (i, 0)),
            pl.BlockSpec((tm, p2.shape[1]), lambda i: (i, 0)),
            pl.BlockSpec(memory_space=pl.ANY),
            const(g2), const(b2), const(wp), const(wg), const(bg), const(g3), const(b3),
        ],
        out_specs=pl.BlockSpec((tm, d), lambda i: (i, 0)),
        out_shape=jax.ShapeDtypeStruct((n_tok, d), F32),
        scratch_shapes=[pltpu.VMEM((2, tm, d), F32), pltpu.SemaphoreType.DMA],
        compiler_params=_cparams(("arbitrary",)),
        name="moe_combine_ln_ple",
    )(dest, h1, rw, p2, ys, g2, b2, wp, wg, bg, g3, b3)


def _gate_columns():
    cols = np.zeros((N_KV, GATE_ROWS), np.int32)
    live = np.zeros((N_KV, GATE_ROWS), np.float32)
    for h in range(N_KV):
        for br in range(3):
            for g in range(GQA):
                cols[h, br * GQA + g] = (h * GQA + g) * 3 + br
                live[h, br * GQA + g] = 1.0
    return cols.reshape(-1), live.reshape(-1)


def _layer(x, p, w_in, w_conv, pe_ck, w1_ck, w2_ck, pe_cv, w1_cv, w2_cv, w_out, ln1_g, ln1_b,
           w_rg, b_rg, w_re, b_re, w_e_in, w_e_out, ln2_g, ln2_b, w_ple, w_ple_gate, b_ple_gate,
           ln3_g, ln3_b):
    bsz, s, d = x.shape
    n_tok = bsz * s
    row = lambda v: v.reshape(1, -1)

    c_q, c_kc, c_vc, c_ks, c_vs, c_kw, c_vw, c_g = 1536, 2048, 2176, 2304, 2432, 2560, 2688, 2816
    wn = jnp.concatenate([w_in[:, 0:c_q], w_in[:, c_kc:c_ks], w_in[:, c_ks:c_vs], w_in[:, c_kw:c_vw]],
                         axis=1).astype(BF16)
    gcols, glive = _gate_columns()
    w_gate = w_in[:, c_g:c_g + 3 * N_HEADS][:, gcols] * glive[None, :]
    wt = jnp.concatenate([w_in[:, c_q:c_kc] * Q_SCALE, w_in[:, c_vs:c_kw], w_in[:, c_vw:c_g], w_gate],
                         axis=1).T.astype(BF16)

    yconv, cmpkv, kvn, qt, vst, vwt, gt = _proj_call(x, wn, wt, w_conv)

    def per_head(a):
        return a.reshape(bsz, s, N_KV, HEAD_DIM).transpose(0, 2, 1, 3)

    grp = CMP_STRIDE * HEAD_DIM
    ckv = jnp.stack([per_head(cmpkv[..., 0:128]), per_head(cmpkv[..., 128:256])], axis=1)
    ckv = ckv.reshape(bsz, 2, N_KV, s // CMP_STRIDE, grp)
    pes = jnp.stack([pe_ck.reshape(2, grp), pe_cv.reshape(2, grp)])
    w1s = jnp.stack([w1_ck, w1_cv]).astype(BF16)
    w2s = jnp.stack([w2_ck, w2_cv]).astype(BF16)
    cmp_out = _compress_call(ckv, pes, w1s, w2s)
    kcmp = cmp_out[:, 0].astype(BF16)
    vct = cmp_out[:, 1].transpose(0, 1, 3, 2).astype(BF16)

    n_sel = s // SEL_BLOCK
    n_cmp = s // CMP_STRIDE
    ka = -(-(HEAD_DIM + n_sel) // LANES) * LANES
    blk_onehot = np.zeros((s, ka - HEAD_DIM), np.float32)
    blk_onehot[np.arange(s), np.arange(s) // SEL_BLOCK] = 1.0
    ksa = jnp.concatenate([per_head(kvn[..., 0:128]),
                           jnp.broadcast_to(jnp.asarray(blk_onehot, BF16), (bsz, N_KV, s, ka - HEAD_DIM))],
                          axis=-1)
    kw = per_head(kvn[..., 128:256])
    ynsa = _attn_call(qt, gt, kcmp, vct, ksa, vst, kw, vwt)

    wr = jnp.zeros((d, LANES), F32).at[:, 0:N_GROUPS].set(w_rg)
    wr = wr.at[:, ROUTE_LANE0:ROUTE_LANE0 + N_EXPERTS].set(w_re)
    br = jnp.zeros((1, LANES), F32).at[0, 0:N_GROUPS].set(b_rg)
    br = br.at[0, ROUTE_LANE0:ROUTE_LANE0 + N_EXPERTS].set(b_re)
    tril = jnp.asarray(np.tril(np.ones((POST_TM, POST_TM), np.float32), -1), BF16)
    h1, ri, rw, cnt = _post_call(x.reshape(n_tok, d), yconv.reshape(n_tok, -1), ynsa.reshape(n_tok, -1),
                                 w_out.astype(BF16), row(ln1_g), row(ln1_b), wr, br, tril)

    counts = cnt[0, ROUTE_LANE0:ROUTE_LANE0 + N_EXPERTS].astype(jnp.int32)
    pcounts = (counts + MOE_CHUNK - 1) // MOE_CHUNK * MOE_CHUNK
    pends = jnp.cumsum(pcounts)
    pstarts = pends - pcounts
    dest = (pstarts[ri[:, 0:2]] + ri[:, 2:4]).reshape(-1).astype(jnp.int32)
    n_asg = n_tok * 2
    n_chunk = -(-n_asg // MOE_CHUNK) + N_EXPERTS
    chunk_exp = jnp.clip(jnp.searchsorted(pends, jnp.arange(n_chunk) * MOE_CHUNK, side='right'),
                         0, N_EXPERTS - 1).astype(jnp.int32)
    n_used = (pends[-1:] // MOE_CHUNK).astype(jnp.int32)

    xs = _dispatch_call(dest, h1, jnp.zeros((n_chunk * MOE_CHUNK, d), F32))
    ys = _expert_call(chunk_exp, n_used, xs, w_e_in.astype(BF16), w_e_out.astype(BF16))
    out = _combine_call(dest, h1, rw, p.reshape(n_tok, -1), ys, row(ln2_g), row(ln2_b),
                        w_ple.astype(BF16), w_ple_gate.astype(BF16), row(b_ple_gate), row(ln3_g), row(ln3_b))
    return out.reshape(bsz, s, d)


def kernel(x, p, w_in, w_conv, pe_ck, w1_ck, w2_ck, pe_cv, w1_cv, w2_cv, w_out, ln1_g, ln1_b, w_rg, b_rg, w_re, b_re, w_e_in, w_e_out, ln2_g, ln2_b, w_ple, w_ple_gate, b_ple_gate, ln3_g, ln3_b):
    assert w_in.shape[0] == DEPTH, "residual scaling ALPHA is derived from DEPTH"
    h = x
    for i in range(DEPTH):
        h = _layer(h, p[i], w_in[i], w_conv[i], pe_ck[i], w1_ck[i], w2_ck[i], pe_cv[i], w1_cv[i], w2_cv[i],
                   w_out[i], ln1_g[i], ln1_b[i], w_rg[i], b_rg[i], w_re[i], b_re[i], w_e_in[i], w_e_out[i],
                   ln2_g[i], ln2_b[i], w_ple[i], w_ple_gate[i], b_ple_gate[i], ln3_g[i], ln3_b[i])
    return h
```

```python
import functools

import jax
import jax.numpy as jnp
import numpy as np
from jax import lax
from jax.experimental import pallas as pl
from jax.experimental.pallas import tpu as pltpu

F32 = jnp.float32
BF16 = jnp.bfloat16
HIGHEST = lax.Precision.HIGHEST

CONV_CH = 512
CONV_W = 3
N_HEADS = 8
HEAD_DIM = 64
N_KV = 2
GQA = N_HEADS // N_KV
CMP_BLOCK = 32
CMP_STRIDE = 16
CMP_HIDDEN = 2 * HEAD_DIM
SEL_BLOCK = 64
N_SEL = 16
WINDOW = 512
ATTN_SCALE = HEAD_DIM ** -0.5
Q_SCALE = ATTN_SCALE * float(np.log2(np.e))
FORCE_SCORE = 1e4
NEG_INF = -1e30
N_GROUPS = 4
EXP_PER_GROUP = 8
N_EXPERTS = N_GROUPS * EXP_PER_GROUP
D_EXPERT = 512
MOE_CHUNK = 256
DEPTH = 1
ALPHA = (2 * DEPTH) ** 0.25
LN_EPS = 1e-5

LANES = 128
SUBLANES = 8
VMEM_LIMIT = 56 * 1024 * 1024

PROJ_TM = 512
Q_TILE = 128
SEL_TK = 512
CMP_CHUNK = 128
WIN_KEYS = WINDOW + Q_TILE
POST_TM = 512
ROW_TM = 256
ROUTE_LANE0 = N_GROUPS
GATE_ROWS = 16
V_ROWS = HEAD_DIM + 16


def _cparams(sem, vmem=VMEM_LIMIT):
    return pltpu.CompilerParams(dimension_semantics=sem, vmem_limit_bytes=vmem)


def _layer_norm(v, g, b):
    mu = jnp.mean(v, axis=-1, keepdims=True)
    d = v - mu
    var = jnp.mean(d * d, axis=-1, keepdims=True)
    return d * lax.rsqrt(var + LN_EPS) * g + b


def _proj_kernel(x_ref, wn_ref, wt_ref, wc_ref,
                 yconv_ref, cmpkv_ref, kvn_ref, qt_ref, vst_ref, vwt_ref, gt_ref,
                 carry_ref):
    si = pl.program_id(1)
    tm = x_ref.shape[1]
    xb = x_ref[0].astype(BF16)
    zn = jnp.dot(xb, wn_ref[...], preferred_element_type=F32)
    zt = lax.dot_general(wt_ref[...], xb, (((1,), (1,)), ((), ())),
                         preferred_element_type=F32)

    cb = zn[:, 0:CONV_CH]
    u = zn[:, CONV_CH:2 * CONV_CH] * zn[:, 2 * CONV_CH:3 * CONV_CH]

    @pl.when(si == 0)
    def _():
        carry_ref[...] = jnp.zeros_like(carry_ref)

    prev = carry_ref[...]
    rows = lax.broadcasted_iota(jnp.int32, u.shape, 0)
    u1 = jnp.where(rows == 0, prev[7:8, :], pltpu.roll(u, 1, 0))
    u2 = jnp.where(rows == 0, prev[6:7, :],
                   jnp.where(rows == 1, prev[7:8, :], pltpu.roll(u, 2, 0)))
    w = wc_ref[...]
    yconv_ref[0] = (cb * (w[0:1, :] * u2 + w[1:2, :] * u1 + w[2:3, :] * u)).astype(BF16)
    carry_ref[...] = u[tm - SUBLANES:tm, :]

    cmpkv_ref[0] = zn[:, 1536:1792]
    kvn_ref[0] = zn[:, 1792:2048].astype(BF16)

    qt_ref[0] = zt[0:512, :].astype(BF16)
    ones = jnp.ones((V_ROWS - HEAD_DIM, LANES), BF16)
    for h in range(N_KV):
        for i in range(tm // LANES):
            cols = slice(i * LANES, (i + 1) * LANES)
            vst_ref[0, h, i, 0:HEAD_DIM, :] = zt[512 + h * 64:512 + (h + 1) * 64, cols].astype(BF16)
            vst_ref[0, h, i, HEAD_DIM:V_ROWS, :] = ones
            vwt_ref[0, h, i, 0:HEAD_DIM, :] = zt[640 + h * 64:640 + (h + 1) * 64, cols].astype(BF16)
            vwt_ref[0, h, i, HEAD_DIM:V_ROWS, :] = ones
    gt_ref[0] = jax.nn.sigmoid(zt[768:800, :])


def _proj_call(x, wn, wt, wc):
    bsz, s, d = x.shape
    tm = PROJ_TM
    nblk = s // LANES
    grid = (bsz, s // tm)
    out_shape = (
        jax.ShapeDtypeStruct((bsz, s, CONV_CH), BF16),
        jax.ShapeDtypeStruct((bsz, s, 256), F32),
        jax.ShapeDtypeStruct((bsz, s, 256), BF16),
        jax.ShapeDtypeStruct((bsz, 512, s), BF16),
        jax.ShapeDtypeStruct((bsz, N_KV, nblk, V_ROWS, LANES), BF16),
        jax.ShapeDtypeStruct((bsz, N_KV, nblk, V_ROWS, LANES), BF16),
        jax.ShapeDtypeStruct((bsz, N_KV * GATE_ROWS, s), F32),
    )
    vspec = pl.BlockSpec((1, N_KV, tm // LANES, V_ROWS, LANES), lambda b, i: (b, 0, i, 0, 0))
    return pl.pallas_call(
        _proj_kernel,
        grid=grid,
        in_specs=[
            pl.BlockSpec((1, tm, d), lambda b, i: (b, i, 0)),
            pl.BlockSpec(wn.shape, lambda b, i: (0, 0)),
            pl.BlockSpec(wt.shape, lambda b, i: (0, 0)),
            pl.BlockSpec(wc.shape, lambda b, i: (0, 0)),
        ],
        out_specs=(
            pl.BlockSpec((1, tm, CONV_CH), lambda b, i: (b, i, 0)),
            pl.BlockSpec((1, tm, 256), lambda b, i: (b, i, 0)),
            pl.BlockSpec((1, tm, 256), lambda b, i: (b, i, 0)),
            pl.BlockSpec((1, 512, tm), lambda b, i: (b, 0, i)),
            vspec, vspec,
            pl.BlockSpec((1, N_KV * GATE_ROWS, tm), lambda b, i: (b, 0, i)),
        ),
        out_shape=out_shape,
        scratch_shapes=[pltpu.VMEM((SUBLANES, CONV_CH), F32)],
        compiler_params=_cparams(("arbitrary", "arbitrary")),
        name="in_proj_conv",
    )(x, wn, wt, wc)


def _compress_kernel(g_ref, pe_ref, w1_ref, w2_ref, o_ref):
    g = g_ref[0, 0, 0]
    pe = pe_ref[0]
    half = g.shape[1]
    a_lo = jnp.dot((g + pe[0:1, :]).astype(BF16), w1_ref[0, 0:half, :], preferred_element_type=F32)
    a_hi = jnp.dot((g + pe[1:2, :]).astype(BF16), w1_ref[0, half:2 * half, :], preferred_element_type=F32)
    n = g.shape[0]
    hid = a_lo + pltpu.roll(a_hi, n - 1, 0)
    act = jax.nn.gelu(hid)
    o_ref[0, 0, 0] = jnp.dot(act.astype(BF16), w2_ref[0], preferred_element_type=F32)


def _compress_call(ckv, pes, w1s, w2s):
    bsz, _, _, n, width = ckv.shape
    return pl.pallas_call(
        _compress_kernel,
        grid=(bsz, 2, N_KV),
        in_specs=[
            pl.BlockSpec((1, 1, 1, n, width), lambda b, k, h: (b, k, h, 0, 0)),
            pl.BlockSpec((1, 2, width), lambda b, k, h: (k, 0, 0)),
            pl.BlockSpec((1, 2 * width, CMP_HIDDEN), lambda b, k, h: (k, 0, 0)),
            pl.BlockSpec((1, CMP_HIDDEN, HEAD_DIM), lambda b, k, h: (k, 0, 0)),
        ],
        out_specs=pl.BlockSpec((1, 1, 1, n, HEAD_DIM), lambda b, k, h: (b, k, h, 0, 0)),
        out_shape=jax.ShapeDtypeStruct((bsz, 2, N_KV, n, HEAD_DIM), F32),
        compiler_params=_cparams(("arbitrary", "arbitrary", "arbitrary")),
        name="compress_mlp",
    )(ckv, pes, w1s, w2s)


def _attn_kernel(n_sel, n_top,
                 q_ref, g_ref, kc_ref, vct_ref, ksa_ref, vst_ref, kw_ref, vwt_ref,
                 o_ref, qa_ref, ps_ref, oc_ref, s_ref):
    qb = pl.program_id(2)
    q0 = qb * Q_TILE
    nq = GQA * Q_TILE
    blk = q_ref[0]
    q4 = jnp.concatenate([blk[g * HEAD_DIM:(g + 1) * HEAD_DIM, :] for g in range(GQA)], axis=1)
    lane = lax.broadcasted_iota(jnp.int32, (1, nq), 1)
    t4 = q0 + (lane & (Q_TILE - 1))

    n_cmp = kc_ref.shape[2]
    chunk = min(CMP_CHUNK, n_cmp)
    ps_ref[0:SUBLANES, :] = jnp.zeros((SUBLANES, Q_TILE), F32)

    def cmp_branch(n):
        sc = jnp.dot(kc_ref[0, 0, 0:n, :], q4, preferred_element_type=F32)
        ci = lax.broadcasted_iota(jnp.int32, (n, nq), 0)
        mask_c = ci * CMP_STRIDE + (CMP_BLOCK - 1) <= t4
        scm = jnp.where(mask_c, sc, NEG_INF)
        m_c = jnp.max(scm, axis=0, keepdims=True)
        e_c = jnp.where(mask_c, jnp.exp2(scm - m_c), 0.0)
        l_c = jnp.sum(e_c, axis=0, keepdims=True)
        p_c = e_c * jnp.where(l_c > 0.0, 1.0 / l_c, 0.0)
        oc_ref[...] = jnp.dot(vct_ref[0, 0, :, 0:n], p_c.astype(BF16), preferred_element_type=F32)
        ps_ref[SUBLANES:SUBLANES + n, :] = (p_c[:, 0:Q_TILE] + p_c[:, Q_TILE:2 * Q_TILE]
                                            + p_c[:, 2 * Q_TILE:3 * Q_TILE] + p_c[:, 3 * Q_TILE:4 * Q_TILE])
        if n < n_cmp:
            ps_ref[SUBLANES + n:, :] = jnp.zeros((n_cmp - n, Q_TILE), F32)

    last_visible = (q0 + Q_TILE - CMP_BLOCK) // CMP_STRIDE
    live_chunks = last_visible // chunk + 1
    for k in range(1, n_cmp // chunk + 1):
        pl.when(live_chunks == k)(functools.partial(cmp_branch, k * chunk))
    o_cmp = oc_ref[...]

    kstart = pl.multiple_of(jnp.maximum(q0 - WINDOW, 0), Q_TILE)
    sw = jnp.dot(kw_ref[0, 0, pl.ds(kstart, WIN_KEYS), :], q4, preferred_element_type=F32)
    dist = t4 - (kstart + lax.broadcasted_iota(jnp.int32, (WIN_KEYS, nq), 0))
    mask_w = (dist >= 0) & (dist < WINDOW)
    swm = jnp.where(mask_w, sw, NEG_INF)
    m_w = jnp.max(swm, axis=0, keepdims=True)
    p_w = jnp.exp2(swm - m_w)
    wblk = kstart // LANES
    vwt = jnp.concatenate([vwt_ref[0, 0, wblk + i] for i in range(WIN_KEYS // LANES)], axis=1)
    acc_w = jnp.dot(vwt, p_w.astype(BF16), preferred_element_type=F32)
    o_win = acc_w[0:HEAD_DIM, :] * (1.0 / acc_w[HEAD_DIM:HEAD_DIM + 1, :])

    per_sel = SEL_BLOCK // CMP_STRIDE
    imp =ps_ref[pl.ds(SUBLANES - 1, n_sel, stride=per_sel), :]
    for off in range(CMP_BLOCK // CMP_STRIDE + per_sel - 2):
        imp = imp + ps_ref[pl.ds(SUBLANES + off, n_sel, stride=per_sel), :]

    jidx = lax.broadcasted_iota(jnp.int32, (n_sel, Q_TILE), 0)
    tq = q0 + lax.broadcasted_iota(jnp.int32, (n_sel, Q_TILE), 1)
    jt = jnp.right_shift(tq, SEL_BLOCK.bit_length() - 1)
    forced = (jidx == 0) | (jidx == jt) | (jidx == jt - 1)
    score = jnp.where(forced, FORCE_SCORE, imp)
    key = jnp.where(jidx > jt, -1, lax.bitcast_convert_type(score, jnp.int32))
    theta = jnp.zeros((1, Q_TILE), jnp.int32)
    for bit in range(30, -1, -1):
        cand = theta | (1 << bit)
        reach = jnp.sum((key >= cand).astype(jnp.int32), axis=0, keepdims=True)
        theta = jnp.where(reach >= n_top, cand, theta)
    above = key > theta
    tied = key == theta
    n_above = jnp.sum(above.astype(jnp.int32), axis=0, keepdims=True)
    lower = (lax.broadcasted_iota(jnp.int32, (n_sel, n_sel), 1)
             < lax.broadcasted_iota(jnp.int32, (n_sel, n_sel), 0)).astype(BF16)
    tied_before = jnp.dot(lower, tied.astype(BF16), preferred_element_type=F32)
    chosen = above | (tied & (tied_before < (n_top - n_above).astype(F32)))
    bias = jnp.where(chosen, 0.0, NEG_INF).astype(BF16)
    qa_ref[0:HEAD_DIM, :] = q4
    qa_ref[HEAD_DIM:HEAD_DIM + n_sel, :] = jnp.concatenate([bias] * GQA, axis=1)
    if qa_ref.shape[0] > HEAD_DIM + n_sel:
        qa_ref[HEAD_DIM + n_sel:, :] = jnp.zeros((qa_ref.shape[0] - HEAD_DIM - n_sel, nq), BF16)

    nsub = SEL_TK // LANES

    def scores(c, buf):
        k0 = pl.multiple_of(c * SEL_TK, SEL_TK)
        s = jnp.dot(ksa_ref[0, 0, pl.ds(k0, SEL_TK), :], qa_ref[...], preferred_element_type=F32)
        s_ref[buf] = s
        return jnp.max(s, axis=0, keepdims=True)

    def accumulate(c, buf, mx, m, acc, causal):
        s = s_ref[buf]
        if causal:
            kpos = c * SEL_TK + lax.broadcasted_iota(jnp.int32, (SEL_TK, nq), 0)
            s = jnp.where(kpos <= t4, s, NEG_INF)
            mx = jnp.max(s, axis=0, keepdims=True)
        m_new = jnp.maximum(m, mx)
        p = jnp.exp2(s - m_new)
        vt = jnp.concatenate([vst_ref[0, 0, c * nsub + i] for i in range(nsub)], axis=1)
        acc = jnp.exp2(m - m_new) * acc + jnp.dot(vt, p.astype(BF16), preferred_element_type=F32)
        return m_new, acc

    def pair_body(pi, carry):
        mx0, m, acc = carry
        c = 2 * pi
        mx1 = scores(c + 1, 1)
        m, acc = accumulate(c, 0, mx0, m, acc, False)
        mx0 = scores(c + 2, 0)
        m, acc = accumulate(c + 1, 1, mx1, m, acc, False)
        return mx0, m, acc

    n_full = qb // (SEL_TK // Q_TILE)
    init = (scores(0, 0), jnp.full((1, nq), NEG_INF, F32), jnp.zeros((V_ROWS, nq), F32))
    mx0, m_s, acc_s = lax.fori_loop(0, n_full // 2, pair_body, init)
    c_last = 2 * (n_full // 2)

    def leftover_then_own():
        mx1 = scores(c_last + 1, 1)
        m1, acc1 = accumulate(c_last, 0, mx0, m_s, acc_s, False)
        return accumulate(c_last + 1, 1, mx1, m1, acc1, True)[1]

    def own_only():
        return accumulate(c_last, 0, mx0, m_s, acc_s, True)[1]

    acc_s = lax.cond(n_full % 2 == 1, leftover_then_own, own_only)
    o_sel = acc_s[0:HEAD_DIM, :] * (1.0 / acc_s[HEAD_DIM:HEAD_DIM + 1, :])

    gt = g_ref[0]

    def gate(br):
        return jnp.concatenate([gt[br * GQA + g:br * GQA + g + 1, :] for g in range(GQA)], axis=1)

    ot = gate(0) * o_cmp + gate(1) * o_sel + gate(2) * o_win
    stacked = jnp.concatenate([ot[:, g * Q_TILE:(g + 1) * Q_TILE] for g in range(GQA)], axis=0)
    o_ref[0] = stacked.T.astype(BF16)


def _attn_call(qt, gt, kcmp, vct, ksa, vst, kw, vwt):
    bsz, _, s = qt.shape
    n_sel = s // SEL_BLOCK
    n_top = min(N_SEL, n_sel)
    n_cmp = kcmp.shape[2]
    ka = ksa.shape[3]
    nblk = s // LANES
    grid = (bsz, N_KV, s // Q_TILE)
    kern = functools.partial(_attn_kernel, n_sel, n_top)
    return pl.pallas_call(
        kern,
        grid=grid,
        in_specs=[
            pl.BlockSpec((1, GQA * HEAD_DIM, Q_TILE), lambda b, h, i: (b, h, i)),
            pl.BlockSpec((1, GATE_ROWS, Q_TILE), lambda b, h, i: (b, h, i)),
            pl.BlockSpec((1, 1, n_cmp, HEAD_DIM), lambda b, h, i: (b, h, 0, 0)),
            pl.BlockSpec((1, 1, HEAD_DIM, n_cmp), lambda b, h, i: (b, h, 0, 0)),
            pl.BlockSpec((1, 1, s, ka), lambda b, h, i: (b, h, 0, 0)),
            pl.BlockSpec((1, 1, nblk, V_ROWS, LANES), lambda b, h, i: (b, h, 0, 0, 0)),
            pl.BlockSpec((1, 1, s, HEAD_DIM), lambda b, h, i: (b, h, 0, 0)),
            pl.BlockSpec((1, 1, nblk, V_ROWS, LANES), lambda b, h, i: (b, h, 0, 0, 0)),
        ],
        out_specs=pl.BlockSpec((1, Q_TILE, GQA * HEAD_DIM), lambda b, h, i: (b, i, h)),
        out_shape=jax.ShapeDtypeStruct((bsz, s, N_HEADS * HEAD_DIM), BF16),
        scratch_shapes=[pltpu.VMEM((ka, GQA * Q_TILE), BF16),
                        pltpu.VMEM((n_cmp + SUBLANES, Q_TILE), F32),
                        pltpu.VMEM((HEAD_DIM, GQA * Q_TILE), F32),
                        pltpu.VMEM((2, SEL_TK, GQA * Q_TILE), F32)],
        compiler_params=_cparams(("arbitrary", "arbitrary", "arbitrary")),
        name="nsa_attention",
    )(qt, gt, kcmp, vct, ksa, vst, kw, vwt)


def _post_kernel(x_ref, yc_ref, yn_ref, wo_ref, g1_ref, b1_ref, wr_ref, br_ref, tril_ref,
                 h1_ref, ri_ref, rw_ref, cnt_ref):
    step = pl.program_id(0)
    half = yc_ref.shape[1]
    mix = (jnp.dot(yc_ref[...], wo_ref[0:half, :], preferred_element_type=F32)
           + jnp.dot(yn_ref[...], wo_ref[half:2 * half, :], preferred_element_type=F32))
    h1 = _layer_norm(ALPHA * x_ref[...] + mix, g1_ref[...], b1_ref[...])
    h1_ref[...] = h1

    logits = jnp.dot(h1, wr_ref[...], precision=HIGHEST, preferred_element_type=F32) + br_ref[...]
    tm = logits.shape[0]
    lane = lax.broadcasted_iota(jnp.int32, (tm, LANES), 1)
    big = jnp.int32(LANES)

    def masked_softmax(valid):
        lg = jnp.where(valid, logits, NEG_INF)
        mx = jnp.max(lg, axis=1, keepdims=True)
        ex = jnp.where(valid, jnp.exp(lg - mx), 0.0)
        return ex / jnp.sum(ex, axis=1, keepdims=True)

    def first_max(vals, valid):
        top = jnp.max(jnp.where(valid, vals, -1.0), axis=1, keepdims=True)
        idx = jnp.min(jnp.where(valid & (vals == top), lane, big), axis=1, keepdims=True)
        return top, idx

    is_group = lane < N_GROUPS
    gp, gsel = first_max(masked_softmax(is_group), is_group)
    lo = ROUTE_LANE0 + EXP_PER_GROUP * gsel
    in_group = (lane >= lo) & (lane < lo + EXP_PER_GROUP)
    eprob = masked_softmax(in_group)
    p1, i1 = first_max(eprob, in_group)
    p2, i2 = first_max(eprob, in_group & (lane != i1))
    den = p1 + p2
    w1 = gp * (p1 / den)
    w2 = gp * (p2 / den)

    @pl.when(step == 0)
    def _():
        cnt_ref[...] = jnp.zeros_like(cnt_ref)

    onehot = (lane == i1) | (lane == i2)
    before = jnp.dot(tril_ref[...], onehot.astype(BF16), preferred_element_type=F32)
    rk = before + cnt_ref[0:1, :]
    r1 = jnp.sum(jnp.where(lane == i1, rk, 0.0), axis=1, keepdims=True)
    r2 = jnp.sum(jnp.where(lane == i2, rk, 0.0), axis=1, keepdims=True)
    cnt_ref[...] = cnt_ref[...] + jnp.sum(onehot.astype(F32), axis=0, keepdims=True)

    col = lax.broadcasted_iota(jnp.int32, (tm, ri_ref.shape[1]), 1)
    ri_ref[...] = jnp.where(col == 0, i1 - ROUTE_LANE0,
                            jnp.where(col == 1, i2 - ROUTE_LANE0,
                                      jnp.where(col == 2, r1.astype(jnp.int32),
                                                jnp.where(col == 3, r2.astype(jnp.int32), 0))))
    rw_ref[...] = jnp.where(col == 0, w1, jnp.where(col == 1, w2, 0.0))


def _post_call(x2, yconv, ynsa, wo, g1, b1, wr, br, tril):
    n_tok, d = x2.shape
    tm = POST_TM
    const = lambda a: pl.BlockSpec(a.shape, lambda i: (0, 0))
    return pl.pallas_call(
        _post_kernel,
        grid=(n_tok // tm,),
        in_specs=[
            pl.BlockSpec((tm, d), lambda i: (i, 0)),
            pl.BlockSpec((tm, yconv.shape[1]), lambda i: (i, 0)),
            pl.BlockSpec((tm, ynsa.shape[1]), lambda i: (i, 0)),
            const(wo), const(g1), const(b1), const(wr), const(br), const(tril),
        ],
        out_specs=(
            pl.BlockSpec((tm, d), lambda i: (i, 0)),
            pl.BlockSpec((tm, SUBLANES), lambda i: (i, 0)),
            pl.BlockSpec((tm, SUBLANES), lambda i: (i, 0)),
            pl.BlockSpec((SUBLANES, LANES), lambda i: (0, 0)),
        ),
        out_shape=(
            jax.ShapeDtypeStruct((n_tok, d), F32),
            jax.ShapeDtypeStruct((n_tok, SUBLANES), jnp.int32),
            jax.ShapeDtypeStruct((n_tok, SUBLANES), F32),
            jax.ShapeDtypeStruct((SUBLANES, LANES), F32),
        ),
        compiler_params=_cparams(("arbitrary",)),
        name="out_proj_ln1_router",
    )(x2, yconv, ynsa, wo, g1, b1, wr, br, tril)


def _row_copy(src_ref, src_row, dst_ref, dst_row, sem):
    return pltpu.make_async_copy(src_ref.at[pl.ds(src_row, 1), :], dst_ref.at[pl.ds(dst_row, 1), :], sem)


def _dispatch_kernel(dest_ref, h_ref, init_ref, xs_ref, sem):
    del init_ref
    tm = h_ref.shape[0]

    def issue(r, _):
        for slot in range(2):
            _row_copy(h_ref, r, xs_ref, dest_ref[2 * r + slot], sem).start()
        return 0

    lax.fori_loop(0, tm, issue, 0)

    def drain(r, _):
        for slot in range(2):
            _row_copy(h_ref, r, xs_ref, dest_ref[2 * r + slot], sem).wait()
        return 0

    lax.fori_loop(0, tm, drain, 0)


def _dispatch_call(dest, h1, xs_init):
    n_tok, d = h1.shape
    tm = ROW_TM
    return pl.pallas_call(
        _dispatch_kernel,
        grid=(n_tok // tm,),
        in_specs=[
            pl.BlockSpec((2 * tm,), lambda i: (i,), memory_space=pltpu.SMEM),
            pl.BlockSpec((tm, d), lambda i: (i, 0)),
            pl.BlockSpec(memory_space=pl.ANY),
        ],
        out_specs=pl.BlockSpec(memory_space=pl.ANY),
        out_shape=jax.ShapeDtypeStruct(xs_init.shape, xs_init.dtype),
        scratch_shapes=[pltpu.SemaphoreType.DMA],
        input_output_aliases={2: 0},
        compiler_params=_cparams(("arbitrary",)),
        name="moe_dispatch",
    )(dest, h1, xs_init)


def _expert_kernel(ce_ref, nu_ref, xs_ref, wi_ref, wo_ref, o_ref):
    del ce_ref
    live = pl.program_id(0) < nu_ref[0]

    @pl.when(live)
    def _():
        xb = xs_ref[...].astype(BF16)
        gu = jnp.dot(xb, wi_ref[0], preferred_element_type=F32)
        gate = gu[:, 0:D_EXPERT]
        act = gate * jax.nn.sigmoid(gate) * gu[:, D_EXPERT:2 * D_EXPERT]
        o_ref[...] = jnp.dot(act.astype(BF16), wo_ref[0], preferred_element_type=F32)

    @pl.when(jnp.logical_not(live))
    def _():
        o_ref[...] = jnp.zeros_like(o_ref)


def _expert_call(chunk_exp, n_used, xs, wi, wo):
    n_rows, d = xs.shape
    n_chunk = n_rows // MOE_CHUNK

    def live(c, nu):
        return jnp.maximum(jnp.minimum(c, nu[0] - 1), 0)

    grid_spec = pltpu.PrefetchScalarGridSpec(
        num_scalar_prefetch=2,
        grid=(n_chunk,),
        in_specs=[
            pl.BlockSpec((MOE_CHUNK, d), lambda c, ce, nu: (live(c, nu), 0)),
            pl.BlockSpec((1,) + wi.shape[1:], lambda c, ce, nu: (ce[live(c, nu)], 0, 0)),
            pl.BlockSpec((1,) + wo.shape[1:], lambda c, ce, nu: (ce[live(c, nu)], 0, 0)),
        ],
        out_specs=pl.BlockSpec((MOE_CHUNK, d), lambda c, ce, nu: (c, 0)),
    )
    return pl.pallas_call(
        _expert_kernel,
        grid_spec=grid_spec,
        out_shape=jax.ShapeDtypeStruct((n_rows, d), F32),
        compiler_params=_cparams(("arbitrary",)),
        name="moe_experts",
    )(chunk_exp, n_used, xs, wi, wo)


def _combine_kernel(dest_ref, h1_ref, rw_ref, p_ref, ys_ref,
                    g2_ref, b2_ref, wp_ref, wg_ref, bg_ref, g3_ref, b3_ref,
                    o_ref, rows_ref, sem):
    tm = h1_ref.shape[0]

    def issue(r, _):
        for slot in range(2):
            _row_copy(ys_ref, dest_ref[2 * r + slot], rows_ref.at[slot], r, sem).start()
        return 0

    lax.fori_loop(0, tm, issue, 0)

    def drain(r, _):
        for slot in range(2):
            _row_copy(ys_ref, dest_ref[2 * r + slot], rows_ref.at[slot], r, sem).wait()
        return 0

    lax.fori_loop(0, tm, drain, 0)

    rw = rw_ref[...]
    ffn = rw[:, 0:1] * rows_ref[0] + rw[:, 1:2] * rows_ref[1]
    h2 = _layer_norm(ALPHA * h1_ref[...] + ffn, g2_ref[...], b2_ref[...])
    emb = jnp.dot(p_ref[...].astype(BF16), wp_ref[...], preferred_element_type=F32)
    gate = jax.nn.sigmoid(jnp.dot(h2.astype(BF16), wg_ref[...], preferred_element_type=F32) + bg_ref[...])
    o_ref[...] = _layer_norm(ALPHA * h2 + emb * gate, g3_ref[...], b3_ref[...])


def _combine_call(dest, h1, rw, p2, ys, g2, b2, wp, wg, bg, g3, b3):
    n_tok, d = h1.shape
    tm = ROW_TM
    const = lambda a: pl.BlockSpec(a.shape, lambda i: (0, 0))
    return pl.pallas_call(
        _combine_kernel,
        grid=(n_tok // tm,),
        in_specs=[
            pl.BlockSpec((2 * tm,), lambda i: (i,), memory_space=pltpu.SMEM),
            pl.BlockSpec((tm, d), lambda i: (i, 0)),
            pl.BlockSpec((tm, rw.shape[1]), lambda i: (i, 0)),
            pl.BlockSpec((tm, p2.shape[1]), lambda i: (i, 0)),
            pl.BlockSpec(memory_space=pl.ANY),
            const(g2), const(b2), const(wp), const(wg), const(bg), const(g3), const(b3),
        ],
        out_specs=pl.BlockSpec((tm, d), lambda i: (i, 0)),
        out_shape=jax.ShapeDtypeStruct((n_tok, d), F32),
        scratch_shapes=[pltpu.VMEM((2, tm, d), F32), pltpu.SemaphoreType.DMA],
        compiler_params=_cparams(("arbitrary",)),
        name="moe_combine_ln_ple",
    )(dest, h1, rw, p2, ys, g2, b2, wp, wg, bg, g3, b3)


def _gate_columns():
    cols = np.zeros((N_KV, GATE_ROWS), np.int32)
    live = np.zeros((N_KV, GATE_ROWS), np.float32)
    for h in range(N_KV):
        for br in range(3):
            for g in range(GQA):
                cols[h, br * GQA + g] = (h * GQA + g) * 3 + br
                live[h, br * GQA + g] = 1.0
    return cols.reshape(-1), live.reshape(-1)


def _layer(x, p, w_in, w_conv, pe_ck, w1_ck, w2_ck, pe_cv, w1_cv, w2_cv, w_out, ln1_g, ln1_b,
           w_rg, b_rg, w_re, b_re, w_e_in, w_e_out, ln2_g, ln2_b, w_ple, w_ple_gate, b_ple_gate,
           ln3_g, ln3_b):
    bsz, s, d = x.shape
    n_tok = bsz * s
    row = lambda v: v.reshape(1, -1)

    c_q, c_kc, c_vc, c_ks, c_vs, c_kw, c_vw, c_g = 1536, 2048, 2176, 2304, 2432, 2560, 2688, 2816
    wn = jnp.concatenate([w_in[:, 0:c_q], w_in[:, c_kc:c_ks], w_in[:, c_ks:c_vs], w_in[:, c_kw:c_vw]],
                         axis=1).astype(BF16)
    gcols, glive = _gate_columns()
    w_gate = w_in[:, c_g:c_g + 3 * N_HEADS][:, gcols] * glive[None, :]
    wt = jnp.concatenate([w_in[:, c_q:c_kc] * Q_SCALE, w_in[:, c_vs:c_kw], w_in[:, c_vw:c_g], w_gate],
                         axis=1).T.astype(BF16)

    yconv, cmpkv, kvn, qt, vst, vwt, gt = _proj_call(x, wn, wt, w_conv)

    def per_head(a):
        return a.reshape(bsz, s, N_KV, HEAD_DIM).transpose(0, 2, 1, 3)

    grp = CMP_STRIDE * HEAD_DIM
    ckv = jnp.stack([per_head(cmpkv[..., 0:128]), per_head(cmpkv[..., 128:256])], axis=1)
    ckv = ckv.reshape(bsz, 2, N_KV, s // CMP_STRIDE, grp)
    pes = jnp.stack([pe_ck.reshape(2, grp), pe_cv.reshape(2, grp)])
    w1s = jnp.stack([w1_ck, w1_cv]).astype(BF16)
    w2s = jnp.stack([w2_ck, w2_cv]).astype(BF16)
    cmp_out = _compress_call(ckv, pes, w1s, w2s)
    kcmp = cmp_out[:, 0].astype(BF16)
    vct = cmp_out[:, 1].transpose(0, 1, 3, 2).astype(BF16)

    n_sel = s // SEL_BLOCK
    n_cmp = s // CMP_STRIDE
    ka = -(-(HEAD_DIM + n_sel) // LANES) * LANES
    blk_onehot = np.zeros((s, ka - HEAD_DIM), np.float32)
    blk_onehot[np.arange(s), np.arange(s) // SEL_BLOCK] = 1.0
    ksa = jnp.concatenate([per_head(kvn[..., 0:128]),
                           jnp.broadcast_to(jnp.asarray(blk_onehot, BF16), (bsz, N_KV, s, ka - HEAD_DIM))],
                          axis=-1)
    kw = per_head(kvn[..., 128:256])
    ynsa = _attn_call(qt, gt, kcmp, vct, ksa, vst, kw, vwt)

    wr = jnp.zeros((d, LANES), F32).at[:, 0:N_GROUPS].set(w_rg)
    wr = wr.at[:, ROUTE_LANE0:ROUTE_LANE0 + N_EXPERTS].set(w_re)
    br = jnp.zeros((1, LANES), F32).at[0, 0:N_GROUPS].set(b_rg)
    br = br.at[0, ROUTE_LANE0:ROUTE_LANE0 + N_EXPERTS].set(b_re)
    tril = jnp.asarray(np.tril(np.ones((POST_TM, POST_TM), np.float32), -1), BF16)
    h1, ri, rw, cnt = _post_call(x.reshape(n_tok, d), yconv.reshape(n_tok, -1), ynsa.reshape(n_tok, -1),
                                 w_out.astype(BF16), row(ln1_g), row(ln1_b), wr, br, tril)

    counts = cnt[0, ROUTE_LANE0:ROUTE_LANE0 + N_EXPERTS].astype(jnp.int32)
    pcounts = (counts + MOE_CHUNK - 1) // MOE_CHUNK * MOE_CHUNK
    pends = jnp.cumsum(pcounts)
    pstarts = pends - pcounts
    dest = (pstarts[ri[:, 0:2]] + ri[:, 2:4]).reshape(-1).astype(jnp.int32)
    n_asg = n_tok * 2
    n_chunk = -(-n_asg // MOE_CHUNK) + N_EXPERTS
    chunk_exp = jnp.clip(jnp.searchsorted(pends, jnp.arange(n_chunk) * MOE_CHUNK, side='right'),
                         0, N_EXPERTS - 1).astype(jnp.int32)
    n_used = (pends[-1:] // MOE_CHUNK).astype(jnp.int32)

    xs = _dispatch_call(dest, h1, jnp.zeros((n_chunk * MOE_CHUNK, d), F32))
    ys = _expert_call(chunk_exp, n_used, xs, w_e_in.astype(BF16), w_e_out.astype(BF16))
    out = _combine_call(dest, h1, rw, p.reshape(n_tok, -1), ys, row(ln2_g), row(ln2_b),
                        w_ple.astype(BF16), w_ple_gate.astype(BF16), row(b_ple_gate), row(ln3_g), row(ln3_b))
    return out.reshape(bsz, s, d)


def kernel(x, p, w_in, w_conv, pe_ck, w1_ck, w2_ck, pe_cv, w1_cv, w2_cv, w_out, ln1_g, ln1_b, w_rg, b_rg, w_re, b_re, w_e_in, w_e_out, ln2_g, ln2_b, w_ple, w_ple_gate, b_ple_gate, ln3_g, ln3_b):
    assert w_in.shape[0] == DEPTH, "residual scaling ALPHA is derived from DEPTH"
    h = x
    for i in range(DEPTH):
        h = _layer(h, p[i], w_in[i], w_conv[i], pe_ck[i], w1_ck[i], w2_ck[i], pe_cv[i], w1_cv[i], w2_cv[i],
                   w_out[i], ln1_g[i], ln1_b[i], w_rg[i], b_rg[i], w_re[i], b_re[i], w_e_in[i], w_e_out[i],
                   ln2_g[i], ln2_b[i], w_ple[i], w_ple_gate[i], b_ple_gate[i], ln3_g[i], ln3_b[i])
    return h
```

```python
import functools

import jax
import jax.numpy as jnp
import numpy as np
from jax import lax
from jax.experimental import pallas as pl
from jax.experimental.pallas import tpu as pltpu

F32 = jnp.float32
BF16 = jnp.bfloat16
HIGHEST = lax.Precision.HIGHEST

CONV_CH = 512
CONV_W = 3
N_HEADS = 8
HEAD_DIM = 64
N_KV = 2
GQA = N_HEADS // N_KV
CMP_BLOCK = 32
CMP_STRIDE = 16
CMP_HIDDEN = 2 * HEAD_DIM
SEL_BLOCK = 64
N_SEL = 16
WINDOW = 512
ATTN_SCALE = HEAD_DIM ** -0.5
Q_SCALE = ATTN_SCALE * float(np.log2(np.e))
FORCE_SCORE = 1e4
NEG_INF = -1e30
N_GROUPS = 4
EXP_PER_GROUP = 8
N_EXPERTS = N_GROUPS * EXP_PER_GROUP
D_EXPERT = 512
MOE_CHUNK = 256
DEPTH = 1
ALPHA = (2 * DEPTH) ** 0.25
LN_EPS = 1e-5

LANES = 128
SUBLANES = 8
VMEM_LIMIT = 56 * 1024 * 1024

PROJ_TM = 512
Q_TILE = 128
SEL_TK = 512
CMP_CHUNK = 128
WIN_KEYS = WINDOW + Q_TILE
POST_TM = 512
ROW_TM = 256
ROUTE_LANE0 = N_GROUPS
GATE_ROWS = 16
V_ROWS = HEAD_DIM + 16


def _cparams(sem, vmem=VMEM_LIMIT):
    return pltpu.CompilerParams(dimension_semantics=sem, vmem_limit_bytes=vmem)


def _layer_norm(v, g, b):
    mu = jnp.mean(v, axis=-1, keepdims=True)
    d = v - mu
    var = jnp.mean(d * d, axis=-1, keepdims=True)
    return d * lax.rsqrt(var + LN_EPS) * g + b


def _proj_kernel(x_ref, wn_ref, wt_ref, wc_ref,
                 yconv_ref, cmpkv_ref, kvn_ref, qt_ref, vst_ref, vwt_ref, gt_ref,
                 carry_ref):
    si = pl.program_id(1)
    tm = x_ref.shape[1]
    xb = x_ref[0].astype(BF16)
    zn = jnp.dot(xb, wn_ref[...], preferred_element_type=F32)
    zt = lax.dot_general(wt_ref[...], xb, (((1,), (1,)), ((), ())),
                         preferred_element_type=F32)

    cb = zn[:, 0:CONV_CH]
    u = zn[:, CONV_CH:2 * CONV_CH] * zn[:, 2 * CONV_CH:3 * CONV_CH]

    @pl.when(si == 0)
    def _():
        carry_ref[...] = jnp.zeros_like(carry_ref)

    prev = carry_ref[...]
    rows = lax.broadcasted_iota(jnp.int32, u.shape, 0)
    u1 = jnp.where(rows == 0, prev[7:8, :], pltpu.roll(u, 1, 0))
    u2 = jnp.where(rows == 0, prev[6:7, :],
                   jnp.where(rows == 1, prev[7:8, :], pltpu.roll(u, 2, 0)))
    w = wc_ref[...]
    yconv_ref[0] = (cb * (w[0:1, :] * u2 + w[1:2, :] * u1 + w[2:3, :] * u)).astype(BF16)
    carry_ref[...] = u[tm - SUBLANES:tm, :]

    cmpkv_ref[0] = zn[:, 1536:1792]
    kvn_ref[0] = zn[:, 1792:2048].astype(BF16)

    qt_ref[0] = zt[0:512, :].astype(BF16)
    ones = jnp.ones((V_ROWS - HEAD_DIM, LANES), BF16)
    for h in range(N_KV):
        for i in range(tm // LANES):
            cols = slice(i * LANES, (i + 1) * LANES)
            vst_ref[0, h, i, 0:HEAD_DIM, :] = zt[512 + h * 64:512 + (h + 1) * 64, cols].astype(BF16)
            vst_ref[0, h, i, HEAD_DIM:V_ROWS, :] = ones
            vwt_ref[0, h, i, 0:HEAD_DIM, :] = zt[640 + h * 64:640 + (h + 1) * 64, cols].astype(BF16)
            vwt_ref[0, h, i, HEAD_DIM:V_ROWS, :] = ones
    gt_ref[0] = jax.nn.sigmoid(zt[768:800, :])


def _proj_call(x, wn, wt, wc):
    bsz, s, d = x.shape
    tm = PROJ_TM
    nblk = s // LANES
    grid = (bsz, s // tm)
    out_shape = (
        jax.ShapeDtypeStruct((bsz, s, CONV_CH), BF16),
        jax.ShapeDtypeStruct((bsz, s, 256), F32),
        jax.ShapeDtypeStruct((bsz, s, 256), BF16),
        jax.ShapeDtypeStruct((bsz, 512, s), BF16),
        jax.ShapeDtypeStruct((bsz, N_KV, nblk, V_ROWS, LANES), BF16),
        jax.ShapeDtypeStruct((bsz, N_KV, nblk, V_ROWS, LANES), BF16),
        jax.ShapeDtypeStruct((bsz, N_KV * GATE_ROWS, s), F32),
    )
    vspec = pl.BlockSpec((1, N_KV, tm // LANES, V_ROWS, LANES), lambda b, i: (b, 0, i, 0, 0))
    return pl.pallas_call(
        _proj_kernel,
        grid=grid,
        in_specs=[
            pl.BlockSpec((1, tm, d), lambda b, i: (b, i, 0)),
            pl.BlockSpec(wn.shape, lambda b, i: (0, 0)),
            pl.BlockSpec(wt.shape, lambda b, i: (0, 0)),
            pl.BlockSpec(wc.shape, lambda b, i: (0, 0)),
        ],
        out_specs=(
            pl.BlockSpec((1, tm, CONV_CH), lambda b, i: (b, i, 0)),
            pl.BlockSpec((1, tm, 256), lambda b, i: (b, i, 0)),
            pl.BlockSpec((1, tm, 256), lambda b, i: (b, i, 0)),
            pl.BlockSpec((1, 512, tm), lambda b, i: (b, 0, i)),
            vspec, vspec,
            pl.BlockSpec((1, N_KV * GATE_ROWS, tm), lambda b, i: (b, 0, i)),
        ),
        out_shape=out_shape,
        scratch_shapes=[pltpu.VMEM((SUBLANES, CONV_CH), F32)],
        compiler_params=_cparams(("arbitrary", "arbitrary")),
        name="in_proj_conv",
    )(x, wn, wt, wc)


def _compress_kernel(g_ref, pe_ref, w1_ref, w2_ref, o_ref):
    g = g_ref[0, 0, 0]
    pe = pe_ref[0]
    half = g.shape[1]
    a_lo = jnp.dot((g + pe[0:1, :]).astype(BF16), w1_ref[0, 0:half, :], preferred_element_type=F32)
    a_hi = jnp.dot((g + pe[1:2, :]).astype(BF16), w1_ref[0, half:2 * half, :], preferred_element_type=F32)
    n = g.shape[0]
    hid = a_lo + pltpu.roll(a_hi, n - 1, 0)
    act = jax.nn.gelu(hid)
    o_ref[0, 0, 0] = jnp.dot(act.astype(BF16), w2_ref[0], preferred_element_type=F32)


def _compress_call(ckv, pes, w1s, w2s):
    bsz, _, _, n, width = ckv.shape
    return pl.pallas_call(
        _compress_kernel,
        grid=(bsz, 2, N_KV),
        in_specs=[
            pl.BlockSpec((1, 1, 1, n, width), lambda b, k, h: (b, k, h, 0, 0)),
            pl.BlockSpec((1, 2, width), lambda b, k, h: (k, 0, 0)),
            pl.BlockSpec((1, 2 * width, CMP_HIDDEN), lambda b, k, h: (k, 0, 0)),
            pl.BlockSpec((1, CMP_HIDDEN, HEAD_DIM), lambda b, k, h: (k, 0, 0)),
        ],
        out_specs=pl.BlockSpec((1, 1, 1, n, HEAD_DIM), lambda b, k, h: (b, k, h, 0, 0)),
        out_shape=jax.ShapeDtypeStruct((bsz, 2, N_KV, n, HEAD_DIM), F32),
        compiler_params=_cparams(("arbitrary", "arbitrary", "arbitrary")),
        name="compress_mlp",
    )(ckv, pes, w1s, w2s)


def _attn_kernel(n_sel, n_top,
                 q_ref, g_ref, kc_ref, vct_ref, ksa_ref, vst_ref, kw_ref, vwt_ref,
                 o_ref, qa_ref, ps_ref, oc_ref, s_ref):
    qb = pl.program_id(2)
    q0 = qb * Q_TILE
    nq = GQA * Q_TILE
    blk = q_ref[0]
    q4 = jnp.concatenate([blk[g * HEAD_DIM:(g + 1) * HEAD_DIM, :] for g in range(GQA)], axis=1)
    lane = lax.broadcasted_iota(jnp.int32, (1, nq), 1)
    t4 = q0 + (lane & (Q_TILE - 1))

    n_cmp = kc_ref.shape[2]
    chunk = min(CMP_CHUNK, n_cmp)
    ps_ref[0:SUBLANES, :] = jnp.zeros((SUBLANES, Q_TILE), F32)

    def cmp_branch(n):
        sc = jnp.dot(kc_ref[0, 0, 0:n, :], q4, preferred_element_type=F32)
        ci = lax.broadcasted_iota(jnp.int32, (n, nq), 0)
        mask_c = ci * CMP_STRIDE + (CMP_BLOCK - 1) <= t4
        scm = jnp.where(mask_c, sc, NEG_INF)
        m_c = jnp.max(scm, axis=0, keepdims=True)
        e_c = jnp.where(mask_c, jnp.exp2(scm - m_c), 0.0)
        l_c = jnp.sum(e_c, axis=0, keepdims=True)
        p_c = e_c * jnp.where(l_c > 0.0, 1.0 / l_c, 0.0)
        oc_ref[...] = jnp.dot(vct_ref[0, 0, :, 0:n], p_c.astype(BF16), preferred_element_type=F32)
        ps_ref[SUBLANES:SUBLANES + n, :] = (p_c[:, 0:Q_TILE] + p_c[:, Q_TILE:2 * Q_TILE]
                                            + p_c[:, 2 * Q_TILE:3 * Q_TILE] + p_c[:, 3 * Q_TILE:4 * Q_TILE])
        if n < n_cmp:
            ps_ref[SUBLANES + n:, :] = jnp.zeros((n_cmp - n, Q_TILE), F32)

    last_visible = (q0 + Q_TILE - CMP_BLOCK) // CMP_STRIDE
    live_chunks = last_visible // chunk + 1
    for k in range(1, n_cmp // chunk + 1):
        pl.when(live_chunks == k)(functools.partial(cmp_branch, k * chunk))
    o_cmp = oc_ref[...]

    kstart = pl.multiple_of(jnp.maximum(q0 - WINDOW, 0), Q_TILE)
    sw = jnp.dot(kw_ref[0, 0, pl.ds(kstart, WIN_KEYS), :], q4, preferred_element_type=F32)
    dist = t4 - (kstart + lax.broadcasted_iota(jnp.int32, (WIN_KEYS, nq), 0))
    mask_w = (dist >= 0) & (dist < WINDOW)
    swm = jnp.where(mask_w, sw, NEG_INF)
    m_w = jnp.max(swm, axis=0, keepdims=True)
    p_w = jnp.exp2(swm - m_w)
    wblk = kstart // LANES
    vwt = jnp.concatenate([vwt_ref[0, 0, wblk + i] for i in range(WIN_KEYS // LANES)], axis=1)
    acc_w = jnp.dot(vwt, p_w.astype(BF16), preferred_element_type=F32)
    o_win = acc_w[0:HEAD_DIM, :] * (1.0 / acc_w[HEAD_DIM:HEAD_DIM + 1, :])

    per_sel = SEL_BLOCK // CMP_STRIDE
    imp =ps_ref[pl.ds(SUBLANES - 1, n_sel, stride=per_sel), :]
    for off in range(CMP_BLOCK // CMP_STRIDE + per_sel - 2):
        imp = imp + ps_ref[pl.ds(SUBLANES + off, n_sel, stride=per_sel), :]

    jidx = lax.broadcasted_iota(jnp.int32, (n_sel, Q_TILE), 0)
    tq = q0 + lax.broadcasted_iota(jnp.int32, (n_sel, Q_TILE), 1)
    jt = jnp.right_shift(tq, SEL_BLOCK.bit_length() - 1)
    forced = (jidx == 0) | (jidx == jt) | (jidx == jt - 1)
    score = jnp.where(forced, FORCE_SCORE, imp)
    key = jnp.where(jidx > jt, -1, lax.bitcast_convert_type(score, jnp.int32))
    theta = jnp.zeros((1, Q_TILE), jnp.int32)
    for bit in range(30, -1, -1):
        cand = theta | (1 << bit)
        reach = jnp.sum((key >= cand).astype(jnp.int32), axis=0, keepdims=True)
        theta = jnp.where(reach >= n_top, cand, theta)
    above = key > theta
    tied = key == theta
    n_above = jnp.sum(above.astype(jnp.int32), axis=0, keepdims=True)
    lower = (lax.broadcasted_iota(jnp.int32, (n_sel, n_sel), 1)
             < lax.broadcasted_iota(jnp.int32, (n_sel, n_sel), 0)).astype(BF16)
    tied_before = jnp.dot(lower, tied.astype(BF16), preferred_element_type=F32)
    chosen = above | (tied & (tied_before < (n_top - n_above).astype(F32)))
    bias = jnp.where(chosen, 0.0, NEG_INF).astype(BF16)
    qa_ref[0:HEAD_DIM, :] = q4
    qa_ref[HEAD_DIM:HEAD_DIM + n_sel, :] = jnp.concatenate([bias] * GQA, axis=1)
    if qa_ref.shape[0] > HEAD_DIM + n_sel:
        qa_ref[HEAD_DIM + n_sel:, :] = jnp.zeros((qa_ref.shape[0] - HEAD_DIM - n_sel, nq), BF16)

    nsub = SEL_TK // LANES

    def scores(c, buf):
        k0 = pl.multiple_of(c * SEL_TK, SEL_TK)
        s = jnp.dot(ksa_ref[0, 0, pl.ds(k0, SEL_TK), :], qa_ref[...], preferred_element_type=F32)
        s_ref[buf] = s
        return jnp.max(s, axis=0, keepdims=True)

    def accumulate(c, buf, mx, m, acc, causal):
        s = s_ref[buf]
        if causal:
            kpos = c * SEL_TK + lax.broadcasted_iota(jnp.int32, (SEL_TK, nq), 0)
            s = jnp.where(kpos <= t4, s, NEG_INF)
            mx = jnp.max(s, axis=0, keepdims=True)
        m_new = jnp.maximum(m, mx)
        p = jnp.exp2(s - m_new)
        vt = jnp.concatenate([vst_ref[0, 0, c * nsub + i] for i in range(nsub)], axis=1)
        acc = jnp.exp2(m - m_new) * acc + jnp.dot(vt, p.astype(BF16), preferred_element_type=F32)
        return m_new, acc

    def pair_body(pi, carry):
        mx0, m, acc = carry
        c = 2 * pi
        mx1 = scores(c + 1, 1)
        m, acc = accumulate(c, 0, mx0, m, acc, False)
        mx0 = scores(c + 2, 0)
        m, acc = accumulate(c + 1, 1, mx1, m, acc, False)
        return mx0, m, acc

    n_full = qb // (SEL_TK // Q_TILE)
    init = (scores(0, 0), jnp.full((1, nq), NEG_INF, F32), jnp.zeros((V_ROWS, nq), F32))
    mx0, m_s, acc_s = lax.fori_loop(0, n_full // 2, pair_body, init)
    c_last = 2 * (n_full // 2)

    def leftover_then_own():
        mx1 = scores(c_last + 1, 1)
        m1, acc1 = accumulate(c_last, 0, mx0, m_s, acc_s, False)
        return accumulate(c_last + 1, 1, mx1, m1, acc1, True)[1]

    def own_only():
        return accumulate(c_last, 0, mx0, m_s, acc_s, True)[1]

    acc_s = lax.cond(n_full % 2 == 1, leftover_then_own, own_only)
    o_sel = acc_s[0:HEAD_DIM, :] * (1.0 / acc_s[HEAD_DIM:HEAD_DIM + 1, :])

    gt = g_ref[0]

    def gate(br):
        return jnp.concatenate([gt[br * GQA + g:br * GQA + g + 1, :] for g in range(GQA)], axis=1)

    ot = gate(0) * o_cmp + gate(1) * o_sel + gate(2) * o_win
    stacked = jnp.concatenate([ot[:, g * Q_TILE:(g + 1) * Q_TILE] for g in range(GQA)], axis=0)
    o_ref[0] = stacked.T.astype(BF16)


def _attn_call(qt, gt, kcmp, vct, ksa, vst, kw, vwt):
    bsz, _, s = qt.shape
    n_sel = s // SEL_BLOCK
    n_top = min(N_SEL, n_sel)
    n_cmp = kcmp.shape[2]
    ka = ksa.shape[3]
    nblk = s // LANES
    grid = (bsz, N_KV, s // Q_TILE)
    kern = functools.partial(_attn_kernel, n_sel, n_top)
    return pl.pallas_call(
        kern,
        grid=grid,
        in_specs=[
            pl.BlockSpec((1, GQA * HEAD_DIM, Q_TILE), lambda b, h, i: (b, h, i)),
            pl.BlockSpec((1, GATE_ROWS, Q_TILE), lambda b, h, i: (b, h, i)),
            pl.BlockSpec((1, 1, n_cmp, HEAD_DIM), lambda b, h, i: (b, h, 0, 0)),
            pl.BlockSpec((1, 1, HEAD_DIM, n_cmp), lambda b, h, i: (b, h, 0, 0)),
            pl.BlockSpec((1, 1, s, ka), lambda b, h, i: (b, h, 0, 0)),
            pl.BlockSpec((1, 1, nblk, V_ROWS, LANES), lambda b, h, i: (b, h, 0, 0, 0)),
            pl.BlockSpec((1, 1, s, HEAD_DIM), lambda b, h, i: (b, h, 0, 0)),
            pl.BlockSpec((1, 1, nblk, V_ROWS, LANES), lambda b, h, i: (b, h, 0, 0, 0)),
        ],
        out_specs=pl.BlockSpec((1, Q_TILE, GQA * HEAD_DIM), lambda b, h, i: (b, i, h)),
        out_shape=jax.ShapeDtypeStruct((bsz, s, N_HEADS * HEAD_DIM), BF16),
        scratch_shapes=[pltpu.VMEM((ka, GQA * Q_TILE), BF16),
                        pltpu.VMEM((n_cmp + SUBLANES, Q_TILE), F32),
                        pltpu.VMEM((HEAD_DIM, GQA * Q_TILE), F32),
                        pltpu.VMEM((2, SEL_TK, GQA * Q_TILE), F32)],
        compiler_params=_cparams(("arbitrary", "arbitrary", "arbitrary")),
        name="nsa_attention",
    )(qt, gt, kcmp, vct, ksa, vst, kw, vwt)


def _post_kernel(x_ref, yc_ref, yn_ref, wo_ref, g1_ref, b1_ref, wr_ref, br_ref, tril_ref,
                 h1_ref, ri_ref, rw_ref, cnt_ref):
    step = pl.program_id(0)
    half = yc_ref.shape[1]
    mix = (jnp.dot(yc_ref[...], wo_ref[0:half, :], preferred_element_type=F32)
           + jnp.dot(yn_ref[...], wo_ref[half:2 * half, :], preferred_element_type=F32))
    h1 = _layer_norm(ALPHA * x_ref[...] + mix, g1_ref[...], b1_ref[...])
    h1_ref[...] = h1

    h_hi = h1.astype(BF16)
    h_lo = (h1 - h_hi.astype(F32)).astype(BF16)
    parts = jnp.dot(jnp.concatenate([h_hi, h_lo], axis=1), wr_ref[...], preferred_element_type=F32)
    logits = parts[:, 0:LANES] + parts[:, LANES:2 * LANES] + br_ref[...]
    tm = logits.shape[0]
    lane = lax.broadcasted_iota(jnp.int32, (tm, LANES), 1)
    big = jnp.int32(LANES)

    def masked_softmax(valid):
        lg = jnp.where(valid, logits, NEG_INF)
        mx = jnp.max(lg, axis=1, keepdims=True)
        ex = jnp.where(valid, jnp.exp(lg - mx), 0.0)
        return ex / jnp.sum(ex, axis=1, keepdims=True)

    def first_max(vals, valid):
        top = jnp.max(jnp.where(valid, vals, -1.0), axis=1, keepdims=True)
        idx = jnp.min(jnp.where(valid & (vals == top), lane, big), axis=1, keepdims=True)
        return top, idx

    is_group = lane < N_GROUPS
    gp, gsel = first_max(masked_softmax(is_group), is_group)
    lo = ROUTE_LANE0 + EXP_PER_GROUP * gsel
    in_group = (lane >= lo) & (lane < lo + EXP_PER_GROUP)
    eprob = masked_softmax(in_group)
    p1, i1 = first_max(eprob, in_group)
    p2, i2 = first_max(eprob, in_group & (lane != i1))
    den = p1 + p2
    w1 = gp * (p1 / den)
    w2 = gp * (p2 / den)

    @pl.when(step == 0)
    def _():
        cnt_ref[...] = jnp.zeros_like(cnt_ref)

    onehot = (lane == i1) | (lane == i2)
    before = jnp.dot(tril_ref[...], onehot.astype(BF16), preferred_element_type=F32)
    rk = before + cnt_ref[0:1, :]
    r1 = jnp.sum(jnp.where(lane == i1, rk, 0.0), axis=1, keepdims=True)
    r2 = jnp.sum(jnp.where(lane == i2, rk, 0.0), axis=1, keepdims=True)
    cnt_ref[...] = cnt_ref[...] + jnp.sum(onehot.astype(F32), axis=0, keepdims=True)

    col = lax.broadcasted_iota(jnp.int32, (tm, ri_ref.shape[1]), 1)
    ri_ref[...] = jnp.where(col == 0, i1 - ROUTE_LANE0,
                            jnp.where(col == 1, i2 - ROUTE_LANE0,
                                      jnp.where(col == 2, r1.astype(jnp.int32),
                                                jnp.where(col == 3, r2.astype(jnp.int32), 0))))
    rw_ref[...] = jnp.where(col == 0, w1, jnp.where(col == 1, w2, 0.0))


def _post_call(x2, yconv, ynsa, wo, g1, b1, wr, br, tril):
    n_tok, d = x2.shape
    tm = POST_TM
    const = lambda a: pl.BlockSpec(a.shape, lambda i: (0, 0))
    return pl.pallas_call(
        _post_kernel,
        grid=(n_tok // tm,),
        in_specs=[
            pl.BlockSpec((tm, d), lambda i: (i, 0)),
            pl.BlockSpec((tm, yconv.shape[1]), lambda i: (i, 0)),
            pl.BlockSpec((tm, ynsa.shape[1]), lambda i: (i, 0)),
            const(wo), const(g1), const(b1), const(wr), const(br), const(tril),
        ],
        out_specs=(
            pl.BlockSpec((tm, d), lambda i: (i, 0)),
            pl.BlockSpec((tm, SUBLANES), lambda i: (i, 0)),
            pl.BlockSpec((tm, SUBLANES), lambda i: (i, 0)),
            pl.BlockSpec((SUBLANES, LANES), lambda i: (0, 0)),
        ),
        out_shape=(
            jax.ShapeDtypeStruct((n_tok, d), F32),
            jax.ShapeDtypeStruct((n_tok, SUBLANES), jnp.int32),
            jax.ShapeDtypeStruct((n_tok, SUBLANES), F32),
            jax.ShapeDtypeStruct((SUBLANES, LANES), F32),
        ),
        compiler_params=_cparams(("arbitrary",)),
        name="out_proj_ln1_router",
    )(x2, yconv, ynsa, wo, g1, b1, wr, br, tril)


def _row_copy(src_ref, src_row, dst_ref, dst_row, sem):
    return pltpu.make_async_copy(src_ref.at[pl.ds(src_row, 1), :], dst_ref.at[pl.ds(dst_row, 1), :], sem)


def _dispatch_kernel(pends_ref, nu_ref, dest_ref, h_ref, xs_ref, zero_ref, sem, zsem):
    tm = h_ref.shape[0]
    n_chunk = xs_ref.shape[0] // MOE_CHUNK

    @pl.when(pl.program_id(0) == 0)
    def _():
        zero_ref[...] = jnp.zeros_like(zero_ref)

        def zero_chunk(row0):
            row0 = pl.multiple_of(row0, MOE_CHUNK)
            return pltpu.make_async_copy(zero_ref, xs_ref.at[pl.ds(row0, MOE_CHUNK), :], zsem)

        def each_padded_chunk(act):
            def per_expert(e, _):
                end = pends_ref[e]
                start = jnp.where(e > 0, pends_ref[jnp.maximum(e - 1, 0)], 0)

                @pl.when(end > start)
                def _():
                    act(zero_chunk(end - MOE_CHUNK))
                return 0

            lax.fori_loop(0, N_EXPERTS, per_expert, 0)

            def per_dead_chunk(c, _):
                act(zero_chunk(c * MOE_CHUNK))
                return 0

            lax.fori_loop(nu_ref[0], n_chunk, per_dead_chunk, 0)

        each_padded_chunk(lambda cp: cp.start())
        each_padded_chunk(lambda cp: cp.wait())

    def issue(r, _):
        for slot in range(2):
            _row_copy(h_ref, r, xs_ref, dest_ref[2 * r + slot], sem).start()
        return 0

    lax.fori_loop(0, tm, issue, 0)
    for slot in range(2):
        pltpu.make_async_copy(h_ref, xs_ref.at[pl.ds(0, tm), :], sem).wait()


def _dispatch_call(pends, n_used, dest, h1, n_rows):
    n_tok, d = h1.shape
    tm = ROW_TM
    grid_spec = pltpu.PrefetchScalarGridSpec(
        num_scalar_prefetch=2,
        grid=(n_tok // tm,),
        in_specs=[
            pl.BlockSpec((2 * tm,), lambda i, pe, nu: (i,), memory_space=pltpu.SMEM),
            pl.BlockSpec((tm, d), lambda i, pe, nu: (i, 0)),
        ],
        out_specs=pl.BlockSpec(memory_space=pl.ANY),
        scratch_shapes=[pltpu.VMEM((MOE_CHUNK, d), F32), pltpu.SemaphoreType.DMA, pltpu.SemaphoreType.DMA],
    )
    return pl.pallas_call(
        _dispatch_kernel,
        grid_spec=grid_spec,
        out_shape=jax.ShapeDtypeStruct((n_rows, d), F32),
        compiler_params=_cparams(("arbitrary",)),
        name="moe_dispatch",
    )(pends, n_used, dest, h1)


def _expert_kernel(ce_ref, nu_ref, xs_ref, wi_ref, wo_ref, o_ref, wib_ref, wob_ref):
    c = pl.program_id(0)
    live = c < nu_ref[0]

    @pl.when(live & ((c == 0) | (ce_ref[c] != ce_ref[jnp.maximum(c - 1, 0)])))
    def _():
        wib_ref[...] = wi_ref[0].astype(BF16)
        wob_ref[...] = wo_ref[0].astype(BF16)

    @pl.when(live)
    def _():
        xb = xs_ref[...].astype(BF16)
        gu = jnp.dot(xb, wib_ref[...], preferred_element_type=F32)
        gate = gu[:, 0:D_EXPERT]
        act = gate * jax.nn.sigmoid(gate) * gu[:, D_EXPERT:2 * D_EXPERT]
        o_ref[...] = jnp.dot(act.astype(BF16), wob_ref[...], preferred_element_type=F32)

    @pl.when(jnp.logical_not(live))
    def _():
        o_ref[...] = jnp.zeros_like(o_ref)


def _expert_call(chunk_exp, n_used, xs, wi, wo):
    n_rows, d = xs.shape
    n_chunk = n_rows // MOE_CHUNK

    def live(c, nu):
        return jnp.maximum(jnp.minimum(c, nu[0] - 1), 0)

    grid_spec = pltpu.PrefetchScalarGridSpec(
        num_scalar_prefetch=2,
        grid=(n_chunk,),
        in_specs=[
            pl.BlockSpec((MOE_CHUNK, d), lambda c, ce, nu: (live(c, nu), 0)),
            pl.BlockSpec((1,) + wi.shape[1:], lambda c, ce, nu: (ce[live(c, nu)], 0, 0)),
            pl.BlockSpec((1,) + wo.shape[1:], lambda c, ce, nu: (ce[live(c, nu)], 0, 0)),
        ],
        out_specs=pl.BlockSpec((MOE_CHUNK, d), lambda c, ce, nu: (c, 0)),
        scratch_shapes=[pltpu.VMEM(wi.shape[1:], BF16), pltpu.VMEM(wo.shape[1:], BF16)],
    )
    return pl.pallas_call(
        _expert_kernel,
        grid_spec=grid_spec,
        out_shape=jax.ShapeDtypeStruct((n_rows, d), F32),
        compiler_params=_cparams(("arbitrary",)),
        name="moe_experts",
    )(chunk_exp, n_used, xs, wi, wo)


def _combine_kernel(dest_ref, dest_next_ref, h1_ref, rw_ref, p_ref, ys_ref,
                    g2_ref, b2_ref, wp_ref, wg_ref, bg_ref, g3_ref, b3_ref,
                    o_ref, rows_ref, sems):
    tm = h1_ref.shape[0]
    step = pl.program_id(0)
    cur = step % 2

    def gather(idx_ref, buf):
        def issue(r, _):
            for slot in range(2):
                _row_copy(ys_ref, idx_ref[2 * r + slot], rows_ref.at[buf, slot], r, sems.at[buf]).start()
            return 0

        lax.fori_loop(0, tm, issue, 0)

    @pl.when(step == 0)
    def _():
        gather(dest_ref, 0)

    @pl.when(step + 1 < pl.num_programs(0))
    def _():
        gather(dest_next_ref, 1 - cur)

    for slot in range(2):
        pltpu.make_async_copy(ys_ref.at[pl.ds(0, tm), :], rows_ref.at[cur, slot], sems.at[cur]).wait()

    rw = rw_ref[...]
    ffn = rw[:, 0:1] * rows_ref[cur, 0] + rw[:, 1:2] * rows_ref[cur, 1]
    h2 = _layer_norm(ALPHA * h1_ref[...] + ffn, g2_ref[...], b2_ref[...])
    emb = jnp.dot(p_ref[...].astype(BF16), wp_ref[...], preferred_element_type=F32)
    gate = jax.nn.sigmoid(jnp.dot(h2.astype(BF16), wg_ref[...], preferred_element_type=F32) + bg_ref[...])
    o_ref[...] = _layer_norm(ALPHA * h2 + emb * gate, g3_ref[...], b3_ref[...])


def _combine_call(dest, h1, rw, p2, ys, g2, b2, wp, wg, bg, g3, b3):
    n_tok, d = h1.shape
    tm = ROW_TM
    const = lambda a: pl.BlockSpec(a.shape, lambda i: (0, 0))
    return pl.pallas_call(
        _combine_kernel,
        grid=(n_tok // tm,),
        in_specs=[
            pl.BlockSpec((2 * tm,), lambda i: (i,), memory_space=pltpu.SMEM),
            pl.BlockSpec((2 * tm,), lambda i: (jnp.minimum(i + 1, n_tok // tm - 1),), memory_space=pltpu.SMEM),
            pl.BlockSpec((tm, d), lambda i: (i, 0)),
            pl.BlockSpec((tm, rw.shape[1]), lambda i: (i, 0)),
            pl.BlockSpec((tm, p2.shape[1]), lambda i: (i, 0)),
            pl.BlockSpec(memory_space=pl.ANY),
            const(g2), const(b2), const(wp), const(wg), const(bg), const(g3), const(b3),
        ],
        out_specs=pl.BlockSpec((tm, d), lambda i: (i, 0)),
        out_shape=jax.ShapeDtypeStruct((n_tok, d), F32),
        scratch_shapes=[pltpu.VMEM((2, 2, tm, d), F32), pltpu.SemaphoreType.DMA((2,))],
        compiler_params=_cparams(("arbitrary",)),
        name="moe_combine_ln_ple",
    )(dest, dest, h1, rw, p2, ys, g2, b2, wp, wg, bg, g3, b3)


def _gate_columns():
    cols = np.zeros((N_KV, GATE_ROWS), np.int32)
    live = np.zeros((N_KV, GATE_ROWS), np.float32)
    for h in range(N_KV):
        for br in range(3):
            for g in range(GQA):
                cols[h, br * GQA + g] = (h * GQA + g) * 3 + br
                live[h, br * GQA + g] = 1.0
    return cols.reshape(-1), live.reshape(-1)


def _layer(x, p, w_in, w_conv, pe_ck, w1_ck, w2_ck, pe_cv, w1_cv, w2_cv, w_out, ln1_g, ln1_b,
           w_rg, b_rg, w_re, b_re, w_e_in, w_e_out, ln2_g, ln2_b, w_ple, w_ple_gate, b_ple_gate,
           ln3_g, ln3_b):
    bsz, s, d = x.shape
    n_tok = bsz * s
    row = lambda v: v.reshape(1, -1)

    c_q, c_kc, c_vc, c_ks, c_vs, c_kw, c_vw, c_g = 1536, 2048, 2176, 2304, 2432, 2560, 2688, 2816
    wn = jnp.concatenate([w_in[:, 0:c_q], w_in[:, c_kc:c_ks], w_in[:, c_ks:c_vs], w_in[:, c_kw:c_vw]],
                         axis=1).astype(BF16)
    gcols, glive = _gate_columns()
    w_gate = w_in[:, c_g:c_g + 3 * N_HEADS][:, gcols] * glive[None, :]
    wt = jnp.concatenate([w_in[:, c_q:c_kc] * Q_SCALE, w_in[:, c_vs:c_kw], w_in[:, c_vw:c_g], w_gate],
                         axis=1).T.astype(BF16)

    yconv, cmpkv, kvn, qt, vst, vwt, gt = _proj_call(x, wn, wt, w_conv)

    def per_head(a):
        return a.reshape(bsz, s, N_KV, HEAD_DIM).transpose(0, 2, 1, 3)

    grp = CMP_STRIDE * HEAD_DIM
    ckv = jnp.stack([per_head(cmpkv[..., 0:128]), per_head(cmpkv[..., 128:256])], axis=1)
    ckv = ckv.reshape(bsz, 2, N_KV, s // CMP_STRIDE, grp)
    pes = jnp.stack([pe_ck.reshape(2, grp), pe_cv.reshape(2, grp)])
    w1s = jnp.stack([w1_ck, w1_cv]).astype(BF16)
    w2s = jnp.stack([w2_ck, w2_cv]).astype(BF16)
    cmp_out = _compress_call(ckv, pes, w1s, w2s)
    kcmp = cmp_out[:, 0].astype(BF16)
    vct = cmp_out[:, 1].transpose(0, 1, 3, 2).astype(BF16)

    n_sel = s // SEL_BLOCK
    n_cmp = s // CMP_STRIDE
    ka = -(-(HEAD_DIM + n_sel) // LANES) * LANES
    blk_onehot = np.zeros((s, ka - HEAD_DIM), np.float32)
    blk_onehot[np.arange(s), np.arange(s) // SEL_BLOCK] = 1.0
    ksa = jnp.concatenate([per_head(kvn[..., 0:128]),
                           jnp.broadcast_to(jnp.asarray(blk_onehot, BF16), (bsz, N_KV, s, ka - HEAD_DIM))],
                          axis=-1)
    kw = per_head(kvn[..., 128:256])
    ynsa = _attn_call(qt, gt, kcmp, vct, ksa, vst, kw, vwt)

    wr = jnp.zeros((d, LANES), F32).at[:, 0:N_GROUPS].set(w_rg)
    wr = wr.at[:, ROUTE_LANE0:ROUTE_LANE0 + N_EXPERTS].set(w_re)
    br = jnp.zeros((1, LANES), F32).at[0, 0:N_GROUPS].set(b_rg)
    br = br.at[0, ROUTE_LANE0:ROUTE_LANE0 + N_EXPERTS].set(b_re)
    wr_hi = wr.astype(BF16)
    wr_lo = (wr - wr_hi.astype(F32)).astype(BF16)
    wr = jnp.concatenate([jnp.concatenate([wr_hi, wr_lo], axis=1),
                          jnp.concatenate([wr_hi, jnp.zeros_like(wr_lo)], axis=1)], axis=0)
    tril = jnp.asarray(np.tril(np.ones((POST_TM, POST_TM), np.float32), -1), BF16)
    h1, ri, rw, cnt = _post_call(x.reshape(n_tok, d), yconv.reshape(n_tok, -1), ynsa.reshape(n_tok, -1),
                                 w_out.astype(BF16), row(ln1_g), row(ln1_b), wr, br, tril)

    counts = cnt[0, ROUTE_LANE0:ROUTE_LANE0 + N_EXPERTS].astype(jnp.int32)
    pcounts = (counts + MOE_CHUNK - 1) // MOE_CHUNK * MOE_CHUNK
    pends = jnp.cumsum(pcounts)
    pstarts = pends - pcounts
    dest = (pstarts[ri[:, 0:2]] + ri[:, 2:4]).reshape(-1).astype(jnp.int32)
    n_asg = n_tok * 2
    n_chunk = -(-n_asg // MOE_CHUNK) + N_EXPERTS
    chunk_row0 = jnp.arange(n_chunk, dtype=jnp.int32) * MOE_CHUNK
    chunk_exp = jnp.minimum(jnp.sum(pends[None, :] <= chunk_row0[:, None], axis=1), N_EXPERTS - 1).astype(jnp.int32)
    n_used = (pends[-1:] // MOE_CHUNK).astype(jnp.int32)

    xs = _dispatch_call(pends.astype(jnp.int32), n_used, dest, h1, n_chunk * MOE_CHUNK)
    ys = _expert_call(chunk_exp, n_used, xs, w_e_in, w_e_out)
    out = _combine_call(dest, h1, rw, p.reshape(n_tok, -1), ys, row(ln2_g), row(ln2_b),
                        w_ple.astype(BF16), w_ple_gate.astype(BF16), row(b_ple_gate), row(ln3_g), row(ln3_b))
    return out.reshape(bsz, s, d)


def kernel(x, p, w_in, w_conv, pe_ck, w1_ck, w2_ck, pe_cv, w1_cv, w2_cv, w_out, ln1_g, ln1_b, w_rg, b_rg, w_re, b_re, w_e_in, w_e_out, ln2_g, ln2_b, w_ple, w_ple_gate, b_ple_gate, ln3_g, ln3_b):
    assert w_in.shape[0] == DEPTH, "residual scaling ALPHA is derived from DEPTH"
    h = x
    for i in range(DEPTH):
        h = _layer(h, p[i], w_in[i], w_conv[i], pe_ck[i], w1_ck[i], w2_ck[i], pe_cv[i], w1_cv[i], w2_cv[i],
                   w_out[i], ln1_g[i], ln1_b[i], w_rg[i], b_rg[i], w_re[i], b_re[i], w_e_in[i], w_e_out[i],
                   ln2_g[i], ln2_b[i], w_ple[i], w_ple_gate[i], b_ple_gate[i], ln3_g[i], ln3_b[i])
    return h
```

```python
import functools

import jax
import jax.numpy as jnp
import numpy as np
from jax import lax
from jax.experimental import pallas as pl
from jax.experimental.pallas import tpu as pltpu

F32 = jnp.float32
BF16 = jnp.bfloat16
HIGHEST = lax.Precision.HIGHEST

CONV_CH = 512
CONV_W = 3
N_HEADS = 8
HEAD_DIM = 64
N_KV = 2
GQA = N_HEADS // N_KV
CMP_BLOCK = 32
CMP_STRIDE = 16
CMP_HIDDEN = 2 * HEAD_DIM
SEL_BLOCK = 64
N_SEL = 16
WINDOW = 512
ATTN_SCALE = HEAD_DIM ** -0.5
Q_SCALE = ATTN_SCALE * float(np.log2(np.e))
FORCE_SCORE = 1e4
NEG_INF = -1e30
N_GROUPS = 4
EXP_PER_GROUP = 8
N_EXPERTS = N_GROUPS * EXP_PER_GROUP
D_EXPERT = 512
MOE_CHUNK = 256
DEPTH = 1
ALPHA = (2 * DEPTH) ** 0.25
LN_EPS = 1e-5

LANES = 128
SUBLANES = 8
VMEM_LIMIT = 56 * 1024 * 1024

PROJ_TM = 512
Q_TILE = 256
SEL_TK = 512
CMP_CHUNK = 128
WIN_KEYS = WINDOW + Q_TILE
POST_TM = 512
ROW_TM = 256
ROW_UNROLL = True
ROUTE_LANE0 = N_GROUPS
GATE_ROWS = 16
V_ROWS = HEAD_DIM + 16


def _cparams(sem, vmem=VMEM_LIMIT):
    return pltpu.CompilerParams(dimension_semantics=sem, vmem_limit_bytes=vmem)


def _layer_norm(v, g, b):
    mu = jnp.mean(v, axis=-1, keepdims=True)
    d = v - mu
    var = jnp.mean(d * d, axis=-1, keepdims=True)
    return d * lax.rsqrt(var + LN_EPS) * g + b


def _proj_kernel(x_ref, wn_ref, wt_ref, wc_ref,
                 yconv_ref, cmpkv_ref, kvn_ref, qt_ref, vst_ref, vwt_ref, gt_ref,
                 carry_ref):
    si = pl.program_id(1)
    tm = x_ref.shape[1]
    xb = x_ref[0].astype(BF16)
    zn = jnp.dot(xb, wn_ref[...], preferred_element_type=F32)
    zt = lax.dot_general(wt_ref[...], xb, (((1,), (1,)), ((), ())),
                         preferred_element_type=F32)

    cb = zn[:, 0:CONV_CH]
    u = zn[:, CONV_CH:2 * CONV_CH] * zn[:, 2 * CONV_CH:3 * CONV_CH]

    @pl.when(si == 0)
    def _():
        carry_ref[...] = jnp.zeros_like(carry_ref)

    prev = carry_ref[...]
    rows = lax.broadcasted_iota(jnp.int32, u.shape, 0)
    u1 = jnp.where(rows == 0, prev[7:8, :], pltpu.roll(u, 1, 0))
    u2 = jnp.where(rows == 0, prev[6:7, :],
                   jnp.where(rows == 1, prev[7:8, :], pltpu.roll(u, 2, 0)))
    w = wc_ref[...]
    yconv_ref[0] = (cb * (w[0:1, :] * u2 + w[1:2, :] * u1 + w[2:3, :] * u)).astype(BF16)
    carry_ref[...] = u[tm - SUBLANES:tm, :]

    cmpkv_ref[0] = zn[:, 1536:1792]
    kvn_ref[0] = zn[:, 1792:2048].astype(BF16)

    qt_ref[0] = zt[0:512, :].astype(BF16)
    ones = jnp.ones((V_ROWS - HEAD_DIM, LANES), BF16)
    for h in range(N_KV):
        for i in range(tm // LANES):
            cols = slice(i * LANES, (i + 1) * LANES)
            vst_ref[0, h, i, 0:HEAD_DIM, :] = zt[512 + h * 64:512 + (h + 1) * 64, cols].astype(BF16)
            vst_ref[0, h, i, HEAD_DIM:V_ROWS, :] = ones
            vwt_ref[0, h, i, 0:HEAD_DIM, :] = zt[640 + h * 64:640 + (h + 1) * 64, cols].astype(BF16)
            vwt_ref[0, h, i, HEAD_DIM:V_ROWS, :] = ones
    gt_ref[0] = jax.nn.sigmoid(zt[768:800, :])


def _proj_call(x, wn, wt, wc):
    bsz, s, d = x.shape
    tm = PROJ_TM
    nblk = s // LANES
    grid = (bsz, s // tm)
    out_shape = (
        jax.ShapeDtypeStruct((bsz, s, CONV_CH), BF16),
        jax.ShapeDtypeStruct((bsz, s, 256), F32),
        jax.ShapeDtypeStruct((bsz, s, 256), BF16),
        jax.ShapeDtypeStruct((bsz, 512, s), BF16),
        jax.ShapeDtypeStruct((bsz, N_KV, nblk, V_ROWS, LANES), BF16),
        jax.ShapeDtypeStruct((bsz, N_KV, nblk, V_ROWS, LANES), BF16),
        jax.ShapeDtypeStruct((bsz, N_KV * GATE_ROWS, s), F32),
    )
    vspec = pl.BlockSpec((1, N_KV, tm // LANES, V_ROWS, LANES), lambda b, i: (b, 0, i, 0, 0))
    return pl.pallas_call(
        _proj_kernel,
        grid=grid,
        in_specs=[
            pl.BlockSpec((1, tm, d), lambda b, i: (b, i, 0)),
            pl.BlockSpec(wn.shape, lambda b, i: (0, 0)),
            pl.BlockSpec(wt.shape, lambda b, i: (0, 0)),
            pl.BlockSpec(wc.shape, lambda b, i: (0, 0)),
        ],
        out_specs=(
            pl.BlockSpec((1, tm, CONV_CH), lambda b, i: (b, i, 0)),
            pl.BlockSpec((1, tm, 256), lambda b, i: (b, i, 0)),
            pl.BlockSpec((1, tm, 256), lambda b, i: (b, i, 0)),
            pl.BlockSpec((1, 512, tm), lambda b, i: (b, 0, i)),
            vspec, vspec,
            pl.BlockSpec((1, N_KV * GATE_ROWS, tm), lambda b, i: (b, 0, i)),
        ),
        out_shape=out_shape,
        scratch_shapes=[pltpu.VMEM((SUBLANES, CONV_CH), F32)],
        compiler_params=_cparams(("arbitrary", "arbitrary")),
        name="in_proj_conv",
    )(x, wn, wt, wc)


def _compress_kernel(g_ref, pe_ref, w1_ref, w2_ref, o_ref):
    g = g_ref[0, 0, 0]
    pe = pe_ref[0]
    half = g.shape[1]
    a_lo = jnp.dot((g + pe[0:1, :]).astype(BF16), w1_ref[0, 0:half, :], preferred_element_type=F32)
    a_hi = jnp.dot((g + pe[1:2, :]).astype(BF16), w1_ref[0, half:2 * half, :], preferred_element_type=F32)
    n = g.shape[0]
    hid = a_lo + pltpu.roll(a_hi, n - 1, 0)
    act = jax.nn.gelu(hid)
    o_ref[0, 0, 0] = jnp.dot(act.astype(BF16), w2_ref[0], preferred_element_type=F32)


def _compress_call(ckv, pes, w1s, w2s):
    bsz, _, _, n, width = ckv.shape
    return pl.pallas_call(
        _compress_kernel,
        grid=(bsz, 2, N_KV),
        in_specs=[
            pl.BlockSpec((1, 1, 1, n, width), lambda b, k, h: (b, k, h, 0, 0)),
            pl.BlockSpec((1, 2, width), lambda b, k, h: (k, 0, 0)),
            pl.BlockSpec((1, 2 * width, CMP_HIDDEN), lambda b, k, h: (k, 0, 0)),
            pl.BlockSpec((1, CMP_HIDDEN, HEAD_DIM), lambda b, k, h: (k, 0, 0)),
        ],
        out_specs=pl.BlockSpec((1, 1, 1, n, HEAD_DIM), lambda b, k, h: (b, k, h, 0, 0)),
        out_shape=jax.ShapeDtypeStruct((bsz, 2, N_KV, n, HEAD_DIM), F32),
        compiler_params=_cparams(("arbitrary", "arbitrary", "arbitrary")),
        name="compress_mlp",
    )(ckv, pes, w1s, w2s)


def _attn_kernel(n_sel, n_top,
                 q_ref, g_ref, kc_ref, vct_ref, ksa_ref, vst_ref, kw_ref, vwt_ref,
                 o_ref, qa_ref, ps_ref, oc_ref, s_ref):
    qb = pl.program_id(2)
    q0 = qb * Q_TILE
    nq = GQA * Q_TILE
    blk = q_ref[0]
    q4 = jnp.concatenate([blk[g * HEAD_DIM:(g + 1) * HEAD_DIM, :] for g in range(GQA)], axis=1)
    lane = lax.broadcasted_iota(jnp.int32, (1, nq), 1)
    t4 = q0 + (lane & (Q_TILE - 1))

    n_cmp = kc_ref.shape[2]
    chunk = min(CMP_CHUNK, n_cmp)
    q_lane_tiles = Q_TILE // LANES
    for h in range(q_lane_tiles):
        ps_ref[h, 0:SUBLANES, :] = jnp.zeros((SUBLANES, LANES), F32)

    def cmp_branch(n):
        sc = jnp.dot(kc_ref[0, 0, 0:n, :], q4, preferred_element_type=F32)
        last_c = jnp.right_shift(t4 - (CMP_BLOCK - 1), CMP_STRIDE.bit_length() - 1)
        scm = jnp.where(lax.broadcasted_iota(jnp.int32, (n, nq), 0) <= last_c, sc, NEG_INF)
        m_c = jnp.max(scm, axis=0, keepdims=True)
        e_c = jnp.exp2(scm - m_c)
        l_c = jnp.sum(e_c, axis=0, keepdims=True)
        p_c = e_c * jnp.where(last_c >= 0, 1.0 / l_c, 0.0)
        oc_ref[...] = jnp.dot(vct_ref[0, 0, :, 0:n], p_c.astype(BF16), preferred_element_type=F32)
        psum = (p_c[:, 0:Q_TILE] + p_c[:, Q_TILE:2 * Q_TILE]
                + p_c[:, 2 * Q_TILE:3 * Q_TILE] + p_c[:, 3 * Q_TILE:4 * Q_TILE])
        for h in range(q_lane_tiles):
            ps_ref[h, SUBLANES:SUBLANES + n, :] = psum[:, h * LANES:(h + 1) * LANES]
            if n < n_cmp:
                ps_ref[h, SUBLANES + n:, :] = jnp.zeros((n_cmp - n, LANES), F32)

    last_visible = (q0 + Q_TILE - CMP_BLOCK) // CMP_STRIDE
    live_chunks = last_visible // chunk + 1
    for k in range(1, n_cmp // chunk + 1):
        pl.when(live_chunks == k)(functools.partial(cmp_branch, k * chunk))
    o_cmp = oc_ref[...]

    kstart = pl.multiple_of(jnp.maximum(q0 - WINDOW, 0), Q_TILE)
    sw = jnp.dot(kw_ref[0, 0, pl.ds(kstart, WIN_KEYS), :], q4, preferred_element_type=F32)
    dist = t4 - (kstart + lax.broadcasted_iota(jnp.int32, (WIN_KEYS, nq), 0))
    mask_w = (dist >= 0) & (dist < WINDOW)
    swm = jnp.where(mask_w, sw, NEG_INF)
    m_w = jnp.max(swm, axis=0, keepdims=True)
    p_w = jnp.exp2(swm - m_w)
    wblk = kstart // LANES
    vwt = jnp.concatenate([vwt_ref[0, 0, wblk + i] for i in range(WIN_KEYS // LANES)], axis=1)
    acc_w = jnp.dot(vwt, p_w.astype(BF16), preferred_element_type=F32)
    o_win = acc_w[0:HEAD_DIM, :] * (1.0 / acc_w[HEAD_DIM:HEAD_DIM + 1, :])

    per_sel = SEL_BLOCK // CMP_STRIDE
    def importance(h):
        acc = ps_ref[h, pl.ds(SUBLANES - 1, n_sel, stride=per_sel), :]
        for off in range(CMP_BLOCK // CMP_STRIDE + per_sel - 2):
            acc = acc + ps_ref[h, pl.ds(SUBLANES + off, n_sel, stride=per_sel), :]
        return acc

    imp = jnp.concatenate([importance(h) for h in range(q_lane_tiles)], axis=1)

    jidx = lax.broadcasted_iota(jnp.int32, (n_sel, Q_TILE), 0)
    tq = q0 + lax.broadcasted_iota(jnp.int32, (n_sel, Q_TILE), 1)
    jt = jnp.right_shift(tq, SEL_BLOCK.bit_length() - 1)
    forced = (jidx == 0) | (jidx == jt) | (jidx == jt - 1)
    score = jnp.where(forced, FORCE_SCORE, imp)
    key = jnp.where(jidx > jt, -1, lax.bitcast_convert_type(score, jnp.int32))
    theta = jnp.zeros((1, Q_TILE), jnp.int32)
    for bit in range(30, -1, -1):
        cand = theta | (1 << bit)
        reach = jnp.sum((key >= cand).astype(jnp.int32), axis=0, keepdims=True)
        theta = jnp.where(reach >= n_top, cand, theta)
    above = key > theta
    tied = key == theta
    n_above = jnp.sum(above.astype(jnp.int32), axis=0, keepdims=True)
    lower = (lax.broadcasted_iota(jnp.int32, (n_sel, n_sel), 1)
             < lax.broadcasted_iota(jnp.int32, (n_sel, n_sel), 0)).astype(BF16)
    tied_before = jnp.dot(lower, tied.astype(BF16), preferred_element_type=F32)
    chosen = above | (tied & (tied_before < (n_top - n_above).astype(F32)))
    bias = jnp.where(chosen, 0.0, NEG_INF).astype(BF16)
    qa_ref[0:HEAD_DIM, :] = q4
    qa_ref[HEAD_DIM:HEAD_DIM + n_sel, :] = jnp.concatenate([bias] * GQA, axis=1)
    if qa_ref.shape[0] > HEAD_DIM + n_sel:
        qa_ref[HEAD_DIM + n_sel:, :] = jnp.zeros((qa_ref.shape[0] - HEAD_DIM - n_sel, nq), BF16)

    nsub = SEL_TK // LANES

    def scores(c, buf):
        k0 = pl.multiple_of(c * SEL_TK, SEL_TK)
        s = jnp.dot(ksa_ref[0, 0, pl.ds(k0, SEL_TK), :], qa_ref[...], preferred_element_type=F32)
        s_ref[buf] = s
        return jnp.max(s, axis=0, keepdims=True)

    def accumulate(c, buf, mx, m, acc, causal):
        s = s_ref[buf]
        if causal:
            kpos = c * SEL_TK + lax.broadcasted_iota(jnp.int32, (SEL_TK, nq), 0)
            s = jnp.where(kpos <= t4, s, NEG_INF)
            mx = jnp.max(s, axis=0, keepdims=True)
        m_new = jnp.maximum(m, mx)
        p = jnp.exp2(s - m_new)
        vt = jnp.concatenate([vst_ref[0, 0, c * nsub + i] for i in range(nsub)], axis=1)
        acc = jnp.exp2(m - m_new) * acc + jnp.dot(vt, p.astype(BF16), preferred_element_type=F32)
        return m_new, acc

    def pair_body(pi, carry):
        mx0, m, acc = carry
        c = 2 * pi
        mx1 = scores(c + 1, 1)
        m, acc = accumulate(c, 0, mx0, m, acc, False)
        mx0 = scores(c + 2, 0)
        m, acc = accumulate(c + 1, 1, mx1, m, acc, False)
        return mx0, m, acc

    n_full = qb // (SEL_TK // Q_TILE)
    init = (scores(0, 0), jnp.full((1, nq), NEG_INF, F32), jnp.zeros((V_ROWS, nq), F32))
    mx0, m_s, acc_s = lax.fori_loop(0, n_full // 2, pair_body, init)
    c_last = 2 * (n_full // 2)

    def leftover_then_own():
        mx1 = scores(c_last + 1, 1)
        m1, acc1 = accumulate(c_last, 0, mx0, m_s, acc_s, False)
        return accumulate(c_last + 1, 1, mx1, m1, acc1, True)[1]

    def own_only():
        return accumulate(c_last, 0, mx0, m_s, acc_s, True)[1]

    acc_s = lax.cond(n_full % 2 == 1, leftover_then_own, own_only)
    o_sel = acc_s[0:HEAD_DIM, :] * (1.0 / acc_s[HEAD_DIM:HEAD_DIM + 1, :])

    gt = g_ref[0]

    def gate(br):
        return jnp.concatenate([gt[br * GQA + g:br * GQA + g + 1, :] for g in range(GQA)], axis=1)

    ot = gate(0) * o_cmp + gate(1) * o_sel + gate(2) * o_win
    stacked = jnp.concatenate([ot[:, g * Q_TILE:(g + 1) * Q_TILE] for g in range(GQA)], axis=0)
    o_ref[0] = stacked.T.astype(BF16)


def _attn_call(qt, gt, kcmp, vct, ksa, vst, kw, vwt):
    bsz, _, s = qt.shape
    n_sel = s // SEL_BLOCK
    n_top = min(N_SEL, n_sel)
    n_cmp = kcmp.shape[2]
    ka = ksa.shape[3]
    nblk = s // LANES
    grid = (bsz, N_KV, s // Q_TILE)
    kern = functools.partial(_attn_kernel, n_sel, n_top)
    return pl.pallas_call(
        kern,
        grid=grid,
        in_specs=[
            pl.BlockSpec((1, GQA * HEAD_DIM, Q_TILE), lambda b, h, i: (b, h, i)),
            pl.BlockSpec((1, GATE_ROWS, Q_TILE), lambda b, h, i: (b, h, i)),
            pl.BlockSpec((1, 1, n_cmp, HEAD_DIM), lambda b, h, i: (b, h, 0, 0)),
            pl.BlockSpec((1, 1, HEAD_DIM, n_cmp), lambda b, h, i: (b, h, 0, 0)),
            pl.BlockSpec((1, 1, s, ka), lambda b, h, i: (b, h, 0, 0)),
            pl.BlockSpec((1, 1, nblk, V_ROWS, LANES), lambda b, h, i: (b, h, 0, 0, 0)),
            pl.BlockSpec((1, 1, s, HEAD_DIM), lambda b, h, i: (b, h, 0, 0)),
            pl.BlockSpec((1, 1, nblk, V_ROWS, LANES), lambda b, h, i: (b, h, 0, 0, 0)),
        ],
        out_specs=pl.BlockSpec((1, Q_TILE, GQA * HEAD_DIM), lambda b, h, i: (b, i, h)),
        out_shape=jax.ShapeDtypeStruct((bsz, s, N_HEADS * HEAD_DIM), BF16),
        scratch_shapes=[pltpu.VMEM((ka, GQA * Q_TILE), BF16),
                        pltpu.VMEM((Q_TILE // LANES, n_cmp + SUBLANES, LANES), F32),
                        pltpu.VMEM((HEAD_DIM, GQA * Q_TILE), F32),
                        pltpu.VMEM((2, SEL_TK, GQA * Q_TILE), F32)],
        compiler_params=_cparams(("arbitrary", "arbitrary", "arbitrary")),
        name="nsa_attention",
    )(qt, gt, kcmp, vct, ksa, vst, kw, vwt)


def _post_kernel(x_ref, yc_ref, yn_ref, wo_ref, g1_ref, b1_ref, wr_ref, br_ref, tril_ref,
                 h1_ref, ri_ref, rw_ref, cnt_ref):
    step = pl.program_id(0)
    half = yc_ref.shape[1]
    mix = (jnp.dot(yc_ref[...], wo_ref[0:half, :], preferred_element_type=F32)
           + jnp.dot(yn_ref[...], wo_ref[half:2 * half, :], preferred_element_type=F32))
    h1 = _layer_norm(ALPHA * x_ref[...] + mix, g1_ref[...], b1_ref[...])
    h1_ref[...] = h1

    h_hi = h1.astype(BF16)
    h_lo = (h1 - h_hi.astype(F32)).astype(BF16)
    parts = jnp.dot(jnp.concatenate([h_hi, h_lo], axis=1), wr_ref[...], preferred_element_type=F32)
    logits = parts[:, 0:LANES] + parts[:, LANES:2 * LANES] + br_ref[...]
    tm = logits.shape[0]
    lane = lax.broadcasted_iota(jnp.int32, (tm, LANES), 1)
    big = jnp.int32(LANES)

    def masked_softmax(valid):
        lg = jnp.where(valid, logits, NEG_INF)
        mx = jnp.max(lg, axis=1, keepdims=True)
        ex = jnp.where(valid, jnp.exp(lg - mx), 0.0)
        return ex / jnp.sum(ex, axis=1, keepdims=True)

    def first_max(vals, valid):
        top = jnp.max(jnp.where(valid, vals, -1.0), axis=1, keepdims=True)
        idx = jnp.min(jnp.where(valid & (vals == top), lane, big), axis=1, keepdims=True)
        return top, idx

    is_group = lane < N_GROUPS
    gp, gsel = first_max(masked_softmax(is_group), is_group)
    lo = ROUTE_LANE0 + EXP_PER_GROUP * gsel
    in_group = (lane >= lo) & (lane < lo + EXP_PER_GROUP)
    eprob = masked_softmax(in_group)
    p1, i1 = first_max(eprob, in_group)
    p2, i2 = first_max(eprob, in_group & (lane != i1))
    den = p1 + p2
    w1 = gp * (p1 / den)
    w2 = gp * (p2 / den)

    @pl.when(step == 0)
    def _():
        cnt_ref[...] = jnp.zeros_like(cnt_ref)

    onehot = (lane == i1) | (lane == i2)
    before = jnp.dot(tril_ref[...], onehot.astype(BF16), preferred_element_type=F32)
    rk = before + cnt_ref[0:1, :]
    r1 = jnp.sum(jnp.where(lane == i1, rk, 0.0), axis=1, keepdims=True)
    r2 = jnp.sum(jnp.where(lane == i2, rk, 0.0), axis=1, keepdims=True)
    cnt_ref[...] = cnt_ref[...] + jnp.sum(onehot.astype(F32), axis=0, keepdims=True)

    col = lax.broadcasted_iota(jnp.int32, (tm, ri_ref.shape[1]), 1)
    ri_ref[...] = jnp.where(col == 0, i1 - ROUTE_LANE0,
                            jnp.where(col == 1, i2 - ROUTE_LANE0,
                                      jnp.where(col == 2, r1.astype(jnp.int32),
                                                jnp.where(col == 3, r2.astype(jnp.int32), 0))))
    rw_ref[...] = jnp.where(col == 0, w1, jnp.where(col == 1, w2, 0.0))


def _post_call(x2, yconv, ynsa, wo, g1, b1, wr, br, tril):
    n_tok, d = x2.shape
    tm = POST_TM
    const = lambda a: pl.BlockSpec(a.shape, lambda i: (0, 0))
    return pl.pallas_call(
        _post_kernel,
        grid=(n_tok // tm,),
        in_specs=[
            pl.BlockSpec((tm, d), lambda i: (i, 0)),
            pl.BlockSpec((tm, yconv.shape[1]), lambda i: (i, 0)),
            pl.BlockSpec((tm, ynsa.shape[1]), lambda i: (i, 0)),
            const(wo), const(g1), const(b1), const(wr), const(br), const(tril),
        ],
        out_specs=(
            pl.BlockSpec((tm, d), lambda i: (i, 0)),
            pl.BlockSpec((tm, SUBLANES), lambda i: (i, 0)),
            pl.BlockSpec((tm, SUBLANES), lambda i: (i, 0)),
            pl.BlockSpec((SUBLANES, LANES), lambda i: (0, 0)),
        ),
        out_shape=(
            jax.ShapeDtypeStruct((n_tok, d), F32),
            jax.ShapeDtypeStruct((n_tok, SUBLANES), jnp.int32),
            jax.ShapeDtypeStruct((n_tok, SUBLANES), F32),
            jax.ShapeDtypeStruct((SUBLANES, LANES), F32),
        ),
        compiler_params=_cparams(("arbitrary",)),
        name="out_proj_ln1_router",
    )(x2, yconv, ynsa, wo, g1, b1, wr, br, tril)


def _row_copy(src_ref, src_row, dst_ref, dst_row, sem):
    return pltpu.make_async_copy(src_ref.at[pl.ds(src_row, 1), :], dst_ref.at[pl.ds(dst_row, 1), :], sem)


def _dispatch_kernel(pends_ref, nu_ref, dest_ref, h_ref, xs_ref, zero_ref, sem, zsem):
    tm = h_ref.shape[0]
    n_chunk = xs_ref.shape[0] // MOE_CHUNK

    @pl.when(pl.program_id(0) == 0)
    def _():
        zero_ref[...] = jnp.zeros_like(zero_ref)

        def zero_chunk(row0):
            row0 = pl.multiple_of(row0, MOE_CHUNK)
            return pltpu.make_async_copy(zero_ref, xs_ref.at[pl.ds(row0, MOE_CHUNK), :], zsem)

        def each_padded_chunk(act):
            def per_expert(e, _):
                end = pends_ref[e]
                start = jnp.where(e > 0, pends_ref[jnp.maximum(e - 1, 0)], 0)

                @pl.when(end > start)
                def _():
                    act(zero_chunk(end - MOE_CHUNK))
                return 0

            lax.fori_loop(0, N_EXPERTS, per_expert, 0)

            def per_dead_chunk(c, _):
                act(zero_chunk(c * MOE_CHUNK))
                return 0

            lax.fori_loop(nu_ref[0], n_chunk, per_dead_chunk, 0)

        each_padded_chunk(lambda cp: cp.start())
        each_padded_chunk(lambda cp: cp.wait())

    def issue(r, _):
        for slot in range(2):
            _row_copy(h_ref, r, xs_ref, dest_ref[2 * r + slot], sem).start()
        return 0

    lax.fori_loop(0, tm, issue, 0, unroll=ROW_UNROLL)
    for slot in range(2):
        pltpu.make_async_copy(h_ref, xs_ref.at[pl.ds(0, tm), :], sem).wait()


def _dispatch_call(pends, n_used, dest, h1, n_rows):
    n_tok, d = h1.shape
    tm = ROW_TM
    grid_spec = pltpu.PrefetchScalarGridSpec(
        num_scalar_prefetch=2,
        grid=(n_tok // tm,),
        in_specs=[
            pl.BlockSpec((2 * tm,), lambda i, pe, nu: (i,), memory_space=pltpu.SMEM),
            pl.BlockSpec((tm, d), lambda i, pe, nu: (i, 0)),
        ],
        out_specs=pl.BlockSpec(memory_space=pl.ANY),
        scratch_shapes=[pltpu.VMEM((MOE_CHUNK, d), F32), pltpu.SemaphoreType.DMA, pltpu.SemaphoreType.DMA],
    )
    return pl.pallas_call(
        _dispatch_kernel,
        grid_spec=grid_spec,
        out_shape=jax.ShapeDtypeStruct((n_rows, d), F32),
        compiler_params=_cparams(("arbitrary",)),
        name="moe_dispatch",
    )(pends, n_used, dest, h1)


def _expert_kernel(ce_ref, nu_ref, xs_ref, wi_ref, wo_ref, o_ref, wib_ref, wob_ref):
    c = pl.program_id(0)
    live = c < nu_ref[0]

    @pl.when(live & ((c == 0) | (ce_ref[c] != ce_ref[jnp.maximum(c - 1, 0)])))
    def _():
        wib_ref[...] = wi_ref[0].astype(BF16)
        wob_ref[...] = wo_ref[0].astype(BF16)

    @pl.when(live)
    def _():
        xb = xs_ref[...].astype(BF16)
        gu = jnp.dot(xb, wib_ref[...], preferred_element_type=F32)
        gate = gu[:, 0:D_EXPERT]
        act = gate * jax.nn.sigmoid(gate) * gu[:, D_EXPERT:2 * D_EXPERT]
        o_ref[...] = jnp.dot(act.astype(BF16), wob_ref[...], preferred_element_type=F32)

    @pl.when(jnp.logical_not(live))
    def _():
        o_ref[...] = jnp.zeros_like(o_ref)


def _expert_call(chunk_exp, n_used, xs, wi, wo):
    n_rows, d = xs.shape
    n_chunk = n_rows // MOE_CHUNK

    def live(c, nu):
        return jnp.maximum(jnp.minimum(c, nu[0] - 1), 0)

    grid_spec = pltpu.PrefetchScalarGridSpec(
        num_scalar_prefetch=2,
        grid=(n_chunk,),
        in_specs=[
            pl.BlockSpec((MOE_CHUNK, d), lambda c, ce, nu: (live(c, nu), 0)),
            pl.BlockSpec((1,) + wi.shape[1:], lambda c, ce, nu: (ce[live(c, nu)], 0, 0)),
            pl.BlockSpec((1,) + wo.shape[1:], lambda c, ce, nu: (ce[live(c, nu)], 0, 0)),
        ],
        out_specs=pl.BlockSpec((MOE_CHUNK, d), lambda c, ce, nu: (c, 0)),
        scratch_shapes=[pltpu.VMEM(wi.shape[1:], BF16), pltpu.VMEM(wo.shape[1:], BF16)],
    )
    return pl.pallas_call(
        _expert_kernel,
        grid_spec=grid_spec,
        out_shape=jax.ShapeDtypeStruct((n_rows, d), F32),
        compiler_params=_cparams(("arbitrary",)),
        name="moe_experts",
    )(chunk_exp, n_used, xs, wi, wo)


def _combine_kernel(dest_ref, dest_next_ref, h1_ref, rw_ref, p_ref, ys_ref,
                    g2_ref, b2_ref, wp_ref, wg_ref, bg_ref, g3_ref, b3_ref,
                    o_ref, rows_ref, sems):
    tm = h1_ref.shape[0]
    step = pl.program_id(0)
    cur = step % 2

    def gather(idx_ref, buf):
        def issue(r, _):
            for slot in range(2):
                _row_copy(ys_ref, idx_ref[2 * r + slot], rows_ref.at[buf, slot], r, sems.at[buf]).start()
            return 0

        lax.fori_loop(0, tm, issue, 0, unroll=ROW_UNROLL)

    @pl.when(step == 0)
    def _():
        gather(dest_ref, 0)

    @pl.when(step + 1 < pl.num_programs(0))
    def _():
        gather(dest_next_ref, 1 - cur)

    for slot in range(2):
        pltpu.make_async_copy(ys_ref.at[pl.ds(0, tm), :], rows_ref.at[cur, slot], sems.at[cur]).wait()

    rw = rw_ref[...]
    ffn = rw[:, 0:1] * rows_ref[cur, 0] + rw[:, 1:2] * rows_ref[cur, 1]
    h2 = _layer_norm(ALPHA * h1_ref[...] + ffn, g2_ref[...], b2_ref[...])
    emb = jnp.dot(p_ref[...].astype(BF16), wp_ref[...], preferred_element_type=F32)
    gate = jax.nn.sigmoid(jnp.dot(h2.astype(BF16), wg_ref[...], preferred_element_type=F32) + bg_ref[...])
    o_ref[...] = _layer_norm(ALPHA * h2 + emb * gate, g3_ref[...], b3_ref[...])


def _combine_call(dest, h1, rw, p2, ys, g2, b2, wp, wg, bg, g3, b3):
    n_tok, d = h1.shape
    tm = ROW_TM
    const = lambda a: pl.BlockSpec(a.shape, lambda i: (0, 0))
    return pl.pallas_call(
        _combine_kernel,
        grid=(n_tok // tm,),
        in_specs=[
            pl.BlockSpec((2 * tm,), lambda i: (i,), memory_space=pltpu.SMEM),
            pl.BlockSpec((2 * tm,), lambda i: (jnp.minimum(i + 1, n_tok // tm - 1),), memory_space=pltpu.SMEM),
            pl.BlockSpec((tm, d), lambda i: (i, 0)),
            pl.BlockSpec((tm, rw.shape[1]), lambda i: (i, 0)),
            pl.BlockSpec((tm, p2.shape[1]), lambda i: (i, 0)),
            pl.BlockSpec(memory_space=pl.ANY),
            const(g2), const(b2), const(wp), const(wg), const(bg), const(g3), const(b3),
        ],
        out_specs=pl.BlockSpec((tm, d), lambda i: (i, 0)),
        out_shape=jax.ShapeDtypeStruct((n_tok, d), F32),
        scratch_shapes=[pltpu.VMEM((2, 2, tm, d), F32), pltpu.SemaphoreType.DMA((2,))],
        compiler_params=_cparams(("arbitrary",)),
        name="moe_combine_ln_ple",
    )(dest, dest, h1, rw, p2, ys, g2, b2, wp, wg, bg, g3, b3)


def _gate_columns():
    cols = np.zeros((N_KV, GATE_ROWS), np.int32)
    live = np.zeros((N_KV, GATE_ROWS), np.float32)
    for h in range(N_KV):
        for br in range(3):
            for g in range(GQA):
                cols[h, br * GQA + g] = (h * GQA + g) * 3 + br
                live[h, br * GQA + g] = 1.0
    return cols.reshape(-1), live.reshape(-1)


def _layer(x, p, w_in, w_conv, pe_ck, w1_ck, w2_ck, pe_cv, w1_cv, w2_cv, w_out, ln1_g, ln1_b,
           w_rg, b_rg, w_re, b_re, w_e_in, w_e_out, ln2_g, ln2_b, w_ple, w_ple_gate, b_ple_gate,
           ln3_g, ln3_b):
    bsz, s, d = x.shape
    n_tok = bsz * s
    row = lambda v: v.reshape(1, -1)

    c_q, c_kc, c_vc, c_ks, c_vs, c_kw, c_vw, c_g = 1536, 2048, 2176, 2304, 2432, 2560, 2688, 2816
    wn = jnp.concatenate([w_in[:, 0:c_q], w_in[:, c_kc:c_ks], w_in[:, c_ks:c_vs], w_in[:, c_kw:c_vw]],
                         axis=1).astype(BF16)
    gcols, glive = _gate_columns()
    w_gate = w_in[:, c_g:c_g + 3 * N_HEADS][:, gcols] * glive[None, :]
    wt = jnp.concatenate([w_in[:, c_q:c_kc] * Q_SCALE, w_in[:, c_vs:c_kw], w_in[:, c_vw:c_g], w_gate],
                         axis=1).T.astype(BF16)

    yconv, cmpkv, kvn, qt, vst, vwt, gt = _proj_call(x, wn, wt, w_conv)

    def per_head(a):
        return a.reshape(bsz, s, N_KV, HEAD_DIM).transpose(0, 2, 1, 3)

    grp = CMP_STRIDE * HEAD_DIM
    ckv = jnp.stack([per_head(cmpkv[..., 0:128]), per_head(cmpkv[..., 128:256])], axis=1)
    ckv = ckv.reshape(bsz, 2, N_KV, s // CMP_STRIDE, grp)
    pes = jnp.stack([pe_ck.reshape(2, grp), pe_cv.reshape(2, grp)])
    w1s = jnp.stack([w1_ck, w1_cv]).astype(BF16)
    w2s = jnp.stack([w2_ck, w2_cv]).astype(BF16)
    cmp_out = _compress_call(ckv, pes, w1s, w2s)
    kcmp = cmp_out[:, 0].astype(BF16)
    vct = cmp_out[:, 1].transpose(0, 1, 3, 2).astype(BF16)

    n_sel = s // SEL_BLOCK
    n_cmp = s // CMP_STRIDE
    ka = -(-(HEAD_DIM + n_sel) // LANES) * LANES
    blk_onehot = np.zeros((s, ka - HEAD_DIM), np.float32)
    blk_onehot[np.arange(s), np.arange(s) // SEL_BLOCK] = 1.0
    ksa = jnp.concatenate([per_head(kvn[..., 0:128]),
                           jnp.broadcast_to(jnp.asarray(blk_onehot, BF16), (bsz, N_KV, s, ka - HEAD_DIM))],
                          axis=-1)
    kw = per_head(kvn[..., 128:256])
    ynsa = _attn_call(qt, gt, kcmp, vct, ksa, vst, kw, vwt)

    wr = jnp.zeros((d, LANES), F32).at[:, 0:N_GROUPS].set(w_rg)
    wr = wr.at[:, ROUTE_LANE0:ROUTE_LANE0 + N_EXPERTS].set(w_re)
    br = jnp.zeros((1, LANES), F32).at[0, 0:N_GROUPS].set(b_rg)
    br = br.at[0, ROUTE_LANE0:ROUTE_LANE0 + N_EXPERTS].set(b_re)
    wr_hi = wr.astype(BF16)
    wr_lo = (wr - wr_hi.astype(F32)).astype(BF16)
    wr = jnp.concatenate([jnp.concatenate([wr_hi, wr_lo], axis=1),
                          jnp.concatenate([wr_hi, jnp.zeros_like(wr_lo)], axis=1)], axis=0)
    tril = jnp.asarray(np.tril(np.ones((POST_TM, POST_TM), np.float32), -1), BF16)
    h1, ri, rw, cnt = _post_call(x.reshape(n_tok, d), yconv.reshape(n_tok, -1), ynsa.reshape(n_tok, -1),
                                 w_out.astype(BF16), row(ln1_g), row(ln1_b), wr, br, tril)

    counts = cnt[0, ROUTE_LANE0:ROUTE_LANE0 + N_EXPERTS].astype(jnp.int32)
    pcounts = (counts + MOE_CHUNK - 1) // MOE_CHUNK * MOE_CHUNK
    pends = jnp.cumsum(pcounts)
    pstarts = pends - pcounts
    dest = (pstarts[ri[:, 0:2]] + ri[:, 2:4]).reshape(-1).astype(jnp.int32)
    n_asg = n_tok * 2
    n_chunk = -(-n_asg // MOE_CHUNK) + N_EXPERTS
    chunk_row0 = jnp.arange(n_chunk, dtype=jnp.int32) * MOE_CHUNK
    chunk_exp = jnp.minimum(jnp.sum(pends[None, :] <= chunk_row0[:, None], axis=1), N_EXPERTS - 1).astype(jnp.int32)
    n_used = (pends[-1:] // MOE_CHUNK).astype(jnp.int32)

    xs = _dispatch_call(pends.astype(jnp.int32), n_used, dest, h1, n_chunk * MOE_CHUNK)
    ys = _expert_call(chunk_exp, n_used, xs, w_e_in, w_e_out)
    out = _combine_call(dest, h1, rw, p.reshape(n_tok, -1), ys, row(ln2_g), row(ln2_b),
                        w_ple.astype(BF16), w_ple_gate.astype(BF16), row(b_ple_gate), row(ln3_g), row(ln3_b))
    return out.reshape(bsz, s, d)


def kernel(x, p, w_in, w_conv, pe_ck, w1_ck, w2_ck, pe_cv, w1_cv, w2_cv, w_out, ln1_g, ln1_b, w_rg, b_rg, w_re, b_re, w_e_in, w_e_out, ln2_g, ln2_b, w_ple, w_ple_gate, b_ple_gate, ln3_g, ln3_b):
    assert w_in.shape[0] == DEPTH, "residual scaling ALPHA is derived from DEPTH"
    h = x
    for i in range(DEPTH):
        h = _layer(h, p[i], w_in[i], w_conv[i], pe_ck[i], w1_ck[i], w2_ck[i], pe_cv[i], w1_cv[i], w2_cv[i],
                   w_out[i], ln1_g[i], ln1_b[i], w_rg[i], b_rg[i], w_re[i], b_re[i], w_e_in[i], w_e_out[i],
                   ln2_g[i], ln2_b[i], w_ple[i], w_ple_gate[i], b_ple_gate[i], ln3_g[i], ln3_b[i])
    return h
```

```python
import functools

import jax
import jax.numpy as jnp
import numpy as np
from jax import lax
from jax.experimental import pallas as pl
from jax.experimental.pallas import tpu as pltpu

F32 = jnp.float32
BF16 = jnp.bfloat16
HIGHEST = lax.Precision.HIGHEST

CONV_CH = 512
CONV_W = 3
N_HEADS = 8
HEAD_DIM = 64
N_KV = 2
GQA = N_HEADS // N_KV
CMP_BLOCK = 32
CMP_STRIDE = 16
CMP_HIDDEN = 2 * HEAD_DIM
SEL_BLOCK = 64
N_SEL = 16
WINDOW = 512
ATTN_SCALE = HEAD_DIM ** -0.5
Q_SCALE = ATTN_SCALE * float(np.log2(np.e))
FORCE_SCORE = 1e4
NEG_INF = -1e30
N_GROUPS = 4
EXP_PER_GROUP = 8
N_EXPERTS = N_GROUPS * EXP_PER_GROUP
D_EXPERT = 512
MOE_CHUNK = 256
DEPTH = 1
ALPHA = (2 * DEPTH) ** 0.25
LN_EPS = 1e-5

LANES = 128
SUBLANES = 8
VMEM_LIMIT = 56 * 1024 * 1024

PROJ_TM = 512
Q_TILE = 256
SEL_TK = 512
CMP_CHUNK = 128
WIN_KEYS = WINDOW + Q_TILE
POST_TM = 512
ROW_TM = 256
ROW_UNROLL = True
ROUTE_LANE0 = N_GROUPS
GATE_ROWS = 16
V_ROWS = HEAD_DIM + 16


def _cparams(sem, vmem=VMEM_LIMIT):
    return pltpu.CompilerParams(dimension_semantics=sem, vmem_limit_bytes=vmem)


def _layer_norm(v, g, b):
    mu = jnp.mean(v, axis=-1, keepdims=True)
    d = v - mu
    var = jnp.mean(d * d, axis=-1, keepdims=True)
    return d * lax.rsqrt(var + LN_EPS) * g + b


def _proj_kernel(x_ref, wn_ref, wt_ref, wc_ref,
                 yconv_ref, cmpkv_ref, kvn_ref, qt_ref, vst_ref, vwt_ref, gt_ref,
                 carry_ref):
    si = pl.program_id(1)
    tm = x_ref.shape[1]
    xb = x_ref[0].astype(BF16)
    zn = jnp.dot(xb, wn_ref[...], preferred_element_type=F32)
    zt = lax.dot_general(wt_ref[...], xb, (((1,), (1,)), ((), ())),
                         preferred_element_type=F32)

    cb = zn[:, 0:CONV_CH]
    u = zn[:, CONV_CH:2 * CONV_CH] * zn[:, 2 * CONV_CH:3 * CONV_CH]

    @pl.when(si == 0)
    def _():
        carry_ref[...] = jnp.zeros_like(carry_ref)

    prev = carry_ref[...]
    rows = lax.broadcasted_iota(jnp.int32, u.shape, 0)
    u1 = jnp.where(rows == 0, prev[7:8, :], pltpu.roll(u, 1, 0))
    u2 = jnp.where(rows == 0, prev[6:7, :],
                   jnp.where(rows == 1, prev[7:8, :], pltpu.roll(u, 2, 0)))
    w = wc_ref[...]
    yconv_ref[0] = (cb * (w[0:1, :] * u2 + w[1:2, :] * u1 + w[2:3, :] * u)).astype(BF16)
    carry_ref[...] = u[tm - SUBLANES:tm, :]

    cmpkv_ref[0] = zn[:, 1536:1792]
    kvn_ref[0] = zn[:, 1792:2048].astype(BF16)

    qt_ref[0] = zt[0:512, :].astype(BF16)
    ones = jnp.ones((V_ROWS - HEAD_DIM, LANES), BF16)
    for h in range(N_KV):
        for i in range(tm // LANES):
            cols = slice(i * LANES, (i + 1) * LANES)
            vst_ref[0, h, i, 0:HEAD_DIM, :] = zt[512 + h * 64:512 + (h + 1) * 64, cols].astype(BF16)
            vst_ref[0, h, i, HEAD_DIM:V_ROWS, :] = ones
            vwt_ref[0, h, i, 0:HEAD_DIM, :] = zt[640 + h * 64:640 + (h + 1) * 64, cols].astype(BF16)
            vwt_ref[0, h, i, HEAD_DIM:V_ROWS, :] = ones
    gt_ref[0] = jax.nn.sigmoid(zt[768:800, :])


def _proj_call(x, wn, wt, wc):
    bsz, s, d = x.shape
    tm = PROJ_TM
    nblk = s // LANES
    grid = (bsz, s // tm)
    out_shape = (
        jax.ShapeDtypeStruct((bsz, s, CONV_CH), BF16),
        jax.ShapeDtypeStruct((bsz, s, 256), F32),
        jax.ShapeDtypeStruct((bsz, s, 256), BF16),
        jax.ShapeDtypeStruct((bsz, 512, s), BF16),
        jax.ShapeDtypeStruct((bsz, N_KV, nblk, V_ROWS, LANES), BF16),
        jax.ShapeDtypeStruct((bsz, N_KV, nblk, V_ROWS, LANES), BF16),
        jax.ShapeDtypeStruct((bsz, N_KV * GATE_ROWS, s), F32),
    )
    vspec = pl.BlockSpec((1, N_KV, tm // LANES, V_ROWS, LANES), lambda b, i: (b, 0, i, 0, 0))
    return pl.pallas_call(
        _proj_kernel,
        grid=grid,
        in_specs=[
            pl.BlockSpec((1, tm, d), lambda b, i: (b, i, 0)),
            pl.BlockSpec(wn.shape, lambda b, i: (0, 0)),
            pl.BlockSpec(wt.shape, lambda b, i: (0, 0)),
            pl.BlockSpec(wc.shape, lambda b, i: (0, 0)),
        ],
        out_specs=(
            pl.BlockSpec((1, tm, CONV_CH), lambda b, i: (b, i, 0)),
            pl.BlockSpec((1, tm, 256), lambda b, i: (b, i, 0)),
            pl.BlockSpec((1, tm, 256), lambda b, i: (b, i, 0)),
            pl.BlockSpec((1, 512, tm), lambda b, i: (b, 0, i)),
            vspec, vspec,
            pl.BlockSpec((1, N_KV * GATE_ROWS, tm), lambda b, i: (b, 0, i)),
        ),
        out_shape=out_shape,
        scratch_shapes=[pltpu.VMEM((SUBLANES, CONV_CH), F32)],
        compiler_params=_cparams(("arbitrary", "arbitrary")),
        name="in_proj_conv",
    )(x, wn, wt, wc)


def _compress_kernel(g_ref, pe_ref, w1_ref, w2_ref, o_ref):
    g = g_ref[0, 0, 0]
    pe = pe_ref[0]
    half = g.shape[1]
    a_lo = jnp.dot((g + pe[0:1, :]).astype(BF16), w1_ref[0, 0:half, :], preferred_element_type=F32)
    a_hi = jnp.dot((g + pe[1:2, :]).astype(BF16), w1_ref[0, half:2 * half, :], preferred_element_type=F32)
    n = g.shape[0]
    hid = a_lo + pltpu.roll(a_hi, n - 1, 0)
    act = jax.nn.gelu(hid)
    o_ref[0, 0, 0] = jnp.dot(act.astype(BF16), w2_ref[0], preferred_element_type=F32)


def _compress_call(ckv, pes, w1s, w2s):
    bsz, _, _, n, width = ckv.shape
    return pl.pallas_call(
        _compress_kernel,
        grid=(bsz, 2, N_KV),
        in_specs=[
            pl.BlockSpec((1, 1, 1, n, width), lambda b, k, h: (b, k, h, 0, 0)),
            pl.BlockSpec((1, 2, width), lambda b, k, h: (k, 0, 0)),
            pl.BlockSpec((1, 2 * width, CMP_HIDDEN), lambda b, k, h: (k, 0, 0)),
            pl.BlockSpec((1, CMP_HIDDEN, HEAD_DIM), lambda b, k, h: (k, 0, 0)),
        ],
        out_specs=pl.BlockSpec((1, 1, 1, n, HEAD_DIM), lambda b, k, h: (b, k, h, 0, 0)),
        out_shape=jax.ShapeDtypeStruct((bsz, 2, N_KV, n, HEAD_DIM), F32),
        compiler_params=_cparams(("arbitrary", "arbitrary", "arbitrary")),
        name="compress_mlp",
    )(ckv, pes, w1s, w2s)


def _attn_kernel(n_sel, n_top,
                 q_ref, g_ref, kc_ref, vct_ref, ksa_ref, vst_ref, kw_ref, vwt_ref,
                 o_ref, qa_ref, ps_ref, oc_ref, s_ref):
    qb = pl.program_id(2)
    q0 = qb * Q_TILE
    nq = GQA * Q_TILE
    blk = q_ref[0]
    q4 = jnp.concatenate([blk[g * HEAD_DIM:(g + 1) * HEAD_DIM, :] for g in range(GQA)], axis=1)
    lane = lax.broadcasted_iota(jnp.int32, (1, nq), 1)
    t4 = q0 + (lane & (Q_TILE - 1))

    n_cmp = kc_ref.shape[2]
    chunk = min(CMP_CHUNK, n_cmp)
    q_lane_tiles = Q_TILE // LANES
    for h in range(q_lane_tiles):
        ps_ref[h, 0:SUBLANES, :] = jnp.zeros((SUBLANES, LANES), F32)

    def cmp_branch(n):
        sc = jnp.dot(kc_ref[0, 0, 0:n, :], q4, preferred_element_type=F32)
        last_c = jnp.right_shift(t4 - (CMP_BLOCK - 1), CMP_STRIDE.bit_length() - 1)
        scm = jnp.where(lax.broadcasted_iota(jnp.int32, (n, nq), 0) <= last_c, sc, NEG_INF)
        m_c = jnp.max(scm, axis=0, keepdims=True)
        e_c = jnp.exp2(scm - m_c)
        l_c = jnp.sum(e_c, axis=0, keepdims=True)
        p_c = e_c * jnp.where(last_c >= 0, 1.0 / l_c, 0.0)
        oc_ref[...] = jnp.dot(vct_ref[0, 0, :, 0:n], p_c.astype(BF16), preferred_element_type=F32)
        psum = (p_c[:, 0:Q_TILE] + p_c[:, Q_TILE:2 * Q_TILE]
                + p_c[:, 2 * Q_TILE:3 * Q_TILE] + p_c[:, 3 * Q_TILE:4 * Q_TILE])
        for h in range(q_lane_tiles):
            ps_ref[h, SUBLANES:SUBLANES + n, :] = psum[:, h * LANES:(h + 1) * LANES]
            if n < n_cmp:
                ps_ref[h, SUBLANES + n:, :] = jnp.zeros((n_cmp - n, LANES), F32)

    last_visible = (q0 + Q_TILE - CMP_BLOCK) // CMP_STRIDE
    live_chunks = last_visible // chunk + 1
    for k in range(1, n_cmp // chunk + 1):
        pl.when(live_chunks == k)(functools.partial(cmp_branch, k * chunk))
    o_cmp = oc_ref[...]

    kstart = pl.multiple_of(jnp.maximum(q0 - WINDOW, 0), Q_TILE)
    sw = jnp.dot(kw_ref[0, 0, pl.ds(kstart, WIN_KEYS), :], q4, preferred_element_type=F32)
    dist = t4 - (kstart + lax.broadcasted_iota(jnp.int32, (WIN_KEYS, nq), 0))
    mask_w = (dist >= 0) & (dist < WINDOW)
    swm = jnp.where(mask_w, sw, NEG_INF)
    m_w = jnp.max(swm, axis=0, keepdims=True)
    p_w = jnp.exp2(swm - m_w)
    wblk = kstart // LANES
    vwt = jnp.concatenate([vwt_ref[0, 0, wblk + i] for i in range(WIN_KEYS // LANES)], axis=1)
    acc_w = jnp.dot(vwt, p_w.astype(BF16), preferred_element_type=F32)
    o_win = acc_w[0:HEAD_DIM, :] * (1.0 / acc_w[HEAD_DIM:HEAD_DIM + 1, :])

    per_sel = SEL_BLOCK // CMP_STRIDE
    def importance(h):
        acc = ps_ref[h, pl.ds(SUBLANES - 1, n_sel, stride=per_sel), :]
        for off in range(CMP_BLOCK // CMP_STRIDE + per_sel - 2):
            acc = acc + ps_ref[h, pl.ds(SUBLANES + off, n_sel, stride=per_sel), :]
        return acc

    imp = jnp.concatenate([importance(h) for h in range(q_lane_tiles)], axis=1)

    jidx = lax.broadcasted_iota(jnp.int32, (n_sel, Q_TILE), 0)
    tq = q0 + lax.broadcasted_iota(jnp.int32, (n_sel, Q_TILE), 1)
    jt = jnp.right_shift(tq, SEL_BLOCK.bit_length() - 1)
    forced = (jidx == 0) | (jidx == jt) | (jidx == jt - 1)
    score = jnp.where(forced, FORCE_SCORE, imp)
    key = jnp.where(jidx > jt, -1, lax.bitcast_convert_type(score, jnp.int32))
    theta = jnp.zeros((1, Q_TILE), jnp.int32)
    for bit in range(30, -1, -1):
        cand = theta | (1 << bit)
        reach = jnp.sum((key >= cand).astype(jnp.int32), axis=0, keepdims=True)
        theta = jnp.where(reach >= n_top, cand, theta)
    above = key > theta
    tied = key == theta
    n_above = jnp.sum(above.astype(jnp.int32), axis=0, keepdims=True)
    lower = (lax.broadcasted_iota(jnp.int32, (n_sel, n_sel), 1)
             < lax.broadcasted_iota(jnp.int32, (n_sel, n_sel), 0)).astype(BF16)
    tied_before = jnp.dot(lower, tied.astype(BF16), preferred_element_type=F32)
    chosen = above | (tied & (tied_before < (n_top - n_above).astype(F32)))
    bias = jnp.where(chosen, 0.0, NEG_INF).astype(BF16)
    qa_ref[0:HEAD_DIM, :] = q4
    qa_ref[HEAD_DIM:HEAD_DIM + n_sel, :] = jnp.concatenate([bias] * GQA, axis=1)
    if qa_ref.shape[0] > HEAD_DIM + n_sel:
        qa_ref[HEAD_DIM + n_sel:, :] = jnp.zeros((qa_ref.shape[0] - HEAD_DIM - n_sel, nq), BF16)

    nsub = SEL_TK // LANES

    def scores(c, buf):
        k0 = pl.multiple_of(c * SEL_TK, SEL_TK)
        s = jnp.dot(ksa_ref[0, 0, pl.ds(k0, SEL_TK), :], qa_ref[...], preferred_element_type=F32)
        s_ref[buf] = s
        return jnp.max(s, axis=0, keepdims=True)

    def accumulate(c, buf, mx, m, acc, causal):
        s = s_ref[buf]
        if causal:
            kpos = c * SEL_TK + lax.broadcasted_iota(jnp.int32, (SEL_TK, nq), 0)
            s = jnp.where(kpos <= t4, s, NEG_INF)
            mx = jnp.max(s, axis=0, keepdims=True)
        m_new = jnp.maximum(m, mx)
        p = jnp.exp2(s - m_new)
        vt = jnp.concatenate([vst_ref[0, 0, c * nsub + i] for i in range(nsub)], axis=1)
        acc = jnp.exp2(m - m_new) * acc + jnp.dot(vt, p.astype(BF16), preferred_element_type=F32)
        return m_new, acc

    def pair_body(pi, carry):
        mx0, m, acc = carry
        c = 2 * pi
        mx1 = scores(c + 1, 1)
        m, acc = accumulate(c, 0, mx0, m, acc, False)
        mx0 = scores(c + 2, 0)
        m, acc = accumulate(c + 1, 1, mx1, m, acc, False)
        return mx0, m, acc

    n_full = qb // (SEL_TK // Q_TILE)
    init = (scores(0, 0), jnp.full((1, nq), NEG_INF, F32), jnp.zeros((V_ROWS, nq), F32))
    mx0, m_s, acc_s = lax.fori_loop(0, n_full // 2, pair_body, init)
    c_last = 2 * (n_full // 2)

    def leftover_then_own():
        mx1 = scores(c_last + 1, 1)
        m1, acc1 = accumulate(c_last, 0, mx0, m_s, acc_s, False)
        return accumulate(c_last + 1, 1, mx1, m1, acc1, True)[1]

    def own_only():
        return accumulate(c_last, 0, mx0, m_s, acc_s, True)[1]

    acc_s = lax.cond(n_full % 2 == 1, leftover_then_own, own_only)
    o_sel = acc_s[0:HEAD_DIM, :] * (1.0 / acc_s[HEAD_DIM:HEAD_DIM + 1, :])

    gt = g_ref[0]

    def gate(br):
        return jnp.concatenate([gt[br * GQA + g:br * GQA + g + 1, :] for g in range(GQA)], axis=1)

    ot = gate(0) * o_cmp + gate(1) * o_sel + gate(2) * o_win
    stacked = jnp.concatenate([ot[:, g * Q_TILE:(g + 1) * Q_TILE] for g in range(GQA)], axis=0)
    o_ref[0] = stacked.T.astype(BF16)


def _attn_call(qt, gt, kcmp, vct, ksa, vst, kw, vwt):
    bsz, _, s = qt.shape
    n_sel = s // SEL_BLOCK
    n_top = min(N_SEL, n_sel)
    n_cmp = kcmp.shape[2]
    ka = ksa.shape[3]
    nblk = s // LANES
    grid = (bsz, N_KV, s // Q_TILE)
    kern = functools.partial(_attn_kernel, n_sel, n_top)
    return pl.pallas_call(
        kern,
        grid=grid,
        in_specs=[
            pl.BlockSpec((1, GQA * HEAD_DIM, Q_TILE), lambda b, h, i: (b, h, i)),
            pl.BlockSpec((1, GATE_ROWS, Q_TILE), lambda b, h, i: (b, h, i)),
            pl.BlockSpec((1, 1, n_cmp, HEAD_DIM), lambda b, h, i: (b, h, 0, 0)),
            pl.BlockSpec((1, 1, HEAD_DIM, n_cmp), lambda b, h, i: (b, h, 0, 0)),
            pl.BlockSpec((1, 1, s, ka), lambda b, h, i: (b, h, 0, 0)),
            pl.BlockSpec((1, 1, nblk, V_ROWS, LANES), lambda b, h, i: (b, h, 0, 0, 0)),
            pl.BlockSpec((1, 1, s, HEAD_DIM), lambda b, h, i: (b, h, 0, 0)),
            pl.BlockSpec((1, 1, nblk, V_ROWS, LANES), lambda b, h, i: (b, h, 0, 0, 0)),
        ],
        out_specs=pl.BlockSpec((1, Q_TILE, GQA * HEAD_DIM), lambda b, h, i: (b, i, h)),
        out_shape=jax.ShapeDtypeStruct((bsz, s, N_HEADS * HEAD_DIM), BF16),
        scratch_shapes=[pltpu.VMEM((ka, GQA * Q_TILE), BF16),
                        pltpu.VMEM((Q_TILE // LANES, n_cmp + SUBLANES, LANES), F32),
                        pltpu.VMEM((HEAD_DIM, GQA * Q_TILE), F32),
                        pltpu.VMEM((2, SEL_TK, GQA * Q_TILE), F32)],
        compiler_params=_cparams(("arbitrary", "arbitrary", "arbitrary")),
        name="nsa_attention",
    )(qt, gt, kcmp, vct, ksa, vst, kw, vwt)


def _post_kernel(x_ref, yc_ref, yn_ref, wo_ref, g1_ref, b1_ref, wr_ref, br_ref, tril_ref,
                 h1_ref, hp_ref, ri_ref, rw_ref, cnt_ref):
    step = pl.program_id(0)
    half = yc_ref.shape[1]
    mix = (jnp.dot(yc_ref[...], wo_ref[0:half, :], preferred_element_type=F32)
           + jnp.dot(yn_ref[...], wo_ref[half:2 * half, :], preferred_element_type=F32))
    h1 = _layer_norm(ALPHA * x_ref[...] + mix, g1_ref[...], b1_ref[...])
    h1_ref[...] = h1

    h_hi = h1.astype(BF16)
    h_hi32 = h_hi.astype(F32)
    h_lo = (h1 - h_hi32).astype(BF16)
    bits = lax.bitcast_convert_type(h_hi32, jnp.uint32)
    dh = bits.shape[1] // 2
    hp_ref[...] = jnp.right_shift(bits[:, 0:dh], jnp.uint32(16)) | bits[:, dh:2 * dh]
    parts = jnp.dot(jnp.concatenate([h_hi, h_lo], axis=1), wr_ref[...], preferred_element_type=F32)
    logits = parts[:, 0:LANES] + parts[:, LANES:2 * LANES] + br_ref[...]
    tm = logits.shape[0]
    lane = lax.broadcasted_iota(jnp.int32, (tm, LANES), 1)
    big = jnp.int32(LANES)

    def masked_softmax(valid):
        lg = jnp.where(valid, logits, NEG_INF)
        mx = jnp.max(lg, axis=1, keepdims=True)
        ex = jnp.where(valid, jnp.exp(lg - mx), 0.0)
        return ex / jnp.sum(ex, axis=1, keepdims=True)

    def first_max(vals, valid):
        top = jnp.max(jnp.where(valid, vals, -1.0), axis=1, keepdims=True)
        idx = jnp.min(jnp.where(valid & (vals == top), lane, big), axis=1, keepdims=True)
        return top, idx

    is_group = lane < N_GROUPS
    gp, gsel = first_max(masked_softmax(is_group), is_group)
    lo = ROUTE_LANE0 + EXP_PER_GROUP * gsel
    in_group = (lane >= lo) & (lane < lo + EXP_PER_GROUP)
    eprob = masked_softmax(in_group)
    p1, i1 = first_max(eprob, in_group)
    p2, i2 = first_max(eprob, in_group & (lane != i1))
    den = p1 + p2
    w1 = gp * (p1 / den)
    w2 = gp * (p2 / den)

    @pl.when(step == 0)
    def _():
        cnt_ref[...] = jnp.zeros_like(cnt_ref)

    onehot = (lane == i1) | (lane == i2)
    before = jnp.dot(tril_ref[...], onehot.astype(BF16), preferred_element_type=F32)
    rk = before + cnt_ref[0:1, :]
    r1 = jnp.sum(jnp.where(lane == i1, rk, 0.0), axis=1, keepdims=True)
    r2 = jnp.sum(jnp.where(lane == i2, rk, 0.0), axis=1, keepdims=True)
    cnt_ref[...] = cnt_ref[...] + jnp.sum(onehot.astype(F32), axis=0, keepdims=True)

    col = lax.broadcasted_iota(jnp.int32, (tm, ri_ref.shape[1]), 1)
    ri_ref[...] = jnp.where(col == 0, i1 - ROUTE_LANE0,
                            jnp.where(col == 1, i2 - ROUTE_LANE0,
                                      jnp.where(col == 2, r1.astype(jnp.int32),
                                                jnp.where(col == 3, r2.astype(jnp.int32), 0))))
    rw_ref[...] = jnp.where(col == 0, w1, jnp.where(col == 1, w2, 0.0))


def _post_call(x2, yconv, ynsa, wo, g1, b1, wr, br, tril):
    n_tok, d = x2.shape
    tm = POST_TM
    const = lambda a: pl.BlockSpec(a.shape, lambda i: (0, 0))
    return pl.pallas_call(
        _post_kernel,
        grid=(n_tok // tm,),
        in_specs=[
            pl.BlockSpec((tm, d), lambda i: (i, 0)),
            pl.BlockSpec((tm, yconv.shape[1]), lambda i: (i, 0)),
            pl.BlockSpec((tm, ynsa.shape[1]), lambda i: (i, 0)),
            const(wo), const(g1), const(b1), const(wr), const(br), const(tril),
        ],
        out_specs=(
            pl.BlockSpec((tm, d), lambda i: (i, 0)),
            pl.BlockSpec((tm, d // 2), lambda i: (i, 0)),
            pl.BlockSpec((tm, SUBLANES), lambda i: (i, 0)),
            pl.BlockSpec((tm, SUBLANES), lambda i: (i, 0)),
            pl.BlockSpec((SUBLANES, LANES), lambda i: (0, 0)),
        ),
        out_shape=(
            jax.ShapeDtypeStruct((n_tok, d), F32),
            jax.ShapeDtypeStruct((n_tok, d // 2), jnp.uint32),
            jax.ShapeDtypeStruct((n_tok, SUBLANES), jnp.int32),
            jax.ShapeDtypeStruct((n_tok, SUBLANES), F32),
            jax.ShapeDtypeStruct((SUBLANES, LANES), F32),
        ),
        compiler_params=_cparams(("arbitrary",)),
        name="out_proj_ln1_router",
    )(x2, yconv, ynsa, wo, g1, b1, wr, br, tril)


def _row_copy(src_ref, src_row, dst_ref, dst_row, sem):
    return pltpu.make_async_copy(src_ref.at[pl.ds(src_row, 1), :], dst_ref.at[pl.ds(dst_row, 1), :], sem)


def _dispatch_kernel(pends_ref, nu_ref, dest_ref, h_ref, xs_ref, zero_ref, sem, zsem):
    tm = h_ref.shape[0]
    n_chunk = xs_ref.shape[0] // MOE_CHUNK

    @pl.when(pl.program_id(0) == 0)
    def _():
        zero_ref[...] = jnp.zeros_like(zero_ref)

        def zero_chunk(row0):
            row0 = pl.multiple_of(row0, MOE_CHUNK)
            return pltpu.make_async_copy(zero_ref, xs_ref.at[pl.ds(row0, MOE_CHUNK), :], zsem)

        def each_padded_chunk(act):
            def per_expert(e, _):
                end = pends_ref[e]
                start = jnp.where(e > 0, pends_ref[jnp.maximum(e - 1, 0)], 0)

                @pl.when(end > start)
                def _():
                    act(zero_chunk(end - MOE_CHUNK))
                return 0

            lax.fori_loop(0, N_EXPERTS, per_expert, 0)

            def per_dead_chunk(c, _):
                act(zero_chunk(c * MOE_CHUNK))
                return 0

            lax.fori_loop(nu_ref[0], n_chunk, per_dead_chunk, 0)

        each_padded_chunk(lambda cp: cp.start())
        each_padded_chunk(lambda cp: cp.wait())

    def issue(r, _):
        for slot in range(2):
            _row_copy(h_ref, r, xs_ref, dest_ref[2 * r + slot], sem).start()
        return 0

    lax.fori_loop(0, tm, issue, 0, unroll=ROW_UNROLL)
    for slot in range(2):
        pltpu.make_async_copy(h_ref, xs_ref.at[pl.ds(0, tm), :], sem).wait()


def _dispatch_call(pends, n_used, dest, h1, n_rows):
    n_tok, d = h1.shape
    tm = ROW_TM
    grid_spec = pltpu.PrefetchScalarGridSpec(
        num_scalar_prefetch=2,
        grid=(n_tok // tm,),
        in_specs=[
            pl.BlockSpec((2 * tm,), lambda i, pe, nu: (i,), memory_space=pltpu.SMEM),
            pl.BlockSpec((tm, d), lambda i, pe, nu: (i, 0)),
        ],
        out_specs=pl.BlockSpec(memory_space=pl.ANY),
        scratch_shapes=[pltpu.VMEM((MOE_CHUNK, d), h1.dtype), pltpu.SemaphoreType.DMA, pltpu.SemaphoreType.DMA],
    )
    return pl.pallas_call(
        _dispatch_kernel,
        grid_spec=grid_spec,
        out_shape=jax.ShapeDtypeStruct((n_rows, d), h1.dtype),
        compiler_params=_cparams(("arbitrary",)),
        name="moe_dispatch",
    )(pends, n_used, dest, h1)


def _expert_kernel(ce_ref, nu_ref, xs_ref, wi_ref, wo_ref, o_ref, wib_ref, wob_ref):
    c = pl.program_id(0)
    live = c < nu_ref[0]

    @pl.when(live & ((c == 0) | (ce_ref[c] != ce_ref[jnp.maximum(c - 1, 0)])))
    def _():
        wib_ref[...] = wi_ref[0].astype(BF16)
        wob_ref[...] = wo_ref[0].astype(BF16)

    @pl.when(live)
    def _():
        pk = xs_ref[...]
        dh = pk.shape[1]
        x_lo = lax.bitcast_convert_type(jnp.left_shift(pk, jnp.uint32(16)), F32).astype(BF16)
        x_hi = lax.bitcast_convert_type(pk & jnp.uint32(0xFFFF0000), F32).astype(BF16)
        gu = (jnp.dot(x_lo, wib_ref[0:dh, :], preferred_element_type=F32)
              + jnp.dot(x_hi, wib_ref[dh:2 * dh, :], preferred_element_type=F32))
        gate = gu[:, 0:D_EXPERT]
        act = gate * jax.nn.sigmoid(gate) * gu[:, D_EXPERT:2 * D_EXPERT]
        o_ref[...] = jnp.dot(act.astype(BF16), wob_ref[...], preferred_element_type=F32)

    @pl.when(jnp.logical_not(live))
    def _():
        o_ref[...] = jnp.zeros_like(o_ref)


def _expert_call(chunk_exp, n_used, xs, wi, wo):
    n_rows, dh = xs.shape
    d = wo.shape[2]
    n_chunk = n_rows // MOE_CHUNK

    def live(c, nu):
        return jnp.maximum(jnp.minimum(c, nu[0] - 1), 0)

    grid_spec = pltpu.PrefetchScalarGridSpec(
        num_scalar_prefetch=2,
        grid=(n_chunk,),
        in_specs=[
            pl.BlockSpec((MOE_CHUNK, dh), lambda c, ce, nu: (live(c, nu), 0)),
            pl.BlockSpec((1,) + wi.shape[1:], lambda c, ce, nu: (ce[live(c, nu)], 0, 0)),
            pl.BlockSpec((1,) + wo.shape[1:], lambda c, ce, nu: (ce[live(c, nu)], 0, 0)),
        ],
        out_specs=pl.BlockSpec((MOE_CHUNK, d), lambda c, ce, nu: (c, 0)),
        scratch_shapes=[pltpu.VMEM(wi.shape[1:], BF16), pltpu.VMEM(wo.shape[1:], BF16)],
    )
    return pl.pallas_call(
        _expert_kernel,
        grid_spec=grid_spec,
        out_shape=jax.ShapeDtypeStruct((n_rows, d), F32),
        compiler_params=_cparams(("arbitrary",)),
        name="moe_experts",
    )(chunk_exp, n_used, xs, wi, wo)


def _combine_kernel(dest_ref, dest_next_ref, h1_ref, rw_ref, p_ref, ys_ref,
                    g2_ref, b2_ref, wp_ref, wg_ref, bg_ref, g3_ref, b3_ref,
                    o_ref, rows_ref, sems):
    tm = h1_ref.shape[0]
    step = pl.program_id(0)
    cur = step % 2

    def gather(idx_ref, buf):
        def issue(r, _):
            for slot in range(2):
                _row_copy(ys_ref, idx_ref[2 * r + slot], rows_ref.at[buf, slot], r, sems.at[buf]).start()
            return 0

        lax.fori_loop(0, tm, issue, 0, unroll=ROW_UNROLL)

    def await_rows(buf):
        for slot in range(2):
            pltpu.make_async_copy(ys_ref.at[pl.ds(0, tm), :], rows_ref.at[buf, slot], sems.at[buf]).wait()

    @pl.when(step == 0)
    def _():
        gather(dest_ref, 0)

    await_rows(cur)
    rw = rw_ref[...]
    ffn = rw[:, 0:1] * rows_ref[cur, 0] + rw[:, 1:2] * rows_ref[cur, 1]
    gather(dest_next_ref, 1 - cur)
    h2 =_layer_norm(ALPHA * h1_ref[...] + ffn, g2_ref[...], b2_ref[...])
    emb = jnp.dot(p_ref[...].astype(BF16), wp_ref[...], preferred_element_type=F32)
    gate = jax.nn.sigmoid(jnp.dot(h2.astype(BF16), wg_ref[...], preferred_element_type=F32) + bg_ref[...])
    o_ref[...] = _layer_norm(ALPHA * h2 + emb * gate, g3_ref[...], b3_ref[...])

    @pl.when(step == pl.num_programs(0) - 1)
    def _():
        await_rows(1 - cur)


def _combine_call(dest, h1, rw, p2, ys, g2, b2, wp, wg, bg, g3, b3):
    n_tok, d = h1.shape
    tm = ROW_TM
    const = lambda a: pl.BlockSpec(a.shape, lambda i: (0, 0))
    return pl.pallas_call(
        _combine_kernel,
        grid=(n_tok // tm,),
        in_specs=[
            pl.BlockSpec((2 * tm,), lambda i: (i,), memory_space=pltpu.SMEM),
            pl.BlockSpec((2 * tm,), lambda i: (jnp.minimum(i + 1, n_tok // tm - 1),), memory_space=pltpu.SMEM),
            pl.BlockSpec((tm, d), lambda i: (i, 0)),
            pl.BlockSpec((tm, rw.shape[1]), lambda i: (i, 0)),
            pl.BlockSpec((tm, p2.shape[1]), lambda i: (i, 0)),
            pl.BlockSpec(memory_space=pl.ANY),
            const(g2), const(b2), const(wp), const(wg), const(bg), const(g3), const(b3),
        ],
        out_specs=pl.BlockSpec((tm, d), lambda i: (i, 0)),
        out_shape=jax.ShapeDtypeStruct((n_tok, d), F32),
        scratch_shapes=[pltpu.VMEM((2, 2, tm, d), F32), pltpu.SemaphoreType.DMA((2,))],
        compiler_params=_cparams(("arbitrary",)),
        name="moe_combine_ln_ple",
    )(dest, dest, h1, rw, p2, ys, g2, b2, wp, wg, bg, g3, b3)


def _gate_columns():
    cols = np.zeros((N_KV, GATE_ROWS), np.int32)
    live = np.zeros((N_KV, GATE_ROWS), np.float32)
    for h in range(N_KV):
        for br in range(3):
            for g in range(GQA):
                cols[h, br * GQA + g] = (h * GQA + g) * 3 + br
                live[h, br * GQA + g] = 1.0
    return cols.reshape(-1), live.reshape(-1)


def _layer(x, p, w_in, w_conv, pe_ck, w1_ck, w2_ck, pe_cv, w1_cv, w2_cv, w_out, ln1_g, ln1_b,
           w_rg, b_rg, w_re, b_re, w_e_in, w_e_out, ln2_g, ln2_b, w_ple, w_ple_gate, b_ple_gate,
           ln3_g, ln3_b):
    bsz, s, d = x.shape
    n_tok = bsz * s
    row = lambda v: v.reshape(1, -1)

    c_q, c_kc, c_vc, c_ks, c_vs, c_kw, c_vw, c_g = 1536, 2048, 2176, 2304, 2432, 2560, 2688, 2816
    wn = jnp.concatenate([w_in[:, 0:c_q], w_in[:, c_kc:c_ks], w_in[:, c_ks:c_vs], w_in[:, c_kw:c_vw]],
                         axis=1).astype(BF16)
    gcols, glive = _gate_columns()
    w_gate = w_in[:, c_g:c_g + 3 * N_HEADS][:, gcols] * glive[None, :]
    wt = jnp.concatenate([w_in[:, c_q:c_kc] * Q_SCALE, w_in[:, c_vs:c_kw], w_in[:, c_vw:c_g], w_gate],
                         axis=1).T.astype(BF16)

    yconv, cmpkv, kvn, qt, vst, vwt, gt = _proj_call(x, wn, wt, w_conv)

    def per_head(a):
        return a.reshape(bsz, s, N_KV, HEAD_DIM).transpose(0, 2, 1, 3)

    grp = CMP_STRIDE * HEAD_DIM
    ckv = jnp.stack([per_head(cmpkv[..., 0:128]), per_head(cmpkv[..., 128:256])], axis=1)
    ckv = ckv.reshape(bsz, 2, N_KV, s // CMP_STRIDE, grp)
    pes = jnp.stack([pe_ck.reshape(2, grp), pe_cv.reshape(2, grp)])
    w1s = jnp.stack([w1_ck, w1_cv]).astype(BF16)
    w2s = jnp.stack([w2_ck, w2_cv]).astype(BF16)
    cmp_out = _compress_call(ckv, pes, w1s, w2s)
    kcmp = cmp_out[:, 0].astype(BF16)
    vct = cmp_out[:, 1].transpose(0, 1, 3, 2).astype(BF16)

    n_sel = s // SEL_BLOCK
    n_cmp = s // CMP_STRIDE
    ka = -(-(HEAD_DIM + n_sel) // LANES) * LANES
    blk_onehot = np.zeros((s, ka - HEAD_DIM), np.float32)
    blk_onehot[np.arange(s), np.arange(s) // SEL_BLOCK] = 1.0
    ksa = jnp.concatenate([per_head(kvn[..., 0:128]),
                           jnp.broadcast_to(jnp.asarray(blk_onehot, BF16), (bsz, N_KV, s, ka - HEAD_DIM))],
                          axis=-1)
    kw = per_head(kvn[..., 128:256])
    ynsa = _attn_call(qt, gt, kcmp, vct, ksa, vst, kw, vwt)

    wr = jnp.zeros((d, LANES), F32).at[:, 0:N_GROUPS].set(w_rg)
    wr = wr.at[:, ROUTE_LANE0:ROUTE_LANE0 + N_EXPERTS].set(w_re)
    br = jnp.zeros((1, LANES), F32).at[0, 0:N_GROUPS].set(b_rg)
    br = br.at[0, ROUTE_LANE0:ROUTE_LANE0 + N_EXPERTS].set(b_re)
    wr_hi = wr.astype(BF16)
    wr_lo = (wr - wr_hi.astype(F32)).astype(BF16)
    wr = jnp.concatenate([jnp.concatenate([wr_hi, wr_lo], axis=1),
                          jnp.concatenate([wr_hi, jnp.zeros_like(wr_lo)], axis=1)], axis=0)
    tril = jnp.asarray(np.tril(np.ones((POST_TM, POST_TM), np.float32), -1), BF16)
    h1, h1_packed, ri, rw, cnt = _post_call(x.reshape(n_tok, d), yconv.reshape(n_tok, -1), ynsa.reshape(n_tok, -1),
                                 w_out.astype(BF16), row(ln1_g), row(ln1_b), wr, br, tril)

    counts = cnt[0, ROUTE_LANE0:ROUTE_LANE0 + N_EXPERTS].astype(jnp.int32)
    pcounts = (counts + MOE_CHUNK - 1) // MOE_CHUNK * MOE_CHUNK
    pends = jnp.cumsum(pcounts)
    pstarts = pends - pcounts
    dest = (pstarts[ri[:, 0:2]] + ri[:, 2:4]).reshape(-1).astype(jnp.int32)
    n_asg = n_tok * 2
    n_chunk = -(-n_asg // MOE_CHUNK) + N_EXPERTS
    chunk_row0 = jnp.arange(n_chunk, dtype=jnp.int32) * MOE_CHUNK
    chunk_exp = jnp.minimum(jnp.sum(pends[None, :] <= chunk_row0[:, None], axis=1), N_EXPERTS - 1).astype(jnp.int32)
    n_used = (pends[-1:] // MOE_CHUNK).astype(jnp.int32)

    xs = _dispatch_call(pends.astype(jnp.int32), n_used, dest, h1_packed, n_chunk * MOE_CHUNK)
    ys = _expert_call(chunk_exp, n_used, xs, w_e_in, w_e_out)
    out = _combine_call(dest, h1, rw, p.reshape(n_tok, -1), ys, row(ln2_g), row(ln2_b),
                        w_ple.astype(BF16), w_ple_gate.astype(BF16), row(b_ple_gate), row(ln3_g), row(ln3_b))
    return out.reshape(bsz, s, d)


def kernel(x, p, w_in, w_conv, pe_ck, w1_ck, w2_ck, pe_cv, w1_cv, w2_cv, w_out, ln1_g, ln1_b, w_rg, b_rg, w_re, b_re, w_e_in, w_e_out, ln2_g, ln2_b, w_ple, w_ple_gate, b_ple_gate, ln3_g, ln3_b):
    assert w_in.shape[0] == DEPTH, "residual scaling ALPHA is derived from DEPTH"
    h = x
    for i in range(DEPTH):
        h = _layer(h, p[i], w_in[i], w_conv[i], pe_ck[i], w1_ck[i], w2_ck[i], pe_cv[i], w1_cv[i], w2_cv[i],
                   w_out[i], ln1_g[i], ln1_b[i], w_rg[i], b_rg[i], w_re[i], b_re[i], w_e_in[i], w_e_out[i],
                   ln2_g[i], ln2_b[i], w_ple[i], w_ple_gate[i], b_ple_gate[i], ln3_g[i], ln3_b[i])
    return h
```

```python
import functools

import jax
import jax.numpy as jnp
import numpy as np
from jax import lax
from jax.experimental import pallas as pl
from jax.experimental.pallas import tpu as pltpu

F32 = jnp.float32
BF16 = jnp.bfloat16
HIGHEST = lax.Precision.HIGHEST

CONV_CH = 512
CONV_W = 3
N_HEADS = 8
HEAD_DIM = 64
N_KV = 2
GQA = N_HEADS // N_KV
CMP_BLOCK = 32
CMP_STRIDE = 16
CMP_HIDDEN = 2 * HEAD_DIM
SEL_BLOCK = 64
N_SEL = 16
WINDOW = 512
ATTN_SCALE = HEAD_DIM ** -0.5
Q_SCALE = ATTN_SCALE * float(np.log2(np.e))
FORCE_SCORE = 1e4
NEG_INF = -1e30
N_GROUPS = 4
EXP_PER_GROUP = 8
N_EXPERTS = N_GROUPS * EXP_PER_GROUP
D_EXPERT = 512
MOE_CHUNK = 256
DEPTH = 1
ALPHA = (2 * DEPTH) ** 0.25
LN_EPS = 1e-5

LANES = 128
SUBLANES = 8
VMEM_LIMIT = 56 * 1024 * 1024

PROJ_TM = 512
Q_TILE = 256
SEL_TK = 512
CMP_CHUNK = 128
WIN_KEYS = WINDOW + Q_TILE
POST_TM = 512
ROW_TM = 256
ROW_UNROLL = True
ROUTE_LANE0 = N_GROUPS
GATE_ROWS = 16
V_ROWS = HEAD_DIM + 16


def _cparams(sem, vmem=VMEM_LIMIT):
    return pltpu.CompilerParams(dimension_semantics=sem, vmem_limit_bytes=vmem)


def _layer_norm(v, g, b):
    mu = jnp.mean(v, axis=-1, keepdims=True)
    d = v - mu
    var = jnp.mean(d * d, axis=-1, keepdims=True)
    return d * lax.rsqrt(var + LN_EPS) * g + b


def _proj_kernel(x_ref, wn_ref, wt_ref, wc_ref,
                 yconv_ref, cmpkv_ref, ksa_ref, kw_ref, qt_ref, vst_ref, vwt_ref, gt_ref,
                 carry_ref):
    si = pl.program_id(1)
    tm = x_ref.shape[1]
    xb = x_ref[0].astype(BF16)
    zn = jnp.dot(xb, wn_ref[...], preferred_element_type=F32)
    zt = lax.dot_general(wt_ref[...], xb, (((1,), (1,)), ((), ())),
                         preferred_element_type=F32)

    cb = zn[:, 0:CONV_CH]
    u = zn[:, CONV_CH:2 * CONV_CH] * zn[:, 2 * CONV_CH:3 * CONV_CH]

    @pl.when(si == 0)
    def _():
        carry_ref[...] = jnp.zeros_like(carry_ref)

    prev = carry_ref[...]
    rows = lax.broadcasted_iota(jnp.int32, u.shape, 0)
    u1 = jnp.where(rows == 0, prev[7:8, :], pltpu.roll(u, 1, 0))
    u2 = jnp.where(rows == 0, prev[6:7, :],
                   jnp.where(rows == 1, prev[7:8, :], pltpu.roll(u, 2, 0)))
    w = wc_ref[...]
    yconv_ref[0] = (cb * (w[0:1, :] * u2 + w[1:2, :] * u1 + w[2:3, :] * u)).astype(BF16)
    carry_ref[...] = u[tm - SUBLANES:tm, :]

    cmpkv_ref[0, 0] = zn[:, 1536:1664]
    cmpkv_ref[0, 1] = zn[:, 1664:1792]
    kw_ref[0] = zn[:, 1920:2048].astype(BF16)
    ksa_ref[0, :, 0:LANES] = zn[:, 1792:1920].astype(BF16)
    n_hot = ksa_ref.shape[2] - LANES
    pos = si * tm + lax.broadcasted_iota(jnp.int32, (tm, n_hot), 0)
    hot = jnp.right_shift(pos, SEL_BLOCK.bit_length() - 1) == lax.broadcasted_iota(jnp.int32, (tm, n_hot), 1)
    ksa_ref[0, :, LANES:] = jnp.where(hot, 1.0, 0.0).astype(BF16)

    qt_ref[0] = zt[0:512, :].astype(BF16)
    ones = jnp.ones((V_ROWS - HEAD_DIM, LANES), BF16)
    for h in range(N_KV):
        for i in range(tm // LANES):
            cols = slice(i * LANES, (i + 1) * LANES)
            vst_ref[0, h, i, 0:HEAD_DIM, :] = zt[512 + h * 64:512 + (h + 1) * 64, cols].astype(BF16)
            vst_ref[0, h, i, HEAD_DIM:V_ROWS, :] = ones
            vwt_ref[0, h, i, 0:HEAD_DIM, :] = zt[640 + h * 64:640 + (h + 1) * 64, cols].astype(BF16)
            vwt_ref[0, h, i, HEAD_DIM:V_ROWS, :] = ones
    gt_ref[0] = jax.nn.sigmoid(zt[768:800, :])


def _proj_call(x, wn, wt, wc):
    bsz, s, d = x.shape
    tm = PROJ_TM
    nblk = s // LANES
    grid = (bsz, s // tm)
    ka = LANES + -(-(s // SEL_BLOCK) // LANES) * LANES
    out_shape = (
        jax.ShapeDtypeStruct((bsz, s, CONV_CH), BF16),
        jax.ShapeDtypeStruct((bsz, 2, s, LANES), F32),
        jax.ShapeDtypeStruct((bsz, s, ka), BF16),
        jax.ShapeDtypeStruct((bsz, s, LANES), BF16),
        jax.ShapeDtypeStruct((bsz, 512, s), BF16),
        jax.ShapeDtypeStruct((bsz, N_KV, nblk, V_ROWS, LANES), BF16),
        jax.ShapeDtypeStruct((bsz, N_KV, nblk, V_ROWS, LANES), BF16),
        jax.ShapeDtypeStruct((bsz, N_KV * GATE_ROWS, s), F32),
    )
    vspec = pl.BlockSpec((1, N_KV, tm // LANES, V_ROWS, LANES), lambda b, i: (b, 0, i, 0, 0))
    return pl.pallas_call(
        _proj_kernel,
        grid=grid,
        in_specs=[
            pl.BlockSpec((1, tm, d), lambda b, i: (b, i, 0)),
            pl.BlockSpec(wn.shape, lambda b, i: (0, 0)),
            pl.BlockSpec(wt.shape, lambda b, i: (0, 0)),
            pl.BlockSpec(wc.shape, lambda b, i: (0, 0)),
        ],
        out_specs=(
            pl.BlockSpec((1, tm, CONV_CH), lambda b, i: (b, i, 0)),
            pl.BlockSpec((1, 2, tm, LANES), lambda b, i: (b, 0, i, 0)),
            pl.BlockSpec((1, tm, ka), lambda b, i: (b, i, 0)),
            pl.BlockSpec((1, tm, LANES), lambda b, i: (b, i, 0)),
            pl.BlockSpec((1, 512, tm), lambda b, i: (b, 0, i)),
            vspec, vspec,
            pl.BlockSpec((1, N_KV * GATE_ROWS, tm), lambda b, i: (b, 0, i)),
        ),
        out_shape=out_shape,
        scratch_shapes=[pltpu.VMEM((SUBLANES, CONV_CH), F32)],
        compiler_params=_cparams(("arbitrary", "arbitrary")),
        name="in_proj_conv",
    )(x, wn, wt, wc)


def _compress_kernel(g_ref, pe_ref, w1_ref, w2_ref, o_ref):
    g = g_ref[0, 0]
    pe = pe_ref[0]
    half = g.shape[1]
    a_lo = jnp.dot((g + pe[0:1, :]).astype(BF16), w1_ref[0, 0:half, :], preferred_element_type=F32)
    a_hi = jnp.dot((g + pe[1:2, :]).astype(BF16), w1_ref[0, half:2 * half, :], preferred_element_type=F32)
    n = g.shape[0]
    hid = a_lo + pltpu.roll(a_hi, n - 1, 0)
    act = jax.nn.gelu(hid)
    o_ref[0, 0] = jnp.dot(act.astype(BF16), w2_ref[0], preferred_element_type=F32)


def _compress_call(ckv, pes, w1s, w2s):
    bsz, _, n, width = ckv.shape
    return pl.pallas_call(
        _compress_kernel,
        grid=(bsz, 2),
        in_specs=[
            pl.BlockSpec((1, 1, n, width), lambda b, k: (b, k, 0, 0)),
            pl.BlockSpec((1, 2, width), lambda b, k: (k, 0, 0)),
            pl.BlockSpec((1, 2 * width, N_KV * CMP_HIDDEN), lambda b, k: (k, 0, 0)),
            pl.BlockSpec((1, N_KV * CMP_HIDDEN, N_KV * HEAD_DIM), lambda b, k: (k, 0, 0)),
        ],
        out_specs=pl.BlockSpec((1, 1, n, N_KV * HEAD_DIM), lambda b, k: (b, k, 0, 0)),
        out_shape=jax.ShapeDtypeStruct((bsz, 2, n, N_KV * HEAD_DIM), F32),
        compiler_params=_cparams(("arbitrary", "arbitrary")),
        name="compress_mlp",
    )(ckv, pes, w1s, w2s)


def _attn_kernel(n_sel, n_top,
                 q_ref, g_ref, kc_ref, vct_ref, ksa_ref, vst_ref, kw_ref, vwt_ref,
                 o_ref, qa_ref, ps_ref, oc_ref, s_ref):
    qb = pl.program_id(2)
    q0 = qb * Q_TILE
    nq = GQA * Q_TILE
    blk = q_ref[0]
    q4 = jnp.concatenate([blk[g * HEAD_DIM:(g + 1) * HEAD_DIM, :] for g in range(GQA)], axis=1)
    lane = lax.broadcasted_iota(jnp.int32, (1, nq), 1)
    t4 = q0 + (lane & (Q_TILE - 1))
    kvh = pl.program_id(1)
    qa_ref[pl.ds(pl.multiple_of(kvh * HEAD_DIM, HEAD_DIM), HEAD_DIM), :] = q4
    qa_ref[pl.ds(pl.multiple_of((N_KV - 1 - kvh) * HEAD_DIM, HEAD_DIM), HEAD_DIM), :] = jnp.zeros_like(q4)
    q2 = qa_ref[0:N_KV * HEAD_DIM, :]

    n_cmp = kc_ref.shape[1]
    chunk = min(CMP_CHUNK, n_cmp)
    q_lane_tiles = Q_TILE // LANES
    for h in range(q_lane_tiles):
        ps_ref[h, 0:SUBLANES, :] = jnp.zeros((SUBLANES, LANES), F32)

    def cmp_branch(n):
        sc = jnp.dot(kc_ref[0, 0:n, :], q2, preferred_element_type=F32)
        last_c = jnp.right_shift(t4 - (CMP_BLOCK - 1), CMP_STRIDE.bit_length() - 1)
        scm = jnp.where(lax.broadcasted_iota(jnp.int32, (n, nq), 0) <= last_c, sc, NEG_INF)
        m_c = jnp.max(scm, axis=0, keepdims=True)
        e_c = jnp.exp2(scm - m_c)
        l_c = jnp.sum(e_c, axis=0, keepdims=True)
        p_c = e_c * jnp.where(last_c >= 0, 1.0 / l_c, 0.0)
        oc_ref[...] = jnp.dot(vct_ref[0, :, 0:n], p_c.astype(BF16), preferred_element_type=F32)
        psum = (p_c[:, 0:Q_TILE] + p_c[:, Q_TILE:2 * Q_TILE]
                + p_c[:, 2 * Q_TILE:3 * Q_TILE] + p_c[:, 3 * Q_TILE:4 * Q_TILE])
        for h in range(q_lane_tiles):
            ps_ref[h, SUBLANES:SUBLANES + n, :] = psum[:, h * LANES:(h + 1) * LANES]
            if n < n_cmp:
                ps_ref[h, SUBLANES + n:, :] = jnp.zeros((n_cmp - n, LANES), F32)

    last_visible = (q0 + Q_TILE - CMP_BLOCK) // CMP_STRIDE
    live_chunks = last_visible // chunk + 1
    for k in range(1, n_cmp // chunk + 1):
        pl.when(live_chunks == k)(functools.partial(cmp_branch, k * chunk))
    o_cmp = oc_ref[...]

    sw = jnp.dot(kw_ref[0, pl.ds(pl.multiple_of(q0, Q_TILE), WIN_KEYS), :], q2,
                 preferred_element_type=F32)
    ql = t4 - q0
    row = lax.broadcasted_iota(jnp.int32, (LANES, nq), 0)
    slabs = []
    for c in range(WIN_KEYS // LANES):
        slab = sw[c * LANES:(c + 1) * LANES, :]
        before_start = WINDOW - 1 - q0 - c * LANES
        if c * LANES < Q_TILE:
            slab = jnp.where(row > jnp.maximum(ql - c * LANES, before_start), slab, NEG_INF)
        elif (c + 1) * LANES > WINDOW:
            slab = jnp.where(row <= ql + (WINDOW - c * LANES), slab, NEG_INF)
        else:
            slab = jnp.where(row > before_start, slab, NEG_INF)
        slabs.append(slab)
    swm = jnp.concatenate(slabs, axis=0)
    m_w = jnp.max(swm, axis=0, keepdims=True)
    p_w = jnp.exp2(swm - m_w)
    wblk = q0 // LANES
    vwt = jnp.concatenate([vwt_ref[0, 0, wblk + i] for i in range(WIN_KEYS // LANES)], axis=1)
    acc_w = jnp.dot(vwt, p_w.astype(BF16), preferred_element_type=F32)
    o_win = acc_w[0:HEAD_DIM, :] * (1.0 / acc_w[HEAD_DIM:HEAD_DIM + 1, :])

    per_sel = SEL_BLOCK // CMP_STRIDE
    def importance(h):
        acc = ps_ref[h, pl.ds(SUBLANES - 1, n_sel, stride=per_sel), :]
        for off in range(CMP_BLOCK // CMP_STRIDE + per_sel - 2):
            acc = acc + ps_ref[h, pl.ds(SUBLANES + off, n_sel, stride=per_sel), :]
        return acc

    imp = jnp.concatenate([importance(h) for h in range(q_lane_tiles)], axis=1)

    jidx = lax.broadcasted_iota(jnp.int32, (n_sel, Q_TILE), 0)
    tq = q0 + lax.broadcasted_iota(jnp.int32, (n_sel, Q_TILE), 1)
    jt = jnp.right_shift(tq, SEL_BLOCK.bit_length() - 1)
    forced = (jidx == 0) | (jidx == jt) | (jidx == jt - 1)
    score = jnp.where(forced, FORCE_SCORE, imp)
    key = jnp.where(jidx > jt, -1, lax.bitcast_convert_type(score, jnp.int32))
    theta = jnp.zeros((1, Q_TILE), jnp.int32)
    for bit in range(30, -1, -1):
        cand = theta | (1 << bit)
        reach = jnp.sum((key >= cand).astype(jnp.int32), axis=0, keepdims=True)
        theta = jnp.where(reach >= n_top, cand, theta)
    above = key > theta
    tied = key == theta
    n_above = jnp.sum(above.astype(jnp.int32), axis=0, keepdims=True)
    lower = (lax.broadcasted_iota(jnp.int32, (n_sel, n_sel), 1)
             < lax.broadcasted_iota(jnp.int32, (n_sel, n_sel), 0)).astype(BF16)
    tied_before = jnp.dot(lower, tied.astype(BF16), preferred_element_type=F32)
    chosen = above | (tied & (tied_before < (n_top - n_above).astype(F32)))
    bias = jnp.where(chosen, 0.0, NEG_INF).astype(BF16)
    bias_row0 = N_KV * HEAD_DIM
    qa_ref[bias_row0:bias_row0 + n_sel, :] = jnp.concatenate([bias] * GQA, axis=1)
    if qa_ref.shape[0] > bias_row0 + n_sel:
        qa_ref[bias_row0 + n_sel:, :] = jnp.zeros((qa_ref.shape[0] - bias_row0 - n_sel, nq), BF16)

    nsub = SEL_TK // LANES

    def scores(c, buf):
        k0 = pl.multiple_of(c * SEL_TK, SEL_TK)
        s = jnp.dot(ksa_ref[0, pl.ds(k0, SEL_TK), :], qa_ref[...], preferred_element_type=F32)
        s_ref[buf] = s
        return jnp.max(s, axis=0, keepdims=True)

    def accumulate(c, buf, mx, m, acc, causal):
        s = s_ref[buf]
        if causal:
            kpos = c * SEL_TK + lax.broadcasted_iota(jnp.int32, (SEL_TK, nq), 0)
            s = jnp.where(kpos <= t4, s, NEG_INF)
            mx = jnp.max(s, axis=0, keepdims=True)
        m_new = jnp.maximum(m, mx)
        p = jnp.exp2(s - m_new)
        vt = jnp.concatenate([vst_ref[0, 0, c * nsub + i] for i in range(nsub)], axis=1)
        acc = jnp.exp2(m - m_new) * acc + jnp.dot(vt, p.astype(BF16), preferred_element_type=F32)
        return m_new, acc

    def pair_body(pi, carry):
        mx0, m, acc = carry
        c = 2 * pi
        mx1 = scores(c + 1, 1)
        m, acc = accumulate(c, 0, mx0, m, acc, False)
        mx0 = scores(c + 2, 0)
        m, acc = accumulate(c + 1, 1, mx1, m, acc, False)
        return mx0, m, acc

    n_full = qb // (SEL_TK // Q_TILE)
    init = (scores(0, 0), jnp.full((1, nq), NEG_INF, F32), jnp.zeros((V_ROWS, nq), F32))
    mx0, m_s, acc_s = lax.fori_loop(0, n_full // 2, pair_body, init)
    c_last = 2 * (n_full // 2)

    def leftover_then_own():
        mx1 = scores(c_last + 1, 1)
        m1, acc1 = accumulate(c_last, 0, mx0, m_s, acc_s, False)
        return accumulate(c_last + 1, 1, mx1, m1, acc1, True)[1]

    def own_only():
        return accumulate(c_last, 0, mx0, m_s, acc_s, True)[1]

    acc_s = lax.cond(n_full % 2 == 1, leftover_then_own, own_only)
    o_sel = acc_s[0:HEAD_DIM, :] * (1.0 / acc_s[HEAD_DIM:HEAD_DIM + 1, :])

    gt = g_ref[0]

    def gate(br):
        return jnp.concatenate([gt[br * GQA + g:br * GQA + g + 1, :] for g in range(GQA)], axis=1)

    ot = gate(0) * o_cmp + gate(1) * o_sel + gate(2) * o_win
    stacked = jnp.concatenate([ot[:, g * Q_TILE:(g + 1) * Q_TILE] for g in range(GQA)], axis=0)
    o_ref[0] = stacked.T.astype(BF16)


def _attn_call(qt, gt, kcmp, vct, ksa, vst, kw, vwt):
    bsz, _, s = qt.shape
    n_sel = s // SEL_BLOCK
    n_top = min(N_SEL, n_sel)
    n_cmp = kcmp.shape[1]
    ka = ksa.shape[2]
    nblk = s // LANES
    grid = (bsz, N_KV, s // Q_TILE)
    kern = functools.partial(_attn_kernel, n_sel, n_top)
    return pl.pallas_call(
        kern,
        grid=grid,
        in_specs=[
            pl.BlockSpec((1, GQA * HEAD_DIM, Q_TILE), lambda b, h, i: (b, h, i)),
            pl.BlockSpec((1, GATE_ROWS, Q_TILE), lambda b, h, i: (b, h, i)),
            pl.BlockSpec((1, n_cmp, N_KV * HEAD_DIM), lambda b, h, i: (b, 0, 0)),
            pl.BlockSpec((1, HEAD_DIM, n_cmp), lambda b, h, i: (b, h, 0)),
            pl.BlockSpec((1, s, ka), lambda b, h, i: (b, 0, 0)),
            pl.BlockSpec((1, 1, nblk, V_ROWS, LANES), lambda b, h, i: (b, h, 0, 0, 0)),
            pl.BlockSpec((1, s + WINDOW, N_KV * HEAD_DIM), lambda b, h, i: (b, 0, 0)),
            pl.BlockSpec((1, 1, nblk + WINDOW // LANES, V_ROWS, LANES), lambda b, h, i: (b, h, 0, 0, 0)),
        ],
        out_specs=pl.BlockSpec((1, Q_TILE, GQA * HEAD_DIM), lambda b, h, i: (b, i, h)),
        out_shape=jax.ShapeDtypeStruct((bsz, s, N_HEADS * HEAD_DIM), BF16),
        scratch_shapes=[pltpu.VMEM((ka, GQA * Q_TILE), BF16),
                        pltpu.VMEM((Q_TILE // LANES, n_cmp + SUBLANES, LANES), F32),
                        pltpu.VMEM((HEAD_DIM, GQA * Q_TILE), F32),
                        pltpu.VMEM((2, SEL_TK, GQA * Q_TILE), F32)],
        compiler_params=_cparams(("arbitrary", "arbitrary", "arbitrary")),
        name="nsa_attention",
    )(qt, gt, kcmp, vct, ksa, vst, kw, vwt)


def _post_kernel(x_ref, yc_ref, yn_ref, wo_ref, g1_ref, b1_ref, wr_ref, br_ref, tril_ref,
                 h1_ref, hp_ref, ri_ref, rw_ref, cnt_ref):
    step = pl.program_id(0)
    half = yc_ref.shape[1]
    mix = (jnp.dot(yc_ref[...], wo_ref[0:half, :], preferred_element_type=F32)
           + jnp.dot(yn_ref[...], wo_ref[half:2 * half, :], preferred_element_type=F32))
    h1 = _layer_norm(ALPHA * x_ref[...] + mix, g1_ref[...], b1_ref[...])
    h1_ref[...] = h1

    h_hi = h1.astype(BF16)
    h_hi32 = h_hi.astype(F32)
    h_lo = (h1 - h_hi32).astype(BF16)
    bits = lax.bitcast_convert_type(h_hi32, jnp.uint32)
    dh = bits.shape[1] // 2
    hp_ref[...] = jnp.right_shift(bits[:, 0:dh], jnp.uint32(16)) | bits[:, dh:2 * dh]
    parts = jnp.dot(jnp.concatenate([h_hi, h_lo], axis=1), wr_ref[...], preferred_element_type=F32)
    logits = parts[:, 0:LANES] + parts[:, LANES:2 * LANES] + br_ref[...]
    tm = logits.shape[0]
    lane = lax.broadcasted_iota(jnp.int32, (tm, LANES), 1)
    big = jnp.int32(LANES)

    def masked_softmax(valid):
        lg = jnp.where(valid, logits, NEG_INF)
        mx = jnp.max(lg, axis=1, keepdims=True)
        ex = jnp.where(valid, jnp.exp(lg - mx), 0.0)
        return ex / jnp.sum(ex, axis=1, keepdims=True)

    def first_max(vals, valid):
        top = jnp.max(jnp.where(valid, vals, -1.0), axis=1, keepdims=True)
        idx = jnp.min(jnp.where(valid & (vals == top), lane, big), axis=1, keepdims=True)
        return top, idx

    is_group = lane < N_GROUPS
    gp, gsel = first_max(masked_softmax(is_group), is_group)
    lo = ROUTE_LANE0 + EXP_PER_GROUP * gsel
    in_group = (lane >= lo) & (lane < lo + EXP_PER_GROUP)
    eprob = masked_softmax(in_group)
    p1, i1 = first_max(eprob, in_group)
    p2, i2 = first_max(eprob, in_group & (lane != i1))
    den = p1 + p2
    w1 = gp * (p1 / den)
    w2 = gp * (p2 / den)

    @pl.when(step == 0)
    def _():
        cnt_ref[...] = jnp.zeros_like(cnt_ref)

    onehot = (lane == i1) | (lane == i2)
    before = jnp.dot(tril_ref[...], onehot.astype(BF16), preferred_element_type=F32)
    rk = before + cnt_ref[0:1, :]
    r1 = jnp.sum(jnp.where(lane == i1, rk, 0.0), axis=1, keepdims=True)
    r2 = jnp.sum(jnp.where(lane == i2, rk, 0.0), axis=1, keepdims=True)
    cnt_ref[...] = cnt_ref[...] + jnp.sum(onehot.astype(F32), axis=0, keepdims=True)

    col = lax.broadcasted_iota(jnp.int32, (tm, ri_ref.shape[1]), 1)
    ri_ref[...] = jnp.where(col == 0, i1 - ROUTE_LANE0,
                            jnp.where(col == 1, i2 - ROUTE_LANE0,
                                      jnp.where(col == 2, r1.astype(jnp.int32),
                                                jnp.where(col == 3, r2.astype(jnp.int32), 0))))
    rw_ref[...] = jnp.where(col == 0, w1, jnp.where(col == 1, w2, 0.0))


def _post_call(x2, yconv, ynsa, wo, g1, b1, wr, br, tril):
    n_tok, d = x2.shape
    tm = POST_TM
    const = lambda a: pl.BlockSpec(a.shape, lambda i: (0, 0))
    return pl.pallas_call(
        _post_kernel,
        grid=(n_tok // tm,),
        in_specs=[
            pl.BlockSpec((tm, d), lambda i: (i, 0)),
            pl.BlockSpec((tm, yconv.shape[1]), lambda i: (i, 0)),
            pl.BlockSpec((tm, ynsa.shape[1]), lambda i: (i, 0)),
            const(wo), const(g1), const(b1), const(wr), const(br), const(tril),
        ],
        out_specs=(
            pl.BlockSpec((tm, d), lambda i: (i, 0)),
            pl.BlockSpec((tm, d // 2), lambda i: (i, 0)),
            pl.BlockSpec((tm, SUBLANES), lambda i: (i, 0)),
            pl.BlockSpec((tm, SUBLANES), lambda i: (i, 0)),
            pl.BlockSpec((SUBLANES, LANES), lambda i: (0, 0)),
        ),
        out_shape=(
            jax.ShapeDtypeStruct((n_tok, d), F32),
            jax.ShapeDtypeStruct((n_tok, d // 2), jnp.uint32),
            jax.ShapeDtypeStruct((n_tok, SUBLANES), jnp.int32),
            jax.ShapeDtypeStruct((n_tok, SUBLANES), F32),
            jax.ShapeDtypeStruct((SUBLANES, LANES), F32),
        ),
        compiler_params=_cparams(("arbitrary",)),
        name="out_proj_ln1_router",
    )(x2, yconv, ynsa, wo, g1, b1, wr, br, tril)


def _row_copy(src_ref, src_row, dst_ref, dst_row, sem):
    return pltpu.make_async_copy(src_ref.at[pl.ds(src_row, 1), :], dst_ref.at[pl.ds(dst_row, 1), :], sem)


def _dispatch_kernel(pends_ref, nu_ref, dest_ref, h_ref, xs_ref, zero_ref, sem, zsem):
    tm = h_ref.shape[0]
    n_chunk = xs_ref.shape[0] // MOE_CHUNK

    @pl.when(pl.program_id(0) == 0)
    def _():
        zero_ref[...] = jnp.zeros_like(zero_ref)

        def zero_chunk(row0):
            row0 = pl.multiple_of(row0, MOE_CHUNK)
            return pltpu.make_async_copy(zero_ref, xs_ref.at[pl.ds(row0, MOE_CHUNK), :], zsem)

        def each_padded_chunk(act):
            def per_expert(e, _):
                end = pends_ref[e]
                start = jnp.where(e > 0, pends_ref[jnp.maximum(e - 1, 0)], 0)

                @pl.when(end > start)
                def _():
                    act(zero_chunk(end - MOE_CHUNK))
                return 0

            lax.fori_loop(0, N_EXPERTS, per_expert, 0)

            def per_dead_chunk(c, _):
                act(zero_chunk(c * MOE_CHUNK))
                return 0

            lax.fori_loop(nu_ref[0], n_chunk, per_dead_chunk, 0)

        each_padded_chunk(lambda cp: cp.start())
        each_padded_chunk(lambda cp: cp.wait())

    def issue(r, _):
        for slot in range(2):
            _row_copy(h_ref, r, xs_ref, dest_ref[2 * r + slot], sem).start()
        return 0

    lax.fori_loop(0, tm, issue, 0, unroll=ROW_UNROLL)
    for slot in range(2):
        pltpu.make_async_copy(h_ref, xs_ref.at[pl.ds(0, tm), :], sem).wait()


def _dispatch_call(pends, n_used, dest, h1, n_rows):
    n_tok, d = h1.shape
    tm = ROW_TM
    grid_spec = pltpu.PrefetchScalarGridSpec(
        num_scalar_prefetch=2,
        grid=(n_tok // tm,),
        in_specs=[
            pl.BlockSpec((2 * tm,), lambda i, pe, nu: (i,), memory_space=pltpu.SMEM),
            pl.BlockSpec((tm, d), lambda i, pe, nu: (i, 0)),
        ],
        out_specs=pl.BlockSpec(memory_space=pl.ANY),
        scratch_shapes=[pltpu.VMEM((MOE_CHUNK, d), h1.dtype), pltpu.SemaphoreType.DMA, pltpu.SemaphoreType.DMA],
    )
    return pl.pallas_call(
        _dispatch_kernel,
        grid_spec=grid_spec,
        out_shape=jax.ShapeDtypeStruct((n_rows, d), h1.dtype),
        compiler_params=_cparams(("arbitrary",)),
        name="moe_dispatch",
    )(pends, n_used, dest, h1)


def _expert_kernel(ce_ref, nu_ref, xs_ref, wi_ref, wo_ref, o_ref, wib_ref, wob_ref):
    c = pl.program_id(0)
    live = c < nu_ref[0]

    @pl.when(live & ((c == 0) | (ce_ref[c] != ce_ref[jnp.maximum(c - 1, 0)])))
    def _():
        wib_ref[...] = wi_ref[0].astype(BF16)
        wob_ref[...] = wo_ref[0].astype(BF16)

    @pl.when(live)
    def _():
        pk = xs_ref[...]
        dh = pk.shape[1]
        x_lo = lax.bitcast_convert_type(jnp.left_shift(pk, jnp.uint32(16)), F32).astype(BF16)
        x_hi = lax.bitcast_convert_type(pk & jnp.uint32(0xFFFF0000), F32).astype(BF16)
        gu = (jnp.dot(x_lo, wib_ref[0:dh, :], preferred_element_type=F32)
              + jnp.dot(x_hi, wib_ref[dh:2 * dh, :], preferred_element_type=F32))
        gate = gu[:, 0:D_EXPERT]
        act = gate * jax.nn.sigmoid(gate) * gu[:, D_EXPERT:2 * D_EXPERT]
        o_ref[...] = jnp.dot(act.astype(BF16), wob_ref[...], preferred_element_type=F32)

    @pl.when(jnp.logical_not(live))
    def _():
        o_ref[...] = jnp.zeros_like(o_ref)


def _expert_call(chunk_exp, n_used, xs, wi, wo):
    n_rows, dh = xs.shape
    d = wo.shape[2]
    n_chunk = n_rows // MOE_CHUNK

    def live(c, nu):
        return jnp.maximum(jnp.minimum(c, nu[0] - 1), 0)

    grid_spec = pltpu.PrefetchScalarGridSpec(
        num_scalar_prefetch=2,
        grid=(n_chunk,),
        in_specs=[
            pl.BlockSpec((MOE_CHUNK, dh), lambda c, ce, nu: (live(c, nu), 0)),
            pl.BlockSpec((1,) + wi.shape[1:], lambda c, ce, nu: (ce[live(c, nu)], 0, 0)),
            pl.BlockSpec((1,) + wo.shape[1:], lambda c, ce, nu: (ce[live(c, nu)], 0, 0)),
        ],
        out_specs=pl.BlockSpec((MOE_CHUNK, d), lambda c, ce, nu: (c, 0)),
        scratch_shapes=[pltpu.VMEM(wi.shape[1:], BF16), pltpu.VMEM(wo.shape[1:], BF16)],
    )
    return pl.pallas_call(
        _expert_kernel,
        grid_spec=grid_spec,
        out_shape=jax.ShapeDtypeStruct((n_rows, d), F32),
        compiler_params=_cparams(("arbitrary",)),
        name="moe_experts",
    )(chunk_exp, n_used, xs, wi, wo)


def _combine_kernel(dest_ref, dest_next_ref, h1_ref, rw_ref, p_ref, ys_ref,
                    g2_ref, b2_ref, wp_ref, wg_ref, bg_ref, g3_ref, b3_ref,
                    o_ref, rows_ref, sems):
    tm = h1_ref.shape[0]
    step = pl.program_id(0)
    cur = step % 2

    def gather(idx_ref, buf):
        def issue(r, _):
            for slot in range(2):
                _row_copy(ys_ref, idx_ref[2 * r + slot], rows_ref.at[buf, slot], r, sems.at[buf]).start()
            return 0

        lax.fori_loop(0, tm, issue, 0, unroll=ROW_UNROLL)

    def await_rows(buf):
        for slot in range(2):
            pltpu.make_async_copy(ys_ref.at[pl.ds(0, tm), :], rows_ref.at[buf, slot], sems.at[buf]).wait()

    @pl.when(step == 0)
    def _():
        gather(dest_ref, 0)

    await_rows(cur)
    rw = rw_ref[...]
    ffn = rw[:, 0:1] * rows_ref[cur, 0] + rw[:, 1:2] * rows_ref[cur, 1]
    gather(dest_next_ref, 1 - cur)
    h2 =_layer_norm(ALPHA * h1_ref[...] + ffn, g2_ref[...], b2_ref[...])
    emb = jnp.dot(p_ref[...].astype(BF16), wp_ref[...], preferred_element_type=F32)
    gate = jax.nn.sigmoid(jnp.dot(h2.astype(BF16), wg_ref[...], preferred_element_type=F32) + bg_ref[...])
    o_ref[...] = _layer_norm(ALPHA * h2 + emb * gate, g3_ref[...], b3_ref[...])

    @pl.when(step == pl.num_programs(0) - 1)
    def _():
        await_rows(1 - cur)


def _combine_call(dest, h1, rw, p2, ys, g2, b2, wp, wg, bg, g3, b3):
    n_tok, d = h1.shape
    tm = ROW_TM
    const = lambda a: pl.BlockSpec(a.shape, lambda i: (0, 0))
    return pl.pallas_call(
        _combine_kernel,
        grid=(n_tok // tm,),
        in_specs=[
            pl.BlockSpec((2 * tm,), lambda i: (i,), memory_space=pltpu.SMEM),
            pl.BlockSpec((2 * tm,), lambda i: (jnp.minimum(i + 1, n_tok // tm - 1),), memory_space=pltpu.SMEM),
            pl.BlockSpec((tm, d), lambda i: (i, 0)),
            pl.BlockSpec((tm, rw.shape[1]), lambda i: (i, 0)),
            pl.BlockSpec((tm, p2.shape[1]), lambda i: (i, 0)),
            pl.BlockSpec(memory_space=pl.ANY),
            const(g2), const(b2), const(wp), const(wg), const(bg), const(g3), const(b3),
        ],
        out_specs=pl.BlockSpec((tm, d), lambda i: (i, 0)),
        out_shape=jax.ShapeDtypeStruct((n_tok, d), F32),
        scratch_shapes=[pltpu.VMEM((2, 2, tm, d), F32), pltpu.SemaphoreType.DMA((2,))],
        compiler_params=_cparams(("arbitrary",)),
        name="moe_combine_ln_ple",
    )(dest, dest, h1, rw, p2, ys, g2, b2, wp, wg, bg, g3, b3)


def _gate_columns():
    cols = np.zeros((N_KV, GATE_ROWS), np.int32)
    live = np.zeros((N_KV, GATE_ROWS), np.float32)
    for h in range(N_KV):
        for br in range(3):
            for g in range(GQA):
                cols[h, br * GQA + g] = (h * GQA + g) * 3 + br
                live[h, br * GQA + g] = 1.0
    return cols.reshape(-1), live.reshape(-1)


def _layer(x, p, w_in, w_conv, pe_ck, w1_ck, w2_ck, pe_cv, w1_cv, w2_cv, w_out, ln1_g, ln1_b,
           w_rg, b_rg, w_re, b_re, w_e_in, w_e_out, ln2_g, ln2_b, w_ple, w_ple_gate, b_ple_gate,
           ln3_g, ln3_b):
    bsz, s, d = x.shape
    n_tok = bsz * s
    row = lambda v: v.reshape(1, -1)

    c_q, c_kc, c_vc, c_ks, c_vs, c_kw, c_vw, c_g = 1536, 2048, 2176, 2304, 2432, 2560, 2688, 2816
    wn = jnp.concatenate([w_in[:, 0:c_q], w_in[:, c_kc:c_ks], w_in[:, c_ks:c_vs], w_in[:, c_kw:c_vw]],
                         axis=1).astype(BF16)
    gcols, glive = _gate_columns()
    w_gate = w_in[:, c_g:c_g + 3 * N_HEADS][:, gcols] * glive[None, :]
    wt = jnp.concatenate([w_in[:, c_q:c_kc] * Q_SCALE, w_in[:, c_vs:c_kw], w_in[:, c_vw:c_g], w_gate],
                         axis=1).T.astype(BF16)

    yconv, cmpkv, ksa, kw, qt, vst, vwt, gt = _proj_call(x, wn, wt, w_conv)

    half_blk = CMP_BLOCK // 2
    eye = jnp.eye(N_KV, dtype=F32)

    def per_head_w1(w1):
        w = w1.reshape(2, half_blk, HEAD_DIM, CMP_HIDDEN)
        w = jnp.einsum('ptdc,hg->pthdgc', w, eye)
        return w.reshape(2 * half_blk * N_KV * HEAD_DIM, N_KV * CMP_HIDDEN)

    def per_head_w2(w2):
        return jnp.einsum('cd,hg->hcgd', w2, eye).reshape(N_KV * CMP_HIDDEN, N_KV * HEAD_DIM)

    def per_head_pe(pe):
        return jnp.broadcast_to(pe.reshape(2, half_blk, 1, HEAD_DIM),
                                (2, half_blk, N_KV, HEAD_DIM)).reshape(2, half_blk * N_KV * HEAD_DIM)

    ckv = cmpkv.reshape(bsz, 2, s // CMP_STRIDE, CMP_STRIDE * N_KV * HEAD_DIM)
    pes = jnp.stack([per_head_pe(pe_ck), per_head_pe(pe_cv)])
    w1s = jnp.stack([per_head_w1(w1_ck), per_head_w1(w1_cv)]).astype(BF16)
    w2s = jnp.stack([per_head_w2(w2_ck), per_head_w2(w2_cv)]).astype(BF16)
    cmp_out = _compress_call(ckv, pes, w1s, w2s)
    kcmp = cmp_out[:, 0].astype(BF16)
    vct = cmp_out[:, 1].transpose(0, 2, 1).astype(BF16)

    kw = jnp.pad(kw, ((0, 0), (WINDOW, 0), (0, 0)))
    vwt = jnp.pad(vwt, ((0, 0), (0, 0), (WINDOW // LANES, 0), (0, 0), (0, 0)))
    ynsa = _attn_call(qt, gt, kcmp, vct, ksa, vst, kw, vwt)

    wr = jnp.zeros((d, LANES), F32).at[:, 0:N_GROUPS].set(w_rg)
    wr = wr.at[:, ROUTE_LANE0:ROUTE_LANE0 + N_EXPERTS].set(w_re)
    br = jnp.zeros((1, LANES), F32).at[0, 0:N_GROUPS].set(b_rg)
    br = br.at[0, ROUTE_LANE0:ROUTE_LANE0 + N_EXPERTS].set(b_re)
    wr_hi = wr.astype(BF16)
    wr_lo = (wr - wr_hi.astype(F32)).astype(BF16)
    wr = jnp.concatenate([jnp.concatenate([wr_hi, wr_lo], axis=1),
                          jnp.concatenate([wr_hi, jnp.zeros_like(wr_lo)], axis=1)], axis=0)
    tril = jnp.asarray(np.tril(np.ones((POST_TM, POST_TM), np.float32), -1), BF16)
    h1, h1_packed, ri, rw, cnt = _post_call(x.reshape(n_tok, d), yconv.reshape(n_tok, -1), ynsa.reshape(n_tok, -1),
                                 w_out.astype(BF16), row(ln1_g), row(ln1_b), wr, br, tril)

    counts = cnt[0, ROUTE_LANE0:ROUTE_LANE0 + N_EXPERTS].astype(jnp.int32)
    pcounts = (counts + MOE_CHUNK - 1) // MOE_CHUNK * MOE_CHUNK
    pends = jnp.cumsum(pcounts)
    pstarts = pends - pcounts
    dest = (pstarts[ri[:, 0:2]] + ri[:, 2:4]).reshape(-1).astype(jnp.int32)
    n_asg = n_tok * 2
    n_chunk = -(-n_asg // MOE_CHUNK) + N_EXPERTS
    chunk_row0 = jnp.arange(n_chunk, dtype=jnp.int32) * MOE_CHUNK
    chunk_exp = jnp.minimum(jnp.sum(pends[None, :] <= chunk_row0[:, None], axis=1), N_EXPERTS - 1).astype(jnp.int32)
    n_used = (pends[-1:] // MOE_CHUNK).astype(jnp.int32)

    xs = _dispatch_call(pends.astype(jnp.int32), n_used, dest, h1_packed, n_chunk * MOE_CHUNK)
    ys = _expert_call(chunk_exp, n_used, xs, w_e_in, w_e_out)
    out = _combine_call(dest, h1, rw, p.reshape(n_tok, -1), ys, row(ln2_g), row(ln2_b),
                        w_ple.astype(BF16), w_ple_gate.astype(BF16), row(b_ple_gate), row(ln3_g), row(ln3_b))
    return out.reshape(bsz, s, d)


def kernel(x, p, w_in, w_conv, pe_ck, w1_ck, w2_ck, pe_cv, w1_cv, w2_cv, w_out, ln1_g, ln1_b, w_rg, b_rg, w_re, b_re, w_e_in, w_e_out, ln2_g, ln2_b, w_ple, w_ple_gate, b_ple_gate, ln3_g, ln3_b):
    assert w_in.shape[0] == DEPTH, "residual scaling ALPHA is derived from DEPTH"
    h = x
    for i in range(DEPTH):
        h = _layer(h, p[i], w_in[i], w_conv[i], pe_ck[i], w1_ck[i], w2_ck[i], pe_cv[i], w1_cv[i], w2_cv[i],
                   w_out[i], ln1_g[i], ln1_b[i], w_rg[i], b_rg[i], w_re[i], b_re[i], w_e_in[i], w_e_out[i],
                   ln2_g[i], ln2_b[i], w_ple[i], w_ple_gate[i], b_ple_gate[i], ln3_g[i], ln3_b[i])
    return h
```

```python
import functools

import jax
import jax.numpy as jnp
import numpy as np
from jax import lax
from jax.experimental import pallas as pl
from jax.experimental.pallas import tpu as pltpu

F32 = jnp.float32
BF16 = jnp.bfloat16
HIGHEST = lax.Precision.HIGHEST

CONV_CH = 512
CONV_W = 3
N_HEADS = 8
HEAD_DIM = 64
N_KV = 2
GQA = N_HEADS // N_KV
CMP_BLOCK = 32
CMP_STRIDE = 16
CMP_HIDDEN = 2 * HEAD_DIM
SEL_BLOCK = 64
N_SEL = 16
WINDOW = 512
ATTN_SCALE = HEAD_DIM ** -0.5
Q_SCALE = ATTN_SCALE * float(np.log2(np.e))
FORCE_SCORE = 1e4
NEG_INF = -1e30
N_GROUPS = 4
EXP_PER_GROUP = 8
N_EXPERTS = N_GROUPS * EXP_PER_GROUP
D_EXPERT = 512
MOE_CHUNK = 512
DEPTH = 1
ALPHA = (2 * DEPTH) ** 0.25
LN_EPS = 1e-5

LANES = 128
SUBLANES = 8
VMEM_LIMIT = 56 * 1024 * 1024

PROJ_TM = 512
Q_TILE = 256
SEL_TK = 512
CMP_CHUNK = 128
WIN_KEYS = WINDOW + Q_TILE
POST_TM = 512
POST_SUB = 512
ROW_TM = 512
ROW_UNROLL = True
ROUTE_LANE0 = N_GROUPS
GATE_ROWS = 16
V_ROWS = HEAD_DIM + 16


def _cparams(sem, vmem=VMEM_LIMIT):
    return pltpu.CompilerParams(dimension_semantics=sem, vmem_limit_bytes=vmem)


def _layer_norm(v, g, b):
    mu = jnp.mean(v, axis=-1, keepdims=True)
    d = v - mu
    var = jnp.mean(d * d, axis=-1, keepdims=True)
    return d * lax.rsqrt(var + LN_EPS) * g + b


def _proj_kernel(x_ref, wn_ref, wt_ref, wc_ref,
                 yconv_ref, cmpkv_ref, ksa_ref, kw_ref, qt_ref, vst_ref, vwt_ref, gt_ref,
                 carry_ref):
    si = pl.program_id(1)
    tm = x_ref.shape[1]
    xb = x_ref[0].astype(BF16)
    zn = jnp.dot(xb, wn_ref[...], preferred_element_type=F32)
    zt = lax.dot_general(wt_ref[...], xb, (((1,), (1,)), ((), ())),
                         preferred_element_type=F32)

    cb = zn[:, 0:CONV_CH]
    u = zn[:, CONV_CH:2 * CONV_CH] * zn[:, 2 * CONV_CH:3 * CONV_CH]

    @pl.when(si == 0)
    def _():
        carry_ref[...] = jnp.zeros_like(carry_ref)

    prev = carry_ref[...]
    rows = lax.broadcasted_iota(jnp.int32, u.shape, 0)
    u1 = jnp.where(rows == 0, prev[7:8, :], pltpu.roll(u, 1, 0))
    u2 = jnp.where(rows == 0, prev[6:7, :],
                   jnp.where(rows == 1, prev[7:8, :], pltpu.roll(u, 2, 0)))
    w = wc_ref[...]
    yconv_ref[0] = (cb * (w[0:1, :] * u2 + w[1:2, :] * u1 + w[2:3, :] * u)).astype(BF16)
    carry_ref[...] = u[tm - SUBLANES:tm, :]

    cmpkv_ref[0, 0] = zn[:, 1536:1664]
    cmpkv_ref[0, 1] = zn[:, 1664:1792]
    kw_ref[0] = zn[:, 1920:2048].astype(BF16)
    ksa_ref[0, :, 0:LANES] = zn[:, 1792:1920].astype(BF16)
    n_hot = ksa_ref.shape[2] - LANES
    pos = si * tm + lax.broadcasted_iota(jnp.int32, (tm, n_hot), 0)
    hot = jnp.right_shift(pos, SEL_BLOCK.bit_length() - 1) == lax.broadcasted_iota(jnp.int32, (tm, n_hot), 1)
    ksa_ref[0, :, LANES:] = jnp.where(hot, 1.0, 0.0).astype(BF16)

    qt_ref[0] = zt[0:512, :].astype(BF16)
    ones = jnp.ones((V_ROWS - HEAD_DIM, LANES), BF16)
    for h in range(N_KV):
        for i in range(tm // LANES):
            cols = slice(i * LANES, (i + 1) * LANES)
            vst_ref[0, h, i, 0:HEAD_DIM, :] = zt[512 + h * 64:512 + (h + 1) * 64, cols].astype(BF16)
            vst_ref[0, h, i, HEAD_DIM:V_ROWS, :] = ones
            vwt_ref[0, h, i, 0:HEAD_DIM, :] = zt[640 + h * 64:640 + (h + 1) * 64, cols].astype(BF16)
            vwt_ref[0, h, i, HEAD_DIM:V_ROWS, :] = ones
    gt_ref[0] = jax.nn.sigmoid(zt[768:800, :])


def _proj_call(x, wn, wt, wc):
    bsz, s, d = x.shape
    tm = PROJ_TM
    nblk = s // LANES
    grid = (bsz, s // tm)
    ka = LANES + -(-(s // SEL_BLOCK) // LANES) * LANES
    out_shape = (
        jax.ShapeDtypeStruct((bsz, s, CONV_CH), BF16),
        jax.ShapeDtypeStruct((bsz, 2, s, LANES), F32),
        jax.ShapeDtypeStruct((bsz, s, ka), BF16),
        jax.ShapeDtypeStruct((bsz, s, LANES), BF16),
        jax.ShapeDtypeStruct((bsz, 512, s), BF16),
        jax.ShapeDtypeStruct((bsz, N_KV, nblk, V_ROWS, LANES), BF16),
        jax.ShapeDtypeStruct((bsz, N_KV, nblk, V_ROWS, LANES), BF16),
        jax.ShapeDtypeStruct((bsz, N_KV * GATE_ROWS, s), F32),
    )
    vspec = pl.BlockSpec((1, N_KV, tm // LANES, V_ROWS, LANES), lambda b, i: (b, 0, i, 0, 0))
    return pl.pallas_call(
        _proj_kernel,
        grid=grid,
        in_specs=[
            pl.BlockSpec((1, tm, d), lambda b, i: (b, i, 0)),
            pl.BlockSpec(wn.shape, lambda b, i: (0, 0)),
            pl.BlockSpec(wt.shape, lambda b, i: (0, 0)),
            pl.BlockSpec(wc.shape, lambda b, i: (0, 0)),
        ],
        out_specs=(
            pl.BlockSpec((1, tm, CONV_CH), lambda b, i: (b, i, 0)),
            pl.BlockSpec((1, 2, tm, LANES), lambda b, i: (b, 0, i, 0)),
            pl.BlockSpec((1, tm, ka), lambda b, i: (b, i, 0)),
            pl.BlockSpec((1, tm, LANES), lambda b, i: (b, i, 0)),
            pl.BlockSpec((1, 512, tm), lambda b, i: (b, 0, i)),
            vspec, vspec,
            pl.BlockSpec((1, N_KV * GATE_ROWS, tm), lambda b, i: (b, 0, i)),
        ),
        out_shape=out_shape,
        scratch_shapes=[pltpu.VMEM((SUBLANES, CONV_CH), F32)],
        compiler_params=_cparams(("arbitrary", "arbitrary")),
        name="in_proj_conv",
    )(x, wn, wt, wc)


def _compress_kernel(g_ref, pe_ref, w1_ref, w2_ref, o_ref):
    g = g_ref[0, 0]
    pe = pe_ref[0]
    half = g.shape[1]
    a_lo = jnp.dot((g + pe[0:1, :]).astype(BF16), w1_ref[0, 0:half, :], preferred_element_type=F32)
    a_hi = jnp.dot((g + pe[1:2, :]).astype(BF16), w1_ref[0, half:2 * half, :], preferred_element_type=F32)
    n = g.shape[0]
    hid = a_lo + pltpu.roll(a_hi, n - 1, 0)
    act = jax.nn.gelu(hid)
    o_ref[0, 0] = jnp.dot(act.astype(BF16), w2_ref[0], preferred_element_type=F32)


def _compress_call(ckv, pes, w1s, w2s):
    bsz, _, n, width = ckv.shape
    return pl.pallas_call(
        _compress_kernel,
        grid=(bsz, 2),
        in_specs=[
            pl.BlockSpec((1, 1, n, width), lambda b, k: (b, k, 0, 0)),
            pl.BlockSpec((1, 2, width), lambda b, k: (k, 0, 0)),
            pl.BlockSpec((1, 2 * width, N_KV * CMP_HIDDEN), lambda b, k: (k, 0, 0)),
            pl.BlockSpec((1, N_KV * CMP_HIDDEN, N_KV * HEAD_DIM), lambda b, k: (k, 0, 0)),
        ],
        out_specs=pl.BlockSpec((1, 1, n, N_KV * HEAD_DIM), lambda b, k: (b, k, 0, 0)),
        out_shape=jax.ShapeDtypeStruct((bsz, 2, n, N_KV * HEAD_DIM), F32),
        compiler_params=_cparams(("arbitrary", "arbitrary")),
        name="compress_mlp",
    )(ckv, pes, w1s, w2s)


def _attn_kernel(n_sel, n_top,
                 q_ref, g_ref, kc_ref, vct_ref, ksa_ref, vst_ref, kw_ref, vwt_ref,
                 o_ref, qa_ref, ps_ref, oc_ref, s_ref):
    qb = pl.program_id(2)
    q0 = qb * Q_TILE
    nq = GQA * Q_TILE
    blk = q_ref[0]
    q4 = jnp.concatenate([blk[g * HEAD_DIM:(g + 1) * HEAD_DIM, :] for g in range(GQA)], axis=1)
    lane = lax.broadcasted_iota(jnp.int32, (1, nq), 1)
    t4 = q0 + (lane & (Q_TILE - 1))
    kvh = pl.program_id(1)
    qa_ref[pl.ds(pl.multiple_of(kvh * HEAD_DIM, HEAD_DIM), HEAD_DIM), :] = q4
    qa_ref[pl.ds(pl.multiple_of((N_KV - 1 - kvh) * HEAD_DIM, HEAD_DIM), HEAD_DIM), :] = jnp.zeros_like(q4)
    q2 = qa_ref[0:N_KV * HEAD_DIM, :]

    n_cmp = kc_ref.shape[1]
    chunk = min(CMP_CHUNK, n_cmp)
    q_lane_tiles = Q_TILE // LANES
    for h in range(q_lane_tiles):
        ps_ref[h, 0:SUBLANES, :] = jnp.zeros((SUBLANES, LANES), F32)

    def cmp_branch(n):
        sc = jnp.dot(kc_ref[0, 0:n, :], q2, preferred_element_type=F32)
        last_c = jnp.right_shift(t4 - (CMP_BLOCK - 1), CMP_STRIDE.bit_length() - 1)
        scm = jnp.where(lax.broadcasted_iota(jnp.int32, (n, nq), 0) <= last_c, sc, NEG_INF)
        m_c = jnp.max(scm, axis=0, keepdims=True)
        e_c = jnp.exp2(scm - m_c)
        l_c = jnp.sum(e_c, axis=0, keepdims=True)
        p_c = e_c * jnp.where(last_c >= 0, 1.0 / l_c, 0.0)
        oc_ref[...] = jnp.dot(vct_ref[0, :, 0:n], p_c.astype(BF16), preferred_element_type=F32)
        psum = (p_c[:, 0:Q_TILE] + p_c[:, Q_TILE:2 * Q_TILE]
                + p_c[:, 2 * Q_TILE:3 * Q_TILE] + p_c[:, 3 * Q_TILE:4 * Q_TILE])
        for h in range(q_lane_tiles):
            ps_ref[h, SUBLANES:SUBLANES + n, :] = psum[:, h * LANES:(h + 1) * LANES]
            if n < n_cmp:
                ps_ref[h, SUBLANES + n:, :] = jnp.zeros((n_cmp - n, LANES), F32)

    last_visible = (q0 + Q_TILE - CMP_BLOCK) // CMP_STRIDE
    live_chunks = last_visible // chunk + 1
    for k in range(1, n_cmp // chunk + 1):
        pl.when(live_chunks == k)(functools.partial(cmp_branch, k * chunk))
    o_cmp = oc_ref[...]

    sw = jnp.dot(kw_ref[0, pl.ds(pl.multiple_of(q0, Q_TILE), WIN_KEYS), :], q2,
                 preferred_element_type=F32)
    ql = t4 - q0
    row = lax.broadcasted_iota(jnp.int32, (LANES, nq), 0)
    slabs = []
    for c in range(WIN_KEYS // LANES):
        slab = sw[c * LANES:(c + 1) * LANES, :]
        before_start = WINDOW - 1 - q0 - c * LANES
        if c * LANES < Q_TILE:
            slab = jnp.where(row > jnp.maximum(ql - c * LANES, before_start), slab, NEG_INF)
        elif (c + 1) * LANES > WINDOW:
            slab = jnp.where(row <= ql + (WINDOW - c * LANES), slab, NEG_INF)
        else:
            slab = jnp.where(row > before_start, slab, NEG_INF)
        slabs.append(slab)
    swm = jnp.concatenate(slabs, axis=0)
    m_w = jnp.max(swm, axis=0, keepdims=True)
    p_w = jnp.exp2(swm - m_w)
    wblk = q0 // LANES
    vwt = jnp.concatenate([vwt_ref[0, 0, wblk + i] for i in range(WIN_KEYS // LANES)], axis=1)
    acc_w = jnp.dot(vwt, p_w.astype(BF16), preferred_element_type=F32)
    o_win = acc_w[0:HEAD_DIM, :] * (1.0 / acc_w[HEAD_DIM:HEAD_DIM + 1, :])

    per_sel = SEL_BLOCK // CMP_STRIDE
    def importance(h):
        acc = ps_ref[h, pl.ds(SUBLANES - 1, n_sel, stride=per_sel), :]
        for off in range(CMP_BLOCK // CMP_STRIDE + per_sel - 2):
            acc = acc + ps_ref[h, pl.ds(SUBLANES + off, n_sel, stride=per_sel), :]
        return acc

    imp = jnp.concatenate([importance(h) for h in range(q_lane_tiles)], axis=1)

    jidx = lax.broadcasted_iota(jnp.int32, (n_sel, Q_TILE), 0)
    tq = q0 + lax.broadcasted_iota(jnp.int32, (n_sel, Q_TILE), 1)
    jt = jnp.right_shift(tq, SEL_BLOCK.bit_length() - 1)
    forced = (jidx == 0) | (jidx == jt) | (jidx == jt - 1)
    score = jnp.where(forced, FORCE_SCORE, imp)
    key = jnp.where(jidx > jt, -1, lax.bitcast_convert_type(score, jnp.int32))
    theta = jnp.zeros((1, Q_TILE), jnp.int32)
    for bit in range(30, -1, -1):
        cand = theta | (1 << bit)
        reach = jnp.sum((key >= cand).astype(jnp.int32), axis=0, keepdims=True)
        theta = jnp.where(reach >= n_top, cand, theta)
    above = key > theta
    tied = key == theta
    n_above = jnp.sum(above.astype(jnp.int32), axis=0, keepdims=True)
    lower = (lax.broadcasted_iota(jnp.int32, (n_sel, n_sel), 1)
             < lax.broadcasted_iota(jnp.int32, (n_sel, n_sel), 0)).astype(BF16)
    tied_before = jnp.dot(lower, tied.astype(BF16), preferred_element_type=F32)
    chosen = above | (tied & (tied_before < (n_top - n_above).astype(F32)))
    bias = jnp.where(chosen, 0.0, NEG_INF).astype(BF16)
    bias_row0 = N_KV * HEAD_DIM
    qa_ref[bias_row0:bias_row0 + n_sel, :] = jnp.concatenate([bias] * GQA, axis=1)
    if qa_ref.shape[0] > bias_row0 + n_sel:
        qa_ref[bias_row0 + n_sel:, :] = jnp.zeros((qa_ref.shape[0] - bias_row0 - n_sel, nq), BF16)

    nsub = SEL_TK // LANES

    def scores(c, buf):
        k0 = pl.multiple_of(c * SEL_TK, SEL_TK)
        s = jnp.dot(ksa_ref[0, pl.ds(k0, SEL_TK), :], qa_ref[...], preferred_element_type=F32)
        s_ref[buf] = s
        return jnp.max(s, axis=0, keepdims=True)

    def accumulate(c, buf, mx, m, acc, causal):
        s = s_ref[buf]
        if causal:
            kpos = c * SEL_TK + lax.broadcasted_iota(jnp.int32, (SEL_TK, nq), 0)
            s = jnp.where(kpos <= t4, s, NEG_INF)
            mx = jnp.max(s, axis=0, keepdims=True)
        m_new = jnp.maximum(m, mx)
        p = jnp.exp2(s - m_new)
        vt = jnp.concatenate([vst_ref[0, 0, c * nsub + i] for i in range(nsub)], axis=1)
        acc = jnp.exp2(m - m_new) * acc + jnp.dot(vt, p.astype(BF16), preferred_element_type=F32)
        return m_new, acc

    def pair_body(pi, carry):
        mx0, m, acc = carry
        c = 2 * pi
        mx1 = scores(c + 1, 1)
        m, acc = accumulate(c, 0, mx0, m, acc, False)
        mx0 = scores(c + 2, 0)
        m, acc = accumulate(c + 1, 1, mx1, m, acc, False)
        return mx0, m, acc

    n_full = qb // (SEL_TK // Q_TILE)
    init = (scores(0, 0), jnp.full((1, nq), NEG_INF, F32), jnp.zeros((V_ROWS, nq), F32))
    mx0, m_s, acc_s = lax.fori_loop(0, n_full // 2, pair_body, init)
    c_last = 2 * (n_full // 2)

    def leftover_then_own():
        mx1 = scores(c_last + 1, 1)
        m1, acc1 = accumulate(c_last, 0, mx0, m_s, acc_s, False)
        return accumulate(c_last + 1, 1, mx1, m1, acc1, True)[1]

    def own_only():
        return accumulate(c_last, 0, mx0, m_s, acc_s, True)[1]

    acc_s = lax.cond(n_full % 2 == 1, leftover_then_own, own_only)
    o_sel = acc_s[0:HEAD_DIM, :] * (1.0 / acc_s[HEAD_DIM:HEAD_DIM + 1, :])

    gt = g_ref[0]

    def gate(br):
        return jnp.concatenate([gt[br * GQA + g:br * GQA + g + 1, :] for g in range(GQA)], axis=1)

    ot = gate(0) * o_cmp + gate(1) * o_sel + gate(2) * o_win
    stacked = jnp.concatenate([ot[:, g * Q_TILE:(g + 1) * Q_TILE] for g in range(GQA)], axis=0)
    o_ref[0] = stacked.T.astype(BF16)


def _attn_call(qt, gt, kcmp, vct, ksa, vst, kw, vwt):
    bsz, _, s = qt.shape
    n_sel = s // SEL_BLOCK
    n_top = min(N_SEL, n_sel)
    n_cmp = kcmp.shape[1]
    ka = ksa.shape[2]
    nblk = s // LANES
    grid = (bsz, N_KV, s // Q_TILE)
    kern = functools.partial(_attn_kernel, n_sel, n_top)
    return pl.pallas_call(
        kern,
        grid=grid,
        in_specs=[
            pl.BlockSpec((1, GQA * HEAD_DIM, Q_TILE), lambda b, h, i: (b, h, i)),
            pl.BlockSpec((1, GATE_ROWS, Q_TILE), lambda b, h, i: (b, h, i)),
            pl.BlockSpec((1, n_cmp, N_KV * HEAD_DIM), lambda b, h, i: (b, 0, 0)),
            pl.BlockSpec((1, HEAD_DIM, n_cmp), lambda b, h, i: (b, h, 0)),
            pl.BlockSpec((1, s, ka), lambda b, h, i: (b, 0, 0)),
            pl.BlockSpec((1, 1, nblk, V_ROWS, LANES), lambda b, h, i: (b, h, 0, 0, 0)),
            pl.BlockSpec((1, s + WINDOW, N_KV * HEAD_DIM), lambda b, h, i: (b, 0, 0)),
            pl.BlockSpec((1, 1, nblk + WINDOW // LANES, V_ROWS, LANES), lambda b, h, i: (b, h, 0, 0, 0)),
        ],
        out_specs=pl.BlockSpec((1, Q_TILE, GQA * HEAD_DIM), lambda b, h, i: (b, i, h)),
        out_shape=jax.ShapeDtypeStruct((bsz, s, N_HEADS * HEAD_DIM), BF16),
        scratch_shapes=[pltpu.VMEM((ka, GQA * Q_TILE), BF16),
                        pltpu.VMEM((Q_TILE // LANES, n_cmp + SUBLANES, LANES), F32),
                        pltpu.VMEM((HEAD_DIM, GQA * Q_TILE), F32),
                        pltpu.VMEM((2, SEL_TK, GQA * Q_TILE), F32)],
        compiler_params=_cparams(("arbitrary", "arbitrary", "arbitrary")),
        name="nsa_attention",
    )(qt, gt, kcmp, vct, ksa, vst, kw, vwt)


def _post_kernel(x_ref, yc_ref, yn_ref, wo_ref, g1_ref, b1_ref, wr_ref, br_ref, tril_ref,
                 h1_ref, hp_ref, ri_ref, rw_ref, cnt_ref):
    step = pl.program_id(0)
    half = yc_ref.shape[1]
    sub = tril_ref.shape[0]

    @pl.when(step == 0)
    def _():
        cnt_ref[...] = jnp.zeros_like(cnt_ref)

    def route_rows(r0, cnt):
        rows = slice(r0, r0 + sub)
        mix = (jnp.dot(yc_ref[rows, :], wo_ref[0:half, :], preferred_element_type=F32)
               + jnp.dot(yn_ref[rows, :], wo_ref[half:2 * half, :], preferred_element_type=F32))
        h1 = _layer_norm(ALPHA * x_ref[rows, :] + mix, g1_ref[...], b1_ref[...])
        h1_ref[rows, :] = h1

        h_hi = h1.astype(BF16)
        h_hi32 = h_hi.astype(F32)
        h_lo = (h1 - h_hi32).astype(BF16)
        bits = lax.bitcast_convert_type(h_hi32, jnp.uint32)
        dh = bits.shape[1] // 2
        hp_ref[rows, :] = jnp.right_shift(bits[:, 0:dh], jnp.uint32(16)) | bits[:, dh:2 * dh]
        parts = jnp.dot(jnp.concatenate([h_hi, h_lo], axis=1), wr_ref[...], preferred_element_type=F32)
        logits = parts[:, 0:LANES] + parts[:, LANES:2 * LANES] + br_ref[...]
        lane = lax.broadcasted_iota(jnp.int32, (sub, LANES), 1)
        big = jnp.int32(LANES)

        def masked_softmax(valid):
            lg = jnp.where(valid, logits, NEG_INF)
            mx = jnp.max(lg, axis=1, keepdims=True)
            ex = jnp.where(valid, jnp.exp(lg - mx), 0.0)
            return ex / jnp.sum(ex, axis=1, keepdims=True)

        def first_max(vals, valid):
            top = jnp.max(jnp.where(valid, vals, -1.0), axis=1, keepdims=True)
            idx = jnp.min(jnp.where(valid & (vals == top), lane, big), axis=1, keepdims=True)
            return top, idx

        is_group = lane < N_GROUPS
        gp, gsel = first_max(masked_softmax(is_group), is_group)
        lo = ROUTE_LANE0 + EXP_PER_GROUP * gsel
        in_group = (lane >= lo) & (lane < lo + EXP_PER_GROUP)
        eprob = masked_softmax(in_group)
        p1, i1 = first_max(eprob, in_group)
        p2, i2 = first_max(eprob, in_group & (lane != i1))
        den = p1 + p2
        w1 = gp * (p1 / den)
        w2 = gp * (p2 / den)

        onehot = (lane == i1) | (lane == i2)
        before = jnp.dot(tril_ref[...], onehot.astype(BF16), preferred_element_type=F32)
        rk = before + cnt
        r1 = jnp.sum(jnp.where(lane == i1, rk, 0.0), axis=1, keepdims=True)
        r2 = jnp.sum(jnp.where(lane == i2, rk, 0.0), axis=1, keepdims=True)

        col = lax.broadcasted_iota(jnp.int32, (sub, ri_ref.shape[1]), 1)
        ri_ref[rows, :] = jnp.where(col == 0, i1 - ROUTE_LANE0,
                                    jnp.where(col == 1, i2 - ROUTE_LANE0,
                                              jnp.where(col == 2, r1.astype(jnp.int32),
                                                        jnp.where(col == 3, r2.astype(jnp.int32), 0))))
        rw_ref[rows, :] = jnp.where(col == 0, w1, jnp.where(col == 1, w2, 0.0))
        return cnt + jnp.sum(onehot.astype(F32), axis=0, keepdims=True)

    cnt = cnt_ref[0:1, :]
    for r0 in range(0, x_ref.shape[0], sub):
        cnt = route_rows(r0, cnt)
    cnt_ref[...] = jnp.broadcast_to(cnt, cnt_ref.shape)


def _post_call(x2, yconv, ynsa, wo, g1, b1, wr, br, tril):
    n_tok, d = x2.shape
    tm = POST_TM
    const = lambda a: pl.BlockSpec(a.shape, lambda i: (0, 0))
    return pl.pallas_call(
        _post_kernel,
        grid=(n_tok // tm,),
        in_specs=[
            pl.BlockSpec((tm, d), lambda i: (i, 0)),
            pl.BlockSpec((tm, yconv.shape[1]), lambda i: (i, 0)),
            pl.BlockSpec((tm, ynsa.shape[1]), lambda i: (i, 0)),
            const(wo), const(g1), const(b1), const(wr), const(br), const(tril),
        ],
        out_specs=(
            pl.BlockSpec((tm, d), lambda i: (i, 0)),
            pl.BlockSpec((tm, d // 2), lambda i: (i, 0)),
            pl.BlockSpec((tm, SUBLANES), lambda i: (i, 0)),
            pl.BlockSpec((tm, SUBLANES), lambda i: (i, 0)),
            pl.BlockSpec((SUBLANES, LANES), lambda i: (0, 0)),
        ),
        out_shape=(
            jax.ShapeDtypeStruct((n_tok, d), F32),
            jax.ShapeDtypeStruct((n_tok, d // 2), jnp.uint32),
            jax.ShapeDtypeStruct((n_tok, SUBLANES), jnp.int32),
            jax.ShapeDtypeStruct((n_tok, SUBLANES), F32),
            jax.ShapeDtypeStruct((SUBLANES, LANES), F32),
        ),
        compiler_params=_cparams(("arbitrary",)),
        name="out_proj_ln1_router",
    )(x2, yconv, ynsa, wo, g1, b1, wr, br, tril)


def _row_copy(src_ref, src_row, dst_ref, dst_row, sem):
    return pltpu.make_async_copy(src_ref.at[pl.ds(src_row, 1), :], dst_ref.at[pl.ds(dst_row, 1), :], sem)


def _dispatch_kernel(pends_ref, nu_ref, dest_ref, h_ref, xs_ref, zero_ref, sem, zsem):
    tm = h_ref.shape[0]
    n_chunk = xs_ref.shape[0] // MOE_CHUNK

    @pl.when(pl.program_id(0) == 0)
    def _():
        zero_ref[...] = jnp.zeros_like(zero_ref)

        def zero_chunk(row0):
            row0 = pl.multiple_of(row0, MOE_CHUNK)
            return pltpu.make_async_copy(zero_ref, xs_ref.at[pl.ds(row0, MOE_CHUNK), :], zsem)

        def each_padded_chunk(act):
            def per_expert(e, _):
                end = pends_ref[e]
                start = jnp.where(e > 0, pends_ref[jnp.maximum(e - 1, 0)], 0)

                @pl.when(end > start)
                def _():
                    act(zero_chunk(end - MOE_CHUNK))
                return 0

            lax.fori_loop(0, N_EXPERTS, per_expert, 0)

            def per_dead_chunk(c, _):
                act(zero_chunk(c * MOE_CHUNK))
                return 0

            lax.fori_loop(nu_ref[0], n_chunk, per_dead_chunk, 0)

        each_padded_chunk(lambda cp: cp.start())
        each_padded_chunk(lambda cp: cp.wait())

    def issue(r, _):
        for slot in range(2):
            _row_copy(h_ref, r, xs_ref, dest_ref[2 * r + slot], sem).start()
        return 0

    lax.fori_loop(0, tm, issue, 0, unroll=ROW_UNROLL)
    for slot in range(2):
        pltpu.make_async_copy(h_ref, xs_ref.at[pl.ds(0, tm), :], sem).wait()


def _dispatch_call(pends, n_used, dest, h1, n_rows):
    n_tok, d = h1.shape
    tm = ROW_TM
    grid_spec = pltpu.PrefetchScalarGridSpec(
        num_scalar_prefetch=2,
        grid=(n_tok // tm,),
        in_specs=[
            pl.BlockSpec((2 * tm,), lambda i, pe, nu: (i,), memory_space=pltpu.SMEM),
            pl.BlockSpec((tm, d), lambda i, pe, nu: (i, 0)),
        ],
        out_specs=pl.BlockSpec(memory_space=pl.ANY),
        scratch_shapes=[pltpu.VMEM((MOE_CHUNK, d), h1.dtype), pltpu.SemaphoreType.DMA, pltpu.SemaphoreType.DMA],
    )
    return pl.pallas_call(
        _dispatch_kernel,
        grid_spec=grid_spec,
        out_shape=jax.ShapeDtypeStruct((n_rows, d), h1.dtype),
        compiler_params=_cparams(("arbitrary",)),
        name="moe_dispatch",
    )(pends, n_used, dest, h1)


def _expert_kernel(ce_ref, nu_ref, xs_ref, wi_ref, wo_ref, o_ref, wib_ref, wob_ref):
    c = pl.program_id(0)
    live = c < nu_ref[0]

    @pl.when(live & ((c == 0) | (ce_ref[c] != ce_ref[jnp.maximum(c - 1, 0)])))
    def _():
        wib_ref[...] = wi_ref[0].astype(BF16)
        wob_ref[...] = wo_ref[0].astype(BF16)

    @pl.when(live)
    def _():
        pk = xs_ref[...]
        dh = pk.shape[1]
        x_lo = lax.bitcast_convert_type(jnp.left_shift(pk, jnp.uint32(16)), F32).astype(BF16)
        x_hi = lax.bitcast_convert_type(pk & jnp.uint32(0xFFFF0000), F32).astype(BF16)
        gu = (jnp.dot(x_lo, wib_ref[0:dh, :], preferred_element_type=F32)
              + jnp.dot(x_hi, wib_ref[dh:2 * dh, :], preferred_element_type=F32))
        gate = gu[:, 0:D_EXPERT]
        act = gate * jax.nn.sigmoid(gate) * gu[:, D_EXPERT:2 * D_EXPERT]
        o_ref[...] = jnp.dot(act.astype(BF16), wob_ref[...], preferred_element_type=F32)

    @pl.when(jnp.logical_not(live))
    def _():
        o_ref[...] = jnp.zeros_like(o_ref)


def _expert_call(chunk_exp, n_used, xs, wi, wo):
    n_rows, dh = xs.shape
    d = wo.shape[2]
    n_chunk = n_rows // MOE_CHUNK

    def live(c, nu):
        return jnp.maximum(jnp.minimum(c, nu[0] - 1), 0)

    grid_spec = pltpu.PrefetchScalarGridSpec(
        num_scalar_prefetch=2,
        grid=(n_chunk,),
        in_specs=[
            pl.BlockSpec((MOE_CHUNK, dh), lambda c, ce, nu: (live(c, nu), 0)),
            pl.BlockSpec((1,) + wi.shape[1:], lambda c, ce, nu: (ce[live(c, nu)], 0, 0)),
            pl.BlockSpec((1,) + wo.shape[1:], lambda c, ce, nu: (ce[live(c, nu)], 0, 0)),
        ],
        out_specs=pl.BlockSpec((MOE_CHUNK, d), lambda c, ce, nu: (c, 0)),
        scratch_shapes=[pltpu.VMEM(wi.shape[1:], BF16), pltpu.VMEM(wo.shape[1:], BF16)],
    )
    return pl.pallas_call(
        _expert_kernel,
        grid_spec=grid_spec,
        out_shape=jax.ShapeDtypeStruct((n_rows, d), F32),
        compiler_params=_cparams(("arbitrary",)),
        name="moe_experts",
    )(chunk_exp, n_used, xs, wi, wo)


def _combine_kernel(dest_ref, dest_next_ref, h1_ref, rw_ref, p_ref, ys_ref,
                    g2_ref, b2_ref, wp_ref, wg_ref, bg_ref, g3_ref, b3_ref,
                    o_ref, rows_ref, sems):
    tm = h1_ref.shape[0]
    step = pl.program_id(0)
    cur = step % 2

    def gather(idx_ref, buf):
        def issue(r, _):
            for slot in range(2):
                _row_copy(ys_ref, idx_ref[2 * r + slot], rows_ref.at[buf, slot], r, sems.at[buf]).start()
            return 0

        lax.fori_loop(0, tm, issue, 0, unroll=ROW_UNROLL)

    def await_rows(buf):
        for slot in range(2):
            pltpu.make_async_copy(ys_ref.at[pl.ds(0, tm), :], rows_ref.at[buf, slot], sems.at[buf]).wait()

    @pl.when(step == 0)
    def _():
        gather(dest_ref, 0)

    await_rows(cur)
    rw = rw_ref[...]
    ffn = rw[:, 0:1] * rows_ref[cur, 0] + rw[:, 1:2] * rows_ref[cur, 1]
    gather(dest_next_ref, 1 - cur)
    h2 =_layer_norm(ALPHA * h1_ref[...] + ffn, g2_ref[...], b2_ref[...])
    emb = jnp.dot(p_ref[...].astype(BF16), wp_ref[...], preferred_element_type=F32)
    gate = jax.nn.sigmoid(jnp.dot(h2.astype(BF16), wg_ref[...], preferred_element_type=F32) + bg_ref[...])
    o_ref[...] = _layer_norm(ALPHA * h2 + emb * gate, g3_ref[...], b3_ref[...])

    @pl.when(step == pl.num_programs(0) - 1)
    def _():
        await_rows(1 - cur)


def _combine_call(dest, h1, rw, p2, ys, g2, b2, wp, wg, bg, g3, b3):
    n_tok, d = h1.shape
    tm = ROW_TM
    const = lambda a: pl.BlockSpec(a.shape, lambda i: (0, 0))
    return pl.pallas_call(
        _combine_kernel,
        grid=(n_tok // tm,),
        in_specs=[
            pl.BlockSpec((2 * tm,), lambda i: (i,), memory_space=pltpu.SMEM),
            pl.BlockSpec((2 * tm,), lambda i: (jnp.minimum(i + 1, n_tok // tm - 1),), memory_space=pltpu.SMEM),
            pl.BlockSpec((tm, d), lambda i: (i, 0)),
            pl.BlockSpec((tm, rw.shape[1]), lambda i: (i, 0)),
            pl.BlockSpec((tm, p2.shape[1]), lambda i: (i, 0)),
            pl.BlockSpec(memory_space=pl.ANY),
            const(g2), const(b2), const(wp), const(wg), const(bg), const(g3), const(b3),
        ],
        out_specs=pl.BlockSpec((tm, d), lambda i: (i, 0)),
        out_shape=jax.ShapeDtypeStruct((n_tok, d), F32),
        scratch_shapes=[pltpu.VMEM((2, 2, tm, d), F32), pltpu.SemaphoreType.DMA((2,))],
        compiler_params=_cparams(("arbitrary",)),
        name="moe_combine_ln_ple",
    )(dest, dest, h1, rw, p2, ys, g2, b2, wp, wg, bg, g3, b3)


def _gate_columns():
    cols = np.zeros((N_KV, GATE_ROWS), np.int32)
    live = np.zeros((N_KV, GATE_ROWS), np.float32)
    for h in range(N_KV):
        for br in range(3):
            for g in range(GQA):
                cols[h, br * GQA + g] = (h * GQA + g) * 3 + br
                live[h, br * GQA + g] = 1.0
    return cols.reshape(-1), live.reshape(-1)


def _layer(x, p, w_in, w_conv, pe_ck, w1_ck, w2_ck, pe_cv, w1_cv, w2_cv, w_out, ln1_g, ln1_b,
           w_rg, b_rg, w_re, b_re, w_e_in, w_e_out, ln2_g, ln2_b, w_ple, w_ple_gate, b_ple_gate,
           ln3_g, ln3_b):
    bsz, s, d = x.shape
    n_tok = bsz * s
    row = lambda v: v.reshape(1, -1)

    c_q, c_kc, c_vc, c_ks, c_vs, c_kw, c_vw, c_g = 1536, 2048, 2176, 2304, 2432, 2560, 2688, 2816
    wn = jnp.concatenate([w_in[:, 0:c_q], w_in[:, c_kc:c_ks], w_in[:, c_ks:c_vs], w_in[:, c_kw:c_vw]],
                         axis=1).astype(BF16)
    gcols, glive = _gate_columns()
    w_gate = w_in[:, c_g:c_g + 3 * N_HEADS][:, gcols] * glive[None, :]
    wt = jnp.concatenate([w_in[:, c_q:c_kc] * Q_SCALE, w_in[:, c_vs:c_kw], w_in[:, c_vw:c_g], w_gate],
                         axis=1).T.astype(BF16)

    yconv, cmpkv, ksa, kw, qt, vst, vwt, gt = _proj_call(x, wn, wt, w_conv)

    half_blk = CMP_BLOCK // 2
    eye = jnp.eye(N_KV, dtype=F32)

    def per_head_w1(w1):
        w = w1.reshape(2, half_blk, HEAD_DIM, CMP_HIDDEN)
        w = jnp.einsum('ptdc,hg->pthdgc', w, eye)
        return w.reshape(2 * half_blk * N_KV * HEAD_DIM, N_KV * CMP_HIDDEN)

    def per_head_w2(w2):
        return jnp.einsum('cd,hg->hcgd', w2, eye).reshape(N_KV * CMP_HIDDEN, N_KV * HEAD_DIM)

    def per_head_pe(pe):
        return jnp.broadcast_to(pe.reshape(2, half_blk, 1, HEAD_DIM),
                                (2, half_blk, N_KV, HEAD_DIM)).reshape(2, half_blk * N_KV * HEAD_DIM)

    ckv = cmpkv.reshape(bsz, 2, s // CMP_STRIDE, CMP_STRIDE * N_KV * HEAD_DIM)
    pes = jnp.stack([per_head_pe(pe_ck), per_head_pe(pe_cv)])
    w1s = jnp.stack([per_head_w1(w1_ck), per_head_w1(w1_cv)]).astype(BF16)
    w2s = jnp.stack([per_head_w2(w2_ck), per_head_w2(w2_cv)]).astype(BF16)
    cmp_out = _compress_call(ckv, pes, w1s, w2s)
    kcmp = cmp_out[:, 0].astype(BF16)
    vct = cmp_out[:, 1].transpose(0, 2, 1).astype(BF16)

    kw = jnp.pad(kw, ((0, 0), (WINDOW, 0), (0, 0)))
    vwt = jnp.pad(vwt, ((0, 0), (0, 0), (WINDOW // LANES, 0), (0, 0), (0, 0)))
    ynsa = _attn_call(qt, gt, kcmp, vct, ksa, vst, kw, vwt)

    wr = jnp.zeros((d, LANES), F32).at[:, 0:N_GROUPS].set(w_rg)
    wr = wr.at[:, ROUTE_LANE0:ROUTE_LANE0 + N_EXPERTS].set(w_re)
    br = jnp.zeros((1, LANES), F32).at[0, 0:N_GROUPS].set(b_rg)
    br = br.at[0, ROUTE_LANE0:ROUTE_LANE0 + N_EXPERTS].set(b_re)
    wr_hi = wr.astype(BF16)
    wr_lo = (wr - wr_hi.astype(F32)).astype(BF16)
    wr = jnp.concatenate([jnp.concatenate([wr_hi, wr_lo], axis=1),
                          jnp.concatenate([wr_hi, jnp.zeros_like(wr_lo)], axis=1)], axis=0)
    tril = jnp.asarray(np.tril(np.ones((POST_SUB, POST_SUB), np.float32), -1), BF16)
    h1, h1_packed, ri, rw, cnt = _post_call(x.reshape(n_tok, d), yconv.reshape(n_tok, -1), ynsa.reshape(n_tok, -1),
                                 w_out.astype(BF16), row(ln1_g), row(ln1_b), wr, br, tril)

    counts = cnt[0, ROUTE_LANE0:ROUTE_LANE0 + N_EXPERTS].astype(jnp.int32)
    pcounts = (counts + MOE_CHUNK - 1) // MOE_CHUNK * MOE_CHUNK
    pends = jnp.cumsum(pcounts)
    pstarts = pends - pcounts
    dest = (pstarts[ri[:, 0:2]] + ri[:, 2:4]).reshape(-1).astype(jnp.int32)
    n_asg = n_tok * 2
    n_chunk = -(-n_asg // MOE_CHUNK) + N_EXPERTS
    chunk_row0 = jnp.arange(n_chunk, dtype=jnp.int32) * MOE_CHUNK
    chunk_exp = jnp.minimum(jnp.sum(pends[None, :] <= chunk_row0[:, None], axis=1), N_EXPERTS - 1).astype(jnp.int32)
    n_used = (pends[-1:] // MOE_CHUNK).astype(jnp.int32)

    xs = _dispatch_call(pends.astype(jnp.int32), n_used, dest, h1_packed, n_chunk * MOE_CHUNK)
    ys = _expert_call(chunk_exp, n_used, xs, w_e_in, w_e_out)
    out = _combine_call(dest, h1, rw, p.reshape(n_tok, -1), ys, row(ln2_g), row(ln2_b),
                        w_ple.astype(BF16), w_ple_gate.astype(BF16), row(b_ple_gate), row(ln3_g), row(ln3_b))
    return out.reshape(bsz, s, d)


def kernel(x, p, w_in, w_conv, pe_ck, w1_ck, w2_ck, pe_cv, w1_cv, w2_cv, w_out, ln1_g, ln1_b, w_rg, b_rg, w_re, b_re, w_e_in, w_e_out, ln2_g, ln2_b, w_ple, w_ple_gate, b_ple_gate, ln3_g, ln3_b):
    assert w_in.shape[0] == DEPTH, "residual scaling ALPHA is derived from DEPTH"
    h = x
    for i in range(DEPTH):
        h = _layer(h, p[i], w_in[i], w_conv[i], pe_ck[i], w1_ck[i], w2_ck[i], pe_cv[i], w1_cv[i], w2_cv[i],
                   w_out[i], ln1_g[i], ln1_b[i], w_rg[i], b_rg[i], w_re[i], b_re[i], w_e_in[i], w_e_out[i],
                   ln2_g[i], ln2_b[i], w_ple[i], w_ple_gate[i], b_ple_gate[i], ln3_g[i], ln3_b[i])
    return h
```

```python
import functools

import jax
import jax.numpy as jnp
import numpy as np
from jax import lax
from jax.experimental import pallas as pl
from jax.experimental.pallas import tpu as pltpu

F32 = jnp.float32
BF16 = jnp.bfloat16
HIGHEST = lax.Precision.HIGHEST

CONV_CH = 512
CONV_W = 3
N_HEADS = 8
HEAD_DIM = 64
N_KV = 2
GQA = N_HEADS // N_KV
CMP_BLOCK = 32
CMP_STRIDE = 16
CMP_HIDDEN = 2 * HEAD_DIM
SEL_BLOCK = 64
N_SEL = 16
WINDOW = 512
ATTN_SCALE = HEAD_DIM ** -0.5
Q_SCALE = ATTN_SCALE * float(np.log2(np.e))
FORCE_SCORE = 1e4
NEG_INF = -1e30
N_GROUPS = 4
EXP_PER_GROUP = 8
N_EXPERTS = N_GROUPS * EXP_PER_GROUP
D_EXPERT = 512
MOE_CHUNK = 512
DEPTH = 1
ALPHA = (2 * DEPTH) ** 0.25
LN_EPS = 1e-5

LANES = 128
SUBLANES = 8
VMEM_LIMIT = 56 * 1024 * 1024

PROJ_TM = 512
Q_TILE = 256
SEL_TK = 512
CMP_CHUNK = 128
WIN_KEYS = WINDOW + Q_TILE
POST_TM = 512
POST_SUB = 512
ROW_TM = 512
ROW_UNROLL = True
ROUTE_LANE0 = N_GROUPS
GATE_ROWS = 16
V_ROWS = HEAD_DIM + 16


def _cparams(sem, vmem=VMEM_LIMIT):
    return pltpu.CompilerParams(dimension_semantics=sem, vmem_limit_bytes=vmem)


def _layer_norm(v, g, b):
    mu = jnp.mean(v, axis=-1, keepdims=True)
    d = v - mu
    var = jnp.mean(d * d, axis=-1, keepdims=True)
    return d * lax.rsqrt(var + LN_EPS) * g + b


def _proj_kernel(x_ref, wn_ref, wt_ref, wc_ref,
                 yconv_ref, cmpkv_ref, ksa_ref, kw_ref, qt_ref, vst_ref, vwt_ref, gt_ref,
                 carry_ref):
    si = pl.program_id(1)
    tm = x_ref.shape[1]
    xb = x_ref[0].astype(BF16)
    zn = jnp.dot(xb, wn_ref[...], preferred_element_type=F32)
    zt = lax.dot_general(wt_ref[...], xb, (((1,), (1,)), ((), ())),
                         preferred_element_type=F32)

    cb = zn[:, 0:CONV_CH]
    u = zn[:, CONV_CH:2 * CONV_CH] * zn[:, 2 * CONV_CH:3 * CONV_CH]

    @pl.when(si == 0)
    def _():
        carry_ref[...] = jnp.zeros_like(carry_ref)

    prev = carry_ref[...]
    rows = lax.broadcasted_iota(jnp.int32, u.shape, 0)
    u1 = jnp.where(rows == 0, prev[7:8, :], pltpu.roll(u, 1, 0))
    u2 = jnp.where(rows == 0, prev[6:7, :],
                   jnp.where(rows == 1, prev[7:8, :], pltpu.roll(u, 2, 0)))
    w = wc_ref[...]
    yconv_ref[0] = (cb * (w[0:1, :] * u2 + w[1:2, :] * u1 + w[2:3, :] * u)).astype(BF16)
    carry_ref[...] = u[tm - SUBLANES:tm, :]

    cmpkv_ref[0, 0] = zn[:, 1536:1664]
    cmpkv_ref[0, 1] = zn[:, 1664:1792]
    kw_ref[0] = zn[:, 1920:2048].astype(BF16)
    ksa_ref[0, :, 0:LANES] = zn[:, 1792:1920].astype(BF16)
    n_hot = ksa_ref.shape[2] - LANES
    pos = si * tm + lax.broadcasted_iota(jnp.int32, (tm, n_hot), 0)
    hot = jnp.right_shift(pos, SEL_BLOCK.bit_length() - 1) == lax.broadcasted_iota(jnp.int32, (tm, n_hot), 1)
    ksa_ref[0, :, LANES:] = jnp.where(hot, 1.0, 0.0).astype(BF16)

    qt_ref[0] = zt[0:512, :].astype(BF16)
    ones = jnp.ones((V_ROWS - HEAD_DIM, LANES), BF16)
    for h in range(N_KV):
        for i in range(tm // LANES):
            cols = slice(i * LANES, (i + 1) * LANES)
            vst_ref[0, h, i, 0:HEAD_DIM, :] = zt[512 + h * 64:512 + (h + 1) * 64, cols].astype(BF16)
            vst_ref[0, h, i, HEAD_DIM:V_ROWS, :] = ones
            vwt_ref[0, h, i, 0:HEAD_DIM, :] = zt[640 + h * 64:640 + (h + 1) * 64, cols].astype(BF16)
            vwt_ref[0, h, i, HEAD_DIM:V_ROWS, :] = ones
    gt_ref[0] = jax.nn.sigmoid(zt[768:800, :])


def _proj_call(x, wn, wt, wc):
    bsz, s, d = x.shape
    tm = PROJ_TM
    nblk = s // LANES
    grid = (bsz, s // tm)
    ka = LANES + -(-(s // SEL_BLOCK) // LANES) * LANES
    out_shape = (
        jax.ShapeDtypeStruct((bsz, s, CONV_CH), BF16),
        jax.ShapeDtypeStruct((bsz, 2, s, LANES), F32),
        jax.ShapeDtypeStruct((bsz, s, ka), BF16),
        jax.ShapeDtypeStruct((bsz, s, LANES), BF16),
        jax.ShapeDtypeStruct((bsz, 512, s), BF16),
        jax.ShapeDtypeStruct((bsz, N_KV, nblk, V_ROWS, LANES), BF16),
        jax.ShapeDtypeStruct((bsz, N_KV, nblk, V_ROWS, LANES), BF16),
        jax.ShapeDtypeStruct((bsz, N_KV * GATE_ROWS, s), F32),
    )
    vspec = pl.BlockSpec((1, N_KV, tm // LANES, V_ROWS, LANES), lambda b, i: (b, 0, i, 0, 0))
    return pl.pallas_call(
        _proj_kernel,
        grid=grid,
        in_specs=[
            pl.BlockSpec((1, tm, d), lambda b, i: (b, i, 0)),
            pl.BlockSpec(wn.shape, lambda b, i: (0, 0)),
            pl.BlockSpec(wt.shape, lambda b, i: (0, 0)),
            pl.BlockSpec(wc.shape, lambda b, i: (0, 0)),
        ],
        out_specs=(
            pl.BlockSpec((1, tm, CONV_CH), lambda b, i: (b, i, 0)),
            pl.BlockSpec((1, 2, tm, LANES), lambda b, i: (b, 0, i, 0)),
            pl.BlockSpec((1, tm, ka), lambda b, i: (b, i, 0)),
            pl.BlockSpec((1, tm, LANES), lambda b, i: (b, i, 0)),
            pl.BlockSpec((1, 512, tm), lambda b, i: (b, 0, i)),
            vspec, vspec,
            pl.BlockSpec((1, N_KV * GATE_ROWS, tm), lambda b, i: (b, 0, i)),
        ),
        out_shape=out_shape,
        scratch_shapes=[pltpu.VMEM((SUBLANES, CONV_CH), F32)],
        compiler_params=_cparams(("arbitrary", "arbitrary")),
        name="in_proj_conv",
    )(x, wn, wt, wc)


def _compress_kernel(g_ref, pe_ref, w1_ref, w2_ref, o_ref):
    g = g_ref[0, 0]
    pe = pe_ref[0]
    half = g.shape[1]
    a_lo = jnp.dot((g + pe[0:1, :]).astype(BF16), w1_ref[0, 0:half, :], preferred_element_type=F32)
    a_hi = jnp.dot((g + pe[1:2, :]).astype(BF16), w1_ref[0, half:2 * half, :], preferred_element_type=F32)
    n = g.shape[0]
    hid = a_lo + pltpu.roll(a_hi, n - 1, 0)
    act = jax.nn.gelu(hid)
    o_ref[0, 0] = jnp.dot(act.astype(BF16), w2_ref[0], preferred_element_type=F32)


def _compress_call(ckv, pes, w1s, w2s):
    bsz, _, n, width = ckv.shape
    return pl.pallas_call(
        _compress_kernel,
        grid=(bsz, 2),
        in_specs=[
            pl.BlockSpec((1, 1, n, width), lambda b, k: (b, k, 0, 0)),
            pl.BlockSpec((1, 2, width), lambda b, k: (k, 0, 0)),
            pl.BlockSpec((1, 2 * width, N_KV * CMP_HIDDEN), lambda b, k: (k, 0, 0)),
            pl.BlockSpec((1, N_KV * CMP_HIDDEN, N_KV * HEAD_DIM), lambda b, k: (k, 0, 0)),
        ],
        out_specs=pl.BlockSpec((1, 1, n, N_KV * HEAD_DIM), lambda b, k: (b, k, 0, 0)),
        out_shape=jax.ShapeDtypeStruct((bsz, 2, n, N_KV * HEAD_DIM), F32),
        compiler_params=_cparams(("arbitrary", "arbitrary")),
        name="compress_mlp",
    )(ckv, pes, w1s, w2s)


def _attn_kernel(n_sel, n_top,
                 q_ref, g_ref, kc_ref, vct_ref, ksa_ref, vst_ref, kw_ref, vwt_ref,
                 o_ref, qa_ref, ps_ref, oc_ref, s_ref):
    qb = pl.program_id(2)
    q0 = qb * Q_TILE
    nq = GQA * Q_TILE
    blk = q_ref[0]
    q4 = jnp.concatenate([blk[g * HEAD_DIM:(g + 1) * HEAD_DIM, :] for g in range(GQA)], axis=1)
    lane = lax.broadcasted_iota(jnp.int32, (1, nq), 1)
    t4 = q0 + (lane & (Q_TILE - 1))
    kvh = pl.program_id(1)
    qa_ref[pl.ds(pl.multiple_of(kvh * HEAD_DIM, HEAD_DIM), HEAD_DIM), :] = q4
    qa_ref[pl.ds(pl.multiple_of((N_KV - 1 - kvh) * HEAD_DIM, HEAD_DIM), HEAD_DIM), :] = jnp.zeros_like(q4)
    q2 = qa_ref[0:N_KV * HEAD_DIM, :]

    n_cmp = kc_ref.shape[1]
    chunk = min(CMP_CHUNK, n_cmp)
    q_lane_tiles = Q_TILE // LANES
    for h in range(q_lane_tiles):
        ps_ref[h, 0:SUBLANES, :] = jnp.zeros((SUBLANES, LANES), F32)

    def cmp_branch(n):
        sc = jnp.dot(kc_ref[0, 0:n, :], q2, preferred_element_type=F32)
        last_c = jnp.right_shift(t4 - (CMP_BLOCK - 1), CMP_STRIDE.bit_length() - 1)
        scm = jnp.where(lax.broadcasted_iota(jnp.int32, (n, nq), 0) <= last_c, sc, NEG_INF)
        m_c = jnp.max(scm, axis=0, keepdims=True)
        e_c = jnp.exp2(scm - m_c)
        l_c = jnp.sum(e_c, axis=0, keepdims=True)
        p_c = e_c * jnp.where(last_c >= 0, 1.0 / l_c, 0.0)
        oc_ref[...] = jnp.dot(vct_ref[0, :, 0:n], p_c.astype(BF16), preferred_element_type=F32)
        psum = (p_c[:, 0:Q_TILE] + p_c[:, Q_TILE:2 * Q_TILE]
                + p_c[:, 2 * Q_TILE:3 * Q_TILE] + p_c[:, 3 * Q_TILE:4 * Q_TILE])
        for h in range(q_lane_tiles):
            ps_ref[h, SUBLANES:SUBLANES + n, :] = psum[:, h * LANES:(h + 1) * LANES]
            if n < n_cmp:
                ps_ref[h, SUBLANES + n:, :] = jnp.zeros((n_cmp - n, LANES), F32)

    last_visible = (q0 + Q_TILE - CMP_BLOCK) // CMP_STRIDE
    live_chunks = last_visible // chunk + 1
    for k in range(1, n_cmp // chunk + 1):
        pl.when(live_chunks == k)(functools.partial(cmp_branch, k * chunk))
    o_cmp = oc_ref[...]

    sw = jnp.dot(kw_ref[0, pl.ds(pl.multiple_of(q0, Q_TILE), WIN_KEYS), :], q2,
                 preferred_element_type=F32)
    ql = t4 - q0
    row = lax.broadcasted_iota(jnp.int32, (LANES, nq), 0)
    slabs = []
    for c in range(WIN_KEYS // LANES):
        slab = sw[c * LANES:(c + 1) * LANES, :]
        before_start = WINDOW - 1 - q0 - c * LANES
        if c * LANES < Q_TILE:
            slab = jnp.where(row > jnp.maximum(ql - c * LANES, before_start), slab, NEG_INF)
        elif (c + 1) * LANES > WINDOW:
            slab = jnp.where(row <= ql + (WINDOW - c * LANES), slab, NEG_INF)
        else:
            slab = jnp.where(row > before_start, slab, NEG_INF)
        slabs.append(slab)
    swm = jnp.concatenate(slabs, axis=0)
    m_w = jnp.max(swm, axis=0, keepdims=True)
    p_w = jnp.exp2(swm - m_w)
    wblk = q0 // LANES
    vwt = jnp.concatenate([vwt_ref[0, 0, wblk + i] for i in range(WIN_KEYS // LANES)], axis=1)
    acc_w = jnp.dot(vwt, p_w.astype(BF16), preferred_element_type=F32)
    o_win = acc_w[0:HEAD_DIM, :] * (1.0 / acc_w[HEAD_DIM:HEAD_DIM + 1, :])

    per_sel = SEL_BLOCK // CMP_STRIDE
    def importance(h):
        acc = ps_ref[h, pl.ds(SUBLANES - 1, n_sel, stride=per_sel), :]
        for off in range(CMP_BLOCK // CMP_STRIDE + per_sel - 2):
            acc = acc + ps_ref[h, pl.ds(SUBLANES + off, n_sel, stride=per_sel), :]
        return acc

    imp = jnp.concatenate([importance(h) for h in range(q_lane_tiles)], axis=1)

    jidx = lax.broadcasted_iota(jnp.int32, (n_sel, Q_TILE), 0)
    tq = q0 + lax.broadcasted_iota(jnp.int32, (n_sel, Q_TILE), 1)
    jt = jnp.right_shift(tq, SEL_BLOCK.bit_length() - 1)
    forced = (jidx == 0) | (jidx == jt) | (jidx == jt - 1)
    score = jnp.where(forced, FORCE_SCORE, imp)
    key = jnp.where(jidx > jt, -1, lax.bitcast_convert_type(score, jnp.int32))
    theta = jnp.zeros((1, Q_TILE), jnp.int32)
    for bit in range(30, -1, -1):
        cand = theta | (1 << bit)
        reach = jnp.sum((key >= cand).astype(jnp.int32), axis=0, keepdims=True)
        theta = jnp.where(reach >= n_top, cand, theta)
    above = key > theta
    tied = key == theta
    n_above = jnp.sum(above.astype(jnp.int32), axis=0, keepdims=True)
    lower = (lax.broadcasted_iota(jnp.int32, (n_sel, n_sel), 1)
             < lax.broadcasted_iota(jnp.int32, (n_sel, n_sel), 0)).astype(BF16)
    tied_before = jnp.dot(lower, tied.astype(BF16), preferred_element_type=F32)
    chosen = above | (tied & (tied_before < (n_top - n_above).astype(F32)))
    bias = jnp.where(chosen, 0.0, NEG_INF).astype(BF16)
    bias_row0 = N_KV * HEAD_DIM
    qa_ref[bias_row0:bias_row0 + n_sel, :] = jnp.concatenate([bias] * GQA, axis=1)
    if qa_ref.shape[0] > bias_row0 + n_sel:
        qa_ref[bias_row0 + n_sel:, :] = jnp.zeros((qa_ref.shape[0] - bias_row0 - n_sel, nq), BF16)

    nsub = SEL_TK // LANES

    def scores(c, buf):
        k0 = pl.multiple_of(c * SEL_TK, SEL_TK)
        s = jnp.dot(ksa_ref[0, pl.ds(k0, SEL_TK), :], qa_ref[...], preferred_element_type=F32)
        s_ref[buf] = s
        return jnp.max(s, axis=0, keepdims=True)

    def accumulate(c, buf, mx, m, acc, causal):
        s = s_ref[buf]
        if causal:
            kpos = c * SEL_TK + lax.broadcasted_iota(jnp.int32, (SEL_TK, nq), 0)
            s = jnp.where(kpos <= t4, s, NEG_INF)
            mx = jnp.max(s, axis=0, keepdims=True)
        m_new = jnp.maximum(m, mx)
        p = jnp.exp2(s - m_new)
        vt = jnp.concatenate([vst_ref[0, 0, c * nsub + i] for i in range(nsub)], axis=1)
        acc = jnp.exp2(m - m_new) * acc + jnp.dot(vt, p.astype(BF16), preferred_element_type=F32)
        return m_new, acc

    def pair_body(pi, carry):
        mx0, m, acc = carry
        c = 2 * pi
        mx1 = scores(c + 1, 1)
        m, acc = accumulate(c, 0, mx0, m, acc, False)
        mx0 = scores(c + 2, 0)
        m, acc = accumulate(c + 1, 1, mx1, m, acc, False)
        return mx0, m, acc

    n_full = qb // (SEL_TK // Q_TILE)
    init = (scores(0, 0), jnp.full((1, nq), NEG_INF, F32), jnp.zeros((V_ROWS, nq), F32))
    mx0, m_s, acc_s = lax.fori_loop(0, n_full // 2, pair_body, init)
    c_last = 2 * (n_full // 2)

    def leftover_then_own():
        mx1 = scores(c_last + 1, 1)
        m1, acc1 = accumulate(c_last, 0, mx0, m_s, acc_s, False)
        return accumulate(c_last + 1, 1, mx1, m1, acc1, True)[1]

    def own_only():
        return accumulate(c_last, 0, mx0, m_s, acc_s, True)[1]

    acc_s = lax.cond(n_full % 2 == 1, leftover_then_own, own_only)
    o_sel = acc_s[0:HEAD_DIM, :] * (1.0 / acc_s[HEAD_DIM:HEAD_DIM + 1, :])

    gt = g_ref[0]

    def gate(br):
        return jnp.concatenate([gt[br * GQA + g:br * GQA + g + 1, :] for g in range(GQA)], axis=1)

    ot = gate(0) * o_cmp + gate(1) * o_sel + gate(2) * o_win
    stacked = jnp.concatenate([ot[:, g * Q_TILE:(g + 1) * Q_TILE] for g in range(GQA)], axis=0)
    o_ref[0] = stacked.T.astype(BF16)


def _attn_call(qt, gt, kcmp, vct, ksa, vst, kw, vwt):
    bsz, _, s = qt.shape
    n_sel = s // SEL_BLOCK
    n_top = min(N_SEL, n_sel)
    n_cmp = kcmp.shape[1]
    ka = ksa.shape[2]
    nblk = s // LANES
    grid = (bsz, N_KV, s // Q_TILE)
    kern = functools.partial(_attn_kernel, n_sel, n_top)
    return pl.pallas_call(
        kern,
        grid=grid,
        in_specs=[
            pl.BlockSpec((1, GQA * HEAD_DIM, Q_TILE), lambda b, h, i: (b, h, i)),
            pl.BlockSpec((1, GATE_ROWS, Q_TILE), lambda b, h, i: (b, h, i)),
            pl.BlockSpec((1, n_cmp, N_KV * HEAD_DIM), lambda b, h, i: (b, 0, 0)),
            pl.BlockSpec((1, HEAD_DIM, n_cmp), lambda b, h, i: (b, h, 0)),
            pl.BlockSpec((1, s, ka), lambda b, h, i: (b, 0, 0)),
            pl.BlockSpec((1, 1, nblk, V_ROWS, LANES), lambda b, h, i: (b, h, 0, 0, 0)),
            pl.BlockSpec((1, s + WINDOW, N_KV * HEAD_DIM), lambda b, h, i: (b, 0, 0)),
            pl.BlockSpec((1, 1, nblk + WINDOW // LANES, V_ROWS, LANES), lambda b, h, i: (b, h, 0, 0, 0)),
        ],
        out_specs=pl.BlockSpec((1, Q_TILE, GQA * HEAD_DIM), lambda b, h, i: (b, i, h)),
        out_shape=jax.ShapeDtypeStruct((bsz, s, N_HEADS * HEAD_DIM), BF16),
        scratch_shapes=[pltpu.VMEM((ka, GQA * Q_TILE), BF16),
                        pltpu.VMEM((Q_TILE // LANES, n_cmp + SUBLANES, LANES), F32),
                        pltpu.VMEM((HEAD_DIM, GQA * Q_TILE), F32),
                        pltpu.VMEM((2, SEL_TK, GQA * Q_TILE), F32)],
        compiler_params=_cparams(("arbitrary", "arbitrary", "arbitrary")),
        name="nsa_attention",
    )(qt, gt, kcmp, vct, ksa, vst, kw, vwt)


def _post_kernel(x_ref, yc_ref, yn_ref, wo_ref, g1_ref, b1_ref, wr_ref, br_ref, tril_ref,
                 h1_ref, hp_ref, ri_ref, rw_ref, cnt_ref):
    step = pl.program_id(0)
    half = yc_ref.shape[1]
    sub = tril_ref.shape[0]

    @pl.when(step == 0)
    def _():
        cnt_ref[...] = jnp.zeros_like(cnt_ref)

    def route_rows(r0, cnt):
        rows = slice(r0, r0 + sub)
        mix = (jnp.dot(yc_ref[rows, :], wo_ref[0:half, :], preferred_element_type=F32)
               + jnp.dot(yn_ref[rows, :], wo_ref[half:2 * half, :], preferred_element_type=F32))
        h1 = _layer_norm(ALPHA * x_ref[rows, :] + mix, g1_ref[...], b1_ref[...])
        h1_ref[rows, :] = h1

        h_hi = h1.astype(BF16)
        h_hi32 = h_hi.astype(F32)
        h_lo = (h1 - h_hi32).astype(BF16)
        bits = lax.bitcast_convert_type(h_hi32, jnp.uint32)
        dh = bits.shape[1] // 2
        hp_ref[rows, :] = jnp.right_shift(bits[:, 0:dh], jnp.uint32(16)) | bits[:, dh:2 * dh]
        parts = jnp.dot(jnp.concatenate([h_hi, h_lo], axis=1), wr_ref[...], preferred_element_type=F32)
        logits = parts[:, 0:LANES] + parts[:, LANES:2 * LANES] + br_ref[...]
        lane = lax.broadcasted_iota(jnp.int32, (sub, LANES), 1)
        big = jnp.int32(LANES)

        def masked_softmax(valid):
            lg = jnp.where(valid, logits, NEG_INF)
            mx = jnp.max(lg, axis=1, keepdims=True)
            ex = jnp.where(valid, jnp.exp(lg - mx), 0.0)
            return ex / jnp.sum(ex, axis=1, keepdims=True)

        def first_max(vals, valid):
            top = jnp.max(jnp.where(valid, vals, -1.0), axis=1, keepdims=True)
            idx = jnp.min(jnp.where(valid & (vals == top), lane, big), axis=1, keepdims=True)
            return top, idx

        is_group = lane < N_GROUPS
        gp, gsel = first_max(masked_softmax(is_group), is_group)
        lo = ROUTE_LANE0 + EXP_PER_GROUP * gsel
        in_group = (lane >= lo) & (lane < lo + EXP_PER_GROUP)
        eprob = masked_softmax(in_group)
        p1, i1 = first_max(eprob, in_group)
        p2, i2 = first_max(eprob, in_group & (lane != i1))
        den = p1 + p2
        w1 = gp * (p1 / den)
        w2 = gp * (p2 / den)

        onehot = (lane == i1) | (lane == i2)
        before = jnp.dot(tril_ref[...], onehot.astype(BF16), preferred_element_type=F32)
        rk = before + cnt
        r1 = jnp.sum(jnp.where(lane == i1, rk, 0.0), axis=1, keepdims=True)
        r2 = jnp.sum(jnp.where(lane == i2, rk, 0.0), axis=1, keepdims=True)

        col = lax.broadcasted_iota(jnp.int32, (sub, ri_ref.shape[1]), 1)
        ri_ref[rows, :] = jnp.where(col == 0, i1 - ROUTE_LANE0,
                                    jnp.where(col == 1, i2 - ROUTE_LANE0,
                                              jnp.where(col == 2, r1.astype(jnp.int32),
                                                        jnp.where(col == 3, r2.astype(jnp.int32), 0))))
        rw_ref[rows, :] = jnp.where(col == 0, w1, jnp.where(col == 1, w2, 0.0))
        return cnt + jnp.sum(onehot.astype(F32), axis=0, keepdims=True)

    cnt = cnt_ref[0:1, :]
    for r0 in range(0, x_ref.shape[0], sub):
        cnt = route_rows(r0, cnt)
    cnt_ref[...] = jnp.broadcast_to(cnt, cnt_ref.shape)


def _post_call(x2, yconv, ynsa, wo, g1, b1, wr, br, tril):
    n_tok, d = x2.shape
    tm = POST_TM
    const = lambda a: pl.BlockSpec(a.shape, lambda i: (0, 0))
    return pl.pallas_call(
        _post_kernel,
        grid=(n_tok // tm,),
        in_specs=[
            pl.BlockSpec((tm, d), lambda i: (i, 0)),
            pl.BlockSpec((tm, yconv.shape[1]), lambda i: (i, 0)),
            pl.BlockSpec((tm, ynsa.shape[1]), lambda i: (i, 0)),
            const(wo), const(g1), const(b1), const(wr), const(br), const(tril),
        ],
        out_specs=(
            pl.BlockSpec((tm, d), lambda i: (i, 0)),
            pl.BlockSpec((tm, d // 2), lambda i: (i, 0)),
            pl.BlockSpec((tm, SUBLANES), lambda i: (i, 0)),
            pl.BlockSpec((tm, SUBLANES), lambda i: (i, 0)),
            pl.BlockSpec((SUBLANES, LANES), lambda i: (0, 0)),
        ),
        out_shape=(
            jax.ShapeDtypeStruct((n_tok, d), F32),
            jax.ShapeDtypeStruct((n_tok, d // 2), jnp.uint32),
            jax.ShapeDtypeStruct((n_tok, SUBLANES), jnp.int32),
            jax.ShapeDtypeStruct((n_tok, SUBLANES), F32),
            jax.ShapeDtypeStruct((SUBLANES, LANES), F32),
        ),
        compiler_params=_cparams(("arbitrary",)),
        name="out_proj_ln1_router",
    )(x2, yconv, ynsa, wo, g1, b1, wr, br, tril)


def _row_copy(src_ref, src_row, dst_ref, dst_row, sem):
    return pltpu.make_async_copy(src_ref.at[pl.ds(src_row, 1), :], dst_ref.at[pl.ds(dst_row, 1), :], sem)


ROUTE_COLS = SUBLANES


def _sorted_row(pstart_ref, route_ref, r, slot):
    return pstart_ref[route_ref[ROUTE_COLS * r + slot]] + route_ref[ROUTE_COLS * r + 2 + slot]


def _dispatch_kernel(pends_ref, nu_ref, pstart_ref, route_ref, h_ref, xs_ref, zero_ref, sem, zsem):
    tm = h_ref.shape[0]
    n_chunk = xs_ref.shape[0] // MOE_CHUNK

    @pl.when(pl.program_id(0) == 0)
    def _():
        zero_ref[...] = jnp.zeros_like(zero_ref)

        def zero_chunk(row0):
            row0 = pl.multiple_of(row0, MOE_CHUNK)
            return pltpu.make_async_copy(zero_ref, xs_ref.at[pl.ds(row0, MOE_CHUNK), :], zsem)

        def each_padded_chunk(act):
            def per_expert(e, _):
                end = pends_ref[e]
                start = jnp.where(e > 0, pends_ref[jnp.maximum(e - 1, 0)], 0)

                @pl.when(end > start)
                def _():
                    act(zero_chunk(end - MOE_CHUNK))
                return 0

            lax.fori_loop(0, N_EXPERTS, per_expert, 0)

            def per_dead_chunk(c, _):
                act(zero_chunk(c * MOE_CHUNK))
                return 0

            lax.fori_loop(nu_ref[0], n_chunk, per_dead_chunk, 0)

        each_padded_chunk(lambda cp: cp.start())
        each_padded_chunk(lambda cp: cp.wait())

    def issue(r, _):
        for slot in range(2):
            _row_copy(h_ref, r, xs_ref, _sorted_row(pstart_ref, route_ref, r, slot), sem).start()
        return 0

    lax.fori_loop(0, tm, issue, 0, unroll=ROW_UNROLL)
    for slot in range(2):
        pltpu.make_async_copy(h_ref, xs_ref.at[pl.ds(0, tm), :], sem).wait()


def _dispatch_call(pends, n_used, pstarts, route, h1, n_rows):
    n_tok, d = h1.shape
    tm = ROW_TM
    grid_spec = pltpu.PrefetchScalarGridSpec(
        num_scalar_prefetch=3,
        grid=(n_tok // tm,),
        in_specs=[
            pl.BlockSpec((ROUTE_COLS * tm,), lambda i, pe, nu, ps: (i,), memory_space=pltpu.SMEM),
            pl.BlockSpec((tm, d), lambda i, pe, nu, ps: (i, 0)),
        ],
        out_specs=pl.BlockSpec(memory_space=pl.ANY),
        scratch_shapes=[pltpu.VMEM((MOE_CHUNK, d), h1.dtype), pltpu.SemaphoreType.DMA, pltpu.SemaphoreType.DMA],
    )
    return pl.pallas_call(
        _dispatch_kernel,
        grid_spec=grid_spec,
        out_shape=jax.ShapeDtypeStruct((n_rows, d), h1.dtype),
        compiler_params=_cparams(("arbitrary",)),
        name="moe_dispatch",
    )(pends, n_used, pstarts, route, h1)


def _expert_kernel(ce_ref, nu_ref, xs_ref, wi_ref, wo_ref, o_ref, wib_ref, wob_ref):
    c = pl.program_id(0)
    live = c < nu_ref[0]

    @pl.when(live & ((c == 0) | (ce_ref[c] != ce_ref[jnp.maximum(c - 1, 0)])))
    def _():
        wib_ref[...] = wi_ref[0].astype(BF16)
        wob_ref[...] = wo_ref[0].astype(BF16)

    @pl.when(live)
    def _():
        pk = xs_ref[...]
        dh = pk.shape[1]
        x_lo = lax.bitcast_convert_type(jnp.left_shift(pk, jnp.uint32(16)), F32).astype(BF16)
        x_hi = lax.bitcast_convert_type(pk & jnp.uint32(0xFFFF0000), F32).astype(BF16)
        gu = (jnp.dot(x_lo, wib_ref[0:dh, :], preferred_element_type=F32)
              + jnp.dot(x_hi, wib_ref[dh:2 * dh, :], preferred_element_type=F32))
        gate = gu[:, 0:D_EXPERT]
        act = gate * jax.nn.sigmoid(gate) * gu[:, D_EXPERT:2 * D_EXPERT]
        o_ref[...] = jnp.dot(act.astype(BF16), wob_ref[...], preferred_element_type=F32)

    @pl.when(jnp.logical_not(live))
    def _():
        o_ref[...] = jnp.zeros_like(o_ref)


def _expert_call(chunk_exp, n_used, xs, wi, wo):
    n_rows, dh = xs.shape
    d = wo.shape[2]
    n_chunk = n_rows // MOE_CHUNK

    def live(c, nu):
        return jnp.maximum(jnp.minimum(c, nu[0] - 1), 0)

    grid_spec = pltpu.PrefetchScalarGridSpec(
        num_scalar_prefetch=2,
        grid=(n_chunk,),
        in_specs=[
            pl.BlockSpec((MOE_CHUNK, dh), lambda c, ce, nu: (live(c, nu), 0)),
            pl.BlockSpec((1,) + wi.shape[1:], lambda c, ce, nu: (ce[live(c, nu)], 0, 0)),
            pl.BlockSpec((1,) + wo.shape[1:], lambda c, ce, nu: (ce[live(c, nu)], 0, 0)),
        ],
        out_specs=pl.BlockSpec((MOE_CHUNK, d), lambda c, ce, nu: (c, 0)),
        scratch_shapes=[pltpu.VMEM(wi.shape[1:], BF16), pltpu.VMEM(wo.shape[1:], BF16)],
    )
    return pl.pallas_call(
        _expert_kernel,
        grid_spec=grid_spec,
        out_shape=jax.ShapeDtypeStruct((n_rows, d), F32),
        compiler_params=_cparams(("arbitrary",)),
        name="moe_experts",
    )(chunk_exp, n_used, xs, wi, wo)


def _combine_kernel(pstart_ref, dest_ref, dest_next_ref, h1_ref, rw_ref, p_ref, ys_ref,
                    g2_ref, b2_ref, wp_ref, wg_ref, bg_ref, g3_ref, b3_ref,
                    o_ref, rows_ref, sems):
    tm = h1_ref.shape[0]
    step = pl.program_id(0)
    cur = step % 2

    def gather(idx_ref, buf):
        def issue(r, _):
            for slot in range(2):
                _row_copy(ys_ref, _sorted_row(pstart_ref, idx_ref, r, slot), rows_ref.at[buf, slot], r,
                          sems.at[buf]).start()
            return 0

        lax.fori_loop(0, tm, issue, 0, unroll=ROW_UNROLL)

    def await_rows(buf):
        for slot in range(2):
            pltpu.make_async_copy(ys_ref.at[pl.ds(0, tm), :], rows_ref.at[buf, slot], sems.at[buf]).wait()

    @pl.when(step == 0)
    def _():
        gather(dest_ref, 0)

    await_rows(cur)
    rw = rw_ref[...]
    ffn = rw[:, 0:1] * rows_ref[cur, 0] + rw[:, 1:2] * rows_ref[cur, 1]
    gather(dest_next_ref, 1 - cur)
    h2 =_layer_norm(ALPHA * h1_ref[...] + ffn, g2_ref[...], b2_ref[...])
    emb = jnp.dot(p_ref[...].astype(BF16), wp_ref[...], preferred_element_type=F32)
    gate = jax.nn.sigmoid(jnp.dot(h2.astype(BF16), wg_ref[...], preferred_element_type=F32) + bg_ref[...])
    o_ref[...] = _layer_norm(ALPHA * h2 + emb * gate, g3_ref[...], b3_ref[...])

    @pl.when(step == pl.num_programs(0) - 1)
    def _():
        await_rows(1 - cur)


def _combine_call(pstarts, route, h1, rw, p2, ys, g2, b2, wp, wg, bg, g3, b3):
    n_tok, d = h1.shape
    tm = ROW_TM
    const = lambda a: pl.BlockSpec(a.shape, lambda i: (0, 0))
    return pl.pallas_call(
        _combine_kernel,
        grid=(n_tok // tm,),
        in_specs=[
            pl.BlockSpec(memory_space=pltpu.SMEM),
            pl.BlockSpec((ROUTE_COLS * tm,), lambda i: (i,), memory_space=pltpu.SMEM),
            pl.BlockSpec((ROUTE_COLS * tm,), lambda i: (jnp.minimum(i + 1, n_tok // tm - 1),),
                         memory_space=pltpu.SMEM),
            pl.BlockSpec((tm, d), lambda i: (i, 0)),
            pl.BlockSpec((tm, rw.shape[1]), lambda i: (i, 0)),
            pl.BlockSpec((tm, p2.shape[1]), lambda i: (i, 0)),
            pl.BlockSpec(memory_space=pl.ANY),
            const(g2), const(b2), const(wp), const(wg), const(bg), const(g3), const(b3),
        ],
        out_specs=pl.BlockSpec((tm, d), lambda i: (i, 0)),
        out_shape=jax.ShapeDtypeStruct((n_tok, d), F32),
        scratch_shapes=[pltpu.VMEM((2, 2, tm, d), F32), pltpu.SemaphoreType.DMA((2,))],
        compiler_params=_cparams(("arbitrary",)),
        name="moe_combine_ln_ple",
    )(pstarts, route, route, h1, rw, p2, ys, g2, b2, wp, wg, bg, g3, b3)


def _gate_columns():
    cols = np.zeros((N_KV, GATE_ROWS), np.int32)
    live = np.zeros((N_KV, GATE_ROWS), np.float32)
    for h in range(N_KV):
        for br in range(3):
            for g in range(GQA):
                cols[h, br * GQA + g] = (h * GQA + g) * 3 + br
                live[h, br * GQA + g] = 1.0
    return cols.reshape(-1), live.reshape(-1)


def _layer(x, p, w_in, w_conv, pe_ck, w1_ck, w2_ck, pe_cv, w1_cv, w2_cv, w_out, ln1_g, ln1_b,
           w_rg, b_rg, w_re, b_re, w_e_in, w_e_out, ln2_g, ln2_b, w_ple, w_ple_gate, b_ple_gate,
           ln3_g, ln3_b):
    bsz, s, d = x.shape
    n_tok = bsz * s
    row = lambda v: v.reshape(1, -1)

    c_q, c_kc, c_vc, c_ks, c_vs, c_kw, c_vw, c_g = 1536, 2048, 2176, 2304, 2432, 2560, 2688, 2816
    wn = jnp.concatenate([w_in[:, 0:c_q], w_in[:, c_kc:c_ks], w_in[:, c_ks:c_vs], w_in[:, c_kw:c_vw]],
                         axis=1).astype(BF16)
    gcols, glive = _gate_columns()
    w_gate = w_in[:, c_g:c_g + 3 * N_HEADS][:, gcols] * glive[None, :]
    wt = jnp.concatenate([w_in[:, c_q:c_kc] * Q_SCALE, w_in[:, c_vs:c_kw], w_in[:, c_vw:c_g], w_gate],
                         axis=1).T.astype(BF16)

    yconv, cmpkv, ksa, kw, qt, vst, vwt, gt = _proj_call(x, wn, wt, w_conv)

    half_blk = CMP_BLOCK // 2
    eye = jnp.eye(N_KV, dtype=F32)

    def per_head_w1(w1):
        w = w1.reshape(2, half_blk, HEAD_DIM, CMP_HIDDEN)
        w = jnp.einsum('ptdc,hg->pthdgc', w, eye)
        return w.reshape(2 * half_blk * N_KV * HEAD_DIM, N_KV * CMP_HIDDEN)

    def per_head_w2(w2):
        return jnp.einsum('cd,hg->hcgd', w2, eye).reshape(N_KV * CMP_HIDDEN, N_KV * HEAD_DIM)

    def per_head_pe(pe):
        return jnp.broadcast_to(pe.reshape(2, half_blk, 1, HEAD_DIM),
                                (2, half_blk, N_KV, HEAD_DIM)).reshape(2, half_blk * N_KV * HEAD_DIM)

    ckv = cmpkv.reshape(bsz, 2, s // CMP_STRIDE, CMP_STRIDE * N_KV * HEAD_DIM)
    pes = jnp.stack([per_head_pe(pe_ck), per_head_pe(pe_cv)])
    w1s = jnp.stack([per_head_w1(w1_ck), per_head_w1(w1_cv)]).astype(BF16)
    w2s = jnp.stack([per_head_w2(w2_ck), per_head_w2(w2_cv)]).astype(BF16)
    cmp_out = _compress_call(ckv, pes, w1s, w2s)
    kcmp = cmp_out[:, 0].astype(BF16)
    vct = cmp_out[:, 1].transpose(0, 2, 1).astype(BF16)

    kw = jnp.pad(kw, ((0, 0), (WINDOW, 0), (0, 0)))
    vwt = jnp.pad(vwt, ((0, 0), (0, 0), (WINDOW // LANES, 0), (0, 0), (0, 0)))
    ynsa = _attn_call(qt, gt, kcmp, vct, ksa, vst, kw, vwt)

    wr = jnp.zeros((d, LANES), F32).at[:, 0:N_GROUPS].set(w_rg)
    wr = wr.at[:, ROUTE_LANE0:ROUTE_LANE0 + N_EXPERTS].set(w_re)
    br = jnp.zeros((1, LANES), F32).at[0, 0:N_GROUPS].set(b_rg)
    br = br.at[0, ROUTE_LANE0:ROUTE_LANE0 + N_EXPERTS].set(b_re)
    wr_hi = wr.astype(BF16)
    wr_lo = (wr - wr_hi.astype(F32)).astype(BF16)
    wr = jnp.concatenate([jnp.concatenate([wr_hi, wr_lo], axis=1),
                          jnp.concatenate([wr_hi, jnp.zeros_like(wr_lo)], axis=1)], axis=0)
    tril = jnp.asarray(np.tril(np.ones((POST_SUB, POST_SUB), np.float32), -1), BF16)
    h1, h1_packed, ri, rw, cnt = _post_call(x.reshape(n_tok, d), yconv.reshape(n_tok, -1), ynsa.reshape(n_tok, -1),
                                 w_out.astype(BF16), row(ln1_g), row(ln1_b), wr, br, tril)

    counts = cnt[0, ROUTE_LANE0:ROUTE_LANE0 + N_EXPERTS].astype(jnp.int32)
    pcounts = (counts + MOE_CHUNK - 1) // MOE_CHUNK * MOE_CHUNK
    pends = jnp.cumsum(pcounts)
    pstarts = (pends - pcounts).astype(jnp.int32)
    route = ri.reshape(-1)
    n_asg = n_tok * 2
    n_chunk = -(-n_asg // MOE_CHUNK) + N_EXPERTS
    chunk_row0 = jnp.arange(n_chunk, dtype=jnp.int32) * MOE_CHUNK
    chunk_exp = jnp.minimum(jnp.sum(pends[None, :] <= chunk_row0[:, None], axis=1), N_EXPERTS - 1).astype(jnp.int32)
    n_used = (pends[-1:] // MOE_CHUNK).astype(jnp.int32)

    xs = _dispatch_call(pends.astype(jnp.int32), n_used, pstarts, route, h1_packed, n_chunk * MOE_CHUNK)
    ys = _expert_call(chunk_exp, n_used, xs, w_e_in, w_e_out)
    out = _combine_call(pstarts, route, h1, rw, p.reshape(n_tok, -1), ys, row(ln2_g), row(ln2_b),
                        w_ple.astype(BF16), w_ple_gate.astype(BF16), row(b_ple_gate), row(ln3_g), row(ln3_b))
    return out.reshape(bsz, s, d)


def kernel(x, p, w_in, w_conv, pe_ck, w1_ck, w2_ck, pe_cv, w1_cv, w2_cv, w_out, ln1_g, ln1_b, w_rg, b_rg, w_re, b_re, w_e_in, w_e_out, ln2_g, ln2_b, w_ple, w_ple_gate, b_ple_gate, ln3_g, ln3_b):
    assert w_in.shape[0] == DEPTH, "residual scaling ALPHA is derived from DEPTH"
    h = x
    for i in range(DEPTH):
        h = _layer(h, p[i], w_in[i], w_conv[i], pe_ck[i], w1_ck[i], w2_ck[i], pe_cv[i], w1_cv[i], w2_cv[i],
                   w_out[i], ln1_g[i], ln1_b[i], w_rg[i], b_rg[i], w_re[i], b_re[i], w_e_in[i], w_e_out[i],
                   ln2_g[i], ln2_b[i], w_ple[i], w_ple_gate[i], b_ple_gate[i], ln3_g[i], ln3_b[i])
    return h
```

```python
import functools

import jax
import jax.numpy as jnp
import numpy as np
from jax import lax
from jax.experimental import pallas as pl
from jax.experimental.pallas import tpu as pltpu

F32 = jnp.float32
BF16 = jnp.bfloat16
HIGHEST = lax.Precision.HIGHEST

CONV_CH = 512
CONV_W = 3
N_HEADS = 8
HEAD_DIM = 64
N_KV = 2
GQA = N_HEADS // N_KV
CMP_BLOCK = 32
CMP_STRIDE = 16
CMP_HIDDEN = 2 * HEAD_DIM
SEL_BLOCK = 64
N_SEL = 16
WINDOW = 512
ATTN_SCALE = HEAD_DIM ** -0.5
Q_SCALE = ATTN_SCALE * float(np.log2(np.e))
FORCE_SCORE = 1e4
NEG_INF = -1e30
N_GROUPS = 4
EXP_PER_GROUP = 8
N_EXPERTS = N_GROUPS * EXP_PER_GROUP
D_EXPERT = 512
MOE_CHUNK = 512
DEPTH = 1
ALPHA = (2 * DEPTH) ** 0.25
LN_EPS = 1e-5

LANES = 128
SUBLANES = 8
VMEM_LIMIT = 56 * 1024 * 1024

PROJ_TM = 512
Q_TILE = 256
SEL_TK = 512
CMP_CHUNK = 128
WIN_KEYS = WINDOW + Q_TILE
POST_TM = 512
POST_SUB = 512
ROW_TM = 512
ROW_UNROLL = True
ROUTE_LANE0 = N_GROUPS
GATE_ROWS = 16
V_ROWS = HEAD_DIM + 16


def _cparams(sem, vmem=VMEM_LIMIT):
    return pltpu.CompilerParams(dimension_semantics=sem, vmem_limit_bytes=vmem)


def _layer_norm(v, g, b):
    mu = jnp.mean(v, axis=-1, keepdims=True)
    d = v - mu
    var = jnp.mean(d * d, axis=-1, keepdims=True)
    return d * lax.rsqrt(var + LN_EPS) * g + b


def _proj_kernel(x_ref, wn_ref, wt_ref, wc_ref,
                 yconv_ref, cmpkv_ref, ksa_ref, kw_ref, qt_ref, vst_ref, vwt_ref, gt_ref,
                 carry_ref):
    si = pl.program_id(1)
    tm = x_ref.shape[1]
    xb = x_ref[0].astype(BF16)
    zn = jnp.dot(xb, wn_ref[...], preferred_element_type=F32)
    zt = lax.dot_general(wt_ref[...], xb, (((1,), (1,)), ((), ())),
                         preferred_element_type=F32)

    cb = zn[:, 0:CONV_CH]
    u = zn[:, CONV_CH:2 * CONV_CH] * zn[:, 2 * CONV_CH:3 * CONV_CH]

    @pl.when(si == 0)
    def _():
        carry_ref[...] = jnp.zeros_like(carry_ref)

    prev = carry_ref[...]
    rows = lax.broadcasted_iota(jnp.int32, u.shape, 0)
    u1 = jnp.where(rows == 0, prev[7:8, :], pltpu.roll(u, 1, 0))
    u2 = jnp.where(rows == 0, prev[6:7, :],
                   jnp.where(rows == 1, prev[7:8, :], pltpu.roll(u, 2, 0)))
    w = wc_ref[...]
    yconv_ref[0] = (cb * (w[0:1, :] * u2 + w[1:2, :] * u1 + w[2:3, :] * u)).astype(BF16)
    carry_ref[...] = u[tm - SUBLANES:tm, :]

    cmpkv_ref[0, 0] = zn[:, 1536:1664]
    cmpkv_ref[0, 1] = zn[:, 1664:1792]
    kw_ref[0] = zn[:, 1920:2048].astype(BF16)
    ksa_ref[0, :, 0:LANES] = zn[:, 1792:1920].astype(BF16)
    n_hot = ksa_ref.shape[2] - LANES
    pos = si * tm + lax.broadcasted_iota(jnp.int32, (tm, n_hot), 0)
    hot = jnp.right_shift(pos, SEL_BLOCK.bit_length() - 1) == lax.broadcasted_iota(jnp.int32, (tm, n_hot), 1)
    ksa_ref[0, :, LANES:] = jnp.where(hot, 1.0, 0.0).astype(BF16)

    qt_ref[0] = zt[0:512, :].astype(BF16)
    ones = jnp.ones((V_ROWS - HEAD_DIM, LANES), BF16)
    for h in range(N_KV):
        for i in range(tm // LANES):
            cols = slice(i * LANES, (i + 1) * LANES)
            vst_ref[0, h, i, 0:HEAD_DIM, :] = zt[512 + h * 64:512 + (h + 1) * 64, cols].astype(BF16)
            vst_ref[0, h, i, HEAD_DIM:V_ROWS, :] = ones
            vwt_ref[0, h, i, 0:HEAD_DIM, :] = zt[640 + h * 64:640 + (h + 1) * 64, cols].astype(BF16)
            vwt_ref[0, h, i, HEAD_DIM:V_ROWS, :] = ones
    gt_ref[0] = jax.nn.sigmoid(zt[768:800, :])


def _proj_call(x, wn, wt, wc):
    bsz, s, d = x.shape
    tm = PROJ_TM
    nblk = s // LANES
    grid = (bsz, s // tm)
    ka = LANES + -(-(s // SEL_BLOCK) // LANES) * LANES
    out_shape = (
        jax.ShapeDtypeStruct((bsz, s, CONV_CH), BF16),
        jax.ShapeDtypeStruct((bsz, 2, s, LANES), F32),
        jax.ShapeDtypeStruct((bsz, s, ka), BF16),
        jax.ShapeDtypeStruct((bsz, s, LANES), BF16),
        jax.ShapeDtypeStruct((bsz, 512, s), BF16),
        jax.ShapeDtypeStruct((bsz, N_KV, nblk, V_ROWS, LANES), BF16),
        jax.ShapeDtypeStruct((bsz, N_KV, nblk, V_ROWS, LANES), BF16),
        jax.ShapeDtypeStruct((bsz, N_KV * GATE_ROWS, s), F32),
    )
    vspec = pl.BlockSpec((1, N_KV, tm // LANES, V_ROWS, LANES), lambda b, i: (b, 0, i, 0, 0))
    return pl.pallas_call(
        _proj_kernel,
        grid=grid,
        in_specs=[
            pl.BlockSpec((1, tm, d), lambda b, i: (b, i, 0)),
            pl.BlockSpec(wn.shape, lambda b, i: (0, 0)),
            pl.BlockSpec(wt.shape, lambda b, i: (0, 0)),
            pl.BlockSpec(wc.shape, lambda b, i: (0, 0)),
        ],
        out_specs=(
            pl.BlockSpec((1, tm, CONV_CH), lambda b, i: (b, i, 0)),
            pl.BlockSpec((1, 2, tm, LANES), lambda b, i: (b, 0, i, 0)),
            pl.BlockSpec((1, tm, ka), lambda b, i: (b, i, 0)),
            pl.BlockSpec((1, tm, LANES), lambda b, i: (b, i, 0)),
            pl.BlockSpec((1, 512, tm), lambda b, i: (b, 0, i)),
            vspec, vspec,
            pl.BlockSpec((1, N_KV * GATE_ROWS, tm), lambda b, i: (b, 0, i)),
        ),
        out_shape=out_shape,
        scratch_shapes=[pltpu.VMEM((SUBLANES, CONV_CH), F32)],
        compiler_params=_cparams(("arbitrary", "arbitrary")),
        name="in_proj_conv",
    )(x, wn, wt, wc)


def _compress_kernel(g_ref, pe_ref, w1_ref, w2_ref, o_ref):
    pe = pe_ref[0]
    n = o_ref.shape[2]
    width = g_ref.shape[3]
    half = CMP_STRIDE * width
    a_lo = jnp.zeros((n, w1_ref.shape[2]), F32)
    a_hi = jnp.zeros((n, w1_ref.shape[2]), F32)
    for t in range(CMP_STRIDE):
        cols = slice(t * width, (t + 1) * width)
        x = g_ref[0, 0, pl.ds(t, n, stride=CMP_STRIDE), :]
        a_lo = a_lo + jnp.dot((x + pe[0:1, cols]).astype(BF16), w1_ref[0, cols, :],
                              preferred_element_type=F32)
        a_hi = a_hi + jnp.dot((x + pe[1:2, cols]).astype(BF16), w1_ref[0, half + t * width:half + (t + 1) * width, :],
                              preferred_element_type=F32)
    hid = a_lo + pltpu.roll(a_hi, n - 1, 0)
    act = jax.nn.gelu(hid)
    o_ref[0, 0] = jnp.dot(act.astype(BF16), w2_ref[0], preferred_element_type=F32)


def _compress_call(ckv, pes, w1s, w2s):
    bsz, _, s, width = ckv.shape
    n = s // CMP_STRIDE
    half = CMP_STRIDE * width
    return pl.pallas_call(
        _compress_kernel,
        grid=(bsz, 2),
        in_specs=[
            pl.BlockSpec((1, 1, s, width), lambda b, k: (b, k, 0, 0)),
            pl.BlockSpec((1, 2, half), lambda b, k: (k, 0, 0)),
            pl.BlockSpec((1, 2 * half, N_KV * CMP_HIDDEN), lambda b, k: (k, 0, 0)),
            pl.BlockSpec((1, N_KV * CMP_HIDDEN, N_KV * HEAD_DIM), lambda b, k: (k, 0, 0)),
        ],
        out_specs=pl.BlockSpec((1, 1, n, N_KV * HEAD_DIM), lambda b, k: (b, k, 0, 0)),
        out_shape=jax.ShapeDtypeStruct((bsz, 2, n, N_KV * HEAD_DIM), F32),
        compiler_params=_cparams(("arbitrary", "arbitrary")),
        name="compress_mlp",
    )(ckv, pes, w1s, w2s)


def _attn_kernel(n_sel, n_top,
                 q_ref, g_ref, kc_ref, vct_ref, ksa_ref, vst_ref, kw_ref, vwt_ref,
                 o_ref, qa_ref, ps_ref, oc_ref, s_ref):
    qb = pl.program_id(2)
    q0 = qb * Q_TILE
    nq = GQA * Q_TILE
    blk = q_ref[0]
    q4 = jnp.concatenate([blk[g * HEAD_DIM:(g + 1) * HEAD_DIM, :] for g in range(GQA)], axis=1)
    lane = lax.broadcasted_iota(jnp.int32, (1, nq), 1)
    t4 = q0 + (lane & (Q_TILE - 1))
    kvh = pl.program_id(1)
    qa_ref[pl.ds(pl.multiple_of(kvh * HEAD_DIM, HEAD_DIM), HEAD_DIM), :] = q4
    qa_ref[pl.ds(pl.multiple_of((N_KV - 1 - kvh) * HEAD_DIM, HEAD_DIM), HEAD_DIM), :] = jnp.zeros_like(q4)
    q2 = qa_ref[0:N_KV * HEAD_DIM, :]

    n_cmp = kc_ref.shape[1]
    chunk = min(CMP_CHUNK, n_cmp)
    q_lane_tiles = Q_TILE // LANES
    for h in range(q_lane_tiles):
        ps_ref[h, 0:SUBLANES, :] = jnp.zeros((SUBLANES, LANES), F32)

    def cmp_branch(n):
        sc = jnp.dot(kc_ref[0, 0:n, :], q2, preferred_element_type=F32)
        last_c = jnp.right_shift(t4 - (CMP_BLOCK - 1), CMP_STRIDE.bit_length() - 1)
        scm = jnp.where(lax.broadcasted_iota(jnp.int32, (n, nq), 0) <= last_c, sc, NEG_INF)
        m_c = jnp.max(scm, axis=0, keepdims=True)
        e_c = jnp.exp2(scm - m_c)
        l_c = jnp.sum(e_c, axis=0, keepdims=True)
        p_c = e_c * jnp.where(last_c >= 0, 1.0 / l_c, 0.0)
        oc_ref[...] = jnp.dot(vct_ref[0, :, 0:n], p_c.astype(BF16), preferred_element_type=F32)
        psum = (p_c[:, 0:Q_TILE] + p_c[:, Q_TILE:2 * Q_TILE]
                + p_c[:, 2 * Q_TILE:3 * Q_TILE] + p_c[:, 3 * Q_TILE:4 * Q_TILE])
        for h in range(q_lane_tiles):
            ps_ref[h, SUBLANES:SUBLANES + n, :] = psum[:, h * LANES:(h + 1) * LANES]
            if n < n_cmp:
                ps_ref[h, SUBLANES + n:, :] = jnp.zeros((n_cmp - n, LANES), F32)

    last_visible = (q0 + Q_TILE - CMP_BLOCK) // CMP_STRIDE
    live_chunks = last_visible // chunk + 1
    for k in range(1, n_cmp // chunk + 1):
        pl.when(live_chunks == k)(functools.partial(cmp_branch, k * chunk))
    o_cmp = oc_ref[...]

    sw = jnp.dot(kw_ref[0, pl.ds(pl.multiple_of(q0, Q_TILE), WIN_KEYS), :], q2,
                 preferred_element_type=F32)
    ql = t4 - q0
    row = lax.broadcasted_iota(jnp.int32, (LANES, nq), 0)
    slabs = []
    for c in range(WIN_KEYS // LANES):
        slab = sw[c * LANES:(c + 1) * LANES, :]
        before_start = WINDOW - 1 - q0 - c * LANES
        if c * LANES < Q_TILE:
            slab = jnp.where(row > jnp.maximum(ql - c * LANES, before_start), slab, NEG_INF)
        elif (c + 1) * LANES > WINDOW:
            slab = jnp.where(row <= ql + (WINDOW - c * LANES), slab, NEG_INF)
        else:
            slab = jnp.where(row > before_start, slab, NEG_INF)
        slabs.append(slab)
    swm = jnp.concatenate(slabs, axis=0)
    m_w = jnp.max(swm, axis=0, keepdims=True)
    p_w = jnp.exp2(swm - m_w)
    wblk = q0 // LANES
    vwt = jnp.concatenate([vwt_ref[0, 0, wblk + i] for i in range(WIN_KEYS // LANES)], axis=1)
    acc_w = jnp.dot(vwt, p_w.astype(BF16), preferred_element_type=F32)
    o_win = acc_w[0:HEAD_DIM, :] * (1.0 / acc_w[HEAD_DIM:HEAD_DIM + 1, :])

    per_sel = SEL_BLOCK // CMP_STRIDE
    def importance(h):
        acc = ps_ref[h, pl.ds(SUBLANES - 1, n_sel, stride=per_sel), :]
        for off in range(CMP_BLOCK // CMP_STRIDE + per_sel - 2):
            acc = acc + ps_ref[h, pl.ds(SUBLANES + off, n_sel, stride=per_sel), :]
        return acc

    imp = jnp.concatenate([importance(h) for h in range(q_lane_tiles)], axis=1)

    jidx = lax.broadcasted_iota(jnp.int32, (n_sel, Q_TILE), 0)
    tq = q0 + lax.broadcasted_iota(jnp.int32, (n_sel, Q_TILE), 1)
    jt = jnp.right_shift(tq, SEL_BLOCK.bit_length() - 1)
    forced = (jidx == 0) | (jidx == jt) | (jidx == jt - 1)
    score = jnp.where(forced, FORCE_SCORE, imp)
    key = jnp.where(jidx > jt, -1, lax.bitcast_convert_type(score, jnp.int32))
    theta = jnp.zeros((1, Q_TILE), jnp.int32)
    for bit in range(30, -1, -1):
        cand = theta | (1 << bit)
        reach = jnp.sum((key >= cand).astype(jnp.int32), axis=0, keepdims=True)
        theta = jnp.where(reach >= n_top, cand, theta)
    above = key > theta
    tied = key == theta
    n_above = jnp.sum(above.astype(jnp.int32), axis=0, keepdims=True)
    lower = (lax.broadcasted_iota(jnp.int32, (n_sel, n_sel), 1)
             < lax.broadcasted_iota(jnp.int32, (n_sel, n_sel), 0)).astype(BF16)
    tied_before = jnp.dot(lower, tied.astype(BF16), preferred_element_type=F32)
    chosen = above | (tied & (tied_before < (n_top - n_above).astype(F32)))
    bias = jnp.where(chosen, 0.0, NEG_INF).astype(BF16)
    bias_row0 = N_KV * HEAD_DIM
    qa_ref[bias_row0:bias_row0 + n_sel, :] = jnp.concatenate([bias] * GQA, axis=1)
    if qa_ref.shape[0] > bias_row0 + n_sel:
        qa_ref[bias_row0 + n_sel:, :] = jnp.zeros((qa_ref.shape[0] - bias_row0 - n_sel, nq), BF16)

    nsub = SEL_TK // LANES

    def scores(c, buf):
        k0 = pl.multiple_of(c * SEL_TK, SEL_TK)
        s = jnp.dot(ksa_ref[0, pl.ds(k0, SEL_TK), :], qa_ref[...], preferred_element_type=F32)
        s_ref[buf] = s
        return jnp.max(s, axis=0, keepdims=True)

    def accumulate(c, buf, mx, m, acc, causal):
        s = s_ref[buf]
        if causal:
            kpos = c * SEL_TK + lax.broadcasted_iota(jnp.int32, (SEL_TK, nq), 0)
            s = jnp.where(kpos <= t4, s, NEG_INF)
            mx = jnp.max(s, axis=0, keepdims=True)
        m_new = jnp.maximum(m, mx)
        p = jnp.exp2(s - m_new)
        vt = jnp.concatenate([vst_ref[0, 0, c * nsub + i] for i in range(nsub)], axis=1)
        acc = jnp.exp2(m - m_new) * acc + jnp.dot(vt, p.astype(BF16), preferred_element_type=F32)
        return m_new, acc

    def pair(c, carry):
        mx0, m, acc = carry
        mx1 = scores(c + 1, 1)
        m, acc = accumulate(c, 0, mx0, m, acc, False)
        mx0 = scores(c + 2, 0)
        m, acc = accumulate(c + 1, 1, mx1, m, acc, False)
        return mx0, m, acc

    n_full = qb // (SEL_TK // Q_TILE)
    n_quads = n_full // 4
    n_pairs = (n_full - 4 * n_quads) // 2
    carry = (scores(0, 0), jnp.full((1, nq), NEG_INF, F32), jnp.zeros((V_ROWS, nq), F32))
    carry = lax.fori_loop(0, n_quads, lambda i, cr: pair(4 * i + 2, pair(4 * i, cr)), carry)
    mx0, m_s, acc_s = lax.fori_loop(0, n_pairs, lambda i, cr: pair(4 * n_quads + 2 * i, cr), carry)
    c_last = 4 * n_quads + 2 * n_pairs

    def leftover_then_own():
        mx1 = scores(c_last + 1, 1)
        m1, acc1 = accumulate(c_last, 0, mx0, m_s, acc_s, False)
        return accumulate(c_last + 1, 1, mx1, m1, acc1, True)[1]

    def own_only():
        return accumulate(c_last, 0, mx0, m_s, acc_s, True)[1]

    acc_s = lax.cond(n_full % 2 == 1, leftover_then_own, own_only)
    o_sel = acc_s[0:HEAD_DIM, :] * (1.0 / acc_s[HEAD_DIM:HEAD_DIM + 1, :])

    gt = g_ref[0]

    def gate(br):
        return jnp.concatenate([gt[br * GQA + g:br * GQA + g + 1, :] for g in range(GQA)], axis=1)

    ot = gate(0) * o_cmp + gate(1) * o_sel + gate(2) * o_win
    stacked = jnp.concatenate([ot[:, g * Q_TILE:(g + 1) * Q_TILE] for g in range(GQA)], axis=0)
    o_ref[0] = stacked.T.astype(BF16)


def _attn_call(qt, gt, kcmp, vct, ksa, vst, kw, vwt):
    bsz, _, s = qt.shape
    n_sel = s // SEL_BLOCK
    n_top = min(N_SEL, n_sel)
    n_cmp = kcmp.shape[1]
    ka = ksa.shape[2]
    nblk = s // LANES
    grid = (bsz, N_KV, s // Q_TILE)
    kern = functools.partial(_attn_kernel, n_sel, n_top)
    return pl.pallas_call(
        kern,
        grid=grid,
        in_specs=[
            pl.BlockSpec((1, GQA * HEAD_DIM, Q_TILE), lambda b, h, i: (b, h, i)),
            pl.BlockSpec((1, GATE_ROWS, Q_TILE), lambda b, h, i: (b, h, i)),
            pl.BlockSpec((1, n_cmp, N_KV * HEAD_DIM), lambda b, h, i: (b, 0, 0)),
            pl.BlockSpec((1, HEAD_DIM, n_cmp), lambda b, h, i: (b, h, 0)),
            pl.BlockSpec((1, s, ka), lambda b, h, i: (b, 0, 0)),
            pl.BlockSpec((1, 1, nblk, V_ROWS, LANES), lambda b, h, i: (b, h, 0, 0, 0)),
            pl.BlockSpec((1, s + WINDOW, N_KV * HEAD_DIM), lambda b, h, i: (b, 0, 0)),
            pl.BlockSpec((1, 1, nblk + WINDOW // LANES, V_ROWS, LANES), lambda b, h, i: (b, h, 0, 0, 0)),
        ],
        out_specs=pl.BlockSpec((1, Q_TILE, GQA * HEAD_DIM), lambda b, h, i: (b, i, h)),
        out_shape=jax.ShapeDtypeStruct((bsz, s, N_HEADS * HEAD_DIM), BF16),
        scratch_shapes=[pltpu.VMEM((ka, GQA * Q_TILE), BF16),
                        pltpu.VMEM((Q_TILE // LANES, n_cmp + SUBLANES, LANES), F32),
                        pltpu.VMEM((HEAD_DIM, GQA * Q_TILE), F32),
                        pltpu.VMEM((2, SEL_TK, GQA * Q_TILE), F32)],
        compiler_params=_cparams(("arbitrary", "arbitrary", "arbitrary")),
        name="nsa_attention",
    )(qt, gt, kcmp, vct, ksa, vst, kw, vwt)


def _post_kernel(x_ref, yc_ref, yn_ref, wo_ref, g1_ref, b1_ref, wr_ref, br_ref, tril_ref,
                 h1_ref, hp_ref, ri_ref, rw_ref, cnt_ref):
    step = pl.program_id(0)
    half = yc_ref.shape[1]
    sub = tril_ref.shape[0]

    @pl.when(step == 0)
    def _():
        cnt_ref[...] = jnp.zeros_like(cnt_ref)

    def route_rows(r0, cnt):
        rows = slice(r0, r0 + sub)
        mix = (jnp.dot(yc_ref[rows, :], wo_ref[0:half, :], preferred_element_type=F32)
               + jnp.dot(yn_ref[rows, :], wo_ref[half:2 * half, :], preferred_element_type=F32))
        h1 = _layer_norm(ALPHA * x_ref[rows, :] + mix, g1_ref[...], b1_ref[...])
        h1_ref[rows, :] = h1

        h_hi = h1.astype(BF16)
        h_hi32 = h_hi.astype(F32)
        h_lo = (h1 - h_hi32).astype(BF16)
        bits = lax.bitcast_convert_type(h_hi32, jnp.uint32)
        dh = bits.shape[1] // 2
        hp_ref[rows, :] = jnp.right_shift(bits[:, 0:dh], jnp.uint32(16)) | bits[:, dh:2 * dh]
        parts = jnp.dot(jnp.concatenate([h_hi, h_lo], axis=1), wr_ref[...], preferred_element_type=F32)
        logits = parts[:, 0:LANES] + parts[:, LANES:2 * LANES] + br_ref[...]
        lane = lax.broadcasted_iota(jnp.int32, (sub, LANES), 1)
        big = jnp.int32(LANES)

        def masked_softmax(valid):
            lg = jnp.where(valid, logits, NEG_INF)
            mx = jnp.max(lg, axis=1, keepdims=True)
            ex = jnp.where(valid, jnp.exp(lg - mx), 0.0)
            return ex / jnp.sum(ex, axis=1, keepdims=True)

        def first_max(vals, valid):
            top = jnp.max(jnp.where(valid, vals, -1.0), axis=1, keepdims=True)
            idx = jnp.min(jnp.where(valid & (vals == top), lane, big), axis=1, keepdims=True)
            return top, idx

        is_group = lane < N_GROUPS
        gp, gsel = first_max(masked_softmax(is_group), is_group)
        lo = ROUTE_LANE0 + EXP_PER_GROUP * gsel
        in_group = (lane >= lo) & (lane < lo + EXP_PER_GROUP)
        eprob = masked_softmax(in_group)
        p1, i1 = first_max(eprob, in_group)
        p2, i2 = first_max(eprob, in_group & (lane != i1))
        den = p1 + p2
        w1 = gp * (p1 / den)
        w2 = gp * (p2 / den)

        onehot = (lane == i1) | (lane == i2)
        before = jnp.dot(tril_ref[...], onehot.astype(BF16), preferred_element_type=F32)
        rk = before + cnt
        r1 = jnp.sum(jnp.where(lane == i1, rk, 0.0), axis=1, keepdims=True)
        r2 = jnp.sum(jnp.where(lane == i2, rk, 0.0), axis=1, keepdims=True)

        col = lax.broadcasted_iota(jnp.int32, (sub, ri_ref.shape[1]), 1)
        ri_ref[rows, :] = jnp.where(col == 0, i1 - ROUTE_LANE0,
                                    jnp.where(col == 1, i2 - ROUTE_LANE0,
                                              jnp.where(col == 2, r1.astype(jnp.int32),
                                                        jnp.where(col == 3, r2.astype(jnp.int32), 0))))
        rw_ref[rows, :] = jnp.where(col == 0, w1, jnp.where(col == 1, w2, 0.0))
        return cnt + jnp.sum(onehot.astype(F32), axis=0, keepdims=True)

    cnt = cnt_ref[0:1, :]
    for r0 in range(0, x_ref.shape[0], sub):
        cnt = route_rows(r0, cnt)
    cnt_ref[...] = jnp.broadcast_to(cnt, cnt_ref.shape)


def _post_call(x2, yconv, ynsa, wo, g1, b1, wr, br, tril):
    n_tok, d = x2.shape
    tm = POST_TM
    const = lambda a: pl.BlockSpec(a.shape, lambda i: (0, 0))
    return pl.pallas_call(
        _post_kernel,
        grid=(n_tok // tm,),
        in_specs=[
            pl.BlockSpec((tm, d), lambda i: (i, 0)),
            pl.BlockSpec((tm, yconv.shape[1]), lambda i: (i, 0)),
            pl.BlockSpec((tm, ynsa.shape[1]), lambda i: (i, 0)),
            const(wo), const(g1), const(b1), const(wr), const(br), const(tril),
        ],
        out_specs=(
            pl.BlockSpec((tm, d), lambda i: (i, 0)),
            pl.BlockSpec((tm, d // 2), lambda i: (i, 0)),
            pl.BlockSpec((tm, SUBLANES), lambda i: (i, 0)),
            pl.BlockSpec((tm, SUBLANES), lambda i: (i, 0)),
            pl.BlockSpec((SUBLANES, LANES), lambda i: (0, 0)),
        ),
        out_shape=(
            jax.ShapeDtypeStruct((n_tok, d), F32),
            jax.ShapeDtypeStruct((n_tok, d // 2), jnp.uint32),
            jax.ShapeDtypeStruct((n_tok, SUBLANES), jnp.int32),
            jax.ShapeDtypeStruct((n_tok, SUBLANES), F32),
            jax.ShapeDtypeStruct((SUBLANES, LANES), F32),
        ),
        compiler_params=_cparams(("arbitrary",)),
        name="out_proj_ln1_router",
    )(x2, yconv, ynsa, wo, g1, b1, wr, br, tril)


def _row_copy(src_ref, src_row, dst_ref, dst_row, sem):
    return pltpu.make_async_copy(src_ref.at[pl.ds(src_row, 1), :], dst_ref.at[pl.ds(dst_row, 1), :], sem)


ROUTE_COLS = SUBLANES


def _sorted_row(pstart_ref, route_ref, r, slot):
    return pstart_ref[route_ref[ROUTE_COLS * r + slot]] + route_ref[ROUTE_COLS * r + 2 + slot]


def _dispatch_kernel(pends_ref, nu_ref, pstart_ref, route_ref, h_ref, xs_ref, zero_ref, sem, zsem):
    tm = h_ref.shape[0]
    n_chunk = xs_ref.shape[0] // MOE_CHUNK

    @pl.when(pl.program_id(0) == 0)
    def _():
        zero_ref[...] = jnp.zeros_like(zero_ref)

        def zero_chunk(row0):
            row0 = pl.multiple_of(row0, MOE_CHUNK)
            return pltpu.make_async_copy(zero_ref, xs_ref.at[pl.ds(row0, MOE_CHUNK), :], zsem)

        def each_padded_chunk(act):
            def per_expert(e, _):
                end = pends_ref[e]
                start = jnp.where(e > 0, pends_ref[jnp.maximum(e - 1, 0)], 0)

                @pl.when(end > start)
                def _():
                    act(zero_chunk(end - MOE_CHUNK))
                return 0

            lax.fori_loop(0, N_EXPERTS, per_expert, 0)

            def per_dead_chunk(c, _):
                act(zero_chunk(c * MOE_CHUNK))
                return 0

            lax.fori_loop(nu_ref[0], n_chunk, per_dead_chunk, 0)

        each_padded_chunk(lambda cp: cp.start())
        each_padded_chunk(lambda cp: cp.wait())

    def issue(r, _):
        for slot in range(2):
            _row_copy(h_ref, r, xs_ref, _sorted_row(pstart_ref, route_ref, r, slot), sem).start()
        return 0

    lax.fori_loop(0, tm, issue, 0, unroll=ROW_UNROLL)
    for slot in range(2):
        pltpu.make_async_copy(h_ref, xs_ref.at[pl.ds(0, tm), :], sem).wait()


def _dispatch_call(pends, n_used, pstarts, route, h1, n_rows):
    n_tok, d = h1.shape
    tm = ROW_TM
    grid_spec = pltpu.PrefetchScalarGridSpec(
        num_scalar_prefetch=3,
        grid=(n_tok // tm,),
        in_specs=[
            pl.BlockSpec((ROUTE_COLS * tm,), lambda i, pe, nu, ps: (i,), memory_space=pltpu.SMEM),
            pl.BlockSpec((tm, d), lambda i, pe, nu, ps: (i, 0)),
        ],
        out_specs=pl.BlockSpec(memory_space=pl.ANY),
        scratch_shapes=[pltpu.VMEM((MOE_CHUNK, d), h1.dtype), pltpu.SemaphoreType.DMA, pltpu.SemaphoreType.DMA],
    )
    return pl.pallas_call(
        _dispatch_kernel,
        grid_spec=grid_spec,
        out_shape=jax.ShapeDtypeStruct((n_rows, d), h1.dtype),
        compiler_params=_cparams(("arbitrary",)),
        name="moe_dispatch",
    )(pends, n_used, pstarts, route, h1)


def _expert_kernel(ce_ref, nu_ref, xs_ref, wi_ref, wo_ref, o_ref, wib_ref, wob_ref):
    c = pl.program_id(0)
    live = c < nu_ref[0]

    @pl.when(live & ((c == 0) | (ce_ref[c] != ce_ref[jnp.maximum(c - 1, 0)])))
    def _():
        wib_ref[...] = wi_ref[0].astype(BF16)
        wob_ref[...] = wo_ref[0].astype(BF16)

    @pl.when(live)
    def _():
        pk = xs_ref[...]
        dh = pk.shape[1]
        x_lo = lax.bitcast_convert_type(jnp.left_shift(pk, jnp.uint32(16)), F32).astype(BF16)
        x_hi = lax.bitcast_convert_type(pk & jnp.uint32(0xFFFF0000), F32).astype(BF16)
        gu = (jnp.dot(x_lo, wib_ref[0:dh, :], preferred_element_type=F32)
              + jnp.dot(x_hi, wib_ref[dh:2 * dh, :], preferred_element_type=F32))
        gate = gu[:, 0:D_EXPERT]
        act = gate * jax.nn.sigmoid(gate) * gu[:, D_EXPERT:2 * D_EXPERT]
        o_ref[...] = jnp.dot(act.astype(BF16), wob_ref[...], preferred_element_type=F32)

    @pl.when(jnp.logical_not(live))
    def _():
        o_ref[...] = jnp.zeros_like(o_ref)


def _expert_call(chunk_exp, n_used, xs, wi, wo):
    n_rows, dh = xs.shape
    d = wo.shape[2]
    n_chunk = n_rows // MOE_CHUNK

    def live(c, nu):
        return jnp.maximum(jnp.minimum(c, nu[0] - 1), 0)

    grid_spec = pltpu.PrefetchScalarGridSpec(
        num_scalar_prefetch=2,
        grid=(n_chunk,),
        in_specs=[
            pl.BlockSpec((MOE_CHUNK, dh), lambda c, ce, nu: (live(c, nu), 0)),
            pl.BlockSpec((1,) + wi.shape[1:], lambda c, ce, nu: (ce[live(c, nu)], 0, 0)),
            pl.BlockSpec((1,) + wo.shape[1:], lambda c, ce, nu: (ce[live(c, nu)], 0, 0)),
        ],
        out_specs=pl.BlockSpec((MOE_CHUNK, d), lambda c, ce, nu: (c, 0)),
        scratch_shapes=[pltpu.VMEM(wi.shape[1:], BF16), pltpu.VMEM(wo.shape[1:], BF16)],
    )
    return pl.pallas_call(
        _expert_kernel,
        grid_spec=grid_spec,
        out_shape=jax.ShapeDtypeStruct((n_rows, d), F32),
        compiler_params=_cparams(("arbitrary",)),
        name="moe_experts",
    )(chunk_exp, n_used, xs, wi, wo)


def _combine_kernel(pstart_ref, dest_ref, dest_next_ref, h1_ref, rw_ref, p_ref, ys_ref,
                    g2_ref, b2_ref, wp_ref, wg_ref, bg_ref, g3_ref, b3_ref,
                    o_ref, rows_ref, sems):
    tm = h1_ref.shape[0]
    step = pl.program_id(0)
    cur = step % 2

    def gather(idx_ref, buf):
        def issue(r, _):
            for slot in range(2):
                _row_copy(ys_ref, _sorted_row(pstart_ref, idx_ref, r, slot), rows_ref.at[buf, slot], r,
                          sems.at[buf]).start()
            return 0

        lax.fori_loop(0, tm, issue, 0, unroll=ROW_UNROLL)

    def await_rows(buf):
        for slot in range(2):
            pltpu.make_async_copy(ys_ref.at[pl.ds(0, tm), :], rows_ref.at[buf, slot], sems.at[buf]).wait()

    @pl.when(step == 0)
    def _():
        gather(dest_ref, 0)

    await_rows(cur)
    rw = rw_ref[...]
    ffn = rw[:, 0:1] * rows_ref[cur, 0] + rw[:, 1:2] * rows_ref[cur, 1]
    gather(dest_next_ref, 1 - cur)
    h2 =_layer_norm(ALPHA * h1_ref[...] + ffn, g2_ref[...], b2_ref[...])
    emb = jnp.dot(p_ref[...].astype(BF16), wp_ref[...], preferred_element_type=F32)
    gate = jax.nn.sigmoid(jnp.dot(h2.astype(BF16), wg_ref[...], preferred_element_type=F32) + bg_ref[...])
    o_ref[...] = _layer_norm(ALPHA * h2 + emb * gate, g3_ref[...], b3_ref[...])

    @pl.when(step == pl.num_programs(0) - 1)
    def _():
        await_rows(1 - cur)


def _combine_call(pstarts, route, h1, rw, p2, ys, g2, b2, wp, wg, bg, g3, b3):
    n_tok, d = h1.shape
    tm = ROW_TM
    const = lambda a: pl.BlockSpec(a.shape, lambda i: (0, 0))
    return pl.pallas_call(
        _combine_kernel,
        grid=(n_tok // tm,),
        in_specs=[
            pl.BlockSpec(memory_space=pltpu.SMEM),
            pl.BlockSpec((ROUTE_COLS * tm,), lambda i: (i,), memory_space=pltpu.SMEM),
            pl.BlockSpec((ROUTE_COLS * tm,), lambda i: (jnp.minimum(i + 1, n_tok // tm - 1),),
                         memory_space=pltpu.SMEM),
            pl.BlockSpec((tm, d), lambda i: (i, 0)),
            pl.BlockSpec((tm, rw.shape[1]), lambda i: (i, 0)),
            pl.BlockSpec((tm, p2.shape[1]), lambda i: (i, 0)),
            pl.BlockSpec(memory_space=pl.ANY),
            const(g2), const(b2), const(wp), const(wg), const(bg), const(g3), const(b3),
        ],
        out_specs=pl.BlockSpec((tm, d), lambda i: (i, 0)),
        out_shape=jax.ShapeDtypeStruct((n_tok, d), F32),
        scratch_shapes=[pltpu.VMEM((2, 2, tm, d), F32), pltpu.SemaphoreType.DMA((2,))],
        compiler_params=_cparams(("arbitrary",)),
        name="moe_combine_ln_ple",
    )(pstarts, route, route, h1, rw, p2, ys, g2, b2, wp, wg, bg, g3, b3)


def _gate_columns():
    cols = np.zeros((N_KV, GATE_ROWS), np.int32)
    live = np.zeros((N_KV, GATE_ROWS), np.float32)
    for h in range(N_KV):
        for br in range(3):
            for g in range(GQA):
                cols[h, br * GQA + g] = (h * GQA + g) * 3 + br
                live[h, br * GQA + g] = 1.0
    return cols.reshape(-1), live.reshape(-1)


def _layer(x, p, w_in, w_conv, pe_ck, w1_ck, w2_ck, pe_cv, w1_cv, w2_cv, w_out, ln1_g, ln1_b,
           w_rg, b_rg, w_re, b_re, w_e_in, w_e_out, ln2_g, ln2_b, w_ple, w_ple_gate, b_ple_gate,
           ln3_g, ln3_b):
    bsz, s, d = x.shape
    n_tok = bsz * s
    row = lambda v: v.reshape(1, -1)

    c_q, c_kc, c_vc, c_ks, c_vs, c_kw, c_vw, c_g = 1536, 2048, 2176, 2304, 2432, 2560, 2688, 2816
    wn = jnp.concatenate([w_in[:, 0:c_q], w_in[:, c_kc:c_ks], w_in[:, c_ks:c_vs], w_in[:, c_kw:c_vw]],
                         axis=1).astype(BF16)
    gcols, glive = _gate_columns()
    w_gate = w_in[:, c_g:c_g + 3 * N_HEADS][:, gcols] * glive[None, :]
    wt = jnp.concatenate([w_in[:, c_q:c_kc] * Q_SCALE, w_in[:, c_vs:c_kw], w_in[:, c_vw:c_g], w_gate],
                         axis=1).T.astype(BF16)

    yconv, cmpkv, ksa, kw, qt, vst, vwt, gt = _proj_call(x, wn, wt, w_conv)

    half_blk = CMP_BLOCK // 2
    eye = jnp.eye(N_KV, dtype=F32)

    def per_head_w1(w1):
        w = w1.reshape(2, half_blk, HEAD_DIM, CMP_HIDDEN)
        w = jnp.einsum('ptdc,hg->pthdgc', w, eye)
        return w.reshape(2 * half_blk * N_KV * HEAD_DIM, N_KV * CMP_HIDDEN)

    def per_head_w2(w2):
        return jnp.einsum('cd,hg->hcgd', w2, eye).reshape(N_KV * CMP_HIDDEN, N_KV * HEAD_DIM)

    def per_head_pe(pe):
        return jnp.broadcast_to(pe.reshape(2, half_blk, 1, HEAD_DIM),
                                (2, half_blk, N_KV, HEAD_DIM)).reshape(2, half_blk * N_KV * HEAD_DIM)

    pes = jnp.stack([per_head_pe(pe_ck), per_head_pe(pe_cv)])
    w1s = jnp.stack([per_head_w1(w1_ck), per_head_w1(w1_cv)]).astype(BF16)
    w2s = jnp.stack([per_head_w2(w2_ck), per_head_w2(w2_cv)]).astype(BF16)
    cmp_out = _compress_call(cmpkv, pes, w1s, w2s)
    kcmp = cmp_out[:, 0].astype(BF16)
    vct = cmp_out[:, 1].transpose(0, 2, 1).astype(BF16)

    kw = jnp.pad(kw, ((0, 0), (WINDOW, 0), (0, 0)))
    vwt = jnp.pad(vwt, ((0, 0), (0, 0), (WINDOW // LANES, 0), (0, 0), (0, 0)))
    ynsa = _attn_call(qt, gt, kcmp, vct, ksa, vst, kw, vwt)

    wr = jnp.zeros((d, LANES), F32).at[:, 0:N_GROUPS].set(w_rg)
    wr = wr.at[:, ROUTE_LANE0:ROUTE_LANE0 + N_EXPERTS].set(w_re)
    br = jnp.zeros((1, LANES), F32).at[0, 0:N_GROUPS].set(b_rg)
    br = br.at[0, ROUTE_LANE0:ROUTE_LANE0 + N_EXPERTS].set(b_re)
    wr_hi = wr.astype(BF16)
    wr_lo = (wr - wr_hi.astype(F32)).astype(BF16)
    wr = jnp.concatenate([jnp.concatenate([wr_hi, wr_lo], axis=1),
                          jnp.concatenate([wr_hi, jnp.zeros_like(wr_lo)], axis=1)], axis=0)
    tril = jnp.asarray(np.tril(np.ones((POST_SUB, POST_SUB), np.float32), -1), BF16)
    h1, h1_packed, ri, rw, cnt = _post_call(x.reshape(n_tok, d), yconv.reshape(n_tok, -1), ynsa.reshape(n_tok, -1),
                                 w_out.astype(BF16), row(ln1_g), row(ln1_b), wr, br, tril)

    counts = cnt[0, ROUTE_LANE0:ROUTE_LANE0 + N_EXPERTS].astype(jnp.int32)
    pcounts = (counts + MOE_CHUNK - 1) // MOE_CHUNK * MOE_CHUNK
    pends = jnp.cumsum(pcounts)
    pstarts = (pends - pcounts).astype(jnp.int32)
    route = ri.reshape(-1)
    n_asg = n_tok * 2
    n_chunk = -(-n_asg // MOE_CHUNK) + N_EXPERTS
    chunk_row0 = jnp.arange(n_chunk, dtype=jnp.int32) * MOE_CHUNK
    chunk_exp = jnp.minimum(jnp.sum(pends[None, :] <= chunk_row0[:, None], axis=1), N_EXPERTS - 1).astype(jnp.int32)
    n_used = (pends[-1:] // MOE_CHUNK).astype(jnp.int32)

    xs = _dispatch_call(pends.astype(jnp.int32), n_used, pstarts, route, h1_packed, n_chunk * MOE_CHUNK)
    ys = _expert_call(chunk_exp, n_used, xs, w_e_in, w_e_out)
    out = _combine_call(pstarts, route, h1, rw, p.reshape(n_tok, -1), ys, row(ln2_g), row(ln2_b),
                        w_ple.astype(BF16), w_ple_gate.astype(BF16), row(b_ple_gate), row(ln3_g), row(ln3_b))
    return out.reshape(bsz, s, d)


def kernel(x, p, w_in, w_conv, pe_ck, w1_ck, w2_ck, pe_cv, w1_cv, w2_cv, w_out, ln1_g, ln1_b, w_rg, b_rg, w_re, b_re, w_e_in, w_e_out, ln2_g, ln2_b, w_ple, w_ple_gate, b_ple_gate, ln3_g, ln3_b):
    assert w_in.shape[0] == DEPTH, "residual scaling ALPHA is derived from DEPTH"
    h = x
    for i in range(DEPTH):
        h = _layer(h, p[i], w_in[i], w_conv[i], pe_ck[i], w1_ck[i], w2_ck[i], pe_cv[i], w1_cv[i], w2_cv[i],
                   w_out[i], ln1_g[i], ln1_b[i], w_rg[i], b_rg[i], w_re[i], b_re[i], w_e_in[i], w_e_out[i],
                   ln2_g[i], ln2_b[i], w_ple[i], w_ple_gate[i], b_ple_gate[i], ln3_g[i], ln3_b[i])
    return h
```

```python
import functools

import jax
import jax.numpy as jnp
import numpy as np
from jax import lax
from jax.experimental import pallas as pl
from jax.experimental.pallas import tpu as pltpu

F32 = jnp.float32
BF16 = jnp.bfloat16
HIGHEST = lax.Precision.HIGHEST

CONV_CH = 512
CONV_W = 3
N_HEADS = 8
HEAD_DIM = 64
N_KV = 2
GQA = N_HEADS // N_KV
CMP_BLOCK = 32
CMP_STRIDE = 16
CMP_HIDDEN = 2 * HEAD_DIM
SEL_BLOCK = 64
N_SEL = 16
WINDOW = 512
ATTN_SCALE = HEAD_DIM ** -0.5
Q_SCALE = ATTN_SCALE * float(np.log2(np.e))
FORCE_SCORE = 1e4
NEG_INF = -1e30
N_GROUPS = 4
EXP_PER_GROUP = 8
N_EXPERTS = N_GROUPS * EXP_PER_GROUP
D_EXPERT = 512
MOE_CHUNK = 512
DEPTH = 1
ALPHA = (2 * DEPTH) ** 0.25
LN_EPS = 1e-5

LANES = 128
SUBLANES = 8
VMEM_LIMIT = 56 * 1024 * 1024

PROJ_TM = 512
Q_TILE = 256
SEL_TK = 512
CMP_CHUNK = 128
WIN_KEYS = WINDOW + Q_TILE
POST_TM = 512
POST_SUB = 512
ROW_TM = 512
ROW_UNROLL = True
ROUTE_LANE0 = N_GROUPS
GATE_ROWS = 16
V_ROWS = HEAD_DIM + 16


def _cparams(sem, vmem=VMEM_LIMIT):
    return pltpu.CompilerParams(dimension_semantics=sem, vmem_limit_bytes=vmem)


def _pack_bf16_pairs(v):
    bits = lax.bitcast_convert_type(v.astype(BF16).astype(F32), jnp.uint32)
    dh = bits.shape[1] // 2
    return jnp.right_shift(bits[:, 0:dh], jnp.uint32(16)) | bits[:, dh:2 * dh]


def _unpack_bf16_pairs(words):
    lo = lax.bitcast_convert_type(jnp.left_shift(words, jnp.uint32(16)), F32)
    hi = lax.bitcast_convert_type(words & jnp.uint32(0xFFFF0000), F32)
    return lo, hi


def _layer_norm(v, g, b):
    mu = jnp.mean(v, axis=-1, keepdims=True)
    d = v - mu
    var = jnp.mean(d * d, axis=-1, keepdims=True)
    return d * lax.rsqrt(var + LN_EPS) * g + b


def _proj_kernel(x_ref, wn_ref, wt_ref, wc_ref,
                 yconv_ref, cmpkv_ref, ksa_ref, kw_ref, qt_ref, vst_ref, vwt_ref, gt_ref,
                 carry_ref):
    si = pl.program_id(1)
    tm = x_ref.shape[1]
    xb = x_ref[0].astype(BF16)
    zn = jnp.dot(xb, wn_ref[...], preferred_element_type=F32)
    zt = lax.dot_general(wt_ref[...], xb, (((1,), (1,)), ((), ())),
                         preferred_element_type=F32)

    cb = zn[:, 0:CONV_CH]
    u = zn[:, CONV_CH:2 * CONV_CH] * zn[:, 2 * CONV_CH:3 * CONV_CH]

    @pl.when(si == 0)
    def _():
        carry_ref[...] = jnp.zeros_like(carry_ref)

    prev = carry_ref[...]
    rows = lax.broadcasted_iota(jnp.int32, u.shape, 0)
    u1 = jnp.where(rows == 0, prev[7:8, :], pltpu.roll(u, 1, 0))
    u2 = jnp.where(rows == 0, prev[6:7, :],
                   jnp.where(rows == 1, prev[7:8, :], pltpu.roll(u, 2, 0)))
    w = wc_ref[...]
    yconv_ref[0] = (cb * (w[0:1, :] * u2 + w[1:2, :] * u1 + w[2:3, :] * u)).astype(BF16)
    carry_ref[...] = u[tm - SUBLANES:tm, :]

    cmpkv_ref[0, 0] = zn[:, 1536:1664]
    cmpkv_ref[0, 1] = zn[:, 1664:1792]
    kw_ref[0] = zn[:, 1920:2048].astype(BF16)
    ksa_ref[0, :, 0:LANES] = zn[:, 1792:1920].astype(BF16)
    n_hot = ksa_ref.shape[2] - LANES
    pos = si * tm + lax.broadcasted_iota(jnp.int32, (tm, n_hot), 0)
    hot = jnp.right_shift(pos, SEL_BLOCK.bit_length() - 1) == lax.broadcasted_iota(jnp.int32, (tm, n_hot), 1)
    ksa_ref[0, :, LANES:] = jnp.where(hot, 1.0, 0.0).astype(BF16)

    qt_ref[0] = zt[0:512, :].astype(BF16)
    ones = jnp.ones((V_ROWS - HEAD_DIM, LANES), BF16)
    for h in range(N_KV):
        for i in range(tm // LANES):
            cols = slice(i * LANES, (i + 1) * LANES)
            vst_ref[0, h, i, 0:HEAD_DIM, :] = zt[512 + h * 64:512 + (h + 1) * 64, cols].astype(BF16)
            vst_ref[0, h, i, HEAD_DIM:V_ROWS, :] = ones
            vwt_ref[0, h, i, 0:HEAD_DIM, :] = zt[640 + h * 64:640 + (h + 1) * 64, cols].astype(BF16)
            vwt_ref[0, h, i, HEAD_DIM:V_ROWS, :] = ones
    gt_ref[0] = jax.nn.sigmoid(zt[768:800, :])


def _proj_call(x, wn, wt, wc):
    bsz, s, d = x.shape
    tm = PROJ_TM
    nblk = s // LANES
    grid = (bsz, s // tm)
    ka = LANES + -(-(s // SEL_BLOCK) // LANES) * LANES
    out_shape = (
        jax.ShapeDtypeStruct((bsz, s, CONV_CH), BF16),
        jax.ShapeDtypeStruct((bsz, 2, s, LANES), F32),
        jax.ShapeDtypeStruct((bsz, s, ka), BF16),
        jax.ShapeDtypeStruct((bsz, s, LANES), BF16),
        jax.ShapeDtypeStruct((bsz, 512, s), BF16),
        jax.ShapeDtypeStruct((bsz, N_KV, nblk, V_ROWS, LANES), BF16),
        jax.ShapeDtypeStruct((bsz, N_KV, nblk, V_ROWS, LANES), BF16),
        jax.ShapeDtypeStruct((bsz, N_KV * GATE_ROWS, s), F32),
    )
    vspec = pl.BlockSpec((1, N_KV, tm // LANES, V_ROWS, LANES), lambda b, i: (b, 0, i, 0, 0))
    return pl.pallas_call(
        _proj_kernel,
        grid=grid,
        in_specs=[
            pl.BlockSpec((1, tm, d), lambda b, i: (b, i, 0)),
            pl.BlockSpec(wn.shape, lambda b, i: (0, 0)),
            pl.BlockSpec(wt.shape, lambda b, i: (0, 0)),
            pl.BlockSpec(wc.shape, lambda b, i: (0, 0)),
        ],
        out_specs=(
            pl.BlockSpec((1, tm, CONV_CH), lambda b, i: (b, i, 0)),
            pl.BlockSpec((1, 2, tm, LANES), lambda b, i: (b, 0, i, 0)),
            pl.BlockSpec((1, tm, ka), lambda b, i: (b, i, 0)),
            pl.BlockSpec((1, tm, LANES), lambda b, i: (b, i, 0)),
            pl.BlockSpec((1, 512, tm), lambda b, i: (b, 0, i)),
            vspec, vspec,
            pl.BlockSpec((1, N_KV * GATE_ROWS, tm), lambda b, i: (b, 0, i)),
        ),
        out_shape=out_shape,
        scratch_shapes=[pltpu.VMEM((SUBLANES, CONV_CH), F32)],
        compiler_params=_cparams(("arbitrary", "arbitrary")),
        name="in_proj_conv",
    )(x, wn, wt, wc)


def _compress_kernel(g_ref, pe_ref, w1_ref, w2_ref, o_ref):
    pe = pe_ref[0]
    n = o_ref.shape[2]
    width = g_ref.shape[3]
    half = CMP_STRIDE * width
    a_lo = jnp.zeros((n, w1_ref.shape[2]), F32)
    a_hi = jnp.zeros((n, w1_ref.shape[2]), F32)
    for t in range(CMP_STRIDE):
        cols = slice(t * width, (t + 1) * width)
        x = g_ref[0, 0, pl.ds(t, n, stride=CMP_STRIDE), :]
        a_lo = a_lo + jnp.dot((x + pe[0:1, cols]).astype(BF16), w1_ref[0, cols, :],
                              preferred_element_type=F32)
        a_hi = a_hi + jnp.dot((x + pe[1:2, cols]).astype(BF16), w1_ref[0, half + t * width:half + (t + 1) * width, :],
                              preferred_element_type=F32)
    hid = a_lo + pltpu.roll(a_hi, n - 1, 0)
    act = jax.nn.gelu(hid)
    o_ref[0, 0] = jnp.dot(act.astype(BF16), w2_ref[0], preferred_element_type=F32)


def _compress_call(ckv, pes, w1s, w2s):
    bsz, _, s, width = ckv.shape
    n = s // CMP_STRIDE
    half = CMP_STRIDE * width
    return pl.pallas_call(
        _compress_kernel,
        grid=(bsz, 2),
        in_specs=[
            pl.BlockSpec((1, 1, s, width), lambda b, k: (b, k, 0, 0)),
            pl.BlockSpec((1, 2, half), lambda b, k: (k, 0, 0)),
            pl.BlockSpec((1, 2 * half, N_KV * CMP_HIDDEN), lambda b, k: (k, 0, 0)),
            pl.BlockSpec((1, N_KV * CMP_HIDDEN, N_KV * HEAD_DIM), lambda b, k: (k, 0, 0)),
        ],
        out_specs=pl.BlockSpec((1, 1, n, N_KV * HEAD_DIM), lambda b, k: (b, k, 0, 0)),
        out_shape=jax.ShapeDtypeStruct((bsz, 2, n, N_KV * HEAD_DIM), F32),
        compiler_params=_cparams(("arbitrary", "arbitrary")),
        name="compress_mlp",
    )(ckv, pes, w1s, w2s)


def _attn_kernel(n_sel, n_top,
                 q_ref, g_ref, kc_ref, vct_ref, ksa_ref, vst_ref, kw_ref, vwt_ref,
                 o_ref, qa_ref, ps_ref, oc_ref, s_ref):
    qb = pl.program_id(2)
    q0 = qb * Q_TILE
    nq = GQA * Q_TILE
    blk = q_ref[0]
    q4 = jnp.concatenate([blk[g * HEAD_DIM:(g + 1) * HEAD_DIM, :] for g in range(GQA)], axis=1)
    lane = lax.broadcasted_iota(jnp.int32, (1, nq), 1)
    t4 = q0 + (lane & (Q_TILE - 1))
    kvh = pl.program_id(1)
    qa_ref[pl.ds(pl.multiple_of(kvh * HEAD_DIM, HEAD_DIM), HEAD_DIM), :] = q4
    qa_ref[pl.ds(pl.multiple_of((N_KV - 1 - kvh) * HEAD_DIM, HEAD_DIM), HEAD_DIM), :] = jnp.zeros_like(q4)
    q2 = qa_ref[0:N_KV * HEAD_DIM, :]

    n_cmp = kc_ref.shape[1]
    chunk = min(CMP_CHUNK, n_cmp)
    q_lane_tiles = Q_TILE // LANES
    for h in range(q_lane_tiles):
        ps_ref[h, 0:SUBLANES, :] = jnp.zeros((SUBLANES, LANES), F32)

    def cmp_branch(n):
        sc = jnp.dot(kc_ref[0, 0:n, :], q2, preferred_element_type=F32)
        last_c = jnp.right_shift(t4 - (CMP_BLOCK - 1), CMP_STRIDE.bit_length() - 1)
        scm = jnp.where(lax.broadcasted_iota(jnp.int32, (n, nq), 0) <= last_c, sc, NEG_INF)
        m_c = jnp.max(scm, axis=0, keepdims=True)
        e_c = jnp.exp2(scm - m_c)
        l_c = jnp.sum(e_c, axis=0, keepdims=True)
        p_c = e_c * jnp.where(last_c >= 0, 1.0 / l_c, 0.0)
        oc_ref[...] = jnp.dot(vct_ref[0, :, 0:n], p_c.astype(BF16), preferred_element_type=F32)
        psum = (p_c[:, 0:Q_TILE] + p_c[:, Q_TILE:2 * Q_TILE]
                + p_c[:, 2 * Q_TILE:3 * Q_TILE] + p_c[:, 3 * Q_TILE:4 * Q_TILE])
        for h in range(q_lane_tiles):
            ps_ref[h, SUBLANES:SUBLANES + n, :] = psum[:, h * LANES:(h + 1) * LANES]
            if n < n_cmp:
                ps_ref[h, SUBLANES + n:, :] = jnp.zeros((n_cmp - n, LANES), F32)

    last_visible = (q0 + Q_TILE - CMP_BLOCK) // CMP_STRIDE
    live_chunks = last_visible // chunk + 1
    for k in range(1, n_cmp // chunk + 1):
        pl.when(live_chunks == k)(functools.partial(cmp_branch, k * chunk))
    o_cmp = oc_ref[...]

    sw = jnp.dot(kw_ref[0, pl.ds(pl.multiple_of(q0, Q_TILE), WIN_KEYS), :], q2,
                 preferred_element_type=F32)
    ql = t4 - q0
    row = lax.broadcasted_iota(jnp.int32, (LANES, nq), 0)
    slabs = []
    for c in range(WIN_KEYS // LANES):
        slab = sw[c * LANES:(c + 1) * LANES, :]
        before_start = WINDOW - 1 - q0 - c * LANES
        if c * LANES < Q_TILE:
            slab = jnp.where(row > jnp.maximum(ql - c * LANES, before_start), slab, NEG_INF)
        elif (c + 1) * LANES > WINDOW:
            slab = jnp.where(row <= ql + (WINDOW - c * LANES), slab, NEG_INF)
        else:
            slab = jnp.where(row > before_start, slab, NEG_INF)
        slabs.append(slab)
    swm = jnp.concatenate(slabs, axis=0)
    m_w = jnp.max(swm, axis=0, keepdims=True)
    p_w = jnp.exp2(swm - m_w)
    wblk = q0 // LANES
    vwt = jnp.concatenate([vwt_ref[0, 0, wblk + i] for i in range(WIN_KEYS // LANES)], axis=1)
    acc_w = jnp.dot(vwt, p_w.astype(BF16), preferred_element_type=F32)
    o_win = acc_w[0:HEAD_DIM, :] * (1.0 / acc_w[HEAD_DIM:HEAD_DIM + 1, :])

    per_sel = SEL_BLOCK // CMP_STRIDE
    def importance(h):
        acc = ps_ref[h, pl.ds(SUBLANES - 1, n_sel, stride=per_sel), :]
        for off in range(CMP_BLOCK // CMP_STRIDE + per_sel - 2):
            acc = acc + ps_ref[h, pl.ds(SUBLANES + off, n_sel, stride=per_sel), :]
        return acc

    imp = jnp.concatenate([importance(h) for h in range(q_lane_tiles)], axis=1)

    jidx = lax.broadcasted_iota(jnp.int32, (n_sel, Q_TILE), 0)
    tq = q0 + lax.broadcasted_iota(jnp.int32, (n_sel, Q_TILE), 1)
    jt = jnp.right_shift(tq, SEL_BLOCK.bit_length() - 1)
    forced = (jidx == 0) | (jidx == jt) | (jidx == jt - 1)
    score = jnp.where(forced, FORCE_SCORE, imp)
    key = jnp.where(jidx > jt, -1, lax.bitcast_convert_type(score, jnp.int32))
    theta = jnp.zeros((1, Q_TILE), jnp.int32)
    for bit in range(30, -1, -1):
        cand = theta | (1 << bit)
        reach = jnp.sum((key >= cand).astype(jnp.int32), axis=0, keepdims=True)
        theta = jnp.where(reach >= n_top, cand, theta)
    above = key > theta
    tied = key == theta
    n_above = jnp.sum(above.astype(jnp.int32), axis=0, keepdims=True)
    lower = (lax.broadcasted_iota(jnp.int32, (n_sel, n_sel), 1)
             < lax.broadcasted_iota(jnp.int32, (n_sel, n_sel), 0)).astype(BF16)
    tied_before = jnp.dot(lower, tied.astype(BF16), preferred_element_type=F32)
    chosen = above | (tied & (tied_before < (n_top - n_above).astype(F32)))
    bias = jnp.where(chosen, 0.0, NEG_INF).astype(BF16)
    bias_row0 = N_KV * HEAD_DIM
    qa_ref[bias_row0:bias_row0 + n_sel, :] = jnp.concatenate([bias] * GQA, axis=1)
    if qa_ref.shape[0] > bias_row0 + n_sel:
        qa_ref[bias_row0 + n_sel:, :] = jnp.zeros((qa_ref.shape[0] - bias_row0 - n_sel, nq), BF16)

    nsub = SEL_TK // LANES

    def scores(c, buf):
        k0 = pl.multiple_of(c * SEL_TK, SEL_TK)
        s = jnp.dot(ksa_ref[0, pl.ds(k0, SEL_TK), :], qa_ref[...], preferred_element_type=F32)
        s_ref[buf] = s
        return jnp.max(s, axis=0, keepdims=True)

    def accumulate(c, buf, mx, m, acc, causal):
        s = s_ref[buf]
        if causal:
            kpos = c * SEL_TK + lax.broadcasted_iota(jnp.int32, (SEL_TK, nq), 0)
            s = jnp.where(kpos <= t4, s, NEG_INF)
            mx = jnp.max(s, axis=0, keepdims=True)
        m_new = jnp.maximum(m, mx)
        p = jnp.exp2(s - m_new)
        vt = jnp.concatenate([vst_ref[0, 0, c * nsub + i] for i in range(nsub)], axis=1)
        acc = jnp.exp2(m - m_new) * acc + jnp.dot(vt, p.astype(BF16), preferred_element_type=F32)
        return m_new, acc

    def pair(c, carry):
        mx0, m, acc = carry
        mx1 = scores(c + 1, 1)
        m, acc = accumulate(c, 0, mx0, m, acc, False)
        mx0 = scores(c + 2, 0)
        m, acc = accumulate(c + 1, 1, mx1, m, acc, False)
        return mx0, m, acc

    n_full = qb // (SEL_TK // Q_TILE)
    n_quads = n_full // 4
    n_pairs = (n_full - 4 * n_quads) // 2
    carry = (scores(0, 0), jnp.full((1, nq), NEG_INF, F32), jnp.zeros((V_ROWS, nq), F32))
    carry = lax.fori_loop(0, n_quads, lambda i, cr: pair(4 * i + 2, pair(4 * i, cr)), carry)
    mx0, m_s, acc_s = lax.fori_loop(0, n_pairs, lambda i, cr: pair(4 * n_quads + 2 * i, cr), carry)
    c_last = 4 * n_quads + 2 * n_pairs

    def leftover_then_own():
        mx1 = scores(c_last + 1, 1)
        m1, acc1 = accumulate(c_last, 0, mx0, m_s, acc_s, False)
        return accumulate(c_last + 1, 1, mx1, m1, acc1, True)[1]

    def own_only():
        return accumulate(c_last, 0, mx0, m_s, acc_s, True)[1]

    acc_s = lax.cond(n_full % 2 == 1, leftover_then_own, own_only)
    o_sel = acc_s[0:HEAD_DIM, :] * (1.0 / acc_s[HEAD_DIM:HEAD_DIM + 1, :])

    gt = g_ref[0]

    def gate(br):
        return jnp.concatenate([gt[br * GQA + g:br * GQA + g + 1, :] for g in range(GQA)], axis=1)

    ot = gate(0) * o_cmp + gate(1) * o_sel + gate(2) * o_win
    stacked = jnp.concatenate([ot[:, g * Q_TILE:(g + 1) * Q_TILE] for g in range(GQA)], axis=0)
    o_ref[0] = stacked.T.astype(BF16)


def _attn_call(qt, gt, kcmp, vct, ksa, vst, kw, vwt):
    bsz, _, s = qt.shape
    n_sel = s // SEL_BLOCK
    n_top = min(N_SEL, n_sel)
    n_cmp = kcmp.shape[1]
    ka = ksa.shape[2]
    nblk = s // LANES
    grid = (bsz, N_KV, s // Q_TILE)
    kern = functools.partial(_attn_kernel, n_sel, n_top)
    return pl.pallas_call(
        kern,
        grid=grid,
        in_specs=[
            pl.BlockSpec((1, GQA * HEAD_DIM, Q_TILE), lambda b, h, i: (b, h, i)),
            pl.BlockSpec((1, GATE_ROWS, Q_TILE), lambda b, h, i: (b, h, i)),
            pl.BlockSpec((1, n_cmp, N_KV * HEAD_DIM), lambda b, h, i: (b, 0, 0)),
            pl.BlockSpec((1, HEAD_DIM, n_cmp), lambda b, h, i: (b, h, 0)),
            pl.BlockSpec((1, s, ka), lambda b, h, i: (b, 0, 0)),
            pl.BlockSpec((1, 1, nblk, V_ROWS, LANES), lambda b, h, i: (b, h, 0, 0, 0)),
            pl.BlockSpec((1, s + WINDOW, N_KV * HEAD_DIM), lambda b, h, i: (b, 0, 0)),
            pl.BlockSpec((1, 1, nblk + WINDOW // LANES, V_ROWS, LANES), lambda b, h, i: (b, h, 0, 0, 0)),
        ],
        out_specs=pl.BlockSpec((1, Q_TILE, GQA * HEAD_DIM), lambda b, h, i: (b, i, h)),
        out_shape=jax.ShapeDtypeStruct((bsz, s, N_HEADS * HEAD_DIM), BF16),
        scratch_shapes=[pltpu.VMEM((ka, GQA * Q_TILE), BF16),
                        pltpu.VMEM((Q_TILE // LANES, n_cmp + SUBLANES, LANES), F32),
                        pltpu.VMEM((HEAD_DIM, GQA * Q_TILE), F32),
                        pltpu.VMEM((2, SEL_TK, GQA * Q_TILE), F32)],
        compiler_params=_cparams(("arbitrary", "arbitrary", "arbitrary")),
        name="nsa_attention",
    )(qt, gt, kcmp, vct, ksa, vst, kw, vwt)


def _post_kernel(x_ref, yc_ref, yn_ref, wo_ref, g1_ref, b1_ref, wr_ref, br_ref, tril_ref,
                 h1_ref, hp_ref, ri_ref, rw_ref, cnt_ref):
    step = pl.program_id(0)
    half = yc_ref.shape[1]
    sub = tril_ref.shape[0]

    @pl.when(step == 0)
    def _():
        cnt_ref[...] = jnp.zeros_like(cnt_ref)

    def route_rows(r0, cnt):
        rows = slice(r0, r0 + sub)
        mix = (jnp.dot(yc_ref[rows, :], wo_ref[0:half, :], preferred_element_type=F32)
               + jnp.dot(yn_ref[rows, :], wo_ref[half:2 * half, :], preferred_element_type=F32))
        h1 = _layer_norm(ALPHA * x_ref[rows, :] + mix, g1_ref[...], b1_ref[...])
        h1_ref[rows, :] = h1

        h_hi = h1.astype(BF16)
        h_hi32 = h_hi.astype(F32)
        h_lo = (h1 - h_hi32).astype(BF16)
        hp_ref[rows, :] = _pack_bf16_pairs(h1)
        parts = jnp.dot(jnp.concatenate([h_hi, h_lo], axis=1), wr_ref[...], preferred_element_type=F32)
        logits = parts[:, 0:LANES] + parts[:, LANES:2 * LANES] + br_ref[...]
        lane = lax.broadcasted_iota(jnp.int32, (sub, LANES), 1)
        big = jnp.int32(LANES)

        def masked_softmax(valid):
            lg = jnp.where(valid, logits, NEG_INF)
            mx = jnp.max(lg, axis=1, keepdims=True)
            ex = jnp.where(valid, jnp.exp(lg - mx), 0.0)
            return ex / jnp.sum(ex, axis=1, keepdims=True)

        def first_max(vals, valid):
            top = jnp.max(jnp.where(valid, vals, -1.0), axis=1, keepdims=True)
            idx = jnp.min(jnp.where(valid & (vals == top), lane, big), axis=1, keepdims=True)
            return top, idx

        is_group = lane < N_GROUPS
        gp, gsel = first_max(masked_softmax(is_group), is_group)
        lo = ROUTE_LANE0 + EXP_PER_GROUP * gsel
        in_group = (lane >= lo) & (lane < lo + EXP_PER_GROUP)
        eprob = masked_softmax(in_group)
        p1, i1 = first_max(eprob, in_group)
        p2, i2 = first_max(eprob, in_group & (lane != i1))
        den = p1 + p2
        w1 = gp * (p1 / den)
        w2 = gp * (p2 / den)

        onehot = (lane == i1) | (lane == i2)
        before = jnp.dot(tril_ref[...], onehot.astype(BF16), preferred_element_type=F32)
        rk = before + cnt
        r1 = jnp.sum(jnp.where(lane == i1, rk, 0.0), axis=1, keepdims=True)
        r2 = jnp.sum(jnp.where(lane == i2, rk, 0.0), axis=1, keepdims=True)

        col = lax.broadcasted_iota(jnp.int32, (sub, ri_ref.shape[1]), 1)
        ri_ref[rows, :] = jnp.where(col == 0, i1 - ROUTE_LANE0,
                                    jnp.where(col == 1, i2 - ROUTE_LANE0,
                                              jnp.where(col == 2, r1.astype(jnp.int32),
                                                        jnp.where(col == 3, r2.astype(jnp.int32), 0))))
        rw_ref[rows, :] = jnp.where(col == 0, w1, jnp.where(col == 1, w2, 0.0))
        return cnt + jnp.sum(onehot.astype(F32), axis=0, keepdims=True)

    cnt = cnt_ref[0:1, :]
    for r0 in range(0, x_ref.shape[0], sub):
        cnt = route_rows(r0, cnt)
    cnt_ref[...] = jnp.broadcast_to(cnt, cnt_ref.shape)


def _post_call(x2, yconv, ynsa, wo, g1, b1, wr, br, tril):
    n_tok, d = x2.shape
    tm = POST_TM
    const = lambda a: pl.BlockSpec(a.shape, lambda i: (0, 0))
    return pl.pallas_call(
        _post_kernel,
        grid=(n_tok // tm,),
        in_specs=[
            pl.BlockSpec((tm, d), lambda i: (i, 0)),
            pl.BlockSpec((tm, yconv.shape[1]), lambda i: (i, 0)),
            pl.BlockSpec((tm, ynsa.shape[1]), lambda i: (i, 0)),
            const(wo), const(g1), const(b1), const(wr), const(br), const(tril),
        ],
        out_specs=(
            pl.BlockSpec((tm, d), lambda i: (i, 0)),
            pl.BlockSpec((tm, d // 2), lambda i: (i, 0)),
            pl.BlockSpec((tm, SUBLANES), lambda i: (i, 0)),
            pl.BlockSpec((tm, SUBLANES), lambda i: (i, 0)),
            pl.BlockSpec((SUBLANES, LANES), lambda i: (0, 0)),
        ),
        out_shape=(
            jax.ShapeDtypeStruct((n_tok, d), F32),
            jax.ShapeDtypeStruct((n_tok, d // 2), jnp.uint32),
            jax.ShapeDtypeStruct((n_tok, SUBLANES), jnp.int32),
            jax.ShapeDtypeStruct((n_tok, SUBLANES), F32),
            jax.ShapeDtypeStruct((SUBLANES, LANES), F32),
        ),
        compiler_params=_cparams(("arbitrary",)),
        name="out_proj_ln1_router",
    )(x2, yconv, ynsa, wo, g1, b1, wr, br, tril)


def _row_copy(src_ref, src_row, dst_ref, dst_row, sem):
    return pltpu.make_async_copy(src_ref.at[pl.ds(src_row, 1), :], dst_ref.at[pl.ds(dst_row, 1), :], sem)


ROUTE_COLS = SUBLANES


def _sorted_row(pstart_ref, route_ref, r, slot):
    return pstart_ref[route_ref[ROUTE_COLS * r + slot]] + route_ref[ROUTE_COLS * r + 2 + slot]


def _dispatch_kernel(pends_ref, nu_ref, pstart_ref, route_ref, h_ref, xs_ref, zero_ref, sem, zsem):
    tm = h_ref.shape[0]
    n_chunk = xs_ref.shape[0] // MOE_CHUNK

    @pl.when(pl.program_id(0) == 0)
    def _():
        zero_ref[...] = jnp.zeros_like(zero_ref)

        def zero_chunk(row0):
            row0 = pl.multiple_of(row0, MOE_CHUNK)
            return pltpu.make_async_copy(zero_ref, xs_ref.at[pl.ds(row0, MOE_CHUNK), :], zsem)

        def each_padded_chunk(act):
            def per_expert(e, _):
                end = pends_ref[e]
                start = jnp.where(e > 0, pends_ref[jnp.maximum(e - 1, 0)], 0)

                @pl.when(end > start)
                def _():
                    act(zero_chunk(end - MOE_CHUNK))
                return 0

            lax.fori_loop(0, N_EXPERTS, per_expert, 0)

            def per_dead_chunk(c, _):
                act(zero_chunk(c * MOE_CHUNK))
                return 0

            lax.fori_loop(nu_ref[0], n_chunk, per_dead_chunk, 0)

        each_padded_chunk(lambda cp: cp.start())
        each_padded_chunk(lambda cp: cp.wait())

    def issue(r, _):
        for slot in range(2):
            _row_copy(h_ref, r, xs_ref, _sorted_row(pstart_ref, route_ref, r, slot), sem).start()
        return 0

    lax.fori_loop(0, tm, issue, 0, unroll=ROW_UNROLL)
    for slot in range(2):
        pltpu.make_async_copy(h_ref, xs_ref.at[pl.ds(0, tm), :], sem).wait()


def _dispatch_call(pends, n_used, pstarts, route, h1, n_rows):
    n_tok, d = h1.shape
    tm = ROW_TM
    grid_spec = pltpu.PrefetchScalarGridSpec(
        num_scalar_prefetch=3,
        grid=(n_tok // tm,),
        in_specs=[
            pl.BlockSpec((ROUTE_COLS * tm,), lambda i, pe, nu, ps: (i,), memory_space=pltpu.SMEM),
            pl.BlockSpec((tm, d), lambda i, pe, nu, ps: (i, 0)),
        ],
        out_specs=pl.BlockSpec(memory_space=pl.ANY),
        scratch_shapes=[pltpu.VMEM((MOE_CHUNK, d), h1.dtype), pltpu.SemaphoreType.DMA, pltpu.SemaphoreType.DMA],
    )
    return pl.pallas_call(
        _dispatch_kernel,
        grid_spec=grid_spec,
        out_shape=jax.ShapeDtypeStruct((n_rows, d), h1.dtype),
        compiler_params=_cparams(("arbitrary",)),
        name="moe_dispatch",
    )(pends, n_used, pstarts, route, h1)


def _expert_kernel(ce_ref, nu_ref, xs_ref, wi_ref, wo_ref, o_ref, wib_ref, wob_ref):
    c = pl.program_id(0)
    live = c < nu_ref[0]

    @pl.when(live & ((c == 0) | (ce_ref[c] != ce_ref[jnp.maximum(c - 1, 0)])))
    def _():
        wib_ref[...] = wi_ref[0].astype(BF16)
        wob_ref[...] = wo_ref[0].astype(BF16)

    @pl.when(live)
    def _():
        x_lo, x_hi = _unpack_bf16_pairs(xs_ref[...])
        dh = x_lo.shape[1]
        gu = (jnp.dot(x_lo.astype(BF16), wib_ref[0:dh, :], preferred_element_type=F32)
              + jnp.dot(x_hi.astype(BF16), wib_ref[dh:2 * dh, :], preferred_element_type=F32))
        gate = gu[:, 0:D_EXPERT]
        act = gate * jax.nn.sigmoid(gate) * gu[:, D_EXPERT:2 * D_EXPERT]
        out = jnp.dot(act.astype(BF16), wob_ref[...], preferred_element_type=F32)
        o_ref[...] = _pack_bf16_pairs(out)

    @pl.when(jnp.logical_not(live))
    def _():
        o_ref[...] = jnp.zeros_like(o_ref)


def _expert_call(chunk_exp, n_used, xs, wi, wo):
    n_rows, dh = xs.shape
    d = wo.shape[2]
    n_chunk = n_rows // MOE_CHUNK

    def live(c, nu):
        return jnp.maximum(jnp.minimum(c, nu[0] - 1), 0)

    grid_spec = pltpu.PrefetchScalarGridSpec(
        num_scalar_prefetch=2,
        grid=(n_chunk,),
        in_specs=[
            pl.BlockSpec((MOE_CHUNK, dh), lambda c, ce, nu: (live(c, nu), 0)),
            pl.BlockSpec((1,) + wi.shape[1:], lambda c, ce, nu: (ce[live(c, nu)], 0, 0)),
            pl.BlockSpec((1,) + wo.shape[1:], lambda c, ce, nu: (ce[live(c, nu)], 0, 0)),
        ],
        out_specs=pl.BlockSpec((MOE_CHUNK, d // 2), lambda c, ce, nu: (c, 0)),
        scratch_shapes=[pltpu.VMEM(wi.shape[1:], BF16), pltpu.VMEM(wo.shape[1:], BF16)],
    )
    return pl.pallas_call(
        _expert_kernel,
        grid_spec=grid_spec,
        out_shape=jax.ShapeDtypeStruct((n_rows, d // 2), jnp.uint32),
        compiler_params=_cparams(("arbitrary",)),
        name="moe_experts",
    )(chunk_exp, n_used, xs, wi, wo)


def _combine_kernel(pstart_ref, dest_ref, dest_next_ref, h1_ref, rw_ref, p_ref, ys_ref,
                    g2_ref, b2_ref, wp_ref, wg_ref, bg_ref, g3_ref, b3_ref,
                    o_ref, rows_ref, sems):
    tm = h1_ref.shape[0]
    step = pl.program_id(0)
    cur = step % 2

    def gather(idx_ref, buf):
        def issue(r, _):
            for slot in range(2):
                _row_copy(ys_ref, _sorted_row(pstart_ref, idx_ref, r, slot), rows_ref.at[buf, slot], r,
                          sems.at[buf]).start()
            return 0

        lax.fori_loop(0, tm, issue, 0, unroll=ROW_UNROLL)

    def await_rows(buf):
        for slot in range(2):
            pltpu.make_async_copy(ys_ref.at[pl.ds(0, tm), :], rows_ref.at[buf, slot], sems.at[buf]).wait()

    @pl.when(step == 0)
    def _():
        gather(dest_ref, 0)

    await_rows(cur)
    rw = rw_ref[...]
    lo0, hi0 = _unpack_bf16_pairs(rows_ref[cur, 0])
    lo1, hi1 = _unpack_bf16_pairs(rows_ref[cur, 1])
    ffn = jnp.concatenate([rw[:, 0:1] * lo0 + rw[:, 1:2] * lo1, rw[:, 0:1] * hi0 + rw[:, 1:2] * hi1], axis=1)
    gather(dest_next_ref, 1 - cur)
    h2 =_layer_norm(ALPHA * h1_ref[...] + ffn, g2_ref[...], b2_ref[...])
    emb = jnp.dot(p_ref[...].astype(BF16), wp_ref[...], preferred_element_type=F32)
    gate = jax.nn.sigmoid(jnp.dot(h2.astype(BF16), wg_ref[...], preferred_element_type=F32) + bg_ref[...])
    o_ref[...] = _layer_norm(ALPHA * h2 + emb * gate, g3_ref[...], b3_ref[...])

    @pl.when(step == pl.num_programs(0) - 1)
    def _():
        await_rows(1 - cur)


def _combine_call(pstarts, route, h1, rw, p2, ys, g2, b2, wp, wg, bg, g3, b3):
    n_tok, d = h1.shape
    tm = ROW_TM
    const = lambda a: pl.BlockSpec(a.shape, lambda i: (0, 0))
    return pl.pallas_call(
        _combine_kernel,
        grid=(n_tok // tm,),
        in_specs=[
            pl.BlockSpec(memory_space=pltpu.SMEM),
            pl.BlockSpec((ROUTE_COLS * tm,), lambda i: (i,), memory_space=pltpu.SMEM),
            pl.BlockSpec((ROUTE_COLS * tm,), lambda i: (jnp.minimum(i + 1, n_tok // tm - 1),),
                         memory_space=pltpu.SMEM),
            pl.BlockSpec((tm, d), lambda i: (i, 0)),
            pl.BlockSpec((tm, rw.shape[1]), lambda i: (i, 0)),
            pl.BlockSpec((tm, p2.shape[1]), lambda i: (i, 0)),
            pl.BlockSpec(memory_space=pl.ANY),
            const(g2), const(b2), const(wp), const(wg), const(bg), const(g3), const(b3),
        ],
        out_specs=pl.BlockSpec((tm, d), lambda i: (i, 0)),
        out_shape=jax.ShapeDtypeStruct((n_tok, d), F32),
        scratch_shapes=[pltpu.VMEM((2, 2, tm, d // 2), jnp.uint32), pltpu.SemaphoreType.DMA((2,))],
        compiler_params=_cparams(("arbitrary",)),
        name="moe_combine_ln_ple",
    )(pstarts, route, route, h1, rw, p2, ys, g2, b2, wp, wg, bg, g3, b3)


def _gate_columns():
    cols = np.zeros((N_KV, GATE_ROWS), np.int32)
    live = np.zeros((N_KV, GATE_ROWS), np.float32)
    for h in range(N_KV):
        for br in range(3):
            for g in range(GQA):
                cols[h, br * GQA + g] = (h * GQA + g) * 3 + br
                live[h, br * GQA + g] = 1.0
    return cols.reshape(-1), live.reshape(-1)


def _layer(x, p, w_in, w_conv, pe_ck, w1_ck, w2_ck, pe_cv, w1_cv, w2_cv, w_out, ln1_g, ln1_b,
           w_rg, b_rg, w_re, b_re, w_e_in, w_e_out, ln2_g, ln2_b, w_ple, w_ple_gate, b_ple_gate,
           ln3_g, ln3_b):
    bsz, s, d = x.shape
    n_tok = bsz * s
    row = lambda v: v.reshape(1, -1)

    c_q, c_kc, c_vc, c_ks, c_vs, c_kw, c_vw, c_g = 1536, 2048, 2176, 2304, 2432, 2560, 2688, 2816
    wn = jnp.concatenate([w_in[:, 0:c_q], w_in[:, c_kc:c_ks], w_in[:, c_ks:c_vs], w_in[:, c_kw:c_vw]],
                         axis=1).astype(BF16)
    gcols, glive = _gate_columns()
    w_gate = w_in[:, c_g:c_g + 3 * N_HEADS][:, gcols] * glive[None, :]
    wt = jnp.concatenate([w_in[:, c_q:c_kc] * Q_SCALE, w_in[:, c_vs:c_kw], w_in[:, c_vw:c_g], w_gate],
                         axis=1).T.astype(BF16)

    yconv, cmpkv, ksa, kw, qt, vst, vwt, gt = _proj_call(x, wn, wt, w_conv)

    half_blk = CMP_BLOCK // 2
    eye = jnp.eye(N_KV, dtype=F32)

    def per_head_w1(w1):
        w = w1.reshape(2, half_blk, HEAD_DIM, CMP_HIDDEN)
        w = jnp.einsum('ptdc,hg->pthdgc', w, eye)
        return w.reshape(2 * half_blk * N_KV * HEAD_DIM, N_KV * CMP_HIDDEN)

    def per_head_w2(w2):
        return jnp.einsum('cd,hg->hcgd', w2, eye).reshape(N_KV * CMP_HIDDEN, N_KV * HEAD_DIM)

    def per_head_pe(pe):
        return jnp.broadcast_to(pe.reshape(2, half_blk, 1, HEAD_DIM),
                                (2, half_blk, N_KV, HEAD_DIM)).reshape(2, half_blk * N_KV * HEAD_DIM)

    pes = jnp.stack([per_head_pe(pe_ck), per_head_pe(pe_cv)])
    w1s = jnp.stack([per_head_w1(w1_ck), per_head_w1(w1_cv)]).astype(BF16)
    w2s = jnp.stack([per_head_w2(w2_ck), per_head_w2(w2_cv)]).astype(BF16)
    cmp_out = _compress_call(cmpkv, pes, w1s, w2s)
    kcmp = cmp_out[:, 0].astype(BF16)
    vct = cmp_out[:, 1].transpose(0, 2, 1).astype(BF16)

    kw = jnp.pad(kw, ((0, 0), (WINDOW, 0), (0, 0)))
    vwt = jnp.pad(vwt, ((0, 0), (0, 0), (WINDOW // LANES, 0), (0, 0), (0, 0)))
    ynsa = _attn_call(qt, gt, kcmp, vct, ksa, vst, kw, vwt)

    wr = jnp.zeros((d, LANES), F32).at[:, 0:N_GROUPS].set(w_rg)
    wr = wr.at[:, ROUTE_LANE0:ROUTE_LANE0 + N_EXPERTS].set(w_re)
    br = jnp.zeros((1, LANES), F32).at[0, 0:N_GROUPS].set(b_rg)
    br = br.at[0, ROUTE_LANE0:ROUTE_LANE0 + N_EXPERTS].set(b_re)
    wr_hi = wr.astype(BF16)
    wr_lo = (wr - wr_hi.astype(F32)).astype(BF16)
    wr = jnp.concatenate([jnp.concatenate([wr_hi, wr_lo], axis=1),
                          jnp.concatenate([wr_hi, jnp.zeros_like(wr_lo)], axis=1)], axis=0)
    tril = jnp.asarray(np.tril(np.ones((POST_SUB, POST_SUB), np.float32), -1), BF16)
    h1, h1_packed, ri, rw, cnt = _post_call(x.reshape(n_tok, d), yconv.reshape(n_tok, -1), ynsa.reshape(n_tok, -1),
                                 w_out.astype(BF16), row(ln1_g), row(ln1_b), wr, br, tril)

    counts = cnt[0, ROUTE_LANE0:ROUTE_LANE0 + N_EXPERTS].astype(jnp.int32)
    pcounts = (counts + MOE_CHUNK - 1) // MOE_CHUNK * MOE_CHUNK
    pends = jnp.cumsum(pcounts)
    pstarts = (pends - pcounts).astype(jnp.int32)
    route = ri.reshape(-1)
    n_asg = n_tok * 2
    n_chunk = -(-n_asg // MOE_CHUNK) + N_EXPERTS
    chunk_row0 = jnp.arange(n_chunk, dtype=jnp.int32) * MOE_CHUNK
    chunk_exp = jnp.minimum(jnp.sum(pends[None, :] <= chunk_row0[:, None], axis=1), N_EXPERTS - 1).astype(jnp.int32)
    n_used = (pends[-1:] // MOE_CHUNK).astype(jnp.int32)

    xs = _dispatch_call(pends.astype(jnp.int32), n_used, pstarts, route, h1_packed, n_chunk * MOE_CHUNK)
    ys = _expert_call(chunk_exp, n_used, xs, w_e_in, w_e_out)
    out = _combine_call(pstarts, route, h1, rw, p.reshape(n_tok, -1), ys, row(ln2_g), row(ln2_b),
                        w_ple.astype(BF16), w_ple_gate.astype(BF16), row(b_ple_gate), row(ln3_g), row(ln3_b))
    return out.reshape(bsz, s, d)


def kernel(x, p, w_in, w_conv, pe_ck, w1_ck, w2_ck, pe_cv, w1_cv, w2_cv, w_out, ln1_g, ln1_b, w_rg, b_rg, w_re, b_re, w_e_in, w_e_out, ln2_g, ln2_b, w_ple, w_ple_gate, b_ple_gate, ln3_g, ln3_b):
    assert w_in.shape[0] == DEPTH, "residual scaling ALPHA is derived from DEPTH"
    h = x
    for i in range(DEPTH):
        h = _layer(h, p[i], w_in[i], w_conv[i], pe_ck[i], w1_ck[i], w2_ck[i], pe_cv[i], w1_cv[i], w2_cv[i],
                   w_out[i], ln1_g[i], ln1_b[i], w_rg[i], b_rg[i], w_re[i], b_re[i], w_e_in[i], w_e_out[i],
                   ln2_g[i], ln2_b[i], w_ple[i], w_ple_gate[i], b_ple_gate[i], ln3_g[i], ln3_b[i])
    return h
```

```python
import functools

import jax
import jax.numpy as jnp
import numpy as np
from jax import lax
from jax.experimental import pallas as pl
from jax.experimental.pallas import tpu as pltpu

F32 = jnp.float32
BF16 = jnp.bfloat16

CONV_CH = 512
N_HEADS = 8
HEAD_DIM = 64
N_KV = 2
GQA = N_HEADS // N_KV
CMP_BLOCK = 32
CMP_STRIDE = 16
CMP_HIDDEN = 2 * HEAD_DIM
SEL_BLOCK = 64
N_SEL = 16
WINDOW = 512
ATTN_SCALE = HEAD_DIM ** -0.5
Q_SCALE = ATTN_SCALE * float(np.log2(np.e))
FORCE_SCORE = 1e4
NEG_INF = -1e30
N_GROUPS = 4
EXP_PER_GROUP = 8
N_EXPERTS = N_GROUPS * EXP_PER_GROUP
D_EXPERT = 512
MOE_CHUNK = 512
DEPTH = 1
ALPHA = (2 * DEPTH) ** 0.25
LN_EPS = 1e-5

LANES = 128
SUBLANES = 8
VMEM_LIMIT = 56 * 1024 * 1024

PROJ_TM = 512
Q_TILE = 256
SEL_TK = 512
CMP_CHUNK = 128
WIN_KEYS = WINDOW + Q_TILE
POST_TM = 512
ROW_TM = 512
ROW_UNROLL = True
ROUTE_LANE0 = N_GROUPS
ROUTE_ROWS = 48
ROUTE_COLS = SUBLANES
GATE_ROWS = 16
V_ROWS = HEAD_DIM + 16


def _cparams(sem, vmem=VMEM_LIMIT):
    return pltpu.CompilerParams(dimension_semantics=sem, vmem_limit_bytes=vmem)


def _pack_bf16_pairs(v):
    bits = lax.bitcast_convert_type(v.astype(BF16).astype(F32), jnp.uint32)
    dh = bits.shape[1] // 2
    return jnp.right_shift(bits[:, 0:dh], jnp.uint32(16)) | bits[:, dh:2 * dh]


def _unpack_bf16_pairs(words):
    lo = lax.bitcast_convert_type(jnp.left_shift(words, jnp.uint32(16)), F32)
    hi = lax.bitcast_convert_type(words & jnp.uint32(0xFFFF0000), F32)
    return lo, hi


def _layer_norm(v, g, b):
    mu = jnp.mean(v, axis=-1, keepdims=True)
    d = v - mu
    var = jnp.mean(d * d, axis=-1, keepdims=True)
    return d * lax.rsqrt(var + LN_EPS) * g + b


def _proj_kernel(x_ref, wn_ref, wt_ref, wc_ref,
                 yconv_ref, cmpkv_ref, ksa_ref, kw_ref, qt_ref, vst_ref, vwt_ref, gt_ref,
                 carry_ref):
    si = pl.program_id(1)
    tm = x_ref.shape[1]
    xb = x_ref[0].astype(BF16)
    zn = jnp.dot(xb, wn_ref[...], preferred_element_type=F32)
    zt = lax.dot_general(wt_ref[...], xb, (((1,), (1,)), ((), ())),
                         preferred_element_type=F32)

    cb = zn[:, 0:CONV_CH]
    u = zn[:, CONV_CH:2 * CONV_CH] * zn[:, 2 * CONV_CH:3 * CONV_CH]

    @pl.when(si == 0)
    def _():
        carry_ref[...] = jnp.zeros_like(carry_ref)

    prev = carry_ref[...]
    rows = lax.broadcasted_iota(jnp.int32, u.shape, 0)
    u1 = jnp.where(rows == 0, prev[7:8, :], pltpu.roll(u, 1, 0))
    u2 = jnp.where(rows == 0, prev[6:7, :],
                   jnp.where(rows == 1, prev[7:8, :], pltpu.roll(u, 2, 0)))
    w = wc_ref[...]
    yconv_ref[0] = (cb * (w[0:1, :] * u2 + w[1:2, :] * u1 + w[2:3, :] * u)).astype(BF16)
    carry_ref[...] = u[tm - SUBLANES:tm, :]

    cmpkv_ref[0, 0] = zn[:, 1536:1664]
    cmpkv_ref[0, 1] = zn[:, 1664:1792]
    kw_ref[0] = zn[:, 1920:2048].astype(BF16)
    ksa_ref[0, :, 0:LANES] = zn[:, 1792:1920].astype(BF16)
    n_hot = ksa_ref.shape[2] - LANES
    pos = si * tm + lax.broadcasted_iota(jnp.int32, (tm, n_hot), 0)
    hot = jnp.right_shift(pos, SEL_BLOCK.bit_length() - 1) == lax.broadcasted_iota(jnp.int32, (tm, n_hot), 1)
    ksa_ref[0, :, LANES:] = jnp.where(hot, 1.0, 0.0).astype(BF16)

    qt_ref[0] = zt[0:512, :].astype(BF16)
    ones = jnp.ones((V_ROWS - HEAD_DIM, LANES), BF16)
    for h in range(N_KV):
        for i in range(tm // LANES):
            cols = slice(i * LANES, (i + 1) * LANES)
            vst_ref[0, h, i, 0:HEAD_DIM, :] = zt[512 + h * 64:512 + (h + 1) * 64, cols].astype(BF16)
            vst_ref[0, h, i, HEAD_DIM:V_ROWS, :] = ones
            vwt_ref[0, h, i, 0:HEAD_DIM, :] = zt[640 + h * 64:640 + (h + 1) * 64, cols].astype(BF16)
            vwt_ref[0, h, i, HEAD_DIM:V_ROWS, :] = ones
    gt_ref[0] = jax.nn.sigmoid(zt[768:800, :])


def _proj_call(x, wn, wt, wc):
    bsz, s, d = x.shape
    tm = PROJ_TM
    nblk = s // LANES
    grid = (bsz, s // tm)
    ka = LANES + -(-(s // SEL_BLOCK) // LANES) * LANES
    out_shape = (
        jax.ShapeDtypeStruct((bsz, s, CONV_CH), BF16),
        jax.ShapeDtypeStruct((bsz, 2, s, LANES), F32),
        jax.ShapeDtypeStruct((bsz, s, ka), BF16),
        jax.ShapeDtypeStruct((bsz, s, LANES), BF16),
        jax.ShapeDtypeStruct((bsz, 512, s), BF16),
        jax.ShapeDtypeStruct((bsz, N_KV, nblk, V_ROWS, LANES), BF16),
        jax.ShapeDtypeStruct((bsz, N_KV, nblk, V_ROWS, LANES), BF16),
        jax.ShapeDtypeStruct((bsz, N_KV * GATE_ROWS, s), F32),
    )
    vspec = pl.BlockSpec((1, N_KV, tm // LANES, V_ROWS, LANES), lambda b, i: (b, 0, i, 0, 0))
    return pl.pallas_call(
        _proj_kernel,
        grid=grid,
        in_specs=[
            pl.BlockSpec((1, tm, d), lambda b, i: (b, i, 0)),
            pl.BlockSpec(wn.shape, lambda b, i: (0, 0)),
            pl.BlockSpec(wt.shape, lambda b, i: (0, 0)),
            pl.BlockSpec(wc.shape, lambda b, i: (0, 0)),
        ],
        out_specs=(
            pl.BlockSpec((1, tm, CONV_CH), lambda b, i: (b, i, 0)),
            pl.BlockSpec((1, 2, tm, LANES), lambda b, i: (b, 0, i, 0)),
            pl.BlockSpec((1, tm, ka), lambda b, i: (b, i, 0)),
            pl.BlockSpec((1, tm, LANES), lambda b, i: (b, i, 0)),
            pl.BlockSpec((1, 512, tm), lambda b, i: (b, 0, i)),
            vspec, vspec,
            pl.BlockSpec((1, N_KV * GATE_ROWS, tm), lambda b, i: (b, 0, i)),
        ),
        out_shape=out_shape,
        scratch_shapes=[pltpu.VMEM((SUBLANES, CONV_CH), F32)],
        compiler_params=_cparams(("arbitrary", "arbitrary")),
        name="in_proj_conv",
    )(x, wn, wt, wc)


def _compress_kernel(g_ref, pe_ref, w1_ref, w2_ref, o_ref):
    pe = pe_ref[0]
    n = o_ref.shape[2]
    width = g_ref.shape[3]
    half = CMP_STRIDE * width
    a_lo = jnp.zeros((n, w1_ref.shape[2]), F32)
    a_hi = jnp.zeros((n, w1_ref.shape[2]), F32)
    for t in range(CMP_STRIDE):
        cols = slice(t * width, (t + 1) * width)
        x = g_ref[0, 0, pl.ds(t, n, stride=CMP_STRIDE), :]
        a_lo = a_lo + jnp.dot((x + pe[0:1, cols]).astype(BF16), w1_ref[0, cols, :],
                              preferred_element_type=F32)
        a_hi = a_hi + jnp.dot((x + pe[1:2, cols]).astype(BF16), w1_ref[0, half + t * width:half + (t + 1) * width, :],
                              preferred_element_type=F32)
    hid = a_lo + pltpu.roll(a_hi, n - 1, 0)
    act = jax.nn.gelu(hid)
    o_ref[0, 0] = jnp.dot(act.astype(BF16), w2_ref[0], preferred_element_type=F32)


def _compress_call(ckv, pes, w1s, w2s):
    bsz, _, s, width = ckv.shape
    n = s // CMP_STRIDE
    half = CMP_STRIDE * width
    return pl.pallas_call(
        _compress_kernel,
        grid=(bsz, 2),
        in_specs=[
            pl.BlockSpec((1, 1, s, width), lambda b, k: (b, k, 0, 0)),
            pl.BlockSpec((1, 2, half), lambda b, k: (k, 0, 0)),
            pl.BlockSpec((1, 2 * half, N_KV * CMP_HIDDEN), lambda b, k: (k, 0, 0)),
            pl.BlockSpec((1, N_KV * CMP_HIDDEN, N_KV * HEAD_DIM), lambda b, k: (k, 0, 0)),
        ],
        out_specs=pl.BlockSpec((1, 1, n, N_KV * HEAD_DIM), lambda b, k: (b, k, 0, 0)),
        out_shape=jax.ShapeDtypeStruct((bsz, 2, n, N_KV * HEAD_DIM), F32),
        compiler_params=_cparams(("arbitrary", "arbitrary")),
        name="compress_mlp",
    )(ckv, pes, w1s, w2s)


def _attn_kernel(n_sel, n_top,
                 q_ref, g_ref, kc_ref, vct_ref, ksa_ref, vst_ref, kw_ref, vwt_ref,
                 o_ref, qa_ref, ps_ref, oc_ref, s_ref):
    qb = pl.program_id(2)
    q0 = qb * Q_TILE
    nq = GQA * Q_TILE
    blk = q_ref[0]
    q4 = jnp.concatenate([blk[g * HEAD_DIM:(g + 1) * HEAD_DIM, :] for g in range(GQA)], axis=1)
    lane = lax.broadcasted_iota(jnp.int32, (1, nq), 1)
    t4 = q0 + (lane & (Q_TILE - 1))
    kvh = pl.program_id(1)
    qa_ref[pl.ds(pl.multiple_of(kvh * HEAD_DIM, HEAD_DIM), HEAD_DIM), :] = q4
    qa_ref[pl.ds(pl.multiple_of((N_KV - 1 - kvh) * HEAD_DIM, HEAD_DIM), HEAD_DIM), :] = jnp.zeros_like(q4)
    q2 = qa_ref[0:N_KV * HEAD_DIM, :]

    n_cmp = kc_ref.shape[1]
    chunk = min(CMP_CHUNK, n_cmp)
    q_lane_tiles = Q_TILE // LANES
    for h in range(q_lane_tiles):
        ps_ref[h, 0:SUBLANES, :] = jnp.zeros((SUBLANES, LANES), F32)

    def cmp_branch(n):
        sc = jnp.dot(kc_ref[0, 0:n, :], q2, preferred_element_type=F32)
        last_c = jnp.right_shift(t4 - (CMP_BLOCK - 1), CMP_STRIDE.bit_length() - 1)
        scm = jnp.where(lax.broadcasted_iota(jnp.int32, (n, nq), 0) <= last_c, sc, NEG_INF)
        m_c = jnp.max(scm, axis=0, keepdims=True)
        e_c = jnp.exp2(scm - m_c)
        l_c = jnp.sum(e_c, axis=0, keepdims=True)
        p_c = e_c * jnp.where(last_c >= 0, 1.0 / l_c, 0.0)
        oc_ref[...] = jnp.dot(vct_ref[0, :, 0:n], p_c.astype(BF16), preferred_element_type=F32)
        psum = (p_c[:, 0:Q_TILE] + p_c[:, Q_TILE:2 * Q_TILE]
                + p_c[:, 2 * Q_TILE:3 * Q_TILE] + p_c[:, 3 * Q_TILE:4 * Q_TILE])
        for h in range(q_lane_tiles):
            ps_ref[h, SUBLANES:SUBLANES + n, :] = psum[:, h * LANES:(h + 1) * LANES]
            if n < n_cmp:
                ps_ref[h, SUBLANES + n:, :] = jnp.zeros((n_cmp - n, LANES), F32)

    last_visible = (q0 + Q_TILE - CMP_BLOCK) // CMP_STRIDE
    live_chunks = last_visible // chunk + 1
    for k in range(1, n_cmp // chunk + 1):
        pl.when(live_chunks == k)(functools.partial(cmp_branch, k * chunk))
    o_cmp = oc_ref[...]

    sw = jnp.dot(kw_ref[0, pl.ds(pl.multiple_of(q0, Q_TILE), WIN_KEYS), :], q2,
                 preferred_element_type=F32)
    ql = t4 - q0
    row = lax.broadcasted_iota(jnp.int32, (LANES, nq), 0)
    slabs = []
    for c in range(WIN_KEYS // LANES):
        slab = sw[c * LANES:(c + 1) * LANES, :]
        before_start = WINDOW - 1 - q0 - c * LANES
        if c * LANES < Q_TILE:
            slab = jnp.where(row > jnp.maximum(ql - c * LANES, before_start), slab, NEG_INF)
        elif (c + 1) * LANES > WINDOW:
            slab = jnp.where(row <= ql + (WINDOW - c * LANES), slab, NEG_INF)
        else:
            slab = jnp.where(row > before_start, slab, NEG_INF)
        slabs.append(slab)
    swm = jnp.concatenate(slabs, axis=0)
    m_w = jnp.max(swm, axis=0, keepdims=True)
    p_w = jnp.exp2(swm - m_w)
    wblk = q0 // LANES
    vwt = jnp.concatenate([vwt_ref[0, 0, wblk + i] for i in range(WIN_KEYS // LANES)], axis=1)
    acc_w = jnp.dot(vwt, p_w.astype(BF16), preferred_element_type=F32)
    o_win = acc_w[0:HEAD_DIM, :] * (1.0 / acc_w[HEAD_DIM:HEAD_DIM + 1, :])

    per_sel = SEL_BLOCK // CMP_STRIDE
    def importance(h):
        acc = ps_ref[h, pl.ds(SUBLANES - 1, n_sel, stride=per_sel), :]
        for off in range(CMP_BLOCK // CMP_STRIDE + per_sel - 2):
            acc = acc + ps_ref[h, pl.ds(SUBLANES + off, n_sel, stride=per_sel), :]
        return acc

    imp = jnp.concatenate([importance(h) for h in range(q_lane_tiles)], axis=1)

    jidx = lax.broadcasted_iota(jnp.int32, (n_sel, Q_TILE), 0)
    tq = q0 + lax.broadcasted_iota(jnp.int32, (n_sel, Q_TILE), 1)
    jt = jnp.right_shift(tq, SEL_BLOCK.bit_length() - 1)
    forced = (jidx == 0) | (jidx == jt) | (jidx == jt - 1)
    score = jnp.where(forced, FORCE_SCORE, imp)
    key = jnp.where(jidx > jt, -1, lax.bitcast_convert_type(score, jnp.int32))
    theta = jnp.zeros((1, Q_TILE), jnp.int32)
    for bit in range(30, -1, -1):
        cand = theta | (1 << bit)
        reach = jnp.sum((key >= cand).astype(jnp.int32), axis=0, keepdims=True)
        theta = jnp.where(reach >= n_top, cand, theta)
    above = key > theta
    tied = key == theta
    n_above = jnp.sum(above.astype(jnp.int32), axis=0, keepdims=True)
    lower = (lax.broadcasted_iota(jnp.int32, (n_sel, n_sel), 1)
             < lax.broadcasted_iota(jnp.int32, (n_sel, n_sel), 0)).astype(BF16)
    tied_before = jnp.dot(lower, tied.astype(BF16), preferred_element_type=F32)
    chosen = above | (tied & (tied_before < (n_top - n_above).astype(F32)))
    bias = jnp.where(chosen, 0.0, NEG_INF).astype(BF16)
    bias_row0 = N_KV * HEAD_DIM
    qa_ref[bias_row0:bias_row0 + n_sel, :] = jnp.concatenate([bias] * GQA, axis=1)
    if qa_ref.shape[0] > bias_row0 + n_sel:
        qa_ref[bias_row0 + n_sel:, :] = jnp.zeros((qa_ref.shape[0] - bias_row0 - n_sel, nq), BF16)

    nsub = SEL_TK // LANES

    def scores(c, buf):
        k0 = pl.multiple_of(c * SEL_TK, SEL_TK)
        s = jnp.dot(ksa_ref[0, pl.ds(k0, SEL_TK), :], qa_ref[...], preferred_element_type=F32)
        s_ref[buf] = s
        return jnp.max(s, axis=0, keepdims=True)

    def accumulate(c, buf, mx, m, acc, causal):
        s = s_ref[buf]
        if causal:
            kpos = c * SEL_TK + lax.broadcasted_iota(jnp.int32, (SEL_TK, nq), 0)
            s = jnp.where(kpos <= t4, s, NEG_INF)
            mx = jnp.max(s, axis=0, keepdims=True)
        m_new = jnp.maximum(m, mx)
        p = jnp.exp2(s - m_new)
        vt = jnp.concatenate([vst_ref[0, 0, c * nsub + i] for i in range(nsub)], axis=1)
        acc = jnp.exp2(m - m_new) * acc + jnp.dot(vt, p.astype(BF16), preferred_element_type=F32)
        return m_new, acc

    def pair(c, carry):
        mx0, m, acc = carry
        mx1 = scores(c + 1, 1)
        m, acc = accumulate(c, 0, mx0, m, acc, False)
        mx0 = scores(c + 2, 0)
        m, acc = accumulate(c + 1, 1, mx1, m, acc, False)
        return mx0, m, acc

    n_full = qb // (SEL_TK // Q_TILE)
    n_quads = n_full // 4
    n_pairs = (n_full - 4 * n_quads) // 2
    carry = (scores(0, 0), jnp.full((1, nq), NEG_INF, F32), jnp.zeros((V_ROWS, nq), F32))
    carry = lax.fori_loop(0, n_quads, lambda i, cr: pair(4 * i + 2, pair(4 * i, cr)), carry)
    mx0, m_s, acc_s = lax.fori_loop(0, n_pairs, lambda i, cr: pair(4 * n_quads + 2 * i, cr), carry)
    c_last = 4 * n_quads + 2 * n_pairs

    def leftover_then_own():
        mx1 = scores(c_last + 1, 1)
        m1, acc1 = accumulate(c_last, 0, mx0, m_s, acc_s, False)
        return accumulate(c_last + 1, 1, mx1, m1, acc1, True)[1]

    def own_only():
        return accumulate(c_last, 0, mx0, m_s, acc_s, True)[1]

    acc_s = lax.cond(n_full % 2 == 1, leftover_then_own, own_only)
    o_sel = acc_s[0:HEAD_DIM, :] * (1.0 / acc_s[HEAD_DIM:HEAD_DIM + 1, :])

    gt = g_ref[0]

    def gate(br):
        return jnp.concatenate([gt[br * GQA + g:br * GQA + g + 1, :] for g in range(GQA)], axis=1)

    ot = gate(0) * o_cmp + gate(1) * o_sel + gate(2) * o_win
    stacked = jnp.concatenate([ot[:, g * Q_TILE:(g + 1) * Q_TILE] for g in range(GQA)], axis=0)
    o_ref[0] = stacked.T.astype(BF16)


def _attn_call(qt, gt, kcmp, vct, ksa, vst, kw, vwt):
    bsz, _, s = qt.shape
    n_sel = s // SEL_BLOCK
    n_top = min(N_SEL, n_sel)
    n_cmp = kcmp.shape[1]
    ka = ksa.shape[2]
    nblk = s // LANES
    grid = (bsz, N_KV, s // Q_TILE)
    kern = functools.partial(_attn_kernel, n_sel, n_top)
    return pl.pallas_call(
        kern,
        grid=grid,
        in_specs=[
            pl.BlockSpec((1, GQA * HEAD_DIM, Q_TILE), lambda b, h, i: (b, h, i)),
            pl.BlockSpec((1, GATE_ROWS, Q_TILE), lambda b, h, i: (b, h, i)),
            pl.BlockSpec((1, n_cmp, N_KV * HEAD_DIM), lambda b, h, i: (b, 0, 0)),
            pl.BlockSpec((1, HEAD_DIM, n_cmp), lambda b, h, i: (b, h, 0)),
            pl.BlockSpec((1, s, ka), lambda b, h, i: (b, 0, 0)),
            pl.BlockSpec((1, 1, nblk, V_ROWS, LANES), lambda b, h, i: (b, h, 0, 0, 0)),
            pl.BlockSpec((1, s + WINDOW, N_KV * HEAD_DIM), lambda b, h, i: (b, 0, 0)),
            pl.BlockSpec((1, 1, nblk + WINDOW // LANES, V_ROWS, LANES), lambda b, h, i: (b, h, 0, 0, 0)),
        ],
        out_specs=pl.BlockSpec((1, Q_TILE, GQA * HEAD_DIM), lambda b, h, i: (b, i, h)),
        out_shape=jax.ShapeDtypeStruct((bsz, s, N_HEADS * HEAD_DIM), BF16),
        scratch_shapes=[pltpu.VMEM((ka, GQA * Q_TILE), BF16),
                        pltpu.VMEM((Q_TILE // LANES, n_cmp + SUBLANES, LANES), F32),
                        pltpu.VMEM((HEAD_DIM, GQA * Q_TILE), F32),
                        pltpu.VMEM((2, SEL_TK, GQA * Q_TILE), F32)],
        compiler_params=_cparams(("arbitrary", "arbitrary", "arbitrary")),
        name="nsa_attention",
    )(qt, gt, kcmp, vct, ksa, vst, kw, vwt)


def _post_kernel(x_ref, yc_ref, yn_ref, wo_ref, g1_ref, b1_ref, wr_ref, br_ref, tril_ref,
                 h1_ref, hp_ref, ri_ref, rw_ref, cnt_ref):
    step = pl.program_id(0)
    half = yc_ref.shape[1]
    sub = tril_ref.shape[0]

    @pl.when(step == 0)
    def _():
        cnt_ref[...] = jnp.zeros_like(cnt_ref)

    def route_rows(r0, cnt):
        rows = slice(r0, r0 + sub)
        mix = (jnp.dot(yc_ref[rows, :], wo_ref[0:half, :], preferred_element_type=F32)
               + jnp.dot(yn_ref[rows, :], wo_ref[half:2 * half, :], preferred_element_type=F32))
        h1 = _layer_norm(ALPHA * x_ref[rows, :] + mix, g1_ref[...], b1_ref[...])
        h1_ref[rows, :] = h1

        h_hi = h1.astype(BF16)
        h_hi32 = h_hi.astype(F32)
        h_lo = (h1 - h_hi32).astype(BF16)
        hp_ref[rows, :] = _pack_bf16_pairs(h1)
        parts = jnp.dot(jnp.concatenate([h_hi, h_lo], axis=1), wr_ref[...], preferred_element_type=F32)
        logits = parts[:, 0:LANES] + parts[:, LANES:2 * LANES] + br_ref[...]
        lt = logits.T[0:ROUTE_ROWS, :]
        rid = lax.broadcasted_iota(jnp.int32, (ROUTE_ROWS, sub), 0)
        big = jnp.int32(ROUTE_ROWS)

        def masked_softmax(valid):
            lg = jnp.where(valid, lt, NEG_INF)
            mx = jnp.max(lg, axis=0, keepdims=True)
            ex = jnp.where(valid, jnp.exp(lg - mx), 0.0)
            return ex / jnp.sum(ex, axis=0, keepdims=True)

        def first_max(vals, valid):
            top = jnp.max(jnp.where(valid, vals, -1.0), axis=0, keepdims=True)
            idx = jnp.min(jnp.where(valid & (vals == top), rid, big), axis=0, keepdims=True)
            return top, idx

        is_group = rid < N_GROUPS
        gp, gsel = first_max(masked_softmax(is_group), is_group)
        lo = ROUTE_LANE0 + EXP_PER_GROUP * gsel
        in_group = (rid >= lo) & (rid < lo + EXP_PER_GROUP)
        eprob = masked_softmax(in_group)
        p1, i1 = first_max(eprob, in_group)
        p2, i2 = first_max(eprob, in_group & (rid != i1))
        den = p1 + p2
        w1 = gp * (p1 / den)
        w2 = gp * (p2 / den)

        onehot = (rid == i1) | (rid == i2)
        before = jnp.dot(onehot.astype(BF16), tril_ref[...], preferred_element_type=F32)
        rk = before + cnt
        r1 = jnp.sum(jnp.where(rid == i1, rk, 0.0), axis=0, keepdims=True)
        r2 = jnp.sum(jnp.where(rid == i2, rk, 0.0), axis=0, keepdims=True)

        rec = lax.broadcasted_iota(jnp.int32, (ROUTE_COLS, sub), 0)
        ri_ref[:, rows] = jnp.where(rec == 0, i1 - ROUTE_LANE0,
                                    jnp.where(rec == 1, i2 - ROUTE_LANE0,
                                              jnp.where(rec == 2, r1.astype(jnp.int32),
                                                        jnp.where(rec == 3, r2.astype(jnp.int32), 0))))
        rw_ref[:, rows] = jnp.where(rec == 0, w1, jnp.where(rec == 1, w2, 0.0))
        return cnt + jnp.sum(onehot.astype(F32), axis=1, keepdims=True)

    cnt = cnt_ref[:, 0:1]
    for r0 in range(0, x_ref.shape[0], sub):
        cnt = route_rows(r0, cnt)
    cnt_ref[...] = jnp.broadcast_to(cnt, cnt_ref.shape)


def _post_call(x2, yconv, ynsa, wo, g1, b1, wr, br, tril):
    n_tok, d = x2.shape
    tm = POST_TM
    const = lambda a: pl.BlockSpec(a.shape, lambda i: (0, 0))
    return pl.pallas_call(
        _post_kernel,
        grid=(n_tok // tm,),
        in_specs=[
            pl.BlockSpec((tm, d), lambda i: (i, 0)),
            pl.BlockSpec((tm, yconv.shape[1]), lambda i: (i, 0)),
            pl.BlockSpec((tm, ynsa.shape[1]), lambda i: (i, 0)),
            const(wo), const(g1), const(b1), const(wr), const(br), const(tril),
        ],
        out_specs=(
            pl.BlockSpec((tm, d), lambda i: (i, 0)),
            pl.BlockSpec((tm, d // 2), lambda i: (i, 0)),
            pl.BlockSpec((ROUTE_COLS, tm), lambda i: (0, i)),
            pl.BlockSpec((ROUTE_COLS, tm), lambda i: (0, i)),
            pl.BlockSpec((ROUTE_ROWS, LANES), lambda i: (0, 0)),
        ),
        out_shape=(
            jax.ShapeDtypeStruct((n_tok, d), F32),
            jax.ShapeDtypeStruct((n_tok, d // 2), jnp.uint32),
            jax.ShapeDtypeStruct((ROUTE_COLS, n_tok), jnp.int32),
            jax.ShapeDtypeStruct((ROUTE_COLS, n_tok), F32),
            jax.ShapeDtypeStruct((ROUTE_ROWS, LANES), F32),
        ),
        compiler_params=_cparams(("arbitrary",)),
        name="out_proj_ln1_router",
    )(x2, yconv, ynsa, wo, g1, b1, wr, br, tril)


def _row_copy(src_ref, src_row, dst_ref, dst_row, sem):
    return pltpu.make_async_copy(src_ref.at[pl.ds(src_row, 1), :], dst_ref.at[pl.ds(dst_row, 1), :], sem)


def _sorted_row(pstart_ref, route_ref, r, slot):
    return pstart_ref[route_ref[slot * ROW_TM + r]] + route_ref[(2 + slot) * ROW_TM + r]


def _dispatch_kernel(pends_ref, nu_ref, pstart_ref, route_ref, h_ref, xs_ref, zero_ref, sem, zsem):
    tm = h_ref.shape[0]
    n_chunk = xs_ref.shape[0] // MOE_CHUNK

    @pl.when(pl.program_id(0) == 0)
    def _():
        zero_ref[...] = jnp.zeros_like(zero_ref)

        def zero_chunk(row0):
            row0 = pl.multiple_of(row0, MOE_CHUNK)
            return pltpu.make_async_copy(zero_ref, xs_ref.at[pl.ds(row0, MOE_CHUNK), :], zsem)

        def each_padded_chunk(act):
            def per_expert(e, _):
                end = pends_ref[e]
                start = jnp.where(e > 0, pends_ref[jnp.maximum(e - 1, 0)], 0)

                @pl.when(end > start)
                def _():
                    act(zero_chunk(end - MOE_CHUNK))
                return 0

            lax.fori_loop(0, N_EXPERTS, per_expert, 0)

            def per_dead_chunk(c, _):
                act(zero_chunk(c * MOE_CHUNK))
                return 0

            lax.fori_loop(nu_ref[0], n_chunk, per_dead_chunk, 0)

        each_padded_chunk(lambda cp: cp.start())
        each_padded_chunk(lambda cp: cp.wait())

    def issue(r, _):
        for slot in range(2):
            _row_copy(h_ref, r, xs_ref, _sorted_row(pstart_ref, route_ref, r, slot), sem).start()
        return 0

    lax.fori_loop(0, tm, issue, 0, unroll=ROW_UNROLL)
    for slot in range(2):
        pltpu.make_async_copy(h_ref, xs_ref.at[pl.ds(0, tm), :], sem).wait()


def _dispatch_call(pends, n_used, pstarts, route, h1, n_rows):
    n_tok, d = h1.shape
    tm = ROW_TM
    grid_spec = pltpu.PrefetchScalarGridSpec(
        num_scalar_prefetch=3,
        grid=(n_tok // tm,),
        in_specs=[
            pl.BlockSpec((ROUTE_COLS * tm,), lambda i, pe, nu, ps: (i,), memory_space=pltpu.SMEM),
            pl.BlockSpec((tm, d), lambda i, pe, nu, ps: (i, 0)),
        ],
        out_specs=pl.BlockSpec(memory_space=pl.ANY),
        scratch_shapes=[pltpu.VMEM((MOE_CHUNK, d), h1.dtype), pltpu.SemaphoreType.DMA, pltpu.SemaphoreType.DMA],
    )
    return pl.pallas_call(
        _dispatch_kernel,
        grid_spec=grid_spec,
        out_shape=jax.ShapeDtypeStruct((n_rows, d), h1.dtype),
        compiler_params=_cparams(("arbitrary",)),
        name="moe_dispatch",
    )(pends, n_used, pstarts, route, h1)


def _expert_kernel(ce_ref, nu_ref, xs_ref, wi_ref, wo_ref, o_ref, wib_ref, wob_ref):
    c = pl.program_id(0)
    live = c < nu_ref[0]

    @pl.when(live & ((c == 0) | (ce_ref[c] != ce_ref[jnp.maximum(c - 1, 0)])))
    def _():
        wib_ref[...] = wi_ref[0].astype(BF16)
        wob_ref[...] = wo_ref[0].astype(BF16)

    @pl.when(live)
    def _():
        x_lo, x_hi = _unpack_bf16_pairs(xs_ref[...])
        xb = jnp.concatenate([x_lo.astype(BF16), x_hi.astype(BF16)], axis=1)
        gu = jnp.dot(xb, wib_ref[...], preferred_element_type=F32)
        gate = gu[:, 0:D_EXPERT]
        act = gate * jax.nn.sigmoid(gate) * gu[:, D_EXPERT:2 * D_EXPERT]
        o_ref[...] = jnp.dot(act.astype(BF16), wob_ref[...], preferred_element_type=F32)

    @pl.when(jnp.logical_not(live))
    def _():
        o_ref[...] = jnp.zeros_like(o_ref)


def _expert_call(chunk_exp, n_used, xs, wi, wo):
    n_rows, dh = xs.shape
    d = wo.shape[2]
    n_chunk = n_rows // MOE_CHUNK

    def live(c, nu):
        return jnp.maximum(jnp.minimum(c, nu[0] - 1), 0)

    grid_spec = pltpu.PrefetchScalarGridSpec(
        num_scalar_prefetch=2,
        grid=(n_chunk,),
        in_specs=[
            pl.BlockSpec((MOE_CHUNK, dh), lambda c, ce, nu: (live(c, nu), 0)),
            pl.BlockSpec((1,) + wi.shape[1:], lambda c, ce, nu: (ce[live(c, nu)], 0, 0)),
            pl.BlockSpec((1,) + wo.shape[1:], lambda c, ce, nu: (ce[live(c, nu)], 0, 0)),
        ],
        out_specs=pl.BlockSpec((MOE_CHUNK, d), lambda c, ce, nu: (c, 0)),
        scratch_shapes=[pltpu.VMEM(wi.shape[1:], BF16), pltpu.VMEM(wo.shape[1:], BF16)],
    )
    return pl.pallas_call(
        _expert_kernel,
        grid_spec=grid_spec,
        out_shape=jax.ShapeDtypeStruct((n_rows, d), F32),
        compiler_params=_cparams(("arbitrary",)),
        name="moe_experts",
    )(chunk_exp, n_used, xs, wi, wo)


def _combine_kernel(pstart_ref, dest_ref, dest_next_ref, h1_ref, rw_ref, p_ref, ys_ref,
                    g2_ref, b2_ref, wp_ref, wg_ref, bg_ref, g3_ref, b3_ref,
                    o_ref, rows_ref, sems):
    tm = h1_ref.shape[0]
    step = pl.program_id(0)
    cur = step % 2

    def gather(idx_ref, buf):
        def issue(r, _):
            for slot in range(2):
                _row_copy(ys_ref, _sorted_row(pstart_ref, idx_ref, r, slot), rows_ref.at[buf, slot], r,
                          sems.at[buf]).start()
            return 0

        lax.fori_loop(0, tm, issue, 0, unroll=ROW_UNROLL)

    def await_rows(buf):
        for slot in range(2):
            pltpu.make_async_copy(ys_ref.at[pl.ds(0, tm), :], rows_ref.at[buf, slot], sems.at[buf]).wait()

    @pl.when(step == 0)
    def _():
        gather(dest_ref, 0)

    await_rows(cur)
    rw = rw_ref[...]
    ffn = rw[:, 0:1] * rows_ref[cur, 0] + rw[:, 1:2] * rows_ref[cur, 1]
    gather(dest_next_ref, 1 - cur)
    h2 =_layer_norm(ALPHA * h1_ref[...] + ffn, g2_ref[...], b2_ref[...])
    emb = jnp.dot(p_ref[...].astype(BF16), wp_ref[...], preferred_element_type=F32)
    gate = jax.nn.sigmoid(jnp.dot(h2.astype(BF16), wg_ref[...], preferred_element_type=F32) + bg_ref[...])
    o_ref[...] = _layer_norm(ALPHA * h2 + emb * gate, g3_ref[...], b3_ref[...])

    @pl.when(step == pl.num_programs(0) - 1)
    def _():
        await_rows(1 - cur)


def _combine_call(pstarts, route, h1, rw, p2, ys, g2, b2, wp, wg, bg, g3, b3):
    n_tok, d = h1.shape
    tm = ROW_TM
    const = lambda a: pl.BlockSpec(a.shape, lambda i: (0, 0))
    return pl.pallas_call(
        _combine_kernel,
        grid=(n_tok // tm,),
        in_specs=[
            pl.BlockSpec(memory_space=pltpu.SMEM),
            pl.BlockSpec((ROUTE_COLS * tm,), lambda i: (i,), memory_space=pltpu.SMEM),
            pl.BlockSpec((ROUTE_COLS * tm,), lambda i: (jnp.minimum(i + 1, n_tok // tm - 1),),
                         memory_space=pltpu.SMEM),
            pl.BlockSpec((tm, d), lambda i: (i, 0)),
            pl.BlockSpec((tm, rw.shape[1]), lambda i: (i, 0)),
            pl.BlockSpec((tm, p2.shape[1]), lambda i: (i, 0)),
            pl.BlockSpec(memory_space=pl.ANY),
            const(g2), const(b2), const(wp), const(wg), const(bg), const(g3), const(b3),
        ],
        out_specs=pl.BlockSpec((tm, d), lambda i: (i, 0)),
        out_shape=jax.ShapeDtypeStruct((n_tok, d), F32),
        scratch_shapes=[pltpu.VMEM((2, 2, tm, d), F32), pltpu.SemaphoreType.DMA((2,))],
        compiler_params=_cparams(("arbitrary",)),
        name="moe_combine_ln_ple",
    )(pstarts, route, route, h1, rw, p2, ys, g2, b2, wp, wg, bg, g3, b3)


def _gate_columns():
    cols = np.zeros((N_KV, GATE_ROWS), np.int32)
    live = np.zeros((N_KV, GATE_ROWS), np.float32)
    for h in range(N_KV):
        for br in range(3):
            for g in range(GQA):
                cols[h, br * GQA + g] = (h * GQA + g) * 3 + br
                live[h, br * GQA + g] = 1.0
    return cols.reshape(-1), live.reshape(-1)


def _layer(x, p, w_in, w_conv, pe_ck, w1_ck, w2_ck, pe_cv, w1_cv, w2_cv, w_out, ln1_g, ln1_b,
           w_rg, b_rg, w_re, b_re, w_e_in, w_e_out, ln2_g, ln2_b, w_ple, w_ple_gate, b_ple_gate,
           ln3_g, ln3_b):
    bsz, s, d = x.shape
    n_tok = bsz * s
    row = lambda v: v.reshape(1, -1)

    c_q, c_kc, c_vc, c_ks, c_vs, c_kw, c_vw, c_g = 1536, 2048, 2176, 2304, 2432, 2560, 2688, 2816
    wn = jnp.concatenate([w_in[:, 0:c_q], w_in[:, c_kc:c_ks], w_in[:, c_ks:c_vs], w_in[:, c_kw:c_vw]],
                         axis=1).astype(BF16)
    gcols, glive = _gate_columns()
    w_gate = w_in[:, c_g:c_g + 3 * N_HEADS][:, gcols] * glive[None, :]
    wt = jnp.concatenate([w_in[:, c_q:c_kc] * Q_SCALE, w_in[:, c_vs:c_kw], w_in[:, c_vw:c_g], w_gate],
                         axis=1).T.astype(BF16)

    yconv, cmpkv, ksa, kw, qt, vst, vwt, gt = _proj_call(x, wn, wt, w_conv)

    half_blk = CMP_BLOCK // 2
    eye = jnp.eye(N_KV, dtype=F32)

    def per_head_w1(w1):
        w = w1.reshape(2, half_blk, HEAD_DIM, CMP_HIDDEN)
        w = jnp.einsum('ptdc,hg->pthdgc', w, eye)
        return w.reshape(2 * half_blk * N_KV * HEAD_DIM, N_KV * CMP_HIDDEN)

    def per_head_w2(w2):
        return jnp.einsum('cd,hg->hcgd', w2, eye).reshape(N_KV * CMP_HIDDEN, N_KV * HEAD_DIM)

    def per_head_pe(pe):
        return jnp.broadcast_to(pe.reshape(2, half_blk, 1, HEAD_DIM),
                                (2, half_blk, N_KV, HEAD_DIM)).reshape(2, half_blk * N_KV * HEAD_DIM)

    pes = jnp.stack([per_head_pe(pe_ck), per_head_pe(pe_cv)])
    w1s = jnp.stack([per_head_w1(w1_ck), per_head_w1(w1_cv)]).astype(BF16)
    w2s = jnp.stack([per_head_w2(w2_ck), per_head_w2(w2_cv)]).astype(BF16)
    cmp_out = _compress_call(cmpkv, pes, w1s, w2s)
    kcmp = cmp_out[:, 0].astype(BF16)
    vct = cmp_out[:, 1].transpose(0, 2, 1).astype(BF16)

    kw = jnp.pad(kw, ((0, 0), (WINDOW, 0), (0, 0)))
    vwt = jnp.pad(vwt, ((0, 0), (0, 0), (WINDOW // LANES, 0), (0, 0), (0, 0)))
    ynsa = _attn_call(qt, gt, kcmp, vct, ksa, vst, kw, vwt)

    wr = jnp.zeros((d, LANES), F32).at[:, 0:N_GROUPS].set(w_rg)
    wr = wr.at[:, ROUTE_LANE0:ROUTE_LANE0 + N_EXPERTS].set(w_re)
    br = jnp.zeros((1, LANES), F32).at[0, 0:N_GROUPS].set(b_rg)
    br = br.at[0, ROUTE_LANE0:ROUTE_LANE0 + N_EXPERTS].set(b_re)
    wr_hi = wr.astype(BF16)
    wr_lo = (wr - wr_hi.astype(F32)).astype(BF16)
    wr = jnp.concatenate([jnp.concatenate([wr_hi, wr_lo], axis=1),
                          jnp.concatenate([wr_hi, jnp.zeros_like(wr_lo)], axis=1)], axis=0)
    earlier = jnp.asarray(np.triu(np.ones((POST_TM, POST_TM), np.float32), 1), BF16)
    h1, h1_packed, ri, rw, cnt = _post_call(x.reshape(n_tok, d), yconv.reshape(n_tok, -1), ynsa.reshape(n_tok, -1),
                                            w_out.astype(BF16), row(ln1_g), row(ln1_b), wr, br, earlier)
    rw = rw.T

    counts = cnt[ROUTE_LANE0:ROUTE_LANE0 + N_EXPERTS, 0].astype(jnp.int32)
    pcounts = (counts + MOE_CHUNK - 1) // MOE_CHUNK * MOE_CHUNK
    pends = jnp.cumsum(pcounts)
    pstarts = (pends - pcounts).astype(jnp.int32)
    route = ri.reshape(ROUTE_COLS, n_tok // ROW_TM, ROW_TM).transpose(1, 0, 2).reshape(-1)
    n_asg = n_tok * 2
    n_chunk = -(-n_asg // MOE_CHUNK) + N_EXPERTS
    chunk_row0 = jnp.arange(n_chunk, dtype=jnp.int32) * MOE_CHUNK
    chunk_exp = jnp.minimum(jnp.sum(pends[None, :] <= chunk_row0[:, None], axis=1), N_EXPERTS - 1).astype(jnp.int32)
    n_used = (pends[-1:] // MOE_CHUNK).astype(jnp.int32)

    xs = _dispatch_call(pends.astype(jnp.int32), n_used, pstarts, route, h1_packed, n_chunk * MOE_CHUNK)
    ys = _expert_call(chunk_exp, n_used, xs, w_e_in, w_e_out)
    out = _combine_call(pstarts, route, h1, rw, p.reshape(n_tok, -1), ys, row(ln2_g), row(ln2_b),
                        w_ple.astype(BF16), w_ple_gate.astype(BF16), row(b_ple_gate), row(ln3_g), row(ln3_b))
    return out.reshape(bsz, s, d)


def kernel(x, p, w_in, w_conv, pe_ck, w1_ck, w2_ck, pe_cv, w1_cv, w2_cv, w_out, ln1_g, ln1_b, w_rg, b_rg, w_re, b_re, w_e_in, w_e_out, ln2_g, ln2_b, w_ple, w_ple_gate, b_ple_gate, ln3_g, ln3_b):
    assert w_in.shape[0] == DEPTH, "residual scaling ALPHA is derived from DEPTH"
    h = x
    for i in range(DEPTH):
        h = _layer(h, p[i], w_in[i], w_conv[i], pe_ck[i], w1_ck[i], w2_ck[i], pe_cv[i], w1_cv[i], w2_cv[i],
                   w_out[i], ln1_g[i], ln1_b[i], w_rg[i], b_rg[i], w_re[i], b_re[i], w_e_in[i], w_e_out[i],
                   ln2_g[i], ln2_b[i], w_ple[i], w_ple_gate[i], b_ple_gate[i], ln3_g[i], ln3_b[i])
    return h
```

```python
import functools

import jax
import jax.numpy as jnp
import numpy as np
from jax import lax
from jax.experimental import pallas as pl
from jax.experimental.pallas import tpu as pltpu

F32 = jnp.float32
BF16 = jnp.bfloat16

CONV_CH = 512
N_HEADS = 8
HEAD_DIM = 64
N_KV = 2
GQA = N_HEADS // N_KV
CMP_BLOCK = 32
CMP_STRIDE = 16
CMP_HIDDEN = 2 * HEAD_DIM
SEL_BLOCK = 64
N_SEL = 16
WINDOW = 512
ATTN_SCALE = HEAD_DIM ** -0.5
Q_SCALE = ATTN_SCALE * float(np.log2(np.e))
FORCE_SCORE = 1e4
NEG_INF = -1e30
N_GROUPS = 4
EXP_PER_GROUP = 8
N_EXPERTS = N_GROUPS * EXP_PER_GROUP
D_EXPERT = 512
MOE_CHUNK = 512
DEPTH = 1
ALPHA = (2 * DEPTH) ** 0.25
LN_EPS = 1e-5

LANES = 128
SUBLANES = 8
VMEM_LIMIT = 56 * 1024 * 1024

PROJ_TM = 512
Q_TILE = 256
SEL_TK = 512
CMP_CHUNK = 128
WIN_KEYS = WINDOW + Q_TILE
POST_TM = 512
ROW_TM = 512
ROW_UNROLL = True
ROUTE_LANE0 = N_GROUPS
ROUTE_ROWS = 48
ROUTE_COLS = SUBLANES
GATE_ROWS = 16
V_ROWS = HEAD_DIM + 16


def _cparams(sem, vmem=VMEM_LIMIT):
    return pltpu.CompilerParams(dimension_semantics=sem, vmem_limit_bytes=vmem)


def _pack_bf16_pairs(v):
    bits = lax.bitcast_convert_type(v.astype(BF16).astype(F32), jnp.uint32)
    dh = bits.shape[1] // 2
    return jnp.right_shift(bits[:, 0:dh], jnp.uint32(16)) | bits[:, dh:2 * dh]


def _unpack_bf16_pairs(words):
    lo = lax.bitcast_convert_type(jnp.left_shift(words, jnp.uint32(16)), F32)
    hi = lax.bitcast_convert_type(words & jnp.uint32(0xFFFF0000), F32)
    return lo, hi


def _layer_norm(v, g, b):
    mu = jnp.mean(v, axis=-1, keepdims=True)
    d = v - mu
    var = jnp.mean(d * d, axis=-1, keepdims=True)
    return d * lax.rsqrt(var + LN_EPS) * g + b


def _proj_kernel(x_ref, wn_ref, wt_ref, wc_ref,
                 yconv_ref, cmpkv_ref, ksa_ref, kw_ref, qt_ref, vst_ref, vwt_ref, gt_ref,
                 carry_ref):
    si = pl.program_id(1)
    tm = x_ref.shape[1]
    xb = x_ref[0].astype(BF16)
    zn = jnp.dot(xb, wn_ref[...], preferred_element_type=F32)
    zt = lax.dot_general(wt_ref[...], xb, (((1,), (1,)), ((), ())),
                         preferred_element_type=F32)

    cb = zn[:, 0:CONV_CH]
    u = zn[:, CONV_CH:2 * CONV_CH] * zn[:, 2 * CONV_CH:3 * CONV_CH]

    @pl.when(si == 0)
    def _():
        carry_ref[...] = jnp.zeros_like(carry_ref)

    prev = carry_ref[...]
    rows = lax.broadcasted_iota(jnp.int32, u.shape, 0)
    u1 = jnp.where(rows == 0, prev[7:8, :], pltpu.roll(u, 1, 0))
    u2 = jnp.where(rows == 0, prev[6:7, :],
                   jnp.where(rows == 1, prev[7:8, :], pltpu.roll(u, 2, 0)))
    w = wc_ref[...]
    yconv_ref[0] = (cb * (w[0:1, :] * u2 + w[1:2, :] * u1 + w[2:3, :] * u)).astype(BF16)
    carry_ref[...] = u[tm - SUBLANES:tm, :]

    cmpkv_ref[0, 0] = zn[:, 1536:1664]
    cmpkv_ref[0, 1] = zn[:, 1664:1792]
    kw_ref[0] = zn[:, 1920:2048].astype(BF16)
    ksa_ref[0, :, 0:LANES] = zn[:, 1792:1920].astype(BF16)
    n_hot = ksa_ref.shape[2] - LANES
    pos = si * tm + lax.broadcasted_iota(jnp.int32, (tm, n_hot), 0)
    hot = jnp.right_shift(pos, SEL_BLOCK.bit_length() - 1) == lax.broadcasted_iota(jnp.int32, (tm, n_hot), 1)
    ksa_ref[0, :, LANES:] = jnp.where(hot, 1.0, 0.0).astype(BF16)

    qt_ref[0] = zt[0:512, :].astype(BF16)
    ones = jnp.ones((V_ROWS - HEAD_DIM, LANES), BF16)
    for h in range(N_KV):
        for i in range(tm // LANES):
            cols = slice(i * LANES, (i + 1) * LANES)
            vst_ref[0, h, i, 0:HEAD_DIM, :] = zt[512 + h * 64:512 + (h + 1) * 64, cols].astype(BF16)
            vst_ref[0, h, i, HEAD_DIM:V_ROWS, :] = ones
            vwt_ref[0, h, i, 0:HEAD_DIM, :] = zt[640 + h * 64:640 + (h + 1) * 64, cols].astype(BF16)
            vwt_ref[0, h, i, HEAD_DIM:V_ROWS, :] = ones
    gt_ref[0] = jax.nn.sigmoid(zt[768:800, :])


def _proj_call(x, wn, wt, wc):
    bsz, s, d = x.shape
    tm = PROJ_TM
    nblk = s // LANES
    grid = (bsz, s // tm)
    ka = LANES + -(-(s // SEL_BLOCK) // LANES) * LANES
    out_shape = (
        jax.ShapeDtypeStruct((bsz, s, CONV_CH), BF16),
        jax.ShapeDtypeStruct((bsz, 2, s, LANES), F32),
        jax.ShapeDtypeStruct((bsz, s, ka), BF16),
        jax.ShapeDtypeStruct((bsz, s, LANES), BF16),
        jax.ShapeDtypeStruct((bsz, 512, s), BF16),
        jax.ShapeDtypeStruct((bsz, N_KV, nblk, V_ROWS, LANES), BF16),
        jax.ShapeDtypeStruct((bsz, N_KV, nblk, V_ROWS, LANES), BF16),
        jax.ShapeDtypeStruct((bsz, N_KV * GATE_ROWS, s), F32),
    )
    vspec = pl.BlockSpec((1, N_KV, tm // LANES, V_ROWS, LANES), lambda b, i: (b, 0, i, 0, 0))
    return pl.pallas_call(
        _proj_kernel,
        grid=grid,
        in_specs=[
            pl.BlockSpec((1, tm, d), lambda b, i: (b, i, 0)),
            pl.BlockSpec(wn.shape, lambda b, i: (0, 0)),
            pl.BlockSpec(wt.shape, lambda b, i: (0, 0)),
            pl.BlockSpec(wc.shape, lambda b, i: (0, 0)),
        ],
        out_specs=(
            pl.BlockSpec((1, tm, CONV_CH), lambda b, i: (b, i, 0)),
            pl.BlockSpec((1, 2, tm, LANES), lambda b, i: (b, 0, i, 0)),
            pl.BlockSpec((1, tm, ka), lambda b, i: (b, i, 0)),
            pl.BlockSpec((1, tm, LANES), lambda b, i: (b, i, 0)),
            pl.BlockSpec((1, 512, tm), lambda b, i: (b, 0, i)),
            vspec, vspec,
            pl.BlockSpec((1, N_KV * GATE_ROWS, tm), lambda b, i: (b, 0, i)),
        ),
        out_shape=out_shape,
        scratch_shapes=[pltpu.VMEM((SUBLANES, CONV_CH), F32)],
        compiler_params=_cparams(("arbitrary", "arbitrary")),
        name="in_proj_conv",
    )(x, wn, wt, wc)


def _compress_kernel(g_ref, pe_ref, w1_ref, w2_ref, o_ref):
    pe = pe_ref[0]
    n = o_ref.shape[2]
    width = g_ref.shape[3]
    half = CMP_STRIDE * width
    a_lo = jnp.zeros((n, w1_ref.shape[2]), F32)
    a_hi = jnp.zeros((n, w1_ref.shape[2]), F32)
    for t in range(CMP_STRIDE):
        cols = slice(t * width, (t + 1) * width)
        x = g_ref[0, 0, pl.ds(t, n, stride=CMP_STRIDE), :]
        a_lo = a_lo + jnp.dot((x + pe[0:1, cols]).astype(BF16), w1_ref[0, cols, :],
                              preferred_element_type=F32)
        a_hi = a_hi + jnp.dot((x + pe[1:2, cols]).astype(BF16), w1_ref[0, half + t * width:half + (t + 1) * width, :],
                              preferred_element_type=F32)
    hid = a_lo + pltpu.roll(a_hi, n - 1, 0)
    act = jax.nn.gelu(hid)
    o_ref[0, 0] = jnp.dot(act.astype(BF16), w2_ref[0], preferred_element_type=F32)


def _compress_call(ckv, pes, w1s, w2s):
    bsz, _, s, width = ckv.shape
    n = s // CMP_STRIDE
    half = CMP_STRIDE * width
    return pl.pallas_call(
        _compress_kernel,
        grid=(bsz, 2),
        in_specs=[
            pl.BlockSpec((1, 1, s, width), lambda b, k: (b, k, 0, 0)),
            pl.BlockSpec((1, 2, half), lambda b, k: (k, 0, 0)),
            pl.BlockSpec((1, 2 * half, N_KV * CMP_HIDDEN), lambda b, k: (k, 0, 0)),
            pl.BlockSpec((1, N_KV * CMP_HIDDEN, N_KV * HEAD_DIM), lambda b, k: (k, 0, 0)),
        ],
        out_specs=pl.BlockSpec((1, 1, n, N_KV * HEAD_DIM), lambda b, k: (b, k, 0, 0)),
        out_shape=jax.ShapeDtypeStruct((bsz, 2, n, N_KV * HEAD_DIM), F32),
        compiler_params=_cparams(("arbitrary", "arbitrary")),
        name="compress_mlp",
    )(ckv, pes, w1s, w2s)


def _attn_kernel(n_sel, n_top,
                 q_ref, g_ref, kc_ref, vct_ref, ksa_ref, vst_ref, kw_ref, vwt_ref,
                 o_ref, qa_ref, ps_ref, oc_ref, s_ref):
    qb = pl.program_id(2)
    q0 = qb * Q_TILE
    nq = GQA * Q_TILE
    blk = q_ref[0]
    q4 = jnp.concatenate([blk[g * HEAD_DIM:(g + 1) * HEAD_DIM, :] for g in range(GQA)], axis=1)
    lane = lax.broadcasted_iota(jnp.int32, (1, nq), 1)
    t4 = q0 + (lane & (Q_TILE - 1))
    kvh = pl.program_id(1)
    qa_ref[pl.ds(pl.multiple_of(kvh * HEAD_DIM, HEAD_DIM), HEAD_DIM), :] = q4
    qa_ref[pl.ds(pl.multiple_of((N_KV - 1 - kvh) * HEAD_DIM, HEAD_DIM), HEAD_DIM), :] = jnp.zeros_like(q4)
    q2 = qa_ref[0:N_KV * HEAD_DIM, :]

    n_cmp = kc_ref.shape[1]
    chunk = min(CMP_CHUNK, n_cmp)
    q_lane_tiles = Q_TILE // LANES
    for h in range(q_lane_tiles):
        ps_ref[h, 0:SUBLANES, :] = jnp.zeros((SUBLANES, LANES), F32)

    def cmp_branch(n):
        sc = jnp.dot(kc_ref[0, 0:n, :], q2, preferred_element_type=F32)
        last_c = jnp.right_shift(t4 - (CMP_BLOCK - 1), CMP_STRIDE.bit_length() - 1)
        scm = jnp.where(lax.broadcasted_iota(jnp.int32, (n, nq), 0) <= last_c, sc, NEG_INF)
        m_c = jnp.max(scm, axis=0, keepdims=True)
        e_c = jnp.exp2(scm - m_c)
        l_c = jnp.sum(e_c, axis=0, keepdims=True)
        p_c = e_c * jnp.where(last_c >= 0, 1.0 / l_c, 0.0)
        oc_ref[...] = jnp.dot(vct_ref[0, :, 0:n], p_c.astype(BF16), preferred_element_type=F32)
        psum = (p_c[:, 0:Q_TILE] + p_c[:, Q_TILE:2 * Q_TILE]
                + p_c[:, 2 * Q_TILE:3 * Q_TILE] + p_c[:, 3 * Q_TILE:4 * Q_TILE])
        for h in range(q_lane_tiles):
            ps_ref[h, SUBLANES:SUBLANES + n, :] = psum[:, h * LANES:(h + 1) * LANES]
            if n < n_cmp:
                ps_ref[h, SUBLANES + n:, :] = jnp.zeros((n_cmp - n, LANES), F32)

    last_visible = (q0 + Q_TILE - CMP_BLOCK) // CMP_STRIDE
    live_chunks = last_visible // chunk + 1
    for k in range(1, n_cmp // chunk + 1):
        pl.when(live_chunks == k)(functools.partial(cmp_branch, k * chunk))
    o_cmp = oc_ref[...]

    sw = jnp.dot(kw_ref[0, pl.ds(pl.multiple_of(q0, Q_TILE), WIN_KEYS), :], q2,
                 preferred_element_type=F32)
    ql = t4 - q0
    row = lax.broadcasted_iota(jnp.int32, (LANES, nq), 0)
    slabs = []
    for c in range(WIN_KEYS // LANES):
        slab = sw[c * LANES:(c + 1) * LANES, :]
        before_start = WINDOW - 1 - q0 - c * LANES
        if c * LANES < Q_TILE:
            slab = jnp.where(row > jnp.maximum(ql - c * LANES, before_start), slab, NEG_INF)
        elif (c + 1) * LANES > WINDOW:
            slab = jnp.where(row <= ql + (WINDOW - c * LANES), slab, NEG_INF)
        else:
            slab = jnp.where(row > before_start, slab, NEG_INF)
        slabs.append(slab)
    swm = jnp.concatenate(slabs, axis=0)
    m_w = jnp.max(swm, axis=0, keepdims=True)
    p_w = jnp.exp2(swm - m_w)
    wblk = q0 // LANES
    vwt = jnp.concatenate([vwt_ref[0, 0, wblk + i] for i in range(WIN_KEYS // LANES)], axis=1)
    acc_w = jnp.dot(vwt, p_w.astype(BF16), preferred_element_type=F32)
    o_win = acc_w[0:HEAD_DIM, :] * (1.0 / acc_w[HEAD_DIM:HEAD_DIM + 1, :])

    per_sel = SEL_BLOCK // CMP_STRIDE
    def importance(h):
        acc = ps_ref[h, pl.ds(SUBLANES - 1, n_sel, stride=per_sel), :]
        for off in range(CMP_BLOCK // CMP_STRIDE + per_sel - 2):
            acc = acc + ps_ref[h, pl.ds(SUBLANES + off, n_sel, stride=per_sel), :]
        return acc

    imp = jnp.concatenate([importance(h) for h in range(q_lane_tiles)], axis=1)

    jidx = lax.broadcasted_iota(jnp.int32, (n_sel, Q_TILE), 0)
    tq = q0 + lax.broadcasted_iota(jnp.int32, (n_sel, Q_TILE), 1)
    jt = jnp.right_shift(tq, SEL_BLOCK.bit_length() - 1)
    forced = (jidx == 0) | (jidx == jt) | (jidx == jt - 1)
    score = jnp.where(forced, FORCE_SCORE, imp)
    key = jnp.where(jidx > jt, -1, lax.bitcast_convert_type(score, jnp.int32))
    theta = jnp.zeros((1, Q_TILE), jnp.int32)
    for bit in range(30, -1, -1):
        cand = theta | (1 << bit)
        reach = jnp.sum((key >= cand).astype(jnp.int32), axis=0, keepdims=True)
        theta = jnp.where(reach >= n_top, cand, theta)
    above = key > theta
    tied = key == theta
    n_above = jnp.sum(above.astype(jnp.int32), axis=0, keepdims=True)
    lower = (lax.broadcasted_iota(jnp.int32, (n_sel, n_sel), 1)
             < lax.broadcasted_iota(jnp.int32, (n_sel, n_sel), 0)).astype(BF16)
    tied_before = jnp.dot(lower, tied.astype(BF16), preferred_element_type=F32)
    chosen = above | (tied & (tied_before < (n_top - n_above).astype(F32)))
    bias = jnp.where(chosen, 0.0, NEG_INF).astype(BF16)
    bias_row0 = N_KV * HEAD_DIM
    qa_ref[bias_row0:bias_row0 + n_sel, :] = jnp.concatenate([bias] * GQA, axis=1)
    if qa_ref.shape[0] > bias_row0 + n_sel:
        qa_ref[bias_row0 + n_sel:, :] = jnp.zeros((qa_ref.shape[0] - bias_row0 - n_sel, nq), BF16)

    nsub = SEL_TK // LANES

    def scores(c, buf):
        k0 = pl.multiple_of(c * SEL_TK, SEL_TK)
        s = jnp.dot(ksa_ref[0, pl.ds(k0, SEL_TK), :], qa_ref[...], preferred_element_type=F32)
        s_ref[buf] = s
        return jnp.max(s, axis=0, keepdims=True)

    def accumulate(c, buf, mx, m, acc, causal):
        s = s_ref[buf]
        if causal:
            kpos = c * SEL_TK + lax.broadcasted_iota(jnp.int32, (SEL_TK, nq), 0)
            s = jnp.where(kpos <= t4, s, NEG_INF)
            mx = jnp.max(s, axis=0, keepdims=True)
        m_new = jnp.maximum(m, mx)
        p = jnp.exp2(s - m_new)
        vt = jnp.concatenate([vst_ref[0, 0, c * nsub + i] for i in range(nsub)], axis=1)
        acc = jnp.exp2(m - m_new) * acc + jnp.dot(vt, p.astype(BF16), preferred_element_type=F32)
        return m_new, acc

    def pair(c, carry):
        mx0, m, acc = carry
        mx1 = scores(c + 1, 1)
        m, acc = accumulate(c, 0, mx0, m, acc, False)
        mx0 = scores(c + 2, 0)
        m, acc = accumulate(c + 1, 1, mx1, m, acc, False)
        return mx0, m, acc

    n_full = qb // (SEL_TK // Q_TILE)
    n_quads = n_full // 4
    n_pairs = (n_full - 4 * n_quads) // 2
    carry = (scores(0, 0), jnp.full((1, nq), NEG_INF, F32), jnp.zeros((V_ROWS, nq), F32))
    carry = lax.fori_loop(0, n_quads, lambda i, cr: pair(4 * i + 2, pair(4 * i, cr)), carry)
    mx0, m_s, acc_s = lax.fori_loop(0, n_pairs, lambda i, cr: pair(4 * n_quads + 2 * i, cr), carry)
    c_last = 4 * n_quads + 2 * n_pairs

    def leftover_then_own():
        mx1 = scores(c_last + 1, 1)
        m1, acc1 = accumulate(c_last, 0, mx0, m_s, acc_s, False)
        return accumulate(c_last + 1, 1, mx1, m1, acc1, True)[1]

    def own_only():
        return accumulate(c_last, 0, mx0, m_s, acc_s, True)[1]

    acc_s = lax.cond(n_full % 2 == 1, leftover_then_own, own_only)
    o_sel = acc_s[0:HEAD_DIM, :] * (1.0 / acc_s[HEAD_DIM:HEAD_DIM + 1, :])

    gt = g_ref[0]

    def gate(br):
        return jnp.concatenate([gt[br * GQA + g:br * GQA + g + 1, :] for g in range(GQA)], axis=1)

    ot = gate(0) * o_cmp + gate(1) * o_sel + gate(2) * o_win
    stacked = jnp.concatenate([ot[:, g * Q_TILE:(g + 1) * Q_TILE] for g in range(GQA)], axis=0)
    o_ref[0] = stacked.T.astype(BF16)


def _attn_call(qt, gt, kcmp, vct, ksa, vst, kw, vwt):
    bsz, _, s = qt.shape
    n_sel = s // SEL_BLOCK
    n_top = min(N_SEL, n_sel)
    n_cmp = kcmp.shape[1]
    ka = ksa.shape[2]
    nblk = s // LANES
    grid = (bsz, N_KV, s // Q_TILE)
    kern = functools.partial(_attn_kernel, n_sel, n_top)
    return pl.pallas_call(
        kern,
        grid=grid,
        in_specs=[
            pl.BlockSpec((1, GQA * HEAD_DIM, Q_TILE), lambda b, h, i: (b, h, i)),
            pl.BlockSpec((1, GATE_ROWS, Q_TILE), lambda b, h, i: (b, h, i)),
            pl.BlockSpec((1, n_cmp, N_KV * HEAD_DIM), lambda b, h, i: (b, 0, 0)),
            pl.BlockSpec((1, HEAD_DIM, n_cmp), lambda b, h, i: (b, h, 0)),
            pl.BlockSpec((1, s, ka), lambda b, h, i: (b, 0, 0)),
            pl.BlockSpec((1, 1, nblk, V_ROWS, LANES), lambda b, h, i: (b, h, 0, 0, 0)),
            pl.BlockSpec((1, s + WINDOW, N_KV * HEAD_DIM), lambda b, h, i: (b, 0, 0)),
            pl.BlockSpec((1, 1, nblk + WINDOW // LANES, V_ROWS, LANES), lambda b, h, i: (b, h, 0, 0, 0)),
        ],
        out_specs=pl.BlockSpec((1, Q_TILE, GQA * HEAD_DIM), lambda b, h, i: (b, i, h)),
        out_shape=jax.ShapeDtypeStruct((bsz, s, N_HEADS * HEAD_DIM), BF16),
        scratch_shapes=[pltpu.VMEM((ka, GQA * Q_TILE), BF16),
                        pltpu.VMEM((Q_TILE // LANES, n_cmp + SUBLANES, LANES), F32),
                        pltpu.VMEM((HEAD_DIM, GQA * Q_TILE), F32),
                        pltpu.VMEM((2, SEL_TK, GQA * Q_TILE), F32)],
        compiler_params=_cparams(("arbitrary", "arbitrary", "arbitrary")),
        name="nsa_attention",
    )(qt, gt, kcmp, vct, ksa, vst, kw, vwt)


def _post_kernel(x_ref, yc_ref, yn_ref, wo_ref, g1_ref, b1_ref, wr_ref, br_ref, tril_ref,
                 h1_ref, hp_ref, ri_ref, rw_ref, cnt_ref):
    step = pl.program_id(0)
    half = yc_ref.shape[1]
    sub = tril_ref.shape[0]

    @pl.when(step == 0)
    def _():
        cnt_ref[...] = jnp.zeros_like(cnt_ref)

    def route_rows(r0, cnt):
        rows = slice(r0, r0 + sub)
        mix = (jnp.dot(yc_ref[rows, :], wo_ref[0:half, :], preferred_element_type=F32)
               + jnp.dot(yn_ref[rows, :], wo_ref[half:2 * half, :], preferred_element_type=F32))
        h1 = _layer_norm(ALPHA * x_ref[rows, :] + mix, g1_ref[...], b1_ref[...])
        h1_ref[rows, :] = h1

        h_hi = h1.astype(BF16)
        h_hi32 = h_hi.astype(F32)
        h_lo = (h1 - h_hi32).astype(BF16)
        hp_ref[rows, :] = _pack_bf16_pairs(h1)
        parts = jnp.dot(jnp.concatenate([h_hi, h_lo], axis=1), wr_ref[...], preferred_element_type=F32)
        logits = parts[:, 0:LANES] + parts[:, LANES:2 * LANES] + br_ref[...]
        lt = logits.T[0:ROUTE_ROWS, :]
        rid = lax.broadcasted_iota(jnp.int32, (ROUTE_ROWS, sub), 0)
        big = jnp.int32(ROUTE_ROWS)

        def masked_softmax(valid):
            lg = jnp.where(valid, lt, NEG_INF)
            mx = jnp.max(lg, axis=0, keepdims=True)
            ex = jnp.where(valid, jnp.exp(lg - mx), 0.0)
            return ex / jnp.sum(ex, axis=0, keepdims=True)

        def first_max(vals, valid):
            top = jnp.max(jnp.where(valid, vals, -1.0), axis=0, keepdims=True)
            idx = jnp.min(jnp.where(valid & (vals == top), rid, big), axis=0, keepdims=True)
            return top, idx

        is_group = rid < N_GROUPS
        gp, gsel = first_max(masked_softmax(is_group), is_group)
        lo = ROUTE_LANE0 + EXP_PER_GROUP * gsel
        in_group = (rid >= lo) & (rid < lo + EXP_PER_GROUP)
        eprob = masked_softmax(in_group)
        p1, i1 = first_max(eprob, in_group)
        p2, i2 = first_max(eprob, in_group & (rid != i1))
        den = p1 + p2
        w1 = gp * (p1 / den)
        w2 = gp * (p2 / den)

        onehot = (rid == i1) | (rid == i2)
        before = jnp.dot(onehot.astype(BF16), tril_ref[...], preferred_element_type=F32)
        rk = before + cnt
        r1 = jnp.sum(jnp.where(rid == i1, rk, 0.0), axis=0, keepdims=True)
        r2 = jnp.sum(jnp.where(rid == i2, rk, 0.0), axis=0, keepdims=True)

        rec = lax.broadcasted_iota(jnp.int32, (ROUTE_COLS, sub), 0)
        ri_ref[:, rows] = jnp.where(rec == 0, i1 - ROUTE_LANE0,
                                    jnp.where(rec == 1, i2 - ROUTE_LANE0,
                                              jnp.where(rec == 2, r1.astype(jnp.int32),
                                                        jnp.where(rec == 3, r2.astype(jnp.int32), 0))))
        rw_ref[:, rows] = jnp.where(rec == 0, w1, jnp.where(rec == 1, w2, 0.0))
        return cnt + jnp.sum(onehot.astype(F32), axis=1, keepdims=True)

    cnt = cnt_ref[:, 0:1]
    for r0 in range(0, x_ref.shape[0], sub):
        cnt = route_rows(r0, cnt)
    cnt_ref[...] = jnp.broadcast_to(cnt, cnt_ref.shape)


def _post_call(x2, yconv, ynsa, wo, g1, b1, wr, br, tril):
    n_tok, d = x2.shape
    tm = POST_TM
    const = lambda a: pl.BlockSpec(a.shape, lambda i: (0, 0))
    return pl.pallas_call(
        _post_kernel,
        grid=(n_tok // tm,),
        in_specs=[
            pl.BlockSpec((tm, d), lambda i: (i, 0)),
            pl.BlockSpec((tm, yconv.shape[1]), lambda i: (i, 0)),
            pl.BlockSpec((tm, ynsa.shape[1]), lambda i: (i, 0)),
            const(wo), const(g1), const(b1), const(wr), const(br), const(tril),
        ],
        out_specs=(
            pl.BlockSpec((tm, d), lambda i: (i, 0)),
            pl.BlockSpec((tm, d // 2), lambda i: (i, 0)),
            pl.BlockSpec((ROUTE_COLS, tm), lambda i: (0, i)),
            pl.BlockSpec((ROUTE_COLS, tm), lambda i: (0, i)),
            pl.BlockSpec((ROUTE_ROWS, LANES), lambda i: (0, 0)),
        ),
        out_shape=(
            jax.ShapeDtypeStruct((n_tok, d), F32),
            jax.ShapeDtypeStruct((n_tok, d // 2), jnp.uint32),
            jax.ShapeDtypeStruct((ROUTE_COLS, n_tok), jnp.int32),
            jax.ShapeDtypeStruct((ROUTE_COLS, n_tok), F32),
            jax.ShapeDtypeStruct((ROUTE_ROWS, LANES), F32),
        ),
        compiler_params=_cparams(("arbitrary",)),
        name="out_proj_ln1_router",
    )(x2, yconv, ynsa, wo, g1, b1, wr, br, tril)


def _start_row_copy(src_ref, src_row, dst_ref, dst_row, sem, queue):
    pltpu.async_copy(src_ref.at[pl.ds(src_row, 1), :], dst_ref.at[pl.ds(dst_row, 1), :], sem, priority=queue)


def _sorted_row(pstart_ref, route_ref, r, slot):
    return pstart_ref[route_ref[slot * ROW_TM + r]] + route_ref[(2 + slot) * ROW_TM + r]


def _dispatch_kernel(pends_ref, nu_ref, pstart_ref, route_ref, h_ref, xs_ref, zero_ref, sem, zsem):
    tm = h_ref.shape[0]
    n_chunk = xs_ref.shape[0] // MOE_CHUNK

    @pl.when(pl.program_id(0) == 0)
    def _():
        zero_ref[...] = jnp.zeros_like(zero_ref)

        def zero_chunk(row0):
            row0 = pl.multiple_of(row0, MOE_CHUNK)
            return pltpu.make_async_copy(zero_ref, xs_ref.at[pl.ds(row0, MOE_CHUNK), :], zsem)

        def each_padded_chunk(act):
            def per_expert(e, _):
                end = pends_ref[e]
                start = jnp.where(e > 0, pends_ref[jnp.maximum(e - 1, 0)], 0)

                @pl.when(end > start)
                def _():
                    act(zero_chunk(end - MOE_CHUNK))
                return 0

            lax.fori_loop(0, N_EXPERTS, per_expert, 0)

            def per_dead_chunk(c, _):
                act(zero_chunk(c * MOE_CHUNK))
                return 0

            lax.fori_loop(nu_ref[0], n_chunk, per_dead_chunk, 0)

        each_padded_chunk(lambda cp: cp.start())
        each_padded_chunk(lambda cp: cp.wait())

    def issue(r, _):
        for slot in range(2):
            _start_row_copy(h_ref, r, xs_ref, _sorted_row(pstart_ref, route_ref, r, slot), sem, slot)
        return 0

    lax.fori_loop(0, tm, issue, 0, unroll=ROW_UNROLL)
    for slot in range(2):
        pltpu.make_async_copy(h_ref, xs_ref.at[pl.ds(0, tm), :], sem).wait()


def _dispatch_call(pends, n_used, pstarts, route, h1, n_rows):
    n_tok, d = h1.shape
    tm = ROW_TM
    grid_spec = pltpu.PrefetchScalarGridSpec(
        num_scalar_prefetch=3,
        grid=(n_tok // tm,),
        in_specs=[
            pl.BlockSpec((ROUTE_COLS * tm,), lambda i, pe, nu, ps: (i,), memory_space=pltpu.SMEM),
            pl.BlockSpec((tm, d), lambda i, pe, nu, ps: (i, 0)),
        ],
        out_specs=pl.BlockSpec(memory_space=pl.ANY),
        scratch_shapes=[pltpu.VMEM((MOE_CHUNK, d), h1.dtype), pltpu.SemaphoreType.DMA, pltpu.SemaphoreType.DMA],
    )
    return pl.pallas_call(
        _dispatch_kernel,
        grid_spec=grid_spec,
        out_shape=jax.ShapeDtypeStruct((n_rows, d), h1.dtype),
        compiler_params=_cparams(("arbitrary",)),
        name="moe_dispatch",
    )(pends, n_used, pstarts, route, h1)


def _expert_kernel(ce_ref, nu_ref, xs_ref, wi_ref, wo_ref, o_ref, wib_ref, wob_ref):
    c = pl.program_id(0)
    live = c < nu_ref[0]

    @pl.when(live & ((c == 0) | (ce_ref[c] != ce_ref[jnp.maximum(c - 1, 0)])))
    def _():
        wib_ref[...] = wi_ref[0].astype(BF16)
        wob_ref[...] = wo_ref[0].astype(BF16)

    @pl.when(live)
    def _():
        x_lo, x_hi = _unpack_bf16_pairs(xs_ref[...])
        xb = jnp.concatenate([x_lo.astype(BF16), x_hi.astype(BF16)], axis=1)
        gu = jnp.dot(xb, wib_ref[...], preferred_element_type=F32)
        gate = gu[:, 0:D_EXPERT]
        act = gate * jax.nn.sigmoid(gate) * gu[:, D_EXPERT:2 * D_EXPERT]
        o_ref[...] = jnp.dot(act.astype(BF16), wob_ref[...], preferred_element_type=F32)

    @pl.when(jnp.logical_not(live))
    def _():
        o_ref[...] = jnp.zeros_like(o_ref)


def _expert_call(chunk_exp, n_used, xs, wi, wo):
    n_rows, dh = xs.shape
    d = wo.shape[2]
    n_chunk = n_rows // MOE_CHUNK

    def live(c, nu):
        return jnp.maximum(jnp.minimum(c, nu[0] - 1), 0)

    grid_spec = pltpu.PrefetchScalarGridSpec(
        num_scalar_prefetch=2,
        grid=(n_chunk,),
        in_specs=[
            pl.BlockSpec((MOE_CHUNK, dh), lambda c, ce, nu: (live(c, nu), 0)),
            pl.BlockSpec((1,) + wi.shape[1:], lambda c, ce, nu: (ce[live(c, nu)], 0, 0)),
            pl.BlockSpec((1,) + wo.shape[1:], lambda c, ce, nu: (ce[live(c, nu)], 0, 0)),
        ],
        out_specs=pl.BlockSpec((MOE_CHUNK, d), lambda c, ce, nu: (c, 0)),
        scratch_shapes=[pltpu.VMEM(wi.shape[1:], BF16), pltpu.VMEM(wo.shape[1:], BF16)],
    )
    return pl.pallas_call(
        _expert_kernel,
        grid_spec=grid_spec,
        out_shape=jax.ShapeDtypeStruct((n_rows, d), F32),
        compiler_params=_cparams(("arbitrary",)),
        name="moe_experts",
    )(chunk_exp, n_used, xs, wi, wo)


def _combine_kernel(pstart_ref, dest_ref, dest_next_ref, h1_ref, rw_ref, p_ref, ys_ref,
                    g2_ref, b2_ref, wp_ref, wg_ref, bg_ref, g3_ref, b3_ref,
                    o_ref, rows_ref, sems):
    tm = h1_ref.shape[0]
    step = pl.program_id(0)
    cur = step % 2

    def gather(idx_ref, buf):
        def issue(r, _):
            for slot in range(2):
                _start_row_copy(ys_ref, _sorted_row(pstart_ref, idx_ref, r, slot), rows_ref.at[buf, slot], r,
                                sems.at[buf], slot)
            return 0

        lax.fori_loop(0, tm, issue, 0, unroll=ROW_UNROLL)

    def await_rows(buf):
        for slot in range(2):
            pltpu.make_async_copy(ys_ref.at[pl.ds(0, tm), :], rows_ref.at[buf, slot], sems.at[buf]).wait()

    @pl.when(step == 0)
    def _():
        gather(dest_ref, 0)

    await_rows(cur)
    rw = rw_ref[...]
    ffn = rw[:, 0:1] * rows_ref[cur, 0] + rw[:, 1:2] * rows_ref[cur, 1]
    gather(dest_next_ref, 1 - cur)
    h2 =_layer_norm(ALPHA * h1_ref[...] + ffn, g2_ref[...], b2_ref[...])
    emb = jnp.dot(p_ref[...].astype(BF16), wp_ref[...], preferred_element_type=F32)
    gate = jax.nn.sigmoid(jnp.dot(h2.astype(BF16), wg_ref[...], preferred_element_type=F32) + bg_ref[...])
    o_ref[...] = _layer_norm(ALPHA * h2 + emb * gate, g3_ref[...], b3_ref[...])

    @pl.when(step == pl.num_programs(0) - 1)
    def _():
        await_rows(1 - cur)


def _combine_call(pstarts, route, h1, rw, p2, ys, g2, b2, wp, wg, bg, g3, b3):
    n_tok, d = h1.shape
    tm = ROW_TM
    const = lambda a: pl.BlockSpec(a.shape, lambda i: (0, 0))
    return pl.pallas_call(
        _combine_kernel,
        grid=(n_tok // tm,),
        in_specs=[
            pl.BlockSpec(memory_space=pltpu.SMEM),
            pl.BlockSpec((ROUTE_COLS * tm,), lambda i: (i,), memory_space=pltpu.SMEM),
            pl.BlockSpec((ROUTE_COLS * tm,), lambda i: (jnp.minimum(i + 1, n_tok // tm - 1),),
                         memory_space=pltpu.SMEM),
            pl.BlockSpec((tm, d), lambda i: (i, 0)),
            pl.BlockSpec((tm, rw.shape[1]), lambda i: (i, 0)),
            pl.BlockSpec((tm, p2.shape[1]), lambda i: (i, 0)),
            pl.BlockSpec(memory_space=pl.ANY),
            const(g2), const(b2), const(wp), const(wg), const(bg), const(g3), const(b3),
        ],
        out_specs=pl.BlockSpec((tm, d), lambda i: (i, 0)),
        out_shape=jax.ShapeDtypeStruct((n_tok, d), F32),
        scratch_shapes=[pltpu.VMEM((2, 2, tm, d), F32), pltpu.SemaphoreType.DMA((2,))],
        compiler_params=_cparams(("arbitrary",)),
        name="moe_combine_ln_ple",
    )(pstarts, route, route, h1, rw, p2, ys, g2, b2, wp, wg, bg, g3, b3)


def _gate_columns():
    cols = np.zeros((N_KV, GATE_ROWS), np.int32)
    live = np.zeros((N_KV, GATE_ROWS), np.float32)
    for h in range(N_KV):
        for br in range(3):
            for g in range(GQA):
                cols[h, br * GQA + g] = (h * GQA + g) * 3 + br
                live[h, br * GQA + g] = 1.0
    return cols.reshape(-1), live.reshape(-1)


def _layer(x, p, w_in, w_conv, pe_ck, w1_ck, w2_ck, pe_cv, w1_cv, w2_cv, w_out, ln1_g, ln1_b,
           w_rg, b_rg, w_re, b_re, w_e_in, w_e_out, ln2_g, ln2_b, w_ple, w_ple_gate, b_ple_gate,
           ln3_g, ln3_b):
    bsz, s, d = x.shape
    n_tok = bsz * s
    row = lambda v: v.reshape(1, -1)

    c_q, c_kc, c_vc, c_ks, c_vs, c_kw, c_vw, c_g = 1536, 2048, 2176, 2304, 2432, 2560, 2688, 2816
    wn = jnp.concatenate([w_in[:, 0:c_q], w_in[:, c_kc:c_ks], w_in[:, c_ks:c_vs], w_in[:, c_kw:c_vw]],
                         axis=1).astype(BF16)
    gcols, glive = _gate_columns()
    w_gate = w_in[:, c_g:c_g + 3 * N_HEADS][:, gcols] * glive[None, :]
    wt = jnp.concatenate([w_in[:, c_q:c_kc] * Q_SCALE, w_in[:, c_vs:c_kw], w_in[:, c_vw:c_g], w_gate],
                         axis=1).T.astype(BF16)

    yconv, cmpkv, ksa, kw, qt, vst, vwt, gt = _proj_call(x, wn, wt, w_conv)

    half_blk = CMP_BLOCK // 2
    eye = jnp.eye(N_KV, dtype=F32)

    def per_head_w1(w1):
        w = w1.reshape(2, half_blk, HEAD_DIM, CMP_HIDDEN)
        w = jnp.einsum('ptdc,hg->pthdgc', w, eye)
        return w.reshape(2 * half_blk * N_KV * HEAD_DIM, N_KV * CMP_HIDDEN)

    def per_head_w2(w2):
        return jnp.einsum('cd,hg->hcgd', w2, eye).reshape(N_KV * CMP_HIDDEN, N_KV * HEAD_DIM)

    def per_head_pe(pe):
        return jnp.broadcast_to(pe.reshape(2, half_blk, 1, HEAD_DIM),
                                (2, half_blk, N_KV, HEAD_DIM)).reshape(2, half_blk * N_KV * HEAD_DIM)

    pes = jnp.stack([per_head_pe(pe_ck), per_head_pe(pe_cv)])
    w1s = jnp.stack([per_head_w1(w1_ck), per_head_w1(w1_cv)]).astype(BF16)
    w2s = jnp.stack([per_head_w2(w2_ck), per_head_w2(w2_cv)]).astype(BF16)
    cmp_out = _compress_call(cmpkv, pes, w1s, w2s)
    kcmp = cmp_out[:, 0].astype(BF16)
    vct = cmp_out[:, 1].transpose(0, 2, 1).astype(BF16)

    kw = jnp.pad(kw, ((0, 0), (WINDOW, 0), (0, 0)))
    vwt = jnp.pad(vwt, ((0, 0), (0, 0), (WINDOW // LANES, 0), (0, 0), (0, 0)))
    ynsa = _attn_call(qt, gt, kcmp, vct, ksa, vst, kw, vwt)

    wr = jnp.zeros((d, LANES), F32).at[:, 0:N_GROUPS].set(w_rg)
    wr = wr.at[:, ROUTE_LANE0:ROUTE_LANE0 + N_EXPERTS].set(w_re)
    br = jnp.zeros((1, LANES), F32).at[0, 0:N_GROUPS].set(b_rg)
    br = br.at[0, ROUTE_LANE0:ROUTE_LANE0 + N_EXPERTS].set(b_re)
    wr_hi = wr.astype(BF16)
    wr_lo = (wr - wr_hi.astype(F32)).astype(BF16)
    wr = jnp.concatenate([jnp.concatenate([wr_hi, wr_lo], axis=1),
                          jnp.concatenate([wr_hi, jnp.zeros_like(wr_lo)], axis=1)], axis=0)
    earlier = jnp.asarray(np.triu(np.ones((POST_TM, POST_TM), np.float32), 1), BF16)
    h1, h1_packed, ri, rw, cnt = _post_call(x.reshape(n_tok, d), yconv.reshape(n_tok, -1), ynsa.reshape(n_tok, -1),
                                            w_out.astype(BF16), row(ln1_g), row(ln1_b), wr, br, earlier)
    rw = rw.T

    counts = cnt[ROUTE_LANE0:ROUTE_LANE0 + N_EXPERTS, 0].astype(jnp.int32)
    pcounts = (counts + MOE_CHUNK - 1) // MOE_CHUNK * MOE_CHUNK
    pends = jnp.cumsum(pcounts)
    pstarts = (pends - pcounts).astype(jnp.int32)
    route = ri.reshape(ROUTE_COLS, n_tok // ROW_TM, ROW_TM).transpose(1, 0, 2).reshape(-1)
    n_asg = n_tok * 2
    n_chunk = -(-n_asg // MOE_CHUNK) + N_EXPERTS
    chunk_row0 = jnp.arange(n_chunk, dtype=jnp.int32) * MOE_CHUNK
    chunk_exp = jnp.minimum(jnp.sum(pends[None, :] <= chunk_row0[:, None], axis=1), N_EXPERTS - 1).astype(jnp.int32)
    n_used = (pends[-1:] // MOE_CHUNK).astype(jnp.int32)

    xs = _dispatch_call(pends.astype(jnp.int32), n_used, pstarts, route, h1_packed, n_chunk * MOE_CHUNK)
    ys = _expert_call(chunk_exp, n_used, xs, w_e_in, w_e_out)
    out = _combine_call(pstarts, route, h1, rw, p.reshape(n_tok, -1), ys, row(ln2_g), row(ln2_b),
                        w_ple.astype(BF16), w_ple_gate.astype(BF16), row(b_ple_gate), row(ln3_g), row(ln3_b))
    return out.reshape(bsz, s, d)


def kernel(x, p, w_in, w_conv, pe_ck, w1_ck, w2_ck, pe_cv, w1_cv, w2_cv, w_out, ln1_g, ln1_b, w_rg, b_rg, w_re, b_re, w_e_in, w_e_out, ln2_g, ln2_b, w_ple, w_ple_gate, b_ple_gate, ln3_g, ln3_b):
    assert w_in.shape[0] == DEPTH, "residual scaling ALPHA is derived from DEPTH"
    h = x
    for i in range(DEPTH):
        h = _layer(h, p[i], w_in[i], w_conv[i], pe_ck[i], w1_ck[i], w2_ck[i], pe_cv[i], w1_cv[i], w2_cv[i],
                   w_out[i], ln1_g[i], ln1_b[i], w_rg[i], b_rg[i], w_re[i], b_re[i], w_e_in[i], w_e_out[i],
                   ln2_g[i], ln2_b[i], w_ple[i], w_ple_gate[i], b_ple_gate[i], ln3_g[i], ln3_b[i])
    return h
```

```python
import functools

import jax
import jax.numpy as jnp
import numpy as np
from jax import lax
from jax.experimental import pallas as pl
from jax.experimental.pallas import tpu as pltpu

F32 = jnp.float32
BF16 = jnp.bfloat16

CONV_CH = 512
N_HEADS = 8
HEAD_DIM = 64
N_KV = 2
GQA = N_HEADS // N_KV
CMP_BLOCK = 32
CMP_STRIDE = 16
CMP_HIDDEN = 2 * HEAD_DIM
SEL_BLOCK = 64
N_SEL = 16
WINDOW = 512
ATTN_SCALE = HEAD_DIM ** -0.5
Q_SCALE = ATTN_SCALE * float(np.log2(np.e))
FORCE_SCORE = 1e4
NEG_INF = -1e30
N_GROUPS = 4
EXP_PER_GROUP = 8
N_EXPERTS = N_GROUPS * EXP_PER_GROUP
D_EXPERT = 512
MOE_CHUNK = 512
DEPTH = 1
ALPHA = (2 * DEPTH) ** 0.25
LN_EPS = 1e-5

LANES = 128
SUBLANES = 8
VMEM_LIMIT = 56 * 1024 * 1024

PROJ_TM = 512
Q_TILE = 256
SEL_TK = 512
CMP_CHUNK = 128
WIN_KEYS = WINDOW + Q_TILE
POST_TM = 512
ROW_TM = 512
ROW_UNROLL = True
ROUTE_LANE0 = N_GROUPS
ROUTE_ROWS = 48
ROUTE_COLS = SUBLANES
GATE_ROWS = 16
V_ROWS = HEAD_DIM + 16


def _cparams(sem, vmem=VMEM_LIMIT):
    return pltpu.CompilerParams(dimension_semantics=sem, vmem_limit_bytes=vmem)


def _pack_bf16_pairs(v):
    bits = lax.bitcast_convert_type(v.astype(BF16).astype(F32), jnp.uint32)
    dh = bits.shape[1] // 2
    return jnp.right_shift(bits[:, 0:dh], jnp.uint32(16)) | bits[:, dh:2 * dh]


def _unpack_bf16_pairs(words):
    lo = lax.bitcast_convert_type(jnp.left_shift(words, jnp.uint32(16)), F32)
    hi = lax.bitcast_convert_type(words & jnp.uint32(0xFFFF0000), F32)
    return lo, hi


def _layer_norm(v, g, b):
    mu = jnp.mean(v, axis=-1, keepdims=True)
    d = v - mu
    var = jnp.mean(d * d, axis=-1, keepdims=True)
    return d * lax.rsqrt(var + LN_EPS) * g + b


def _proj_kernel(x_ref, wn_ref, wt_ref, wc_ref,
                 yconv_ref, cmpkv_ref, ksa_ref, kw_ref, qt_ref, vst_ref, vwt_ref, gt_ref,
                 carry_ref):
    si = pl.program_id(1)
    tm = x_ref.shape[1]
    xb = x_ref[0].astype(BF16)
    zn = jnp.dot(xb, wn_ref[...], preferred_element_type=F32)
    zt = lax.dot_general(wt_ref[...], xb, (((1,), (1,)), ((), ())),
                         preferred_element_type=F32)

    cb = zn[:, 0:CONV_CH]
    u = zn[:, CONV_CH:2 * CONV_CH] * zn[:, 2 * CONV_CH:3 * CONV_CH]

    @pl.when(si == 0)
    def _():
        carry_ref[...] = jnp.zeros_like(carry_ref)

    prev = carry_ref[...]
    rows = lax.broadcasted_iota(jnp.int32, u.shape, 0)
    u1 = jnp.where(rows == 0, prev[7:8, :], pltpu.roll(u, 1, 0))
    u2 = jnp.where(rows == 0, prev[6:7, :],
                   jnp.where(rows == 1, prev[7:8, :], pltpu.roll(u, 2, 0)))
    w = wc_ref[...]
    yconv_ref[0] = (cb * (w[0:1, :] * u2 + w[1:2, :] * u1 + w[2:3, :] * u)).astype(BF16)
    carry_ref[...] = u[tm - SUBLANES:tm, :]

    cmpkv_ref[0, 0] = zn[:, 1536:1664]
    cmpkv_ref[0, 1] = zn[:, 1664:1792]
    kw_ref[0] = zn[:, 1920:2048].astype(BF16)
    ksa_ref[0, :, 0:LANES] = zn[:, 1792:1920].astype(BF16)
    n_hot = ksa_ref.shape[2] - LANES
    pos = si * tm + lax.broadcasted_iota(jnp.int32, (tm, n_hot), 0)
    hot = jnp.right_shift(pos, SEL_BLOCK.bit_length() - 1) == lax.broadcasted_iota(jnp.int32, (tm, n_hot), 1)
    ksa_ref[0, :, LANES:] = jnp.where(hot, 1.0, 0.0).astype(BF16)

    qt_ref[0] = zt[0:512, :].astype(BF16)
    ones = jnp.ones((V_ROWS - HEAD_DIM, LANES), BF16)
    for h in range(N_KV):
        for i in range(tm // LANES):
            cols = slice(i * LANES, (i + 1) * LANES)
            vst_ref[0, h, i, 0:HEAD_DIM, :] = zt[512 + h * 64:512 + (h + 1) * 64, cols].astype(BF16)
            vst_ref[0, h, i, HEAD_DIM:V_ROWS, :] = ones
            vwt_ref[0, h, i, 0:HEAD_DIM, :] = zt[640 + h * 64:640 + (h + 1) * 64, cols].astype(BF16)
            vwt_ref[0, h, i, HEAD_DIM:V_ROWS, :] = ones
    gt_ref[0] = jax.nn.sigmoid(zt[768:800, :])


def _proj_call(x, wn, wt, wc):
    bsz, s, d = x.shape
    tm = PROJ_TM
    nblk = s // LANES
    grid = (bsz, s // tm)
    ka = LANES + -(-(s // SEL_BLOCK) // LANES) * LANES
    out_shape = (
        jax.ShapeDtypeStruct((bsz, s, CONV_CH), BF16),
        jax.ShapeDtypeStruct((bsz, 2, s, LANES), F32),
        jax.ShapeDtypeStruct((bsz, s, ka), BF16),
        jax.ShapeDtypeStruct((bsz, s, LANES), BF16),
        jax.ShapeDtypeStruct((bsz, 512, s), BF16),
        jax.ShapeDtypeStruct((bsz, N_KV, nblk, V_ROWS, LANES), BF16),
        jax.ShapeDtypeStruct((bsz, N_KV, nblk, V_ROWS, LANES), BF16),
        jax.ShapeDtypeStruct((bsz, N_KV * GATE_ROWS, s), F32),
    )
    vspec = pl.BlockSpec((1, N_KV, tm // LANES, V_ROWS, LANES), lambda b, i: (b, 0, i, 0, 0))
    return pl.pallas_call(
        _proj_kernel,
        grid=grid,
        in_specs=[
            pl.BlockSpec((1, tm, d), lambda b, i: (b, i, 0)),
            pl.BlockSpec(wn.shape, lambda b, i: (0, 0)),
            pl.BlockSpec(wt.shape, lambda b, i: (0, 0)),
            pl.BlockSpec(wc.shape, lambda b, i: (0, 0)),
        ],
        out_specs=(
            pl.BlockSpec((1, tm, CONV_CH), lambda b, i: (b, i, 0)),
            pl.BlockSpec((1, 2, tm, LANES), lambda b, i: (b, 0, i, 0)),
            pl.BlockSpec((1, tm, ka), lambda b, i: (b, i, 0)),
            pl.BlockSpec((1, tm, LANES), lambda b, i: (b, i, 0)),
            pl.BlockSpec((1, 512, tm), lambda b, i: (b, 0, i)),
            vspec, vspec,
            pl.BlockSpec((1, N_KV * GATE_ROWS, tm), lambda b, i: (b, 0, i)),
        ),
        out_shape=out_shape,
        scratch_shapes=[pltpu.VMEM((SUBLANES, CONV_CH), F32)],
        compiler_params=_cparams(("arbitrary", "arbitrary")),
        name="in_proj_conv",
    )(x, wn, wt, wc)


def _compress_kernel(g_ref, pe_ref, w1_ref, w2_ref, o_ref):
    pe = pe_ref[0]
    n = o_ref.shape[2]
    width = g_ref.shape[3]
    half = CMP_STRIDE * width
    a_lo = jnp.zeros((n, w1_ref.shape[2]), F32)
    a_hi = jnp.zeros((n, w1_ref.shape[2]), F32)
    for t in range(CMP_STRIDE):
        cols = slice(t * width, (t + 1) * width)
        x = g_ref[0, 0, pl.ds(t, n, stride=CMP_STRIDE), :]
        a_lo = a_lo + jnp.dot((x + pe[0:1, cols]).astype(BF16), w1_ref[0, cols, :],
                              preferred_element_type=F32)
        a_hi = a_hi + jnp.dot((x + pe[1:2, cols]).astype(BF16), w1_ref[0, half + t * width:half + (t + 1) * width, :],
                              preferred_element_type=F32)
    hid = a_lo + pltpu.roll(a_hi, n - 1, 0)
    act = jax.nn.gelu(hid)
    o_ref[0, 0] = jnp.dot(act.astype(BF16), w2_ref[0], preferred_element_type=F32)


def _compress_call(ckv, pes, w1s, w2s):
    bsz, _, s, width = ckv.shape
    n = s // CMP_STRIDE
    half = CMP_STRIDE * width
    return pl.pallas_call(
        _compress_kernel,
        grid=(bsz, 2),
        in_specs=[
            pl.BlockSpec((1, 1, s, width), lambda b, k: (b, k, 0, 0)),
            pl.BlockSpec((1, 2, half), lambda b, k: (k, 0, 0)),
            pl.BlockSpec((1, 2 * half, N_KV * CMP_HIDDEN), lambda b, k: (k, 0, 0)),
            pl.BlockSpec((1, N_KV * CMP_HIDDEN, N_KV * HEAD_DIM), lambda b, k: (k, 0, 0)),
        ],
        out_specs=pl.BlockSpec((1, 1, n, N_KV * HEAD_DIM), lambda b, k: (b, k, 0, 0)),
        out_shape=jax.ShapeDtypeStruct((bsz, 2, n, N_KV * HEAD_DIM), F32),
        compiler_params=_cparams(("arbitrary", "arbitrary")),
        name="compress_mlp",
    )(ckv, pes, w1s, w2s)


def _attn_kernel(n_sel, n_top,
                 q_ref, g_ref, kc_ref, vct_ref, ksa_ref, vst_ref, kw_ref, vwt_ref,
                 o_ref, qa_ref, ps_ref, oc_ref, s_ref):
    for kvh in range(N_KV):
        heads = pl.ds(kvh * GQA * HEAD_DIM, GQA * HEAD_DIM)
        _attn_head(n_sel, n_top, kvh,
                   q_ref.at[:, heads, :], g_ref.at[:, pl.ds(kvh * GATE_ROWS, GATE_ROWS), :], kc_ref,
                   vct_ref.at[:, pl.ds(kvh * HEAD_DIM, HEAD_DIM), :], ksa_ref, vst_ref.at[:, pl.ds(kvh, 1)],
                   kw_ref, vwt_ref.at[:, pl.ds(kvh, 1)], o_ref.at[:, :, heads], qa_ref, ps_ref, oc_ref, s_ref)


def _attn_head(n_sel, n_top, kvh,
               q_ref, g_ref, kc_ref, vct_ref, ksa_ref, vst_ref, kw_ref, vwt_ref,
               o_ref, qa_ref, ps_ref, oc_ref, s_ref):
    qb = pl.program_id(1)
    q0 = qb * Q_TILE
    nq = GQA * Q_TILE
    blk = q_ref[0]
    q4 = jnp.concatenate([blk[g * HEAD_DIM:(g + 1) * HEAD_DIM, :] for g in range(GQA)], axis=1)
    lane = lax.broadcasted_iota(jnp.int32, (1, nq), 1)
    t4 = q0 + (lane & (Q_TILE - 1))
    other = N_KV - 1 - kvh
    qa_ref[kvh * HEAD_DIM:(kvh + 1) * HEAD_DIM, :] = q4
    qa_ref[other * HEAD_DIM:(other + 1) * HEAD_DIM, :] = jnp.zeros_like(q4)
    q2 = qa_ref[0:N_KV * HEAD_DIM, :]

    n_cmp = kc_ref.shape[1]
    chunk = min(CMP_CHUNK, n_cmp)
    q_lane_tiles = Q_TILE // LANES
    for h in range(q_lane_tiles):
        ps_ref[h, 0:SUBLANES, :] = jnp.zeros((SUBLANES, LANES), F32)

    def cmp_branch(n):
        sc = jnp.dot(kc_ref[0, 0:n, :], q2, preferred_element_type=F32)
        last_c = jnp.right_shift(t4 - (CMP_BLOCK - 1), CMP_STRIDE.bit_length() - 1)
        scm = jnp.where(lax.broadcasted_iota(jnp.int32, (n, nq), 0) <= last_c, sc, NEG_INF)
        m_c = jnp.max(scm, axis=0, keepdims=True)
        e_c = jnp.exp2(scm - m_c)
        l_c = jnp.sum(e_c, axis=0, keepdims=True)
        p_c = e_c * jnp.where(last_c >= 0, 1.0 / l_c, 0.0)
        oc_ref[...] = jnp.dot(vct_ref[0, :, 0:n], p_c.astype(BF16), preferred_element_type=F32)
        psum = (p_c[:, 0:Q_TILE] + p_c[:, Q_TILE:2 * Q_TILE]
                + p_c[:, 2 * Q_TILE:3 * Q_TILE] + p_c[:, 3 * Q_TILE:4 * Q_TILE])
        for h in range(q_lane_tiles):
            ps_ref[h, SUBLANES:SUBLANES + n, :] = psum[:, h * LANES:(h + 1) * LANES]
            if n < n_cmp:
                ps_ref[h, SUBLANES + n:, :] = jnp.zeros((n_cmp - n, LANES), F32)

    last_visible = (q0 + Q_TILE - CMP_BLOCK) // CMP_STRIDE
    live_chunks = last_visible // chunk + 1
    for k in range(1, n_cmp // chunk + 1):
        pl.when(live_chunks == k)(functools.partial(cmp_branch, k * chunk))
    o_cmp = oc_ref[...]

    sw = jnp.dot(kw_ref[0, pl.ds(pl.multiple_of(q0, Q_TILE), WIN_KEYS), :], q2,
                 preferred_element_type=F32)
    ql = t4 - q0
    row = lax.broadcasted_iota(jnp.int32, (LANES, nq), 0)
    slabs = []
    for c in range(WIN_KEYS // LANES):
        slab = sw[c * LANES:(c + 1) * LANES, :]
        before_start = WINDOW - 1 - q0 - c * LANES
        if c * LANES < Q_TILE:
            slab = jnp.where(row > jnp.maximum(ql - c * LANES, before_start), slab, NEG_INF)
        elif (c + 1) * LANES > WINDOW:
            slab = jnp.where(row <= ql + (WINDOW - c * LANES), slab, NEG_INF)
        else:
            slab = jnp.where(row > before_start, slab, NEG_INF)
        slabs.append(slab)
    swm = jnp.concatenate(slabs, axis=0)
    m_w = jnp.max(swm, axis=0, keepdims=True)
    p_w = jnp.exp2(swm - m_w)
    wblk = q0 // LANES
    vwt = jnp.concatenate([vwt_ref[0, 0, wblk + i] for i in range(WIN_KEYS // LANES)], axis=1)
    acc_w = jnp.dot(vwt, p_w.astype(BF16), preferred_element_type=F32)
    o_win = acc_w[0:HEAD_DIM, :] * (1.0 / acc_w[HEAD_DIM:HEAD_DIM + 1, :])

    per_sel = SEL_BLOCK // CMP_STRIDE
    def importance(h):
        acc = ps_ref[h, pl.ds(SUBLANES - 1, n_sel, stride=per_sel), :]
        for off in range(CMP_BLOCK // CMP_STRIDE + per_sel - 2):
            acc = acc + ps_ref[h, pl.ds(SUBLANES + off, n_sel, stride=per_sel), :]
        return acc

    imp = jnp.concatenate([importance(h) for h in range(q_lane_tiles)], axis=1)

    jidx = lax.broadcasted_iota(jnp.int32, (n_sel, Q_TILE), 0)
    tq = q0 + lax.broadcasted_iota(jnp.int32, (n_sel, Q_TILE), 1)
    jt = jnp.right_shift(tq, SEL_BLOCK.bit_length() - 1)
    forced = (jidx == 0) | (jidx == jt) | (jidx == jt - 1)
    score = jnp.where(forced, FORCE_SCORE, imp)
    key = jnp.where(jidx > jt, -1, lax.bitcast_convert_type(score, jnp.int32))
    theta = jnp.zeros((1, Q_TILE), jnp.int32)
    for bit in range(30, -1, -1):
        cand = theta | (1 << bit)
        reach = jnp.sum((key >= cand).astype(jnp.int32), axis=0, keepdims=True)
        theta = jnp.where(reach >= n_top, cand, theta)
    above = key > theta
    tied = key == theta
    n_above = jnp.sum(above.astype(jnp.int32), axis=0, keepdims=True)
    lower = (lax.broadcasted_iota(jnp.int32, (n_sel, n_sel), 1)
             < lax.broadcasted_iota(jnp.int32, (n_sel, n_sel), 0)).astype(BF16)
    tied_before = jnp.dot(lower, tied.astype(BF16), preferred_element_type=F32)
    chosen = above | (tied & (tied_before < (n_top - n_above).astype(F32)))
    bias = jnp.where(chosen, 0.0, NEG_INF).astype(BF16)
    bias_row0 = N_KV * HEAD_DIM
    qa_ref[bias_row0:bias_row0 + n_sel, :] = jnp.concatenate([bias] * GQA, axis=1)
    if qa_ref.shape[0] > bias_row0 + n_sel:
        qa_ref[bias_row0 + n_sel:, :] = jnp.zeros((qa_ref.shape[0] - bias_row0 - n_sel, nq), BF16)

    nsub = SEL_TK // LANES

    def scores(c, buf):
        k0 = pl.multiple_of(c * SEL_TK, SEL_TK)
        s = jnp.dot(ksa_ref[0, pl.ds(k0, SEL_TK), :], qa_ref[...], preferred_element_type=F32)
        s_ref[buf] = s
        return jnp.max(s, axis=0, keepdims=True)

    def accumulate(c, buf, mx, m, acc, causal):
        s = s_ref[buf]
        if causal:
            kpos = c * SEL_TK + lax.broadcasted_iota(jnp.int32, (SEL_TK, nq), 0)
            s = jnp.where(kpos <= t4, s, NEG_INF)
            mx = jnp.max(s, axis=0, keepdims=True)
        m_new = jnp.maximum(m, mx)
        p = jnp.exp2(s - m_new)
        vt = jnp.concatenate([vst_ref[0, 0, c * nsub + i] for i in range(nsub)], axis=1)
        acc = jnp.exp2(m - m_new) * acc + jnp.dot(vt, p.astype(BF16), preferred_element_type=F32)
        return m_new, acc

    def pair(c, carry):
        mx0, m, acc = carry
        mx1 = scores(c + 1, 1)
        m, acc = accumulate(c, 0, mx0, m, acc, False)
        mx0 = scores(c + 2, 0)
        m, acc = accumulate(c + 1, 1, mx1, m, acc, False)
        return mx0, m, acc

    n_full = qb // (SEL_TK // Q_TILE)
    n_quads = n_full // 4
    n_pairs = (n_full - 4 * n_quads) // 2
    carry = (scores(0, 0), jnp.full((1, nq), NEG_INF, F32), jnp.zeros((V_ROWS, nq), F32))
    carry = lax.fori_loop(0, n_quads, lambda i, cr: pair(4 * i + 2, pair(4 * i, cr)), carry)
    mx0, m_s, acc_s = lax.fori_loop(0, n_pairs, lambda i, cr: pair(4 * n_quads + 2 * i, cr), carry)
    c_last = 4 * n_quads + 2 * n_pairs

    def leftover_then_own():
        mx1 = scores(c_last + 1, 1)
        m1, acc1 = accumulate(c_last, 0, mx0, m_s, acc_s, False)
        return accumulate(c_last + 1, 1, mx1, m1, acc1, True)[1]

    def own_only():
        return accumulate(c_last, 0, mx0, m_s, acc_s, True)[1]

    acc_s = lax.cond(n_full % 2 == 1, leftover_then_own, own_only)
    o_sel = acc_s[0:HEAD_DIM, :] * (1.0 / acc_s[HEAD_DIM:HEAD_DIM + 1, :])

    gt = g_ref[0]

    def gate(br):
        return jnp.concatenate([gt[br * GQA + g:br * GQA + g + 1, :] for g in range(GQA)], axis=1)

    ot = gate(0) * o_cmp + gate(1) * o_sel + gate(2) * o_win
    stacked = jnp.concatenate([ot[:, g * Q_TILE:(g + 1) * Q_TILE] for g in range(GQA)], axis=0)
    o_ref[0] = stacked.T.astype(BF16)


def _attn_call(qt, gt, kcmp, vct, ksa, vst, kw, vwt):
    bsz, _, s = qt.shape
    n_sel = s // SEL_BLOCK
    n_top = min(N_SEL, n_sel)
    n_cmp = kcmp.shape[1]
    ka = ksa.shape[2]
    nblk = s // LANES
    grid = (bsz, s // Q_TILE)
    kern = functools.partial(_attn_kernel, n_sel, n_top)
    return pl.pallas_call(
        kern,
        grid=grid,
        in_specs=[
            pl.BlockSpec((1, N_HEADS * HEAD_DIM, Q_TILE), lambda b, i: (b, 0, i)),
            pl.BlockSpec((1, N_KV * GATE_ROWS, Q_TILE), lambda b, i: (b, 0, i)),
            pl.BlockSpec((1, n_cmp, N_KV * HEAD_DIM), lambda b, i: (b, 0, 0)),
            pl.BlockSpec((1, N_KV * HEAD_DIM, n_cmp), lambda b, i: (b, 0, 0)),
            pl.BlockSpec((1, s, ka), lambda b, i: (b, 0, 0)),
            pl.BlockSpec((1, N_KV, nblk, V_ROWS, LANES), lambda b, i: (b, 0, 0, 0, 0)),
            pl.BlockSpec((1, s + WINDOW, N_KV * HEAD_DIM), lambda b, i: (b, 0, 0)),
            pl.BlockSpec((1, N_KV, nblk + WINDOW // LANES, V_ROWS, LANES), lambda b, i: (b, 0, 0, 0, 0)),
        ],
        out_specs=pl.BlockSpec((1, Q_TILE, N_HEADS * HEAD_DIM), lambda b, i: (b, i, 0)),
        out_shape=jax.ShapeDtypeStruct((bsz, s, N_HEADS * HEAD_DIM), BF16),
        scratch_shapes=[pltpu.VMEM((ka, GQA * Q_TILE), BF16),
                        pltpu.VMEM((Q_TILE // LANES, n_cmp + SUBLANES, LANES), F32),
                        pltpu.VMEM((HEAD_DIM, GQA * Q_TILE), F32),
                        pltpu.VMEM((2, SEL_TK, GQA * Q_TILE), F32)],
        compiler_params=_cparams(("arbitrary", "arbitrary")),
        name="nsa_attention",
    )(qt, gt, kcmp, vct, ksa, vst, kw, vwt)


def _post_kernel(x_ref, yc_ref, yn_ref, wo_ref, g1_ref, b1_ref, wr_ref, br_ref, tril_ref,
                 h1_ref, hp_ref, ri_ref, rw_ref, cnt_ref):
    step = pl.program_id(0)
    half = yc_ref.shape[1]
    sub = tril_ref.shape[0]

    @pl.when(step == 0)
    def _():
        cnt_ref[...] = jnp.zeros_like(cnt_ref)

    def route_rows(r0, cnt):
        rows = slice(r0, r0 + sub)
        mix = (jnp.dot(yc_ref[rows, :], wo_ref[0:half, :], preferred_element_type=F32)
               + jnp.dot(yn_ref[rows, :], wo_ref[half:2 * half, :], preferred_element_type=F32))
        h1 = _layer_norm(ALPHA * x_ref[rows, :] + mix, g1_ref[...], b1_ref[...])
        h1_ref[rows, :] = h1

        h_hi = h1.astype(BF16)
        h_hi32 = h_hi.astype(F32)
        h_lo = (h1 - h_hi32).astype(BF16)
        hp_ref[rows, :] = _pack_bf16_pairs(h1)
        parts = jnp.dot(jnp.concatenate([h_hi, h_lo], axis=1), wr_ref[...], preferred_element_type=F32)
        logits = parts[:, 0:LANES] + parts[:, LANES:2 * LANES] + br_ref[...]
        lt = logits.T[0:ROUTE_ROWS, :]
        rid = lax.broadcasted_iota(jnp.int32, (ROUTE_ROWS, sub), 0)
        big = jnp.int32(ROUTE_ROWS)

        def masked_softmax(valid):
            lg = jnp.where(valid, lt, NEG_INF)
            mx = jnp.max(lg, axis=0, keepdims=True)
            ex = jnp.where(valid, jnp.exp(lg - mx), 0.0)
            return ex / jnp.sum(ex, axis=0, keepdims=True)

        def first_max(vals, valid):
            top = jnp.max(jnp.where(valid, vals, -1.0), axis=0, keepdims=True)
            idx = jnp.min(jnp.where(valid & (vals == top), rid, big), axis=0, keepdims=True)
            return top, idx

        is_group = rid < N_GROUPS
        gp, gsel = first_max(masked_softmax(is_group), is_group)
        lo = ROUTE_LANE0 + EXP_PER_GROUP * gsel
        in_group = (rid >= lo) & (rid < lo + EXP_PER_GROUP)
        eprob = masked_softmax(in_group)
        p1, i1 = first_max(eprob, in_group)
        p2, i2 = first_max(eprob, in_group & (rid != i1))
        den = p1 + p2
        w1 = gp * (p1 / den)
        w2 = gp * (p2 / den)

        onehot = (rid == i1) | (rid == i2)
        before = jnp.dot(onehot.astype(BF16), tril_ref[...], preferred_element_type=F32)
        rk = before + cnt
        r1 = jnp.sum(jnp.where(rid == i1, rk, 0.0), axis=0, keepdims=True)
        r2 = jnp.sum(jnp.where(rid == i2, rk, 0.0), axis=0, keepdims=True)

        rec = lax.broadcasted_iota(jnp.int32, (ROUTE_COLS, sub), 0)
        ri_ref[:, rows] = jnp.where(rec == 0, i1 - ROUTE_LANE0,
                                    jnp.where(rec == 1, i2 - ROUTE_LANE0,
                                              jnp.where(rec == 2, r1.astype(jnp.int32),
                                                        jnp.where(rec == 3, r2.astype(jnp.int32), 0))))
        rw_ref[:, rows] = jnp.where(rec == 0, w1, jnp.where(rec == 1, w2, 0.0))
        return cnt + jnp.sum(onehot.astype(F32), axis=1, keepdims=True)

    cnt = cnt_ref[:, 0:1]
    for r0 in range(0, x_ref.shape[0], sub):
        cnt = route_rows(r0, cnt)
    cnt_ref[...] = jnp.broadcast_to(cnt, cnt_ref.shape)


def _post_call(x2, yconv, ynsa, wo, g1, b1, wr, br, tril):
    n_tok, d = x2.shape
    tm = POST_TM
    const = lambda a: pl.BlockSpec(a.shape, lambda i: (0, 0))
    return pl.pallas_call(
        _post_kernel,
        grid=(n_tok // tm,),
        in_specs=[
            pl.BlockSpec((tm, d), lambda i: (i, 0)),
            pl.BlockSpec((tm, yconv.shape[1]), lambda i: (i, 0)),
            pl.BlockSpec((tm, ynsa.shape[1]), lambda i: (i, 0)),
            const(wo), const(g1), const(b1), const(wr), const(br), const(tril),
        ],
        out_specs=(
            pl.BlockSpec((tm, d), lambda i: (i, 0)),
            pl.BlockSpec((tm, d // 2), lambda i: (i, 0)),
            pl.BlockSpec((ROUTE_COLS, tm), lambda i: (0, i)),
            pl.BlockSpec((ROUTE_COLS, tm), lambda i: (0, i)),
            pl.BlockSpec((ROUTE_ROWS, LANES), lambda i: (0, 0)),
        ),
        out_shape=(
            jax.ShapeDtypeStruct((n_tok, d), F32),
            jax.ShapeDtypeStruct((n_tok, d // 2), jnp.uint32),
            jax.ShapeDtypeStruct((ROUTE_COLS, n_tok), jnp.int32),
            jax.ShapeDtypeStruct((ROUTE_COLS, n_tok), F32),
            jax.ShapeDtypeStruct((ROUTE_ROWS, LANES), F32),
        ),
        compiler_params=_cparams(("arbitrary",)),
        name="out_proj_ln1_router",
    )(x2, yconv, ynsa, wo, g1, b1, wr, br, tril)


def _start_row_copy(src_ref, src_row, dst_ref, dst_row, sem, queue):
    pltpu.async_copy(src_ref.at[pl.ds(src_row, 1), :], dst_ref.at[pl.ds(dst_row, 1), :], sem, priority=queue)


def _sorted_row(dest_ref, r, slot):
    return dest_ref[slot * ROW_TM + r]


def _dispatch_kernel(pends_ref, nu_ref, dest_ref, h_ref, xs_ref, zero_ref, sem, zsem):
    tm = h_ref.shape[0]
    n_chunk = xs_ref.shape[0] // MOE_CHUNK

    @pl.when(pl.program_id(0) == 0)
    def _():
        zero_ref[...] = jnp.zeros_like(zero_ref)

        def zero_chunk(row0):
            row0 = pl.multiple_of(row0, MOE_CHUNK)
            return pltpu.make_async_copy(zero_ref, xs_ref.at[pl.ds(row0, MOE_CHUNK), :], zsem)

        def each_padded_chunk(act):
            def per_expert(e, _):
                end = pends_ref[e]
                start = jnp.where(e > 0, pends_ref[jnp.maximum(e - 1, 0)], 0)

                @pl.when(end > start)
                def _():
                    act(zero_chunk(end - MOE_CHUNK))
                return 0

            lax.fori_loop(0, N_EXPERTS, per_expert, 0)

            def per_dead_chunk(c, _):
                act(zero_chunk(c * MOE_CHUNK))
                return 0

            lax.fori_loop(nu_ref[0], n_chunk, per_dead_chunk, 0)

        each_padded_chunk(lambda cp: cp.start())
        each_padded_chunk(lambda cp: cp.wait())

    def issue(r, _):
        for slot in range(2):
            _start_row_copy(h_ref, r, xs_ref, _sorted_row(dest_ref, r, slot), sem, slot)
        return 0

    lax.fori_loop(0, tm, issue, 0, unroll=ROW_UNROLL)
    for slot in range(2):
        pltpu.make_async_copy(h_ref, xs_ref.at[pl.ds(0, tm), :], sem).wait()


def _dispatch_call(pends, n_used, dest, h1, n_rows):
    n_tok, d = h1.shape
    tm = ROW_TM
    grid_spec = pltpu.PrefetchScalarGridSpec(
        num_scalar_prefetch=2,
        grid=(n_tok // tm,),
        in_specs=[
            pl.BlockSpec((2 * tm,), lambda i, pe, nu: (i,), memory_space=pltpu.SMEM),
            pl.BlockSpec((tm, d), lambda i, pe, nu: (i, 0)),
        ],
        out_specs=pl.BlockSpec(memory_space=pl.ANY),
        scratch_shapes=[pltpu.VMEM((MOE_CHUNK, d), h1.dtype), pltpu.SemaphoreType.DMA, pltpu.SemaphoreType.DMA],
    )
    return pl.pallas_call(
        _dispatch_kernel,
        grid_spec=grid_spec,
        out_shape=jax.ShapeDtypeStruct((n_rows, d), h1.dtype),
        compiler_params=_cparams(("arbitrary",)),
        name="moe_dispatch",
    )(pends, n_used, dest, h1)


def _expert_kernel(ce_ref, nu_ref, xs_ref, wi_ref, wo_ref, o_ref, wib_ref, wob_ref):
    c = pl.program_id(0)
    live = c < nu_ref[0]

    @pl.when(live & ((c == 0) | (ce_ref[c] != ce_ref[jnp.maximum(c - 1, 0)])))
    def _():
        wib_ref[...] = wi_ref[0].astype(BF16)
        wob_ref[...] = wo_ref[0].astype(BF16)

    @pl.when(live)
    def _():
        x_lo, x_hi = _unpack_bf16_pairs(xs_ref[...])
        xb = jnp.concatenate([x_lo.astype(BF16), x_hi.astype(BF16)], axis=1)
        gu = jnp.dot(xb, wib_ref[...], preferred_element_type=F32)
        gate = gu[:, 0:D_EXPERT]
        act = gate * jax.nn.sigmoid(gate) * gu[:, D_EXPERT:2 * D_EXPERT]
        o_ref[...] = jnp.dot(act.astype(BF16), wob_ref[...], preferred_element_type=F32)

    @pl.when(jnp.logical_not(live))
    def _():
        o_ref[...] = jnp.zeros_like(o_ref)


def _expert_call(chunk_exp, n_used, xs, wi, wo):
    n_rows, dh = xs.shape
    d = wo.shape[2]
    n_chunk = n_rows // MOE_CHUNK

    def live(c, nu):
        return jnp.maximum(jnp.minimum(c, nu[0] - 1), 0)

    grid_spec = pltpu.PrefetchScalarGridSpec(
        num_scalar_prefetch=2,
        grid=(n_chunk,),
        in_specs=[
            pl.BlockSpec((MOE_CHUNK, dh), lambda c, ce, nu: (live(c, nu), 0)),
            pl.BlockSpec((1,) + wi.shape[1:], lambda c, ce, nu: (ce[live(c, nu)], 0, 0)),
            pl.BlockSpec((1,) + wo.shape[1:], lambda c, ce, nu: (ce[live(c, nu)], 0, 0)),
        ],
        out_specs=pl.BlockSpec((MOE_CHUNK, d), lambda c, ce, nu: (c, 0)),
        scratch_shapes=[pltpu.VMEM(wi.shape[1:], BF16), pltpu.VMEM(wo.shape[1:], BF16)],
    )
    return pl.pallas_call(
        _expert_kernel,
        grid_spec=grid_spec,
        out_shape=jax.ShapeDtypeStruct((n_rows, d), F32),
        compiler_params=_cparams(("arbitrary",)),
        name="moe_experts",
    )(chunk_exp, n_used, xs, wi, wo)


def _combine_kernel(dest_ref, dest_next_ref, h1_ref, rw_ref, p_ref, ys_ref,
                    g2_ref, b2_ref, wp_ref, wg_ref, bg_ref, g3_ref, b3_ref,
                    o_ref, rows_ref, sems):
    tm = h1_ref.shape[0]
    step = pl.program_id(0)
    cur = step % 2

    def gather(idx_ref, buf):
        def issue(r, _):
            for slot in range(2):
                _start_row_copy(ys_ref, _sorted_row(idx_ref, r, slot), rows_ref.at[buf, slot], r,
                                sems.at[buf], slot)
            return 0

        lax.fori_loop(0, tm, issue, 0, unroll=ROW_UNROLL)

    def await_rows(buf):
        for slot in range(2):
            pltpu.make_async_copy(ys_ref.at[pl.ds(0, tm), :], rows_ref.at[buf, slot], sems.at[buf]).wait()

    @pl.when(step == 0)
    def _():
        gather(dest_ref, 0)

    await_rows(cur)
    rw = rw_ref[...]
    ffn = rw[:, 0:1] * rows_ref[cur, 0] + rw[:, 1:2] * rows_ref[cur, 1]
    gather(dest_next_ref, 1 - cur)
    h2 =_layer_norm(ALPHA * h1_ref[...] + ffn, g2_ref[...], b2_ref[...])
    emb = jnp.dot(p_ref[...].astype(BF16), wp_ref[...], preferred_element_type=F32)
    gate = jax.nn.sigmoid(jnp.dot(h2.astype(BF16), wg_ref[...], preferred_element_type=F32) + bg_ref[...])
    o_ref[...] = _layer_norm(ALPHA * h2 + emb * gate, g3_ref[...], b3_ref[...])

    @pl.when(step == pl.num_programs(0) - 1)
    def _():
        await_rows(1 - cur)


def _combine_call(dest, h1, rw, p2, ys, g2, b2, wp, wg, bg, g3, b3):
    n_tok, d = h1.shape
    tm = ROW_TM
    const = lambda a: pl.BlockSpec(a.shape, lambda i: (0, 0))
    return pl.pallas_call(
        _combine_kernel,
        grid=(n_tok // tm,),
        in_specs=[
            pl.BlockSpec((2 * tm,), lambda i: (i,), memory_space=pltpu.SMEM),
            pl.BlockSpec((2 * tm,), lambda i: (jnp.minimum(i + 1, n_tok // tm - 1),), memory_space=pltpu.SMEM),
            pl.BlockSpec((tm, d), lambda i: (i, 0)),
            pl.BlockSpec((tm, rw.shape[1]), lambda i: (i, 0)),
            pl.BlockSpec((tm, p2.shape[1]), lambda i: (i, 0)),
            pl.BlockSpec(memory_space=pl.ANY),
            const(g2), const(b2), const(wp), const(wg), const(bg), const(g3), const(b3),
        ],
        out_specs=pl.BlockSpec((tm, d), lambda i: (i, 0)),
        out_shape=jax.ShapeDtypeStruct((n_tok, d), F32),
        scratch_shapes=[pltpu.VMEM((2, 2, tm, d), F32), pltpu.SemaphoreType.DMA((2,))],
        compiler_params=_cparams(("arbitrary",)),
        name="moe_combine_ln_ple",
    )(dest, dest, h1, rw, p2, ys, g2, b2, wp, wg, bg, g3, b3)


def _gate_columns():
    cols = np.zeros((N_KV, GATE_ROWS), np.int32)
    live = np.zeros((N_KV, GATE_ROWS), np.float32)
    for h in range(N_KV):
        for br in range(3):
            for g in range(GQA):
                cols[h, br * GQA + g] = (h * GQA + g) * 3 + br
                live[h, br * GQA + g] = 1.0
    return cols.reshape(-1), live.reshape(-1)


def _layer(x, p, w_in, w_conv, pe_ck, w1_ck, w2_ck, pe_cv, w1_cv, w2_cv, w_out, ln1_g, ln1_b,
           w_rg, b_rg, w_re, b_re, w_e_in, w_e_out, ln2_g, ln2_b, w_ple, w_ple_gate, b_ple_gate,
           ln3_g, ln3_b):
    bsz, s, d = x.shape
    n_tok = bsz * s
    row = lambda v: v.reshape(1, -1)

    c_q, c_kc, c_vc, c_ks, c_vs, c_kw, c_vw, c_g = 1536, 2048, 2176, 2304, 2432, 2560, 2688, 2816
    wn = jnp.concatenate([w_in[:, 0:c_q], w_in[:, c_kc:c_ks], w_in[:, c_ks:c_vs], w_in[:, c_kw:c_vw]],
                         axis=1).astype(BF16)
    gcols, glive = _gate_columns()
    w_gate = w_in[:, c_g:c_g + 3 * N_HEADS][:, gcols] * glive[None, :]
    wt = jnp.concatenate([w_in[:, c_q:c_kc] * Q_SCALE, w_in[:, c_vs:c_kw], w_in[:, c_vw:c_g], w_gate],
                         axis=1).T.astype(BF16)

    yconv, cmpkv, ksa, kw, qt, vst, vwt, gt = _proj_call(x, wn, wt, w_conv)

    half_blk = CMP_BLOCK // 2
    eye = jnp.eye(N_KV, dtype=F32)

    def per_head_w1(w1):
        w = w1.reshape(2, half_blk, HEAD_DIM, CMP_HIDDEN)
        w = jnp.einsum('ptdc,hg->pthdgc', w, eye)
        return w.reshape(2 * half_blk * N_KV * HEAD_DIM, N_KV * CMP_HIDDEN)

    def per_head_w2(w2):
        return jnp.einsum('cd,hg->hcgd', w2, eye).reshape(N_KV * CMP_HIDDEN, N_KV * HEAD_DIM)

    def per_head_pe(pe):
        return jnp.broadcast_to(pe.reshape(2, half_blk, 1, HEAD_DIM),
                                (2, half_blk, N_KV, HEAD_DIM)).reshape(2, half_blk * N_KV * HEAD_DIM)

    pes = jnp.stack([per_head_pe(pe_ck), per_head_pe(pe_cv)])
    w1s = jnp.stack([per_head_w1(w1_ck), per_head_w1(w1_cv)]).astype(BF16)
    w2s = jnp.stack([per_head_w2(w2_ck), per_head_w2(w2_cv)]).astype(BF16)
    cmp_out = _compress_call(cmpkv, pes, w1s, w2s)
    kcmp = cmp_out[:, 0].astype(BF16)
    vct = cmp_out[:, 1].transpose(0, 2, 1).astype(BF16)

    kw = jnp.pad(kw, ((0, 0), (WINDOW, 0), (0, 0)))
    vwt = jnp.pad(vwt, ((0, 0), (0, 0), (WINDOW // LANES, 0), (0, 0), (0, 0)))
    ynsa = _attn_call(qt, gt, kcmp, vct, ksa, vst, kw, vwt)

    wr = jnp.zeros((d, LANES), F32).at[:, 0:N_GROUPS].set(w_rg)
    wr = wr.at[:, ROUTE_LANE0:ROUTE_LANE0 + N_EXPERTS].set(w_re)
    br = jnp.zeros((1, LANES), F32).at[0, 0:N_GROUPS].set(b_rg)
    br = br.at[0, ROUTE_LANE0:ROUTE_LANE0 + N_EXPERTS].set(b_re)
    wr_hi = wr.astype(BF16)
    wr_lo = (wr - wr_hi.astype(F32)).astype(BF16)
    wr = jnp.concatenate([jnp.concatenate([wr_hi, wr_lo], axis=1),
                          jnp.concatenate([wr_hi, jnp.zeros_like(wr_lo)], axis=1)], axis=0)
    earlier = jnp.asarray(np.triu(np.ones((POST_TM, POST_TM), np.float32), 1), BF16)
    h1, h1_packed, ri, rw, cnt = _post_call(x.reshape(n_tok, d), yconv.reshape(n_tok, -1), ynsa.reshape(n_tok, -1),
                                            w_out.astype(BF16), row(ln1_g), row(ln1_b), wr, br, earlier)
    rw = rw.T

    counts = cnt[ROUTE_LANE0:ROUTE_LANE0 + N_EXPERTS, 0].astype(jnp.int32)
    pcounts = (counts + MOE_CHUNK - 1) // MOE_CHUNK * MOE_CHUNK
    pends = jnp.cumsum(pcounts)
    pstarts = (pends - pcounts).astype(jnp.int32)
    dest = (pstarts[ri[0:2]] + ri[2:4]).reshape(2, n_tok // ROW_TM, ROW_TM).transpose(1, 0, 2).reshape(-1)
    n_asg = n_tok * 2
    n_chunk = -(-n_asg // MOE_CHUNK) + N_EXPERTS
    chunk_row0 = jnp.arange(n_chunk, dtype=jnp.int32) * MOE_CHUNK
    chunk_exp = jnp.minimum(jnp.sum(pends[None, :] <= chunk_row0[:, None], axis=1), N_EXPERTS - 1).astype(jnp.int32)
    n_used = (pends[-1:] // MOE_CHUNK).astype(jnp.int32)

    xs = _dispatch_call(pends.astype(jnp.int32), n_used, dest, h1_packed, n_chunk * MOE_CHUNK)
    ys = _expert_call(chunk_exp, n_used, xs, w_e_in, w_e_out)
    out = _combine_call(dest, h1, rw, p.reshape(n_tok, -1), ys, row(ln2_g), row(ln2_b),
                        w_ple.astype(BF16), w_ple_gate.astype(BF16), row(b_ple_gate), row(ln3_g), row(ln3_b))
    return out.reshape(bsz, s, d)


def kernel(x, p, w_in, w_conv, pe_ck, w1_ck, w2_ck, pe_cv, w1_cv, w2_cv, w_out, ln1_g, ln1_b, w_rg, b_rg, w_re, b_re, w_e_in, w_e_out, ln2_g, ln2_b, w_ple, w_ple_gate, b_ple_gate, ln3_g, ln3_b):
    assert w_in.shape[0] == DEPTH, "residual scaling ALPHA is derived from DEPTH"
    h = x
    for i in range(DEPTH):
        h = _layer(h, p[i], w_in[i], w_conv[i], pe_ck[i], w1_ck[i], w2_ck[i], pe_cv[i], w1_cv[i], w2_cv[i],
                   w_out[i], ln1_g[i], ln1_b[i], w_rg[i], b_rg[i], w_re[i], b_re[i], w_e_in[i], w_e_out[i],
                   ln2_g[i], ln2_b[i], w_ple[i], w_ple_gate[i], b_ple_gate[i], ln3_g[i], ln3_b[i])
    return h
```

```python
import functools

import jax
import jax.numpy as jnp
import numpy as np
from jax import lax
from jax.experimental import pallas as pl
from jax.experimental.pallas import tpu as pltpu

F32 = jnp.float32
BF16 = jnp.bfloat16

CONV_CH = 512
N_HEADS = 8
HEAD_DIM = 64
N_KV = 2
GQA = N_HEADS // N_KV
CMP_BLOCK = 32
CMP_STRIDE = 16
CMP_HIDDEN = 2 * HEAD_DIM
SEL_BLOCK = 64
N_SEL = 16
WINDOW = 512
ATTN_SCALE = HEAD_DIM ** -0.5
Q_SCALE = ATTN_SCALE * float(np.log2(np.e))
FORCE_SCORE = 1e4
NEG_INF = -1e30
N_GROUPS = 4
EXP_PER_GROUP = 8
N_EXPERTS = N_GROUPS * EXP_PER_GROUP
D_EXPERT = 512
MOE_CHUNK = 512
DEPTH = 1
ALPHA = (2 * DEPTH) ** 0.25
LN_EPS = 1e-5

LANES = 128
SUBLANES = 8
VMEM_LIMIT = 56 * 1024 * 1024

PROJ_TM = 512
Q_TILE = 256
SEL_TK = 512
CMP_CHUNK = 128
WIN_KEYS = WINDOW + Q_TILE
POST_TM = 512
ROW_TM = 512
ROW_UNROLL = True
ROUTE_LANE0 = N_GROUPS
ROUTE_ROWS = 48
ROUTE_COLS = SUBLANES
GATE_ROWS = 16
V_ROWS = HEAD_DIM + 16


def _cparams(sem, vmem=VMEM_LIMIT):
    return pltpu.CompilerParams(dimension_semantics=sem, vmem_limit_bytes=vmem)


def _pack_bf16_pairs(v):
    bits = lax.bitcast_convert_type(v.astype(BF16).astype(F32), jnp.uint32)
    dh = bits.shape[1] // 2
    return jnp.right_shift(bits[:, 0:dh], jnp.uint32(16)) | bits[:, dh:2 * dh]


def _unpack_bf16_pairs(words):
    lo = lax.bitcast_convert_type(jnp.left_shift(words, jnp.uint32(16)), F32)
    hi = lax.bitcast_convert_type(words & jnp.uint32(0xFFFF0000), F32)
    return lo, hi


def _layer_norm(v, g, b):
    mu = jnp.mean(v, axis=-1, keepdims=True)
    d = v - mu
    var = jnp.mean(d * d, axis=-1, keepdims=True)
    return d * lax.rsqrt(var + LN_EPS) * g + b


def _proj_kernel(x_ref, wn_ref, wt_ref, wc_ref,
                 yconv_ref, cmpkv_ref, ksa_ref, kw_ref, qt_ref, vst_ref, vwt_ref, gt_ref,
                 carry_ref):
    si = pl.program_id(1)
    tm = x_ref.shape[1]
    xb = x_ref[0].astype(BF16)
    zn = jnp.dot(xb, wn_ref[...], preferred_element_type=F32)
    zt = lax.dot_general(wt_ref[...], xb, (((1,), (1,)), ((), ())),
                         preferred_element_type=F32)

    cb = zn[:, 0:CONV_CH]
    u = zn[:, CONV_CH:2 * CONV_CH] * zn[:, 2 * CONV_CH:3 * CONV_CH]

    @pl.when(si == 0)
    def _():
        carry_ref[...] = jnp.zeros_like(carry_ref)

    prev = carry_ref[...]
    rows = lax.broadcasted_iota(jnp.int32, u.shape, 0)
    u1 = jnp.where(rows == 0, prev[7:8, :], pltpu.roll(u, 1, 0))
    u2 = jnp.where(rows == 0, prev[6:7, :],
                   jnp.where(rows == 1, prev[7:8, :], pltpu.roll(u, 2, 0)))
    w = wc_ref[...]
    yconv_ref[0] = (cb * (w[0:1, :] * u2 + w[1:2, :] * u1 + w[2:3, :] * u)).astype(BF16)
    carry_ref[...] = u[tm - SUBLANES:tm, :]

    cmpkv_ref[0, 0] = zn[:, 1536:1664]
    cmpkv_ref[0, 1] = zn[:, 1664:1792]
    kw_ref[0] = zn[:, 1920:2048].astype(BF16)
    ksa_ref[0, :, 0:LANES] = zn[:, 1792:1920].astype(BF16)
    n_hot = ksa_ref.shape[2] - LANES
    pos = si * tm + lax.broadcasted_iota(jnp.int32, (tm, n_hot), 0)
    hot = jnp.right_shift(pos, SEL_BLOCK.bit_length() - 1) == lax.broadcasted_iota(jnp.int32, (tm, n_hot), 1)
    ksa_ref[0, :, LANES:] = jnp.where(hot, 1.0, 0.0).astype(BF16)

    qt_ref[0] = zt[0:512, :].astype(BF16)
    ones = jnp.ones((V_ROWS - HEAD_DIM, LANES), BF16)
    for h in range(N_KV):
        for i in range(tm // LANES):
            cols = slice(i * LANES, (i + 1) * LANES)
            vst_ref[0, h, i, 0:HEAD_DIM, :] = zt[512 + h * 64:512 + (h + 1) * 64, cols].astype(BF16)
            vst_ref[0, h, i, HEAD_DIM:V_ROWS, :] = ones
            vwt_ref[0, h, i, 0:HEAD_DIM, :] = zt[640 + h * 64:640 + (h + 1) * 64, cols].astype(BF16)
            vwt_ref[0, h, i, HEAD_DIM:V_ROWS, :] = ones
    gt_ref[0] = jax.nn.sigmoid(zt[768:800, :])


def _proj_call(x, wn, wt, wc):
    bsz, s, d = x.shape
    tm = PROJ_TM
    nblk = s // LANES
    grid = (bsz, s // tm)
    ka = LANES + -(-(s // SEL_BLOCK) // LANES) * LANES
    out_shape = (
        jax.ShapeDtypeStruct((bsz, s, CONV_CH), BF16),
        jax.ShapeDtypeStruct((bsz, 2, s, LANES), F32),
        jax.ShapeDtypeStruct((bsz, s, ka), BF16),
        jax.ShapeDtypeStruct((bsz, s, LANES), BF16),
        jax.ShapeDtypeStruct((bsz, 512, s), BF16),
        jax.ShapeDtypeStruct((bsz, N_KV, nblk, V_ROWS, LANES), BF16),
        jax.ShapeDtypeStruct((bsz, N_KV, nblk, V_ROWS, LANES), BF16),
        jax.ShapeDtypeStruct((bsz, N_KV * GATE_ROWS, s), F32),
    )
    vspec = pl.BlockSpec((1, N_KV, tm // LANES, V_ROWS, LANES), lambda b, i: (b, 0, i, 0, 0))
    return pl.pallas_call(
        _proj_kernel,
        grid=grid,
        in_specs=[
            pl.BlockSpec((1, tm, d), lambda b, i: (b, i, 0)),
            pl.BlockSpec(wn.shape, lambda b, i: (0, 0)),
            pl.BlockSpec(wt.shape, lambda b, i: (0, 0)),
            pl.BlockSpec(wc.shape, lambda b, i: (0, 0)),
        ],
        out_specs=(
            pl.BlockSpec((1, tm, CONV_CH), lambda b, i: (b, i, 0)),
            pl.BlockSpec((1, 2, tm, LANES), lambda b, i: (b, 0, i, 0)),
            pl.BlockSpec((1, tm, ka), lambda b, i: (b, i, 0)),
            pl.BlockSpec((1, tm, LANES), lambda b, i: (b, i, 0)),
            pl.BlockSpec((1, 512, tm), lambda b, i: (b, 0, i)),
            vspec, vspec,
            pl.BlockSpec((1, N_KV * GATE_ROWS, tm), lambda b, i: (b, 0, i)),
        ),
        out_shape=out_shape,
        scratch_shapes=[pltpu.VMEM((SUBLANES, CONV_CH), F32)],
        compiler_params=_cparams(("arbitrary", "arbitrary")),
        name="in_proj_conv",
    )(x, wn, wt, wc)


def _compress_kernel(g_ref, pe_ref, w1_ref, w2_ref, o_ref):
    pe = pe_ref[0]
    n = o_ref.shape[2]
    width = g_ref.shape[3]
    half = CMP_STRIDE * width
    a_lo = jnp.zeros((n, w1_ref.shape[2]), F32)
    a_hi = jnp.zeros((n, w1_ref.shape[2]), F32)
    for t in range(CMP_STRIDE):
        cols = slice(t * width, (t + 1) * width)
        x = g_ref[0, 0, pl.ds(t, n, stride=CMP_STRIDE), :]
        a_lo = a_lo + jnp.dot((x + pe[0:1, cols]).astype(BF16), w1_ref[0, cols, :],
                              preferred_element_type=F32)
        a_hi = a_hi + jnp.dot((x + pe[1:2, cols]).astype(BF16), w1_ref[0, half + t * width:half + (t + 1) * width, :],
                              preferred_element_type=F32)
    hid = a_lo + pltpu.roll(a_hi, n - 1, 0)
    act = jax.nn.gelu(hid)
    o_ref[0, 0] = jnp.dot(act.astype(BF16), w2_ref[0], preferred_element_type=F32)


def _compress_call(ckv, pes, w1s, w2s):
    bsz, _, s, width = ckv.shape
    n = s // CMP_STRIDE
    half = CMP_STRIDE * width
    return pl.pallas_call(
        _compress_kernel,
        grid=(bsz, 2),
        in_specs=[
            pl.BlockSpec((1, 1, s, width), lambda b, k: (b, k, 0, 0)),
            pl.BlockSpec((1, 2, half), lambda b, k: (k, 0, 0)),
            pl.BlockSpec((1, 2 * half, N_KV * CMP_HIDDEN), lambda b, k: (k, 0, 0)),
            pl.BlockSpec((1, N_KV * CMP_HIDDEN, N_KV * HEAD_DIM), lambda b, k: (k, 0, 0)),
        ],
        out_specs=pl.BlockSpec((1, 1, n, N_KV * HEAD_DIM), lambda b, k: (b, k, 0, 0)),
        out_shape=jax.ShapeDtypeStruct((bsz, 2, n, N_KV * HEAD_DIM), F32),
        compiler_params=_cparams(("arbitrary", "arbitrary")),
        name="compress_mlp",
    )(ckv, pes, w1s, w2s)


def _attn_kernel(n_sel, n_top,
                 q_ref, g_ref, kc_ref, vct_ref, ksa_ref, vst_ref, kw_ref, vwt_ref,
                 o_ref, qa_ref, ps_ref, oc_ref, s_ref):
    for kvh in range(N_KV):
        heads = pl.ds(kvh * GQA * HEAD_DIM, GQA * HEAD_DIM)
        _attn_head(n_sel, n_top, kvh,
                   q_ref.at[:, heads, :], g_ref.at[:, pl.ds(kvh * GATE_ROWS, GATE_ROWS), :], kc_ref,
                   vct_ref.at[:, pl.ds(kvh * HEAD_DIM, HEAD_DIM), :], ksa_ref, vst_ref.at[:, pl.ds(kvh, 1)],
                   kw_ref, vwt_ref.at[:, pl.ds(kvh, 1)], o_ref.at[:, :, heads], qa_ref, ps_ref, oc_ref, s_ref)


def _attn_head(n_sel, n_top, kvh,
               q_ref, g_ref, kc_ref, vct_ref, ksa_ref, vst_ref, kw_ref, vwt_ref,
               o_ref, qa_ref, ps_ref, oc_ref, s_ref):
    qb = pl.program_id(1)
    q0 = qb * Q_TILE
    nq = GQA * Q_TILE
    blk = q_ref[0]
    q4 = jnp.concatenate([blk[g * HEAD_DIM:(g + 1) * HEAD_DIM, :] for g in range(GQA)], axis=1)
    lane = lax.broadcasted_iota(jnp.int32, (1, nq), 1)
    t4 = q0 + (lane & (Q_TILE - 1))
    other = N_KV - 1 - kvh
    qa_ref[kvh * HEAD_DIM:(kvh + 1) * HEAD_DIM, :] = q4
    qa_ref[other * HEAD_DIM:(other + 1) * HEAD_DIM, :] = jnp.zeros_like(q4)
    q2 = qa_ref[0:N_KV * HEAD_DIM, :]

    n_cmp = kc_ref.shape[1]
    chunk = min(CMP_CHUNK, n_cmp)
    q_lane_tiles = Q_TILE // LANES
    for h in range(q_lane_tiles):
        ps_ref[h, 0:SUBLANES, :] = jnp.zeros((SUBLANES, LANES), F32)

    def cmp_branch(n):
        sc = jnp.dot(kc_ref[0, 0:n, :], q2, preferred_element_type=F32)
        last_c = jnp.right_shift(t4 - (CMP_BLOCK - 1), CMP_STRIDE.bit_length() - 1)
        scm = jnp.where(lax.broadcasted_iota(jnp.int32, (n, nq), 0) <= last_c, sc, NEG_INF)
        m_c = jnp.max(scm, axis=0, keepdims=True)
        e_c = jnp.exp2(scm - m_c)
        l_c = jnp.sum(e_c, axis=0, keepdims=True)
        p_c = e_c * jnp.where(last_c >= 0, 1.0 / l_c, 0.0)
        oc_ref[...] = jnp.dot(vct_ref[0, :, 0:n], p_c.astype(BF16), preferred_element_type=F32)
        psum = (p_c[:, 0:Q_TILE] + p_c[:, Q_TILE:2 * Q_TILE]
                + p_c[:, 2 * Q_TILE:3 * Q_TILE] + p_c[:, 3 * Q_TILE:4 * Q_TILE])
        for h in range(q_lane_tiles):
            ps_ref[h, SUBLANES:SUBLANES + n, :] = psum[:, h * LANES:(h + 1) * LANES]
            if n < n_cmp:
                ps_ref[h, SUBLANES + n:, :] = jnp.zeros((n_cmp - n, LANES), F32)

    last_visible = (q0 + Q_TILE - CMP_BLOCK) // CMP_STRIDE
    live_chunks = last_visible // chunk + 1
    for k in range(1, n_cmp // chunk + 1):
        pl.when(live_chunks == k)(functools.partial(cmp_branch, k * chunk))
    o_cmp = oc_ref[...]

    sw = jnp.dot(kw_ref[0, pl.ds(pl.multiple_of(q0, Q_TILE), WIN_KEYS), :], q2,
                 preferred_element_type=F32)
    ql = t4 - q0
    row = lax.broadcasted_iota(jnp.int32, (LANES, nq), 0)
    slabs = []
    for c in range(WIN_KEYS // LANES):
        slab = sw[c * LANES:(c + 1) * LANES, :]
        before_start = WINDOW - 1 - q0 - c * LANES
        if c * LANES < Q_TILE:
            slab = jnp.where(row > jnp.maximum(ql - c * LANES, before_start), slab, NEG_INF)
        elif (c + 1) * LANES > WINDOW:
            slab = jnp.where(row <= ql + (WINDOW - c * LANES), slab, NEG_INF)
        else:
            slab = jnp.where(row > before_start, slab, NEG_INF)
        slabs.append(slab)
    swm = jnp.concatenate(slabs, axis=0)
    m_w = jnp.max(swm, axis=0, keepdims=True)
    p_w = jnp.exp2(swm - m_w)
    wblk = q0 // LANES
    vwt = jnp.concatenate([vwt_ref[0, 0, wblk + i] for i in range(WIN_KEYS // LANES)], axis=1)
    acc_w = jnp.dot(vwt, p_w.astype(BF16), preferred_element_type=F32)
    o_win = acc_w[0:HEAD_DIM, :] * (1.0 / acc_w[HEAD_DIM:HEAD_DIM + 1, :])

    per_sel = SEL_BLOCK // CMP_STRIDE
    def importance(h):
        acc = ps_ref[h, pl.ds(SUBLANES - 1, n_sel, stride=per_sel), :]
        for off in range(CMP_BLOCK // CMP_STRIDE + per_sel - 2):
            acc = acc + ps_ref[h, pl.ds(SUBLANES + off, n_sel, stride=per_sel), :]
        return acc

    imp = jnp.concatenate([importance(h) for h in range(q_lane_tiles)], axis=1)

    jidx = lax.broadcasted_iota(jnp.int32, (n_sel, Q_TILE), 0)
    tq = q0 + lax.broadcasted_iota(jnp.int32, (n_sel, Q_TILE), 1)
    jt = jnp.right_shift(tq, SEL_BLOCK.bit_length() - 1)
    forced = (jidx == 0) | (jidx == jt) | (jidx == jt - 1)
    score = jnp.where(forced, FORCE_SCORE, imp)
    key = jnp.where(jidx > jt, -1, lax.bitcast_convert_type(score, jnp.int32))
    theta = jnp.zeros((1, Q_TILE), jnp.int32)
    for bit in range(30, -1, -1):
        cand = theta | (1 << bit)
        reach = jnp.sum((key >= cand).astype(jnp.int32), axis=0, keepdims=True)
        theta = jnp.where(reach >= n_top, cand, theta)
    above = key > theta
    tied = key == theta
    n_above = jnp.sum(above.astype(jnp.int32), axis=0, keepdims=True)
    lower = (lax.broadcasted_iota(jnp.int32, (n_sel, n_sel), 1)
             < lax.broadcasted_iota(jnp.int32, (n_sel, n_sel), 0)).astype(BF16)
    tied_before = jnp.dot(lower, tied.astype(BF16), preferred_element_type=F32)
    chosen = above | (tied & (tied_before < (n_top - n_above).astype(F32)))
    bias = jnp.where(chosen, 0.0, NEG_INF).astype(BF16)
    bias_row0 = N_KV * HEAD_DIM
    qa_ref[bias_row0:bias_row0 + n_sel, :] = jnp.concatenate([bias] * GQA, axis=1)
    if qa_ref.shape[0] > bias_row0 + n_sel:
        qa_ref[bias_row0 + n_sel:, :] = jnp.zeros((qa_ref.shape[0] - bias_row0 - n_sel, nq), BF16)

    nsub = SEL_TK // LANES

    def scores(c, buf):
        k0 = pl.multiple_of(c * SEL_TK, SEL_TK)
        s = jnp.dot(ksa_ref[0, pl.ds(k0, SEL_TK), :], qa_ref[...], preferred_element_type=F32)
        s_ref[buf] = s
        return jnp.max(s, axis=0, keepdims=True)

    def accumulate(c, buf, mx, m, acc, causal):
        s = s_ref[buf]
        if causal:
            kpos = c * SEL_TK + lax.broadcasted_iota(jnp.int32, (SEL_TK, nq), 0)
            s = jnp.where(kpos <= t4, s, NEG_INF)
            mx = jnp.max(s, axis=0, keepdims=True)
        m_new = jnp.maximum(m, mx)
        p = jnp.exp2(s - m_new)
        vt = jnp.concatenate([vst_ref[0, 0, c * nsub + i] for i in range(nsub)], axis=1)
        acc = jnp.exp2(m - m_new) * acc + jnp.dot(vt, p.astype(BF16), preferred_element_type=F32)
        return m_new, acc

    def pair(c, carry):
        mx0, m, acc = carry
        mx1 = scores(c + 1, 1)
        m, acc = accumulate(c, 0, mx0, m, acc, False)
        mx0 = scores(c + 2, 0)
        m, acc = accumulate(c + 1, 1, mx1, m, acc, False)
        return mx0, m, acc

    n_full = qb // (SEL_TK // Q_TILE)
    n_quads = n_full // 4
    n_pairs = (n_full - 4 * n_quads) // 2
    carry = (scores(0, 0), jnp.full((1, nq), NEG_INF, F32), jnp.zeros((V_ROWS, nq), F32))
    carry = lax.fori_loop(0, n_quads, lambda i, cr: pair(4 * i + 2, pair(4 * i, cr)), carry)
    mx0, m_s, acc_s = lax.fori_loop(0, n_pairs, lambda i, cr: pair(4 * n_quads + 2 * i, cr), carry)
    c_last = 4 * n_quads + 2 * n_pairs

    def leftover_then_own():
        mx1 = scores(c_last + 1, 1)
        m1, acc1 = accumulate(c_last, 0, mx0, m_s, acc_s, False)
        return accumulate(c_last + 1, 1, mx1, m1, acc1, True)[1]

    def own_only():
        return accumulate(c_last, 0, mx0, m_s, acc_s, True)[1]

    acc_s = lax.cond(n_full % 2 == 1, leftover_then_own, own_only)
    o_sel = acc_s[0:HEAD_DIM, :] * (1.0 / acc_s[HEAD_DIM:HEAD_DIM + 1, :])

    gt = g_ref[0]

    def gate(br):
        return jnp.concatenate([gt[br * GQA + g:br * GQA + g + 1, :] for g in range(GQA)], axis=1)

    ot = gate(0) * o_cmp + gate(1) * o_sel + gate(2) * o_win
    stacked = jnp.concatenate([ot[:, g * Q_TILE:(g + 1) * Q_TILE] for g in range(GQA)], axis=0)
    o_ref[0] = stacked.T.astype(BF16)


def _attn_call(qt, gt, kcmp, vct, ksa, vst, kw, vwt):
    bsz, _, s = qt.shape
    n_sel = s // SEL_BLOCK
    n_top = min(N_SEL, n_sel)
    n_cmp = kcmp.shape[1]
    ka = ksa.shape[2]
    nblk = s // LANES
    grid = (bsz, s // Q_TILE)
    kern = functools.partial(_attn_kernel, n_sel, n_top)
    return pl.pallas_call(
        kern,
        grid=grid,
        in_specs=[
            pl.BlockSpec((1, N_HEADS * HEAD_DIM, Q_TILE), lambda b, i: (b, 0, i)),
            pl.BlockSpec((1, N_KV * GATE_ROWS, Q_TILE), lambda b, i: (b, 0, i)),
            pl.BlockSpec((1, n_cmp, N_KV * HEAD_DIM), lambda b, i: (b, 0, 0)),
            pl.BlockSpec((1, N_KV * HEAD_DIM, n_cmp), lambda b, i: (b, 0, 0)),
            pl.BlockSpec((1, s, ka), lambda b, i: (b, 0, 0)),
            pl.BlockSpec((1, N_KV, nblk, V_ROWS, LANES), lambda b, i: (b, 0, 0, 0, 0)),
            pl.BlockSpec((1, s + WINDOW, N_KV * HEAD_DIM), lambda b, i: (b, 0, 0)),
            pl.BlockSpec((1, N_KV, nblk + WINDOW // LANES, V_ROWS, LANES), lambda b, i: (b, 0, 0, 0, 0)),
        ],
        out_specs=pl.BlockSpec((1, Q_TILE, N_HEADS * HEAD_DIM), lambda b, i: (b, i, 0)),
        out_shape=jax.ShapeDtypeStruct((bsz, s, N_HEADS * HEAD_DIM), BF16),
        scratch_shapes=[pltpu.VMEM((ka, GQA * Q_TILE), BF16),
                        pltpu.VMEM((Q_TILE // LANES, n_cmp + SUBLANES, LANES), F32),
                        pltpu.VMEM((HEAD_DIM, GQA * Q_TILE), F32),
                        pltpu.VMEM((2, SEL_TK, GQA * Q_TILE), F32)],
        compiler_params=_cparams(("arbitrary", "arbitrary")),
        name="nsa_attention",
    )(qt, gt, kcmp, vct, ksa, vst, kw, vwt)


def _post_kernel(x_ref, yc_ref, yn_ref, wo_ref, g1_ref, b1_ref, wr_ref, br_ref, tril_ref,
                 h1_ref, hp_ref, ri_ref, rw_ref, cnt_ref):
    step = pl.program_id(0)
    half = yc_ref.shape[1]
    sub = tril_ref.shape[0]

    @pl.when(step == 0)
    def _():
        cnt_ref[...] = jnp.zeros_like(cnt_ref)

    def route_rows(r0, cnt):
        rows = slice(r0, r0 + sub)
        mix = (jnp.dot(yc_ref[rows, :], wo_ref[0:half, :], preferred_element_type=F32)
               + jnp.dot(yn_ref[rows, :], wo_ref[half:2 * half, :], preferred_element_type=F32))
        h1 = _layer_norm(ALPHA * x_ref[rows, :] + mix, g1_ref[...], b1_ref[...])
        h1_ref[rows, :] = h1

        h_hi = h1.astype(BF16)
        h_hi32 = h_hi.astype(F32)
        h_lo = (h1 - h_hi32).astype(BF16)
        hp_ref[rows, :] = _pack_bf16_pairs(h1)
        parts = jnp.dot(jnp.concatenate([h_hi, h_lo], axis=1), wr_ref[...], preferred_element_type=F32)
        logits = parts[:, 0:LANES] + parts[:, LANES:2 * LANES] + br_ref[...]
        lt = logits.T[0:ROUTE_ROWS, :]
        rid = lax.broadcasted_iota(jnp.int32, (ROUTE_ROWS, sub), 0)
        big = jnp.int32(ROUTE_ROWS)

        def masked_softmax(valid):
            lg = jnp.where(valid, lt, NEG_INF)
            mx = jnp.max(lg, axis=0, keepdims=True)
            ex = jnp.where(valid, jnp.exp(lg - mx), 0.0)
            return ex / jnp.sum(ex, axis=0, keepdims=True)

        def first_max(vals, valid):
            top = jnp.max(jnp.where(valid, vals, -1.0), axis=0, keepdims=True)
            idx = jnp.min(jnp.where(valid & (vals == top), rid, big), axis=0, keepdims=True)
            return top, idx

        is_group = rid < N_GROUPS
        gp, gsel = first_max(masked_softmax(is_group), is_group)
        lo = ROUTE_LANE0 + EXP_PER_GROUP * gsel
        in_group = (rid >= lo) & (rid < lo + EXP_PER_GROUP)
        eprob = masked_softmax(in_group)
        p1, i1 = first_max(eprob, in_group)
        p2, i2 = first_max(eprob, in_group & (rid != i1))
        den = p1 + p2
        w1 = gp * (p1 / den)
        w2 = gp * (p2 / den)

        onehot = (rid == i1) | (rid == i2)
        before = jnp.dot(onehot.astype(BF16), tril_ref[...], preferred_element_type=F32)
        rk = before + cnt
        r1 = jnp.sum(jnp.where(rid == i1, rk, 0.0), axis=0, keepdims=True)
        r2 = jnp.sum(jnp.where(rid == i2, rk, 0.0), axis=0, keepdims=True)

        rec = lax.broadcasted_iota(jnp.int32, (ROUTE_COLS, sub), 0)
        ri_ref[:, rows] = jnp.where(rec == 0, i1 - ROUTE_LANE0,
                                    jnp.where(rec == 1, i2 - ROUTE_LANE0,
                                              jnp.where(rec == 2, r1.astype(jnp.int32),
                                                        jnp.where(rec == 3, r2.astype(jnp.int32), 0))))
        rw_ref[:, rows] = jnp.where(rec == 0, w1, jnp.where(rec == 1, w2, 0.0))
        return cnt + jnp.sum(onehot.astype(F32), axis=1, keepdims=True)

    cnt = cnt_ref[:, 0:1]
    for r0 in range(0, x_ref.shape[0], sub):
        cnt = route_rows(r0, cnt)
    cnt_ref[...] = jnp.broadcast_to(cnt, cnt_ref.shape)


def _post_call(x2, yconv, ynsa, wo, g1, b1, wr, br, tril):
    n_tok, d = x2.shape
    tm = POST_TM
    const = lambda a: pl.BlockSpec(a.shape, lambda i: (0, 0))
    return pl.pallas_call(
        _post_kernel,
        grid=(n_tok // tm,),
        in_specs=[
            pl.BlockSpec((tm, d), lambda i: (i, 0)),
            pl.BlockSpec((tm, yconv.shape[1]), lambda i: (i, 0)),
            pl.BlockSpec((tm, ynsa.shape[1]), lambda i: (i, 0)),
            const(wo), const(g1), const(b1), const(wr), const(br), const(tril),
        ],
        out_specs=(
            pl.BlockSpec((tm, d), lambda i: (i, 0)),
            pl.BlockSpec((tm, d // 2), lambda i: (i, 0)),
            pl.BlockSpec((ROUTE_COLS, tm), lambda i: (0, i)),
            pl.BlockSpec((ROUTE_COLS, tm), lambda i: (0, i)),
            pl.BlockSpec((ROUTE_ROWS, LANES), lambda i: (0, 0)),
        ),
        out_shape=(
            jax.ShapeDtypeStruct((n_tok, d), F32),
            jax.ShapeDtypeStruct((n_tok, d // 2), jnp.uint32),
            jax.ShapeDtypeStruct((ROUTE_COLS, n_tok), jnp.int32),
            jax.ShapeDtypeStruct((ROUTE_COLS, n_tok), F32),
            jax.ShapeDtypeStruct((ROUTE_ROWS, LANES), F32),
        ),
        compiler_params=_cparams(("arbitrary",)),
        name="out_proj_ln1_router",
    )(x2, yconv, ynsa, wo, g1, b1, wr, br, tril)


def _start_row_copy(src_ref, src_row, dst_ref, dst_row, sem, queue):
    pltpu.async_copy(src_ref.at[pl.ds(src_row, 1), :], dst_ref.at[pl.ds(dst_row, 1), :], sem, priority=queue)


def _sorted_row(dest_ref, r, slot):
    return dest_ref[slot * ROW_TM + r]


def _dispatch_kernel(pends_ref, nu_ref, dest_ref, h_ref, xs_ref, zero_ref, sem, zsem):
    tm = h_ref.shape[0]
    n_chunk = xs_ref.shape[0] // MOE_CHUNK

    @pl.when(pl.program_id(0) == 0)
    def _():
        zero_ref[...] = jnp.zeros_like(zero_ref)

        def zero_chunk(row0):
            row0 = pl.multiple_of(row0, MOE_CHUNK)
            return pltpu.make_async_copy(zero_ref, xs_ref.at[pl.ds(row0, MOE_CHUNK), :], zsem)

        def each_padded_chunk(act):
            def per_expert(e, _):
                end = pends_ref[e]
                start = jnp.where(e > 0, pends_ref[jnp.maximum(e - 1, 0)], 0)

                @pl.when(end > start)
                def _():
                    act(zero_chunk(end - MOE_CHUNK))
                return 0

            lax.fori_loop(0, N_EXPERTS, per_expert, 0)

            def per_dead_chunk(c, _):
                act(zero_chunk(c * MOE_CHUNK))
                return 0

            lax.fori_loop(nu_ref[0], n_chunk, per_dead_chunk, 0)

        each_padded_chunk(lambda cp: cp.start())
        each_padded_chunk(lambda cp: cp.wait())

    def issue(r, _):
        for slot in range(2):
            _start_row_copy(h_ref, r, xs_ref, _sorted_row(dest_ref, r, slot), sem, slot)
        return 0

    lax.fori_loop(0, tm, issue, 0, unroll=ROW_UNROLL)
    for slot in range(2):
        pltpu.make_async_copy(h_ref, xs_ref.at[pl.ds(0, tm), :], sem).wait()


def _dispatch_call(pends, n_used, dest, h1, n_rows):
    n_tok, d = h1.shape
    tm = ROW_TM
    grid_spec = pltpu.PrefetchScalarGridSpec(
        num_scalar_prefetch=2,
        grid=(n_tok // tm,),
        in_specs=[
            pl.BlockSpec((2 * tm,), lambda i, pe, nu: (i,), memory_space=pltpu.SMEM),
            pl.BlockSpec((tm, d), lambda i, pe, nu: (i, 0)),
        ],
        out_specs=pl.BlockSpec(memory_space=pl.ANY),
        scratch_shapes=[pltpu.VMEM((MOE_CHUNK, d), h1.dtype), pltpu.SemaphoreType.DMA, pltpu.SemaphoreType.DMA],
    )
    return pl.pallas_call(
        _dispatch_kernel,
        grid_spec=grid_spec,
        out_shape=jax.ShapeDtypeStruct((n_rows, d), h1.dtype),
        compiler_params=_cparams(("arbitrary",)),
        name="moe_dispatch",
    )(pends, n_used, dest, h1)


def _expert_kernel(ce_ref, nu_ref, xs_ref, wi_ref, wo_ref, o_ref, wib_ref, wob_ref):
    c = pl.program_id(0)
    live = c < nu_ref[0]

    @pl.when(live & ((c == 0) | (ce_ref[c] != ce_ref[jnp.maximum(c - 1, 0)])))
    def _():
        wib_ref[...] = wi_ref[0].astype(BF16)
        wob_ref[...] = wo_ref[0].astype(BF16)

    @pl.when(live)
    def _():
        x_lo, x_hi = _unpack_bf16_pairs(xs_ref[...])
        xb = jnp.concatenate([x_lo.astype(BF16), x_hi.astype(BF16)], axis=1)
        gu = jnp.dot(xb, wib_ref[...], preferred_element_type=F32)
        gate = gu[:, 0:D_EXPERT]
        act = gate * jax.nn.sigmoid(gate) * gu[:, D_EXPERT:2 * D_EXPERT]
        o_ref[...] = jnp.dot(act.astype(BF16), wob_ref[...], preferred_element_type=F32)

    @pl.when(jnp.logical_not(live))
    def _():
        o_ref[...] = jnp.zeros_like(o_ref)


def _expert_call(chunk_exp, n_used, xs, wi, wo):
    n_rows, dh = xs.shape
    d = wo.shape[2]
    n_chunk = n_rows // MOE_CHUNK

    def live(c, nu):
        return jnp.maximum(jnp.minimum(c, nu[0] - 1), 0)

    grid_spec = pltpu.PrefetchScalarGridSpec(
        num_scalar_prefetch=2,
        grid=(n_chunk,),
        in_specs=[
            pl.BlockSpec((MOE_CHUNK, dh), lambda c, ce, nu: (live(c, nu), 0)),
            pl.BlockSpec((1,) + wi.shape[1:], lambda c, ce, nu: (ce[live(c, nu)], 0, 0)),
            pl.BlockSpec((1,) + wo.shape[1:], lambda c, ce, nu: (ce[live(c, nu)], 0, 0)),
        ],
        out_specs=pl.BlockSpec((MOE_CHUNK, d), lambda c, ce, nu: (c, 0)),
        scratch_shapes=[pltpu.VMEM(wi.shape[1:], BF16), pltpu.VMEM(wo.shape[1:], BF16)],
    )
    return pl.pallas_call(
        _expert_kernel,
        grid_spec=grid_spec,
        out_shape=jax.ShapeDtypeStruct((n_rows, d), F32),
        compiler_params=_cparams(("arbitrary",)),
        name="moe_experts",
    )(chunk_exp, n_used, xs, wi, wo)


def _combine_kernel(dest_ref, dest_next_ref, h1_ref, rw_ref, p_ref, ys_ref,
                    g2_ref, b2_ref, wp_ref, wg_ref, bg_ref, g3_ref, b3_ref,
                    o_ref, rows_ref, sems):
    tm = h1_ref.shape[0]
    step = pl.program_id(0)
    cur = step % 2

    def gather(idx_ref, buf):
        def issue(r, _):
            for slot in range(2):
                _start_row_copy(ys_ref, _sorted_row(idx_ref, r, slot), rows_ref.at[buf, slot], r,
                                sems.at[buf], slot)
            return 0

        lax.fori_loop(0, tm, issue, 0, unroll=ROW_UNROLL)

    def await_rows(buf):
        for slot in range(2):
            pltpu.make_async_copy(ys_ref.at[pl.ds(0, tm), :], rows_ref.at[buf, slot], sems.at[buf]).wait()

    @pl.when(step == 0)
    def _():
        gather(dest_ref, 0)

    await_rows(cur)
    rw = rw_ref[...]
    ffn = rw[:, 0:1] * rows_ref[cur, 0] + rw[:, 1:2] * rows_ref[cur, 1]
    gather(dest_next_ref, 1 - cur)
    h2 =_layer_norm(ALPHA * h1_ref[...] + ffn, g2_ref[...], b2_ref[...])
    emb = jnp.dot(p_ref[...].astype(BF16), wp_ref[...], preferred_element_type=F32)
    gate = jax.nn.sigmoid(jnp.dot(h2.astype(BF16), wg_ref[...], preferred_element_type=F32) + bg_ref[...])
    o_ref[...] = _layer_norm(ALPHA * h2 + emb * gate, g3_ref[...], b3_ref[...])

    @pl.when(step == pl.num_programs(0) - 1)
    def _():
        await_rows(1 - cur)


def _combine_call(dest, h1, rw, p2, ys, g2, b2, wp, wg, bg, g3, b3):
    n_tok, d = h1.shape
    tm = ROW_TM
    const = lambda a: pl.BlockSpec(a.shape, lambda i: (0, 0))
    return pl.pallas_call(
        _combine_kernel,
        grid=(n_tok // tm,),
        in_specs=[
            pl.BlockSpec((2 * tm,), lambda i: (i,), memory_space=pltpu.SMEM),
            pl.BlockSpec((2 * tm,), lambda i: (jnp.minimum(i + 1, n_tok // tm - 1),), memory_space=pltpu.SMEM),
            pl.BlockSpec((tm, d), lambda i: (i, 0)),
            pl.BlockSpec((tm, rw.shape[1]), lambda i: (i, 0)),
            pl.BlockSpec((tm, p2.shape[1]), lambda i: (i, 0)),
            pl.BlockSpec(memory_space=pl.ANY),
            const(g2), const(b2), const(wp), const(wg), const(bg), const(g3), const(b3),
        ],
        out_specs=pl.BlockSpec((tm, d), lambda i: (i, 0)),
        out_shape=jax.ShapeDtypeStruct((n_tok, d), F32),
        scratch_shapes=[pltpu.VMEM((2, 2, tm, d), F32), pltpu.SemaphoreType.DMA((2,))],
        compiler_params=_cparams(("arbitrary",)),
        name="moe_combine_ln_ple",
    )(dest, dest, h1, rw, p2, ys, g2, b2, wp, wg, bg, g3, b3)


def _gate_columns():
    cols = np.zeros((N_KV, GATE_ROWS), np.int32)
    live = np.zeros((N_KV, GATE_ROWS), np.float32)
    for h in range(N_KV):
        for br in range(3):
            for g in range(GQA):
                cols[h, br * GQA + g] = (h * GQA + g) * 3 + br
                live[h, br * GQA + g] = 1.0
    return cols.reshape(-1), live.reshape(-1)


def _layer(x, p, w_in, w_conv, pe_ck, w1_ck, w2_ck, pe_cv, w1_cv, w2_cv, w_out, ln1_g, ln1_b,
           w_rg, b_rg, w_re, b_re, w_e_in, w_e_out, ln2_g, ln2_b, w_ple, w_ple_gate, b_ple_gate,
           ln3_g, ln3_b):
    bsz, s, d = x.shape
    n_tok = bsz * s
    row = lambda v: v.reshape(1, -1)

    c_q, c_kc, c_vc, c_ks, c_vs, c_kw, c_vw, c_g = 1536, 2048, 2176, 2304, 2432, 2560, 2688, 2816
    wn = jnp.concatenate([w_in[:, 0:c_q], w_in[:, c_kc:c_ks], w_in[:, c_ks:c_vs], w_in[:, c_kw:c_vw]],
                         axis=1).astype(BF16)
    gcols, glive = _gate_columns()
    w_gate = w_in[:, c_g:c_g + 3 * N_HEADS][:, gcols] * glive[None, :]
    wt = jnp.concatenate([w_in[:, c_q:c_kc] * Q_SCALE, w_in[:, c_vs:c_kw], w_in[:, c_vw:c_g], w_gate],
                         axis=1).T.astype(BF16)

    yconv, cmpkv, ksa, kw, qt, vst, vwt, gt = _proj_call(x, wn, wt, w_conv)

    half_blk = CMP_BLOCK // 2
    eye = jnp.eye(N_KV, dtype=F32)

    def per_head_w1(w1):
        w = w1.reshape(2, half_blk, HEAD_DIM, CMP_HIDDEN)
        w = jnp.einsum('ptdc,hg->pthdgc', w, eye)
        return w.reshape(2 * half_blk * N_KV * HEAD_DIM, N_KV * CMP_HIDDEN)

    def per_head_w2(w2):
        return jnp.einsum('cd,hg->hcgd', w2, eye).reshape(N_KV * CMP_HIDDEN, N_KV * HEAD_DIM)

    def per_head_pe(pe):
        return jnp.broadcast_to(pe.reshape(2, half_blk, 1, HEAD_DIM),
                                (2, half_blk, N_KV, HEAD_DIM)).reshape(2, half_blk * N_KV * HEAD_DIM)

    pes = jnp.stack([per_head_pe(pe_ck), per_head_pe(pe_cv)])
    w1s = jnp.stack([per_head_w1(w1_ck), per_head_w1(w1_cv)]).astype(BF16)
    w2s = jnp.stack([per_head_w2(w2_ck), per_head_w2(w2_cv)]).astype(BF16)
    cmp_out = _compress_call(cmpkv, pes, w1s, w2s)
    kcmp = cmp_out[:, 0].astype(BF16)
    vct = cmp_out[:, 1].transpose(0, 2, 1).astype(BF16)

    kw = jnp.pad(kw, ((0, 0), (WINDOW, 0), (0, 0)))
    vwt = jnp.pad(vwt, ((0, 0), (0, 0), (WINDOW // LANES, 0), (0, 0), (0, 0)))
    ynsa = _attn_call(qt, gt, kcmp, vct, ksa, vst, kw, vwt)

    wr = jnp.zeros((d, LANES), F32).at[:, 0:N_GROUPS].set(w_rg)
    wr = wr.at[:, ROUTE_LANE0:ROUTE_LANE0 + N_EXPERTS].set(w_re)
    br = jnp.zeros((1, LANES), F32).at[0, 0:N_GROUPS].set(b_rg)
    br = br.at[0, ROUTE_LANE0:ROUTE_LANE0 + N_EXPERTS].set(b_re)
    wr_hi = wr.astype(BF16)
    wr_lo = (wr - wr_hi.astype(F32)).astype(BF16)
    wr = jnp.concatenate([jnp.concatenate([wr_hi, wr_lo], axis=1),
                          jnp.concatenate([wr_hi, jnp.zeros_like(wr_lo)], axis=1)], axis=0)
    earlier = jnp.asarray(np.triu(np.ones((POST_TM, POST_TM), np.float32), 1), BF16)
    h1, h1_packed, ri, rw, cnt = _post_call(x.reshape(n_tok, d), yconv.reshape(n_tok, -1), ynsa.reshape(n_tok, -1),
                                            w_out.astype(BF16), row(ln1_g), row(ln1_b), wr, br, earlier)
    rw = rw.T

    counts = cnt[ROUTE_LANE0:ROUTE_LANE0 + N_EXPERTS, 0].astype(jnp.int32)
    pcounts = (counts + MOE_CHUNK - 1) // MOE_CHUNK * MOE_CHUNK
    pends = jnp.cumsum(pcounts)
    pstarts = (pends - pcounts).astype(jnp.int32)
    seg0 = jnp.sum(jnp.where(ri[0:2, :, None] == jnp.arange(N_EXPERTS, dtype=jnp.int32), pstarts, 0), axis=-1)
    dest = (seg0 + ri[2:4]).reshape(2, n_tok // ROW_TM, ROW_TM).transpose(1, 0, 2).reshape(-1)
    n_asg = n_tok * 2
    n_chunk = -(-n_asg // MOE_CHUNK) + N_EXPERTS
    chunk_row0 = jnp.arange(n_chunk, dtype=jnp.int32) * MOE_CHUNK
    chunk_exp = jnp.minimum(jnp.sum(pends[None, :] <= chunk_row0[:, None], axis=1), N_EXPERTS - 1).astype(jnp.int32)
    n_used = (pends[-1:] // MOE_CHUNK).astype(jnp.int32)

    xs = _dispatch_call(pends.astype(jnp.int32), n_used, dest, h1_packed, n_chunk * MOE_CHUNK)
    ys = _expert_call(chunk_exp, n_used, xs, w_e_in, w_e_out)
    out = _combine_call(dest, h1, rw, p.reshape(n_tok, -1), ys, row(ln2_g), row(ln2_b),
                        w_ple.astype(BF16), w_ple_gate.astype(BF16), row(b_ple_gate), row(ln3_g), row(ln3_b))
    return out.reshape(bsz, s, d)


def kernel(x, p, w_in, w_conv, pe_ck, w1_ck, w2_ck, pe_cv, w1_cv, w2_cv, w_out, ln1_g, ln1_b, w_rg, b_rg, w_re, b_re, w_e_in, w_e_out, ln2_g, ln2_b, w_ple, w_ple_gate, b_ple_gate, ln3_g, ln3_b):
    assert w_in.shape[0] == DEPTH, "residual scaling ALPHA is derived from DEPTH"
    h = x
    for i in range(DEPTH):
        h = _layer(h, p[i], w_in[i], w_conv[i], pe_ck[i], w1_ck[i], w2_ck[i], pe_cv[i], w1_cv[i], w2_cv[i],
                   w_out[i], ln1_g[i], ln1_b[i], w_rg[i], b_rg[i], w_re[i], b_re[i], w_e_in[i], w_e_out[i],
                   ln2_g[i], ln2_b[i], w_ple[i], w_ple_gate[i], b_ple_gate[i], ln3_g[i], ln3_b[i])
    return h
```
